```python
import jax, jax.numpy as jnp
from jax import lax
import numpy as np

D_MODEL = 1024
BATCH = 8
SEQ = 2048
DEPTH = 1

ATTN_HEADS = 8
ATTN_KV_HEADS = 2
HEAD_DIM = 64
ATTN_GROUP = ATTN_HEADS // ATTN_KV_HEADS
ATTN_WIDTH = ATTN_HEADS * HEAD_DIM
KV_WIDTH = ATTN_KV_HEADS * HEAD_DIM
ROT_DIM = HEAD_DIM // 4
ROPE_THETA = 500000.0
IDX_HEADS = 4
IDX_DIM = 64
TOPK_MAX = 256
Q_BLOCK = 128
SSM_HEADS = 16
SSM_HEAD_DIM = 64
SSM_WIDTH = SSM_HEADS * SSM_HEAD_DIM
SSM_GROUPS = 4
SSM_HEADS_PER_GROUP = SSM_HEADS // SSM_GROUPS
SSM_STATE = 64
CONV_K = 4
CHUNK = 128
CONV_WIDTH = SSM_WIDTH + 2 * SSM_GROUPS * SSM_STATE
N_BRANCH = 2
EPS = 1e-6
SPLIT_SIZES = (ATTN_WIDTH, KV_WIDTH, KV_WIDTH, ATTN_WIDTH,
               IDX_HEADS * IDX_DIM, IDX_DIM, IDX_HEADS,
               SSM_WIDTH, SSM_WIDTH, SSM_GROUPS * SSM_STATE, SSM_GROUPS * SSM_STATE, SSM_HEADS,
               N_BRANCH * D_MODEL)
IN_WIDTH = sum(SPLIT_SIZES)
SPLIT_OFFSETS = tuple(int(o) for o in np.cumsum(SPLIT_SIZES)[:-1])

kernel_name = "hybrid_dsa_ssd_gated_merge"


def _rmsnorm(x, w):
    xf = x.astype(jnp.float32)
    y = xf * lax.rsqrt(jnp.mean(xf * xf, axis=-1, keepdims=True) + EPS)
    return (y * w.astype(jnp.float32)).astype(x.dtype)


def _partial_rope(x, cos, sin):
    half = ROT_DIM // 2
    x1 = x[..., :half].astype(jnp.float32)
    x2 = x[..., half:ROT_DIM].astype(jnp.float32)
    rot = jnp.concatenate([x1 * cos - x2 * sin, x2 * cos + x1 * sin], axis=-1).astype(x.dtype)
    return jnp.concatenate([rot, x[..., ROT_DIM:]], axis=-1)


def _sparse_attention(q, k, v, q_idx, k_idx, w_idx):
    bsz, seq = q.shape[0], q.shape[1]
    n_blk = seq // Q_BLOCK
    top_k = min(TOPK_MAX, seq // 4)
    key_pos = jnp.arange(seq)
    idx_scale = IDX_DIM ** -0.5
    head_w_scale = IDX_HEADS ** -0.5
    attn_scale = HEAD_DIM ** -0.5

    def to_blocks(a):
        return jnp.moveaxis(a.reshape(bsz, n_blk, Q_BLOCK, *a.shape[2:]), 1, 0)

    def one_block(args):
        qb, qib, wb, t0 = args
        q_pos = t0 + jnp.arange(Q_BLOCK)
        causal = key_pos[None, :] <= q_pos[:, None]
        logits = jnp.einsum('bthd,bsd->bths', qib, k_idx).astype(jnp.float32) * idx_scale
        score = jnp.einsum('bth,bths->bts', wb.astype(jnp.float32) * head_w_scale, jax.nn.relu(logits))
        score = jnp.where(causal[None], score, -jnp.inf)
        _, sel = lax.top_k(score, top_k)
        valid = sel <= q_pos[None, :, None]
        k_sel = jax.vmap(lambda kb, ib: kb[ib])(k, sel)
        v_sel = jax.vmap(lambda vb, ib: vb[ib])(v, sel)
        qg = qb.reshape(bsz, Q_BLOCK, ATTN_KV_HEADS, ATTN_GROUP, HEAD_DIM)
        s = jnp.einsum('btkgd,btskd->btkgs', qg, k_sel).astype(jnp.float32) * attn_scale
        s = jnp.where(valid[:, :, None, None, :], s, -jnp.inf)
        p = jax.nn.softmax(s, axis=-1).astype(v.dtype)
        o = jnp.einsum('btkgs,btskd->btkgd', p, v_sel)
        return o.reshape(bsz, Q_BLOCK, ATTN_WIDTH)

    out = lax.map(one_block, (to_blocks(q), to_blocks(q_idx), to_blocks(w_idx),
                              jnp.arange(n_blk) * Q_BLOCK))
    return jnp.moveaxis(out, 0, 1).reshape(bsz, seq, ATTN_WIDTH)


def _causal_depthwise_conv(x, w, b):
    y = lax.conv_general_dilated(x, w[:, None, :].astype(x.dtype), window_strides=(1,),
                                 padding=[(CONV_K - 1, 0)],
                                 dimension_numbers=('NWC', 'WIO', 'NWC'),
                                 feature_group_count=x.shape[-1])
    return y + b.astype(x.dtype)


def _segsum(a):
    t = a.shape[-1]
    a_rep = jnp.broadcast_to(a[..., None], a.shape + (t,))
    a_rep = jnp.where(jnp.tril(jnp.ones((t, t), bool), -1), a_rep, 0.0)
    seg = jnp.cumsum(a_rep, axis=-2)
    return jnp.where(jnp.tril(jnp.ones((t, t), bool)), seg, -jnp.inf)


def _ssd(x, dt, a, bm, cm):
    bsz, seq = x.shape[0], x.shape[1]
    nc = seq // CHUNK
    g, r = SSM_GROUPS, SSM_HEADS_PER_GROUP
    xd = (x * dt[..., None]).reshape(bsz, nc, CHUNK, g, r, SSM_HEAD_DIM)
    da = (dt * a).reshape(bsz, nc, CHUNK, g, r).transpose(0, 3, 4, 1, 2)
    bc = bm.reshape(bsz, nc, CHUNK, g, SSM_STATE)
    cc = cm.reshape(bsz, nc, CHUNK, g, SSM_STATE)
    a_cum = jnp.cumsum(da, axis=-1)
    l_mat = jnp.exp(_segsum(da))
    cb = jnp.einsum('bclgn,bcsgn->bcgls', cc, bc)
    y_diag = jnp.einsum('bcgls,bgrcls,bcsgrp->bclgrp', cb, l_mat, xd)
    decay_states = jnp.exp(a_cum[..., -1:] - a_cum)
    states = jnp.einsum('bclgn,bgrcl,bclgrp->bcgrpn', bc, decay_states, xd)
    states = jnp.concatenate([jnp.zeros_like(states[:, :1]), states], axis=1)
    chunk_a = jnp.pad(a_cum[..., -1], ((0, 0), (0, 0), (0, 0), (1, 0)))
    decay_chunk = jnp.exp(_segsum(chunk_a))
    new_states = jnp.einsum('bgrzc,bcgrpn->bzgrpn', decay_chunk, states)
    states = new_states[:, :-1]
    y_off = jnp.einsum('bclgn,bcgrpn,bgrcl->bclgrp', cc, states, jnp.exp(a_cum))
    return (y_diag + y_off).reshape(bsz, seq, SSM_HEADS, SSM_HEAD_DIM)


def _gated_group_rmsnorm(y, z, w):
    yz = (y * jax.nn.silu(z)).astype(jnp.float32)
    yg = yz.reshape(*y.shape[:-1], SSM_GROUPS, SSM_WIDTH // SSM_GROUPS)
    yg = yg * lax.rsqrt(jnp.mean(yg * yg, axis=-1, keepdims=True) + EPS)
    return (yg.reshape(y.shape) * w.astype(jnp.float32)).astype(y.dtype)


def setup_inputs(seed: int = 0) -> dict:
    key = jax.random.key(seed)
    ks = jax.random.split(key, 16)
    f32 = jnp.float32
    x = jax.random.normal(ks[0], (BATCH, SEQ, D_MODEL), f32)
    offset = jax.random.randint(ks[1], (BATCH, 1), 0, 4096, dtype=jnp.int32)
    positions = jnp.arange(SEQ, dtype=jnp.int32)[None, :] + offset
    norm_w = 1.0 + 0.02 * jax.random.normal(ks[2], (DEPTH, D_MODEL), f32)
    w_in = jax.random.normal(ks[3], (DEPTH, D_MODEL, IN_WIDTH), f32) * D_MODEL ** -0.5
    gate_bias = 0.01 * jax.random.normal(ks[4], (DEPTH, N_BRANCH * D_MODEL), f32)
    conv_w = jax.random.normal(ks[5], (DEPTH, CONV_K, CONV_WIDTH), f32) * CONV_K ** -0.5
    conv_b = 0.01 * jax.random.normal(ks[6], (DEPTH, CONV_WIDTH), f32)
    u = jax.random.uniform(ks[7], (DEPTH, SSM_HEADS), f32)
    dt0 = jnp.exp(u * (jnp.log(0.1) - jnp.log(0.001)) + jnp.log(0.001))
    dt_bias = dt0 + jnp.log(-jnp.expm1(-dt0))
    a_log = jnp.log(jax.random.uniform(ks[8], (DEPTH, SSM_HEADS), f32, 1.0, 16.0))
    d_skip = 1.0 + 0.1 * jax.random.normal(ks[9], (DEPTH, SSM_HEADS), f32)
    ssm_norm_w = 1.0 + 0.02 * jax.random.normal(ks[10], (DEPTH, SSM_WIDTH), f32)
    w_branch_a = jax.random.normal(ks[11], (DEPTH, ATTN_WIDTH, D_MODEL), f32) * ATTN_WIDTH ** -0.5
    w_branch_b = jax.random.normal(ks[12], (DEPTH, SSM_WIDTH, D_MODEL), f32) * SSM_WIDTH ** -0.5
    w_out = jax.random.normal(ks[13], (DEPTH, D_MODEL, D_MODEL), f32) * D_MODEL ** -0.5
    final_norm_w = 1.0 + 0.02 * jax.random.normal(ks[14], (D_MODEL,), f32)
    return {"x": x, "positions": positions, "norm_w": norm_w, "w_in": w_in, "gate_bias": gate_bias,
            "conv_w": conv_w, "conv_b": conv_b, "dt_bias": dt_bias, "a_log": a_log, "d_skip": d_skip,
            "ssm_norm_w": ssm_norm_w, "w_branch_a": w_branch_a, "w_branch_b": w_branch_b,
            "w_out": w_out, "final_norm_w": final_norm_w}


def reference(x, positions, norm_w, w_in, gate_bias, conv_w, conv_b, dt_bias, a_log, d_skip,
              ssm_norm_w, w_branch_a, w_branch_b, w_out, final_norm_w):
    bsz, seq, _ = x.shape
    inv_freq = ROPE_THETA ** (-jnp.arange(0, ROT_DIM, 2, dtype=jnp.float32) / ROT_DIM)
    ang = positions.astype(jnp.float32)[..., None] * inv_freq
    cos = jnp.cos(ang)[:, :, None, :]
    sin = jnp.sin(ang)[:, :, None, :]

    for i in range(DEPTH):
        h = _rmsnorm(x, norm_w[i])
        proj = h @ w_in[i].astype(h.dtype)
        (q, k, v, z_a, q_idx, k_idx, w_idx,
         z_b, x_b, b_b, c_b, dt_raw, gates) = jnp.split(proj, SPLIT_OFFSETS, axis=-1)

        q = _partial_rope(q.reshape(bsz, seq, ATTN_HEADS, HEAD_DIM), cos, sin)
        k = _partial_rope(k.reshape(bsz, seq, ATTN_KV_HEADS, HEAD_DIM), cos, sin)
        v = v.reshape(bsz, seq, ATTN_KV_HEADS, HEAD_DIM)
        q_idx = _partial_rope(q_idx.reshape(bsz, seq, IDX_HEADS, IDX_DIM), cos, sin)
        k_idx = _partial_rope(k_idx[:, :, None, :], cos, sin)[:, :, 0]
        o_a = _sparse_attention(q, k, v, q_idx, k_idx, w_idx) * jax.nn.silu(z_a)

        xbc = jnp.concatenate([x_b, b_b, c_b], axis=-1)
        xbc = jax.nn.silu(_causal_depthwise_conv(xbc, conv_w[i], conv_b[i]))
        x_s, b_s, c_s = jnp.split(xbc, [SSM_WIDTH, SSM_WIDTH + SSM_GROUPS * SSM_STATE], axis=-1)
        dt = jax.nn.softplus(dt_raw.astype(jnp.float32) + dt_bias[i].astype(jnp.float32))
        a = -jnp.exp(a_log[i].astype(jnp.float32))
        x_h = x_s.reshape(bsz, seq, SSM_HEADS, SSM_HEAD_DIM).astype(jnp.float32)
        y = _ssd(x_h, dt, a,
                 b_s.reshape(bsz, seq, SSM_GROUPS, SSM_STATE).astype(jnp.float32),
                 c_s.reshape(bsz, seq, SSM_GROUPS, SSM_STATE).astype(jnp.float32))
        y = y + d_skip[i].astype(jnp.float32)[:, None] * x_h
        y = y.reshape(bsz, seq, SSM_WIDTH).astype(x.dtype)
        o_b = _gated_group_rmsnorm(y, z_b, ssm_norm_w[i])

        g = jax.nn.sigmoid((gates + gate_bias[i].astype(gates.dtype)).astype(jnp.float32)).astype(x.dtype)
        g_a, g_b = jnp.split(g, [D_MODEL], axis=-1)
        merged = g_a * (o_a @ w_branch_a[i].astype(o_a.dtype)) + g_b * (o_b @ w_branch_b[i].astype(o_b.dtype))
        x = x + merged @ w_out[i].astype(merged.dtype)

    return _rmsnorm(x, final_norm_w)
```

```python
import functools

import numpy as np
import jax
import jax.numpy as jnp
from jax import lax
from jax.experimental import pallas as pl
from jax.experimental.pallas import tpu as pltpu

D_MODEL = 1024
ATTN_HEADS = 8
ATTN_KV_HEADS = 2
HEAD_DIM = 64
ATTN_GROUP = ATTN_HEADS // ATTN_KV_HEADS
ATTN_WIDTH = ATTN_HEADS * HEAD_DIM
KV_WIDTH = ATTN_KV_HEADS * HEAD_DIM
ROT_DIM = HEAD_DIM // 4
ROT_HALF = ROT_DIM // 2
ROPE_THETA = 500000.0
IDX_HEADS = 4
IDX_DIM = 64
TOPK_MAX = 256
SSM_HEADS = 16
SSM_HEAD_DIM = 64
SSM_WIDTH = SSM_HEADS * SSM_HEAD_DIM
SSM_GROUPS = 4
SSM_STATE = 64
CONV_K = 4
CHUNK = 128
BC_WIDTH = 2 * SSM_GROUPS * SSM_STATE
N_BRANCH = 2
EPS = 1e-6
SPLIT_SIZES = (ATTN_WIDTH, KV_WIDTH, KV_WIDTH, ATTN_WIDTH,
               IDX_HEADS * IDX_DIM, IDX_DIM, IDX_HEADS,
               SSM_WIDTH, SSM_WIDTH, SSM_GROUPS * SSM_STATE, SSM_GROUPS * SSM_STATE, SSM_HEADS,
               N_BRANCH * D_MODEL)
SPLIT_OFFSETS = tuple(int(o) for o in np.cumsum((0,) + SPLIT_SIZES))

LANES = 128
SUBLANES = 8
VMEM_LIMIT_BYTES = 56 * 1024 * 1024

PROJ_ROWS = 256
Q_TILE = 128
KEY_TILE = 128
MERGE_ROWS = 512
W_IDX_ROWS = 8
BCDT_WIDTH = BC_WIDTH + LANES
NEG_BIG = -1e30

_F32 = jnp.float32
_BF16 = jnp.bfloat16
_NT = (((1,), (1,)), ((), ()))
_TN = (((0,), (0,)), ((), ()))


def _dot(a, b):
    return jnp.dot(a, b, preferred_element_type=_F32)


def _silu(x):
    return x * jax.nn.sigmoid(x)


def _proj_kernel(x_ref, nw_ref, cos_t_ref, sin_t_ref, cos_k_ref, sin_k_ref,
                 w_t_ref, w_k_ref, w_kp_ref, w_zb_ref, w_xb_ref, w_bcdt_ref, w_g_ref, w_za_ref,
                 q_t_ref, qi_t_ref, wi_t_ref, v_t_ref, k_ref, kidx_ref,
                 zb_ref, xb_ref, bcdt_ref, g_ref, za_ref):
    x = x_ref[0]
    h = x * lax.rsqrt(jnp.mean(x * x, axis=-1, keepdims=True) + EPS) * nw_ref[...]
    h = h.astype(_BF16)

    t = lax.dot_general(w_t_ref[...], h, _NT, preferred_element_type=_F32)
    cos_t = cos_t_ref[0]
    sin_t = sin_t_ref[0]

    def rope_t(block, n_heads, scale, out_ref):
        out_ref[0] = (block * scale).astype(out_ref.dtype)
        for hd in range(n_heads):
            x1 = block[hd * HEAD_DIM: hd * HEAD_DIM + ROT_HALF]
            x2 = block[hd * HEAD_DIM + ROT_HALF: hd * HEAD_DIM + ROT_DIM]
            rot = jnp.concatenate([x1 * cos_t - x2 * sin_t, x2 * cos_t + x1 * sin_t], axis=0)
            out_ref[0, hd * HEAD_DIM: hd * HEAD_DIM + ROT_DIM, :] = (rot * scale).astype(out_ref.dtype)

    o_q, o_qi, o_v, o_w = 0, ATTN_WIDTH, ATTN_WIDTH + IDX_HEADS * IDX_DIM, ATTN_WIDTH + IDX_HEADS * IDX_DIM + KV_WIDTH
    rope_t(t[o_q:o_qi], ATTN_HEADS, HEAD_DIM ** -0.5, q_t_ref)
    rope_t(t[o_qi:o_v], IDX_HEADS, IDX_DIM ** -0.5, qi_t_ref)
    v_t_ref[0] = t[o_v:o_w].astype(_BF16)
    wi_t_ref[0] = t[o_w:o_w + W_IDX_ROWS] * (IDX_HEADS ** -0.5)

    ks = _dot(h, w_k_ref[...])
    ksp = _dot(h, w_kp_ref[...])
    kr = ks * cos_k_ref[0] + ksp * sin_k_ref[0]
    k_ref[0] = kr[:, :KV_WIDTH].astype(_BF16)
    kidx_ref[0] = kr[:, KV_WIDTH:KV_WIDTH + IDX_DIM].astype(_BF16)

    zb_ref[0] = _dot(h, w_zb_ref[...])
    xb_ref[0] = _dot(h, w_xb_ref[...])
    bcdt_ref[0] = _dot(h, w_bcdt_ref[...])
    g_ref[0] = _dot(h, w_g_ref[...])
    za_ref[0] = _dot(h, w_za_ref[...])


def _const_spec(shape):
    nd = len(shape)
    return pl.BlockSpec(shape, lambda *_: (0,) * nd, pipeline_mode=pl.Buffered(1))


def _projection(x, nw, cos_t, sin_t, cos_k, sin_k, weights):
    bsz, seq, _ = x.shape
    tm = PROJ_ROWS
    n_t = weights[0].shape[0]
    tok = lambda width: pl.BlockSpec((1, tm, width), lambda b, i: (b, i, 0))
    feat = lambda rows: pl.BlockSpec((1, rows, tm), lambda b, i: (b, 0, i))
    in_specs = [tok(D_MODEL), _const_spec((1, D_MODEL)), feat(ROT_HALF), feat(ROT_HALF),
                tok(2 * LANES), tok(2 * LANES)] + [_const_spec(w.shape) for w in weights]
    out_shape = (
        jax.ShapeDtypeStruct((bsz, ATTN_WIDTH, seq), _BF16),
        jax.ShapeDtypeStruct((bsz, IDX_HEADS * IDX_DIM, seq), _BF16),
        jax.ShapeDtypeStruct((bsz, W_IDX_ROWS, seq), _F32),
        jax.ShapeDtypeStruct((bsz, KV_WIDTH, seq), _BF16),
        jax.ShapeDtypeStruct((bsz, seq, KV_WIDTH), _BF16),
        jax.ShapeDtypeStruct((bsz, seq, IDX_DIM), _BF16),
        jax.ShapeDtypeStruct((bsz, seq, SSM_WIDTH), _F32),
        jax.ShapeDtypeStruct((bsz, seq, SSM_WIDTH), _F32),
        jax.ShapeDtypeStruct((bsz, seq, BCDT_WIDTH), _F32),
        jax.ShapeDtypeStruct((bsz, seq, N_BRANCH * D_MODEL), _F32),
        jax.ShapeDtypeStruct((bsz, seq, ATTN_WIDTH), _F32),
    )
    out_specs = (feat(ATTN_WIDTH), feat(IDX_HEADS * IDX_DIM), feat(W_IDX_ROWS), feat(KV_WIDTH),
                 tok(KV_WIDTH), tok(IDX_DIM), tok(SSM_WIDTH), tok(SSM_WIDTH), tok(BCDT_WIDTH),
                 tok(N_BRANCH * D_MODEL), tok(ATTN_WIDTH))
    del n_t
    return pl.pallas_call(
        _proj_kernel,
        grid=(bsz, seq // tm),
        in_specs=in_specs,
        out_specs=out_specs,
        out_shape=out_shape,
        compiler_params=pltpu.CompilerParams(
            dimension_semantics=("parallel", "parallel"), vmem_limit_bytes=VMEM_LIMIT_BYTES),
        name="proj",
    )(x, nw, cos_t, sin_t, cos_k, sin_k, *weights)


def _attn_kernel(q_t_ref, qi_t_ref, wi_t_ref, k_ref, kidx_ref, v_t_ref, za_ref, o_ref,
                 score_scr, bias_scr, *, top_k):
    j = pl.program_id(1)
    n_chunks = j + 1
    tq = Q_TILE
    q_pos = j * tq + lax.broadcasted_iota(jnp.int32, (1, tq), 1)
    key_iota = lax.broadcasted_iota(jnp.int32, (KEY_TILE, tq), 0)

    qi_t = qi_t_ref[0]
    qi_cat = jnp.concatenate([qi_t[hd * IDX_DIM:(hd + 1) * IDX_DIM] for hd in range(IDX_HEADS)], axis=1)
    wi_t = wi_t_ref[0]
    wi_cat = jnp.concatenate([wi_t[hd:hd + 1] for hd in range(IDX_HEADS)], axis=1)

    def score_chunk(c, carry):
        start = pl.multiple_of(c * KEY_TILE, KEY_TILE)
        logits = _dot(kidx_ref[0, pl.ds(start, KEY_TILE), :], qi_cat)
        weighted = jnp.maximum(logits, 0.0) * wi_cat
        score = weighted[:, 0:tq]
        for hd in range(1, IDX_HEADS):
            score = score + weighted[:, hd * tq:(hd + 1) * tq]
        causal = (start + key_iota) <= q_pos
        score_scr[pl.ds(start, KEY_TILE), :] = jnp.where(causal, score, -jnp.inf)
        return carry

    lax.fori_loop(0, n_chunks, score_chunk, 0)

    def count_ge(cand, strict):
        def body(c, acc):
            start = pl.multiple_of(c * KEY_TILE, KEY_TILE)
            s = score_scr[pl.ds(start, KEY_TILE), :]
            hit = (s > cand) if strict else (s >= cand)
            inc = jnp.where(hit, 1.0, 0.0)
            return acc + inc.reshape(KEY_TILE // SUBLANES, SUBLANES, tq).sum(axis=0)
        acc = lax.fori_loop(0, n_chunks, body, jnp.zeros((SUBLANES, tq), _F32))
        return jnp.sum(acc, axis=0, keepdims=True)

    int_min = jnp.int32(-2 ** 31)

    def ordered_to_float(kbits):
        return pltpu.bitcast(kbits ^ ((kbits >> 31) & jnp.int32(0x7FFFFFFF)), _F32)

    def bit_body(i, t_bits):
        bit = jnp.where(i == 0, int_min, jnp.int32(1) << (31 - i))
        cand_bits = t_bits ^ bit
        cnt = count_ge(ordered_to_float(cand_bits), strict=False)
        return jnp.where(cnt >= top_k, cand_bits, t_bits)

    t_bits = lax.fori_loop(0, 32, bit_body, jnp.full((1, tq), int_min, jnp.int32))
    few = (q_pos + 1) <= top_k
    thr = jnp.where(few, -jnp.inf, ordered_to_float(t_bits))
    n_above = count_ge(thr, strict=True)
    n_ties_kept = jnp.where(few, 0.0, top_k - n_above)

    row = lax.broadcasted_iota(jnp.int32, (KEY_TILE, KEY_TILE), 0)
    col = lax.broadcasted_iota(jnp.int32, (KEY_TILE, KEY_TILE), 1)
    strict_lower = jnp.where(col < row, 1.0, 0.0).astype(_BF16)

    def bias_chunk(c, ties_before):
        start = pl.multiple_of(c * KEY_TILE, KEY_TILE)
        s = score_scr[pl.ds(start, KEY_TILE), :]
        tie = jnp.where(s == thr, 1.0, 0.0)
        rank = _dot(strict_lower, tie.astype(_BF16)) + ties_before
        keep = (s > thr) | ((s == thr) & (rank < n_ties_kept))
        bias_scr[pl.ds(start, KEY_TILE), :] = jnp.where(keep, 0.0, NEG_BIG)
        return ties_before + jnp.sum(tie, axis=0, keepdims=True)

    lax.fori_loop(0, n_chunks, bias_chunk, jnp.zeros((1, tq), _F32))

    q_t = q_t_ref[0]
    zeros_half = jnp.zeros((HEAD_DIM, ATTN_GROUP * tq), _BF16)
    out_t = []
    for g in range(ATTN_KV_HEADS):
        q_cat = jnp.concatenate(
            [q_t[(g * ATTN_GROUP + hd) * HEAD_DIM:(g * ATTN_GROUP + hd + 1) * HEAD_DIM] for hd in range(ATTN_GROUP)],
            axis=1)
        q_pad = jnp.concatenate([q_cat, zeros_half] if g == 0 else [zeros_half, q_cat], axis=0)

        def attn_chunk(c, carry, q_pad=q_pad, g=g):
            m, l, acc = carry
            start = pl.multiple_of(c * KEY_TILE, KEY_TILE)
            s = _dot(k_ref[0, pl.ds(start, KEY_TILE), :], q_pad)
            b = bias_scr[pl.ds(start, KEY_TILE), :]
            s = s + jnp.concatenate([b] * ATTN_GROUP, axis=1)
            m_new = jnp.maximum(m, jnp.max(s, axis=0, keepdims=True))
            alpha = jnp.exp(m - m_new)
            p = jnp.exp(s - m_new)
            l_new = alpha * l + jnp.sum(p, axis=0, keepdims=True)
            v_c = v_t_ref[0, g * HEAD_DIM:(g + 1) * HEAD_DIM, pl.ds(start, KEY_TILE)]
            acc_new = alpha * acc + _dot(v_c, p.astype(_BF16))
            return m_new, l_new, acc_new

        init = (jnp.full((1, ATTN_GROUP * tq), NEG_BIG, _F32),
                jnp.zeros((1, ATTN_GROUP * tq), _F32),
                jnp.zeros((HEAD_DIM, ATTN_GROUP * tq), _F32))
        m, l, acc = lax.fori_loop(0, n_chunks, attn_chunk, init)
        out_t.append(acc / l)

    za = za_ref[0]
    for pair in range(ATTN_HEADS // 2):
        g, first = divmod(2 * pair, ATTN_GROUP)
        both = jnp.concatenate([out_t[g][:, first * tq:(first + 1) * tq],
                                out_t[g][:, (first + 1) * tq:(first + 2) * tq]], axis=0)
        z = za[:, pair * LANES:(pair + 1) * LANES]
        o_ref[0, :, pair * LANES:(pair + 1) * LANES] = (both.T * _silu(z)).astype(o_ref.dtype)


def _attention(q_t, qi_t, wi_t, k, kidx, v_t, za):
    bsz, seq, _ = k.shape
    tq = Q_TILE
    top_k = min(TOPK_MAX, seq // 4)
    feat = lambda rows: pl.BlockSpec((1, rows, tq), lambda b, j: (b, 0, j))
    full = lambda s1, s2: pl.BlockSpec((1, s1, s2), lambda b, j: (b, 0, 0))
    return pl.pallas_call(
        functools.partial(_attn_kernel, top_k=top_k),
        grid=(bsz, seq // tq),
        in_specs=[feat(ATTN_WIDTH), feat(IDX_HEADS * IDX_DIM), feat(W_IDX_ROWS),
                  full(seq, KV_WIDTH), full(seq, IDX_DIM), full(KV_WIDTH, seq),
                  pl.BlockSpec((1, tq, ATTN_WIDTH), lambda b, j: (b, j, 0))],
        out_specs=pl.BlockSpec((1, tq, ATTN_WIDTH), lambda b, j: (b, j, 0)),
        out_shape=jax.ShapeDtypeStruct((bsz, seq, ATTN_WIDTH), _BF16),
        scratch_shapes=[pltpu.VMEM((seq, tq), _F32), pltpu.VMEM((seq, tq), _F32)],
        compiler_params=pltpu.CompilerParams(
            dimension_semantics=("parallel", "arbitrary"), vmem_limit_bytes=VMEM_LIMIT_BYTES),
        name="attn",
    )(q_t, qi_t, wi_t, k, kidx, v_t, za)


def _split3(x):
    hi = x.astype(_BF16)
    r1 = x - hi.astype(_F32)
    mid = r1.astype(_BF16)
    lo = (r1 - mid.astype(_F32)).astype(_BF16)
    return hi, mid, lo


def _causal_conv_silu(x, tail_ref, w_ref, b_ref):
    rows = x.shape[0]
    tail = tail_ref[...]
    sub = lax.broadcasted_iota(jnp.int32, (SUBLANES, x.shape[1]), 0)
    w = w_ref[...]
    y = x * w[CONV_K - 1:CONV_K] + b_ref[...]
    for shift in range(1, CONV_K):
        rolled = pltpu.roll(x, shift, 0)
        head = jnp.where(sub < shift, pltpu.roll(tail, shift, 0), rolled[:SUBLANES])
        shifted = jnp.concatenate([head, rolled[SUBLANES:]], axis=0)
        y = y + shifted * w[CONV_K - 1 - shift:CONV_K - shift]
    tail_ref[...] = x[rows - SUBLANES:]
    return _silu(y)


def _ssd_kernel(xb_ref, bcdt_ref, zb_ref, cwx_ref, cbx_ref, cwbc_ref, cbbc_ref,
                dtb_ref, alog_ref, dskip_ref, nw_ref, o_ref,
                tailx_scr, tailbc_scr, state_scr, y_scr):
    @pl.when(pl.program_id(1) == 0)
    def _():
        tailx_scr[...] = jnp.zeros_like(tailx_scr)
        tailbc_scr[...] = jnp.zeros_like(tailbc_scr)
        state_scr[...] = jnp.zeros_like(state_scr)

    bcdt = bcdt_ref[0]
    xs = _causal_conv_silu(xb_ref[0], tailx_scr, cwx_ref, cbx_ref)
    bcs = _causal_conv_silu(bcdt[:, :BC_WIDTH], tailbc_scr, cwbc_ref, cbbc_ref)

    dt_in = bcdt[:, BC_WIDTH:] + dtb_ref[...]
    dt = jnp.maximum(dt_in, 0.0) + jnp.log1p(jnp.exp(-jnp.abs(dt_in)))
    da = dt * (-jnp.exp(alog_ref[...]))

    r = lax.broadcasted_iota(jnp.int32, (CHUNK, CHUNK), 0)
    c = lax.broadcasted_iota(jnp.int32, (CHUNK, CHUNK), 1)
    lower = c <= r
    lower_b = jnp.where(lower, 1.0, 0.0).astype(_BF16)
    upper_b = jnp.where(r <= c, 1.0, 0.0).astype(_BF16)
    a_cum = sum(_dot(lower_b, part) for part in _split3(da))
    a_cum_t = sum(_dot(part, upper_b) for part in _split3(da.T))

    lane_hi = lax.broadcasted_iota(jnp.int32, (1, LANES), 1) >= SSM_HEAD_DIM
    dskip = dskip_ref[...]

    for grp in range(SSM_GROUPS):
        blk, half = divmod(grp, 2)
        b_blk = bcs[:, blk * LANES:(blk + 1) * LANES]
        c_blk = bcs[:, BC_WIDTH // 2 + blk * LANES: BC_WIDTH // 2 + (blk + 1) * LANES]
        in_grp = lane_hi if half else jnp.logical_not(lane_hi)
        c_g = jnp.where(in_grp, c_blk, 0.0).astype(_BF16)
        b_blk16 = b_blk.astype(_BF16)
        cb = lax.dot_general(c_g, b_blk16, _NT, preferred_element_type=_F32)

        for pr in range(2):
            pair = grp * 2 + pr
            h_a, h_b = 2 * pair, 2 * pair + 1

            def per_pair(arr):
                return jnp.where(lane_hi, arr[:, h_b:h_b + 1], arr[:, h_a:h_a + 1])

            dt_p = per_pair(dt)
            acum_p = per_pair(a_cum)
            last_p = per_pair(a_cum[CHUNK - 1:CHUNK])
            x_p = xs[:, pair * LANES:(pair + 1) * LANES]
            xd = x_p * dt_p

            y = None
            for hh, keep_hi in ((h_a, False), (h_b, True)):
                seg = a_cum[:, hh:hh + 1] - a_cum_t[hh:hh + 1, :]
                decay = jnp.where(lower, jnp.exp(jnp.where(lower, seg, 0.0)), 0.0)
                m_h = (cb * decay).astype(_BF16)
                sel = lane_hi if keep_hi else jnp.logical_not(lane_hi)
                part = _dot(m_h, jnp.where(sel, xd, 0.0).astype(_BF16))
                y = part if y is None else y + part

            st_prev = state_scr[pair]
            y_off = _dot(c_g, st_prev.astype(_BF16)) * jnp.exp(acum_p)
            w = (xd * jnp.exp(last_p - acum_p)).astype(_BF16)
            st_new = lax.dot_general(b_blk16, w, _TN, preferred_element_type=_F32)
            state_scr[pair] = jnp.exp(last_p) * st_prev + st_new
            y_scr[:, pair * LANES:(pair + 1) * LANES] = (
                y + y_off + dskip[:, pair * LANES:(pair + 1) * LANES] * x_p)

    gw = SSM_WIDTH // SSM_GROUPS
    for grp in range(SSM_GROUPS):
        sl = slice(grp * gw, (grp + 1) * gw)
        yz = y_scr[:, sl] * _silu(zb_ref[0, :, sl])
        ms = jnp.mean(yz * yz, axis=-1, keepdims=True)
        o_ref[0, :, sl] = (yz * lax.rsqrt(ms + EPS) * nw_ref[:, sl]).astype(o_ref.dtype)


def _ssd(xb, bcdt, zb, cwx, cbx, cwbc, cbbc, dtb, alog, dskip, nw):
    bsz, seq, _ = xb.shape
    tok = lambda width: pl.BlockSpec((1, CHUNK, width), lambda b, i: (b, i, 0))
    consts = (cwx, cbx, cwbc, cbbc, dtb, alog, dskip, nw)
    return pl.pallas_call(
        _ssd_kernel,
        grid=(bsz, seq // CHUNK),
        in_specs=[tok(SSM_WIDTH), tok(BCDT_WIDTH), tok(SSM_WIDTH)] + [_const_spec(a.shape) for a in consts],
        out_specs=tok(SSM_WIDTH),
        out_shape=jax.ShapeDtypeStruct((bsz, seq, SSM_WIDTH), _BF16),
        scratch_shapes=[pltpu.VMEM((SUBLANES, SSM_WIDTH), _F32), pltpu.VMEM((SUBLANES, BC_WIDTH), _F32),
                        pltpu.VMEM((SSM_HEADS // 2, 2 * SSM_STATE, LANES), _F32),
                        pltpu.VMEM((CHUNK, SSM_WIDTH), _F32)],
        compiler_params=pltpu.CompilerParams(
            dimension_semantics=("parallel", "arbitrary"), vmem_limit_bytes=VMEM_LIMIT_BYTES),
        name="ssd",
    )(xb, bcdt, zb, *consts)


def _merge_kernel(x_ref, oa_ref, ob_ref, g_ref, gb_ref, wa_ref, wb_ref, wo_ref, fw_ref, o_ref, *, final_norm):
    gates = jax.nn.sigmoid(g_ref[...] + gb_ref[...])
    merged = (gates[:, :D_MODEL] * _dot(oa_ref[...], wa_ref[...])
              + gates[:, D_MODEL:] * _dot(ob_ref[...], wb_ref[...]))
    y = x_ref[...] + _dot(merged.astype(_BF16), wo_ref[...])
    if final_norm:
        y = y * lax.rsqrt(jnp.mean(y * y, axis=-1, keepdims=True) + EPS) * fw_ref[...]
    o_ref[...] = y


def _merge(x2, oa2, ob2, g2, gate_bias, wa, wb, wo, fw, final_norm):
    rows = x2.shape[0]
    tm = MERGE_ROWS
    tok = lambda width: pl.BlockSpec((tm, width), lambda i: (i, 0))
    consts = (gate_bias, wa, wb, wo, fw)
    return pl.pallas_call(
        functools.partial(_merge_kernel, final_norm=final_norm),
        grid=(rows // tm,),
        in_specs=[tok(D_MODEL), tok(ATTN_WIDTH), tok(SSM_WIDTH), tok(N_BRANCH * D_MODEL)]
                 + [_const_spec(a.shape) for a in consts],
        out_specs=tok(D_MODEL),
        out_shape=jax.ShapeDtypeStruct((rows, D_MODEL), _F32),
        compiler_params=pltpu.CompilerParams(
            dimension_semantics=("parallel",), vmem_limit_bytes=VMEM_LIMIT_BYTES),
        name="merge",
    )(x2, oa2, ob2, g2, *consts)


def _rotate_cols(w, n_heads):
    w3 = w.reshape(w.shape[0], n_heads, HEAD_DIM)
    rot = jnp.concatenate([-w3[..., ROT_HALF:ROT_DIM], w3[..., :ROT_HALF],
                           jnp.zeros_like(w3[..., ROT_DIM:])], axis=-1)
    return rot.reshape(w.shape)


def _layer_weights(w_in):
    o = SPLIT_OFFSETS
    seg = lambda i: w_in[:, o[i]:o[i + 1]]
    w_q, w_k, w_v, w_za, w_qi, w_ki, w_wi, w_zb, w_xb, w_b, w_c, w_dt, w_g = (seg(i) for i in range(13))
    pad = lambda w, n: jnp.pad(w, ((0, 0), (0, n - w.shape[1])))
    w_t = jnp.concatenate([w_q, w_qi, w_v, pad(w_wi, W_IDX_ROWS)], axis=1).T
    w_kside = pad(jnp.concatenate([w_k, w_ki], axis=1), 2 * LANES)
    w_kside_rot = pad(jnp.concatenate([_rotate_cols(w_k, ATTN_KV_HEADS), _rotate_cols(w_ki, 1)], axis=1), 2 * LANES)
    w_bcdt = jnp.concatenate([w_b, w_c, pad(w_dt, LANES)], axis=1)
    return tuple(w.astype(_BF16) for w in (w_t, w_kside, w_kside_rot, w_zb, w_xb, w_bcdt, w_g, w_za))


def _rope_tables(positions):
    inv_freq = ROPE_THETA ** (-jnp.arange(0, ROT_DIM, 2, dtype=_F32) / ROT_DIM)
    ang = positions.astype(_F32)[..., None] * inv_freq
    cos, sin = jnp.cos(ang), jnp.sin(ang)
    ones = jnp.ones(ang.shape[:-1] + (HEAD_DIM - ROT_DIM,), _F32)
    cos_k = jnp.concatenate([cos, cos, ones], axis=-1)
    sin_k = jnp.concatenate([sin, sin, 0.0 * ones], axis=-1)
    n_rep = 2 * LANES // HEAD_DIM
    return (jnp.swapaxes(cos, 1, 2), jnp.swapaxes(sin, 1, 2),
            jnp.tile(cos_k, (1, 1, n_rep)), jnp.tile(sin_k, (1, 1, n_rep)))


def _pad_lanes(v, n):
    return jnp.pad(v, (0, n - v.shape[0]))[None, :]


def kernel(x, positions, norm_w, w_in, gate_bias, conv_w, conv_b, dt_bias, a_log, d_skip,
           ssm_norm_w, w_branch_a, w_branch_b, w_out, final_norm_w):
    bsz, seq, _ = x.shape
    depth = norm_w.shape[0]
    cos_t, sin_t, cos_k, sin_k = _rope_tables(positions)
    for i in range(depth):
        weights = _layer_weights(w_in[i])
        (q_t, qi_t, wi_t, v_t, k, kidx, zb, xb, bcdt, gates, za) = _projection(
            x, norm_w[i][None, :], cos_t, sin_t, cos_k, sin_k, weights)
        o_a = _attention(q_t, qi_t, wi_t, k, kidx, v_t, za)
        o_b = _ssd(xb, bcdt, zb,
                   conv_w[i][:, :SSM_WIDTH], conv_b[i][None, :SSM_WIDTH],
                   conv_w[i][:, SSM_WIDTH:], conv_b[i][None, SSM_WIDTH:],
                   _pad_lanes(dt_bias[i], LANES), _pad_lanes(a_log[i], LANES),
                   jnp.repeat(d_skip[i], SSM_HEAD_DIM)[None, :], ssm_norm_w[i][None, :])
        x = _merge(x.reshape(bsz * seq, D_MODEL), o_a.reshape(bsz * seq, ATTN_WIDTH),
                   o_b.reshape(bsz * seq, SSM_WIDTH), gates.reshape(bsz * seq, N_BRANCH * D_MODEL),
                   gate_bias[i][None, :], w_branch_a[i].astype(_BF16), w_branch_b[i].astype(_BF16),
                   w_out[i].astype(_BF16), final_norm_w[None, :],
                   final_norm=(i == depth - 1)).reshape(bsz, seq, D_MODEL)
    return x
```

```python
import functools

import numpy as np
import jax
import jax.numpy as jnp
from jax import lax
from jax.experimental import pallas as pl
from jax.experimental.pallas import tpu as pltpu

D_MODEL = 1024
ATTN_HEADS = 8
ATTN_KV_HEADS = 2
HEAD_DIM = 64
ATTN_GROUP = ATTN_HEADS // ATTN_KV_HEADS
ATTN_WIDTH = ATTN_HEADS * HEAD_DIM
KV_WIDTH = ATTN_KV_HEADS * HEAD_DIM
ROT_DIM = HEAD_DIM // 4
ROT_HALF = ROT_DIM // 2
ROPE_THETA = 500000.0
IDX_HEADS = 4
IDX_DIM = 64
TOPK_MAX = 256
SSM_HEADS = 16
SSM_HEAD_DIM = 64
SSM_WIDTH = SSM_HEADS * SSM_HEAD_DIM
SSM_GROUPS = 4
SSM_STATE = 64
CONV_K = 4
CHUNK = 128
BC_WIDTH = 2 * SSM_GROUPS * SSM_STATE
N_BRANCH = 2
EPS = 1e-6
SPLIT_SIZES = (ATTN_WIDTH, KV_WIDTH, KV_WIDTH, ATTN_WIDTH,
               IDX_HEADS * IDX_DIM, IDX_DIM, IDX_HEADS,
               SSM_WIDTH, SSM_WIDTH, SSM_GROUPS * SSM_STATE, SSM_GROUPS * SSM_STATE, SSM_HEADS,
               N_BRANCH * D_MODEL)
SPLIT_OFFSETS = tuple(int(o) for o in np.cumsum((0,) + SPLIT_SIZES))

LANES = 128
SUBLANES = 8
VMEM_LIMIT_BYTES = 56 * 1024 * 1024

PROJ_ROWS = 256
Q_TILE = 256
KEY_TILE = 256
COUNT_ROWS = 64
LOG2_E = 1.4426950408889634
MERGE_ROWS = 512
W_IDX_ROWS = 8
BCDT_WIDTH = BC_WIDTH + LANES
NEG_BIG = -1e30

_F32 = jnp.float32
_BF16 = jnp.bfloat16
_NT = (((1,), (1,)), ((), ()))
_TN = (((0,), (0,)), ((), ()))


def _dot(a, b):
    return jnp.dot(a, b, preferred_element_type=_F32)


def _silu(x):
    return x * jax.nn.sigmoid(x)


def _proj_kernel(x_ref, nw_ref, cos_t_ref, sin_t_ref, cos_k_ref, sin_k_ref,
                 w_t_ref, w_k_ref, w_kp_ref, w_zb_ref, w_xb_ref, w_bcdt_ref, w_g_ref, w_za_ref,
                 q_t_ref, qi_t_ref, wi_t_ref, v_t_ref, k_ref, kidx_ref,
                 zb_ref, xb_ref, bcdt_ref, g_ref, za_ref):
    x = x_ref[0]
    h = x * lax.rsqrt(jnp.mean(x * x, axis=-1, keepdims=True) + EPS) * nw_ref[...]
    h = h.astype(_BF16)

    t = lax.dot_general(w_t_ref[...], h, _NT, preferred_element_type=_F32)
    cos_t = cos_t_ref[0]
    sin_t = sin_t_ref[0]

    def rope_t(block, n_heads, scale, out_ref):
        out_ref[0] = (block * scale).astype(out_ref.dtype)
        for hd in range(n_heads):
            x1 = block[hd * HEAD_DIM: hd * HEAD_DIM + ROT_HALF]
            x2 = block[hd * HEAD_DIM + ROT_HALF: hd * HEAD_DIM + ROT_DIM]
            rot = jnp.concatenate([x1 * cos_t - x2 * sin_t, x2 * cos_t + x1 * sin_t], axis=0)
            out_ref[0, hd * HEAD_DIM: hd * HEAD_DIM + ROT_DIM, :] = (rot * scale).astype(out_ref.dtype)

    o_q, o_qi, o_v, o_w = 0, ATTN_WIDTH, ATTN_WIDTH + IDX_HEADS * IDX_DIM, ATTN_WIDTH + IDX_HEADS * IDX_DIM + KV_WIDTH
    rope_t(t[o_q:o_qi], ATTN_HEADS, LOG2_E * HEAD_DIM ** -0.5, q_t_ref)
    rope_t(t[o_qi:o_v], IDX_HEADS, IDX_DIM ** -0.5, qi_t_ref)
    v_t_ref[0] = t[o_v:o_w].astype(_BF16)
    wi_t_ref[0] = t[o_w:o_w + W_IDX_ROWS] * (IDX_HEADS ** -0.5)

    ks = _dot(h, w_k_ref[...])
    ksp = _dot(h, w_kp_ref[...])
    kr = ks * cos_k_ref[0] + ksp * sin_k_ref[0]
    k_ref[0] = kr[:, :KV_WIDTH].astype(_BF16)
    kidx_ref[0] = kr[:, KV_WIDTH:KV_WIDTH + IDX_DIM].astype(_BF16)

    zb_ref[0] = _dot(h, w_zb_ref[...])
    xb_ref[0] = _dot(h, w_xb_ref[...])
    bcdt_ref[0] = _dot(h, w_bcdt_ref[...])
    g_ref[0] = _dot(h, w_g_ref[...])
    za_ref[0] = _dot(h, w_za_ref[...])


def _const_spec(shape):
    nd = len(shape)
    return pl.BlockSpec(shape, lambda *_: (0,) * nd, pipeline_mode=pl.Buffered(1))


def _projection(x, nw, cos_t, sin_t, cos_k, sin_k, weights):
    bsz, seq, _ = x.shape
    tm = PROJ_ROWS
    tok = lambda width: pl.BlockSpec((1, tm, width), lambda b, i: (b, i, 0))
    feat = lambda rows: pl.BlockSpec((1, rows, tm), lambda b, i: (b, 0, i))
    in_specs = [tok(D_MODEL), _const_spec((1, D_MODEL)), feat(ROT_HALF), feat(ROT_HALF),
                tok(2 * LANES), tok(2 * LANES)] + [_const_spec(w.shape) for w in weights]
    out_shape = (
        jax.ShapeDtypeStruct((bsz, ATTN_WIDTH, seq), _BF16),
        jax.ShapeDtypeStruct((bsz, IDX_HEADS * IDX_DIM, seq), _BF16),
        jax.ShapeDtypeStruct((bsz, W_IDX_ROWS, seq), _F32),
        jax.ShapeDtypeStruct((bsz, KV_WIDTH, seq), _BF16),
        jax.ShapeDtypeStruct((bsz, seq, KV_WIDTH), _BF16),
        jax.ShapeDtypeStruct((bsz, seq, IDX_DIM), _BF16),
        jax.ShapeDtypeStruct((bsz, seq, SSM_WIDTH), _F32),
        jax.ShapeDtypeStruct((bsz, seq, SSM_WIDTH), _F32),
        jax.ShapeDtypeStruct((bsz, seq, BCDT_WIDTH), _F32),
        jax.ShapeDtypeStruct((bsz, seq, N_BRANCH * D_MODEL), _F32),
        jax.ShapeDtypeStruct((bsz, seq, ATTN_WIDTH), _F32),
    )
    out_specs = (feat(ATTN_WIDTH), feat(IDX_HEADS * IDX_DIM), feat(W_IDX_ROWS), feat(KV_WIDTH),
                 tok(KV_WIDTH), tok(IDX_DIM), tok(SSM_WIDTH), tok(SSM_WIDTH), tok(BCDT_WIDTH),
                 tok(N_BRANCH * D_MODEL), tok(ATTN_WIDTH))
    return pl.pallas_call(
        _proj_kernel,
        grid=(bsz, seq // tm),
        in_specs=in_specs,
        out_specs=out_specs,
        out_shape=out_shape,
        compiler_params=pltpu.CompilerParams(
            dimension_semantics=("parallel", "parallel"), vmem_limit_bytes=VMEM_LIMIT_BYTES),
        name="proj",
    )(x, nw, cos_t, sin_t, cos_k, sin_k, *weights)


def _attn_kernel(q_t_ref, qi_t_ref, wi_t_ref, k_ref, kidx_ref, v_t_ref, za_ref, o_ref,
                 score_scr, bias_scr, qpad_scr, m_scr, l_scr, acc_scr, *, top_k):
    j = pl.program_id(1)
    n_chunks = j + 1
    tq = Q_TILE
    q_pos = j * tq + lax.broadcasted_iota(jnp.int32, (1, tq), 1)
    key_iota = lax.broadcasted_iota(jnp.int32, (KEY_TILE, tq), 0)

    qi_t = qi_t_ref[0]
    qi_cat = jnp.concatenate([qi_t[hd * IDX_DIM:(hd + 1) * IDX_DIM] for hd in range(IDX_HEADS)], axis=1)
    wi_t = wi_t_ref[0]
    wi_cat = jnp.concatenate([wi_t[hd:hd + 1] for hd in range(IDX_HEADS)], axis=1)

    def score_chunk(c, carry):
        start = pl.multiple_of(c * KEY_TILE, KEY_TILE)
        logits = _dot(kidx_ref[0, pl.ds(start, KEY_TILE), :], qi_cat)
        weighted = jnp.maximum(logits, 0.0) * wi_cat
        score = weighted[:, 0:tq]
        for hd in range(1, IDX_HEADS):
            score = score + weighted[:, hd * tq:(hd + 1) * tq]
        causal = (start + key_iota) <= q_pos
        score_scr[pl.ds(start, KEY_TILE), :] = jnp.where(causal, score, -jnp.inf)
        return carry

    lax.fori_loop(0, n_chunks, score_chunk, 0)

    def count_ge(cand, strict):
        def body(c, acc):
            start = pl.multiple_of(c * KEY_TILE, KEY_TILE)
            s = score_scr[pl.ds(start, KEY_TILE), :]
            hit = (s > cand) if strict else (s >= cand)
            inc = jnp.where(hit, 1.0, 0.0)
            return acc + inc.reshape(KEY_TILE // COUNT_ROWS, COUNT_ROWS, tq).sum(axis=0)
        acc = lax.fori_loop(0, n_chunks, body, jnp.zeros((COUNT_ROWS, tq), _F32))
        return jnp.sum(acc, axis=0, keepdims=True)

    int_min = jnp.int32(-2 ** 31)

    def ordered_to_float(kbits):
        return pltpu.bitcast(kbits ^ ((kbits >> 31) & jnp.int32(0x7FFFFFFF)), _F32)

    def bit_body(i, t_bits):
        bit = jnp.where(i == 0, int_min, jnp.int32(1) << (31 - i))
        cand_bits = t_bits ^ bit
        cnt = count_ge(ordered_to_float(cand_bits), strict=False)
        return jnp.where(cnt >= top_k, cand_bits, t_bits)

    t_bits = lax.fori_loop(0, 32, bit_body, jnp.full((1, tq), int_min, jnp.int32))
    few = (q_pos + 1) <= top_k
    thr = jnp.where(few, -jnp.inf, ordered_to_float(t_bits))
    n_above = count_ge(thr, strict=True)
    n_ties_kept = jnp.where(few, 0.0, top_k - n_above)

    row = lax.broadcasted_iota(jnp.int32, (KEY_TILE, KEY_TILE), 0)
    col = lax.broadcasted_iota(jnp.int32, (KEY_TILE, KEY_TILE), 1)
    strict_lower = jnp.where(col < row, 1.0, 0.0).astype(_BF16)

    def bias_chunk(c, ties_before):
        start = pl.multiple_of(c * KEY_TILE, KEY_TILE)
        s = score_scr[pl.ds(start, KEY_TILE), :]
        tie = jnp.where(s == thr, 1.0, 0.0)
        rank = _dot(strict_lower, tie.astype(_BF16)) + ties_before
        keep = (s > thr) | ((s == thr) & (rank < n_ties_kept))
        bias_scr[pl.ds(start, KEY_TILE), :] = jnp.where(keep, 0.0, NEG_BIG)
        return ties_before + jnp.sum(tie, axis=0, keepdims=True)

    lax.fori_loop(0, n_chunks, bias_chunk, jnp.zeros((1, tq), _F32))

    n_pairs = ATTN_HEADS // 2
    zeros_half = jnp.zeros((HEAD_DIM, 2 * tq), _BF16)
    for pair in range(n_pairs):
        g = (2 * pair) // ATTN_GROUP
        q_cat = jnp.concatenate([q_t_ref[0, (2 * pair + hd) * HEAD_DIM:(2 * pair + hd + 1) * HEAD_DIM, :]
                                 for hd in range(2)], axis=1)
        qpad_scr[pair] = jnp.concatenate([q_cat, zeros_half] if g == 0 else [zeros_half, q_cat], axis=0)
    m_scr[...] = jnp.full(m_scr.shape, NEG_BIG, _F32)
    l_scr[...] = jnp.zeros(l_scr.shape, _F32)
    acc_scr[...] = jnp.zeros(acc_scr.shape, _F32)

    def attn_tile(c, carry):
        start = pl.multiple_of(c * KEY_TILE, KEY_TILE)
        k_t = k_ref[0, pl.ds(start, KEY_TILE), :]
        b = bias_scr[pl.ds(start, KEY_TILE), :]
        b2 = jnp.concatenate([b, b], axis=1)
        for pair in range(n_pairs):
            g = (2 * pair) // ATTN_GROUP
            s = _dot(k_t, qpad_scr[pair]) + b2
            m_old = m_scr[pair]
            m_new = jnp.maximum(m_old, jnp.max(s, axis=0, keepdims=True))
            alpha = jnp.exp2(m_old - m_new)
            p = jnp.exp2(s - m_new)
            l_scr[pair] = alpha * l_scr[pair] + jnp.sum(p, axis=0, keepdims=True)
            v_c = v_t_ref[0, g * HEAD_DIM:(g + 1) * HEAD_DIM, pl.ds(start, KEY_TILE)]
            acc_scr[pair] = alpha * acc_scr[pair] + _dot(v_c, p.astype(_BF16))
            m_scr[pair] = m_new
        return carry

    lax.fori_loop(0, n_chunks, attn_tile, 0)

    for pair in range(n_pairs):
        o_t = acc_scr[pair] / l_scr[pair]
        both = jnp.concatenate([o_t[:, :tq], o_t[:, tq:]], axis=0)
        z = za_ref[0, :, pair * LANES:(pair + 1) * LANES]
        o_ref[0, :, pair * LANES:(pair + 1) * LANES] = (both.T * _silu(z)).astype(o_ref.dtype)


def _attention(q_t, qi_t, wi_t, k, kidx, v_t, za):
    bsz, seq, _ = k.shape
    tq = Q_TILE
    top_k = min(TOPK_MAX, seq // 4)
    feat = lambda rows: pl.BlockSpec((1, rows, tq), lambda b, j: (b, 0, j))
    full = lambda s1, s2: pl.BlockSpec((1, s1, s2), lambda b, j: (b, 0, 0))
    return pl.pallas_call(
        functools.partial(_attn_kernel, top_k=top_k),
        grid=(bsz, seq // tq),
        in_specs=[feat(ATTN_WIDTH), feat(IDX_HEADS * IDX_DIM), feat(W_IDX_ROWS),
                  full(seq, KV_WIDTH), full(seq, IDX_DIM), full(KV_WIDTH, seq),
                  pl.BlockSpec((1, tq, ATTN_WIDTH), lambda b, j: (b, j, 0))],
        out_specs=pl.BlockSpec((1, tq, ATTN_WIDTH), lambda b, j: (b, j, 0)),
        out_shape=jax.ShapeDtypeStruct((bsz, seq, ATTN_WIDTH), _BF16),
        scratch_shapes=[pltpu.VMEM((seq, tq), _F32), pltpu.VMEM((seq, tq), _F32),
                        pltpu.VMEM((ATTN_HEADS // 2, 2 * HEAD_DIM, 2 * tq), _BF16),
                        pltpu.VMEM((ATTN_HEADS // 2, 1, 2 * tq), _F32),
                        pltpu.VMEM((ATTN_HEADS // 2, 1, 2 * tq), _F32),
                        pltpu.VMEM((ATTN_HEADS // 2, HEAD_DIM, 2 * tq), _F32)],
        compiler_params=pltpu.CompilerParams(
            dimension_semantics=("parallel", "arbitrary"), vmem_limit_bytes=VMEM_LIMIT_BYTES),
        name="attn",
    )(q_t, qi_t, wi_t, k, kidx, v_t, za)


def _split3(x):
    hi = x.astype(_BF16)
    r1 = x - hi.astype(_F32)
    mid = r1.astype(_BF16)
    lo = (r1 - mid.astype(_F32)).astype(_BF16)
    return hi, mid, lo


def _causal_conv_silu(x, tail_ref, w_ref, b_ref):
    rows = x.shape[0]
    tail = tail_ref[...]
    sub = lax.broadcasted_iota(jnp.int32, (SUBLANES, x.shape[1]), 0)
    w = w_ref[...]
    y = x * w[CONV_K - 1:CONV_K] + b_ref[...]
    for shift in range(1, CONV_K):
        rolled = pltpu.roll(x, shift, 0)
        head = jnp.where(sub < shift, pltpu.roll(tail, shift, 0), rolled[:SUBLANES])
        shifted = jnp.concatenate([head, rolled[SUBLANES:]], axis=0)
        y = y + shifted * w[CONV_K - 1 - shift:CONV_K - shift]
    tail_ref[...] = x[rows - SUBLANES:]
    return _silu(y)


def _ssd_kernel(xb_ref, bcdt_ref, zb_ref, cwx_ref, cbx_ref, cwbc_ref, cbbc_ref,
                dtb_ref, alog_ref, dskip_ref, nw_ref, o_ref,
                tailx_scr, tailbc_scr, state_scr, y_scr):
    @pl.when(pl.program_id(1) == 0)
    def _():
        tailx_scr[...] = jnp.zeros_like(tailx_scr)
        tailbc_scr[...] = jnp.zeros_like(tailbc_scr)
        state_scr[...] = jnp.zeros_like(state_scr)

    bcdt = bcdt_ref[0]
    xs = _causal_conv_silu(xb_ref[0], tailx_scr, cwx_ref, cbx_ref)
    bcs = _causal_conv_silu(bcdt[:, :BC_WIDTH], tailbc_scr, cwbc_ref, cbbc_ref)

    dt_in = bcdt[:, BC_WIDTH:] + dtb_ref[...]
    dt = jnp.maximum(dt_in, 0.0) + jnp.log1p(jnp.exp(-jnp.abs(dt_in)))
    da = dt * (-jnp.exp(alog_ref[...]))

    r = lax.broadcasted_iota(jnp.int32, (CHUNK, CHUNK), 0)
    c = lax.broadcasted_iota(jnp.int32, (CHUNK, CHUNK), 1)
    lower = c <= r
    lower_b = jnp.where(lower, 1.0, 0.0).astype(_BF16)
    upper_b = jnp.where(r <= c, 1.0, 0.0).astype(_BF16)
    a_cum = sum(_dot(lower_b, part) for part in _split3(da))
    a_cum_t = sum(_dot(part, upper_b) for part in _split3(da.T))

    lane_hi = lax.broadcasted_iota(jnp.int32, (1, LANES), 1) >= SSM_HEAD_DIM
    dskip = dskip_ref[...]

    for grp in range(SSM_GROUPS):
        blk, half = divmod(grp, 2)
        b_blk = bcs[:, blk * LANES:(blk + 1) * LANES]
        c_blk = bcs[:, BC_WIDTH // 2 + blk * LANES: BC_WIDTH // 2 + (blk + 1) * LANES]
        in_grp = lane_hi if half else jnp.logical_not(lane_hi)
        c_g = jnp.where(in_grp, c_blk, 0.0).astype(_BF16)
        b_blk16 = b_blk.astype(_BF16)
        cb = lax.dot_general(c_g, b_blk16, _NT, preferred_element_type=_F32)

        for pr in range(2):
            pair = grp * 2 + pr
            h_a, h_b = 2 * pair, 2 * pair + 1

            def per_pair(arr):
                return jnp.where(lane_hi, arr[:, h_b:h_b + 1], arr[:, h_a:h_a + 1])

            dt_p = per_pair(dt)
            acum_p = per_pair(a_cum)
            last_p = per_pair(a_cum[CHUNK - 1:CHUNK])
            x_p = xs[:, pair * LANES:(pair + 1) * LANES]
            xd = x_p * dt_p

            y = None
            for hh, keep_hi in ((h_a, False), (h_b, True)):
                seg = a_cum[:, hh:hh + 1] - a_cum_t[hh:hh + 1, :]
                decay = jnp.where(lower, jnp.exp(jnp.where(lower, seg, 0.0)), 0.0)
                m_h = (cb * decay).astype(_BF16)
                sel = lane_hi if keep_hi else jnp.logical_not(lane_hi)
                part = _dot(m_h, jnp.where(sel, xd, 0.0).astype(_BF16))
                y = part if y is None else y + part

            st_prev = state_scr[pair]
            y_off = _dot(c_g, st_prev.astype(_BF16)) * jnp.exp(acum_p)
            w = (xd * jnp.exp(last_p - acum_p)).astype(_BF16)
            st_new = lax.dot_general(b_blk16, w, _TN, preferred_element_type=_F32)
            state_scr[pair] = jnp.exp(last_p) * st_prev + st_new
            y_scr[:, pair * LANES:(pair + 1) * LANES] = (
                y + y_off + dskip[:, pair * LANES:(pair + 1) * LANES] * x_p)

    gw = SSM_WIDTH // SSM_GROUPS
    for grp in range(SSM_GROUPS):
        sl = slice(grp * gw, (grp + 1) * gw)
        yz = y_scr[:, sl] * _silu(zb_ref[0, :, sl])
        ms = jnp.mean(yz * yz, axis=-1, keepdims=True)
        o_ref[0, :, sl] = (yz * lax.rsqrt(ms + EPS) * nw_ref[:, sl]).astype(o_ref.dtype)


def _ssd(xb, bcdt, zb, cwx, cbx, cwbc, cbbc, dtb, alog, dskip, nw):
    bsz, seq, _ = xb.shape
    tok = lambda width: pl.BlockSpec((1, CHUNK, width), lambda b, i: (b, i, 0))
    consts = (cwx, cbx, cwbc, cbbc, dtb, alog, dskip, nw)
    return pl.pallas_call(
        _ssd_kernel,
        grid=(bsz, seq // CHUNK),
        in_specs=[tok(SSM_WIDTH), tok(BCDT_WIDTH), tok(SSM_WIDTH)] + [_const_spec(a.shape) for a in consts],
        out_specs=tok(SSM_WIDTH),
        out_shape=jax.ShapeDtypeStruct((bsz, seq, SSM_WIDTH), _BF16),
        scratch_shapes=[pltpu.VMEM((SUBLANES, SSM_WIDTH), _F32), pltpu.VMEM((SUBLANES, BC_WIDTH), _F32),
                        pltpu.VMEM((SSM_HEADS // 2, 2 * SSM_STATE, LANES), _F32),
                        pltpu.VMEM((CHUNK, SSM_WIDTH), _F32)],
        compiler_params=pltpu.CompilerParams(
            dimension_semantics=("parallel", "arbitrary"), vmem_limit_bytes=VMEM_LIMIT_BYTES),
        name="ssd",
    )(xb, bcdt, zb, *consts)


def _merge_kernel(x_ref, oa_ref, ob_ref, g_ref, gb_ref, wa_ref, wb_ref, wo_ref, fw_ref, o_ref, *, final_norm):
    gates = jax.nn.sigmoid(g_ref[...] + gb_ref[...])
    merged = (gates[:, :D_MODEL] * _dot(oa_ref[...], wa_ref[...])
              + gates[:, D_MODEL:] * _dot(ob_ref[...], wb_ref[...]))
    y = x_ref[...] + _dot(merged.astype(_BF16), wo_ref[...])
    if final_norm:
        y = y * lax.rsqrt(jnp.mean(y * y, axis=-1, keepdims=True) + EPS) * fw_ref[...]
    o_ref[...] = y


def _merge(x2, oa2, ob2, g2, gate_bias, wa, wb, wo, fw, final_norm):
    rows = x2.shape[0]
    tm = MERGE_ROWS
    tok = lambda width: pl.BlockSpec((tm, width), lambda i: (i, 0))
    consts = (gate_bias, wa, wb, wo, fw)
    return pl.pallas_call(
        functools.partial(_merge_kernel, final_norm=final_norm),
        grid=(rows // tm,),
        in_specs=[tok(D_MODEL), tok(ATTN_WIDTH), tok(SSM_WIDTH), tok(N_BRANCH * D_MODEL)]
                 + [_const_spec(a.shape) for a in consts],
        out_specs=tok(D_MODEL),
        out_shape=jax.ShapeDtypeStruct((rows, D_MODEL), _F32),
        compiler_params=pltpu.CompilerParams(
            dimension_semantics=("parallel",), vmem_limit_bytes=VMEM_LIMIT_BYTES),
        name="merge",
    )(x2, oa2, ob2, g2, *consts)


def _rotate_cols(w, n_heads):
    w3 = w.reshape(w.shape[0], n_heads, HEAD_DIM)
    rot = jnp.concatenate([-w3[..., ROT_HALF:ROT_DIM], w3[..., :ROT_HALF],
                           jnp.zeros_like(w3[..., ROT_DIM:])], axis=-1)
    return rot.reshape(w.shape)


def _layer_weights(w_in):
    o = SPLIT_OFFSETS
    seg = lambda i: w_in[:, o[i]:o[i + 1]]
    w_q, w_k, w_v, w_za, w_qi, w_ki, w_wi, w_zb, w_xb, w_b, w_c, w_dt, w_g = (seg(i) for i in range(13))
    pad = lambda w, n: jnp.pad(w, ((0, 0), (0, n - w.shape[1])))
    w_t = jnp.concatenate([w_q, w_qi, w_v, pad(w_wi, W_IDX_ROWS)], axis=1).T
    w_kside = pad(jnp.concatenate([w_k, w_ki], axis=1), 2 * LANES)
    w_kside_rot = pad(jnp.concatenate([_rotate_cols(w_k, ATTN_KV_HEADS), _rotate_cols(w_ki, 1)], axis=1), 2 * LANES)
    w_bcdt = jnp.concatenate([w_b, w_c, pad(w_dt, LANES)], axis=1)
    return tuple(w.astype(_BF16) for w in (w_t, w_kside, w_kside_rot, w_zb, w_xb, w_bcdt, w_g, w_za))


def _rope_tables(positions):
    inv_freq = ROPE_THETA ** (-jnp.arange(0, ROT_DIM, 2, dtype=_F32) / ROT_DIM)
    ang = positions.astype(_F32)[..., None] * inv_freq
    cos, sin = jnp.cos(ang), jnp.sin(ang)
    ones = jnp.ones(ang.shape[:-1] + (HEAD_DIM - ROT_DIM,), _F32)
    cos_k = jnp.concatenate([cos, cos, ones], axis=-1)
    sin_k = jnp.concatenate([sin, sin, 0.0 * ones], axis=-1)
    n_rep = 2 * LANES // HEAD_DIM
    return (jnp.swapaxes(cos, 1, 2), jnp.swapaxes(sin, 1, 2),
            jnp.tile(cos_k, (1, 1, n_rep)), jnp.tile(sin_k, (1, 1, n_rep)))


def _pad_lanes(v, n):
    return jnp.pad(v, (0, n - v.shape[0]))[None, :]


def kernel(x, positions, norm_w, w_in, gate_bias, conv_w, conv_b, dt_bias, a_log, d_skip,
           ssm_norm_w, w_branch_a, w_branch_b, w_out, final_norm_w):
    bsz, seq, _ = x.shape
    depth = norm_w.shape[0]
    cos_t, sin_t, cos_k, sin_k = _rope_tables(positions)
    for i in range(depth):
        weights = _layer_weights(w_in[i])
        (q_t, qi_t, wi_t, v_t, k, kidx, zb, xb, bcdt, gates, za) = _projection(
            x, norm_w[i][None, :], cos_t, sin_t, cos_k, sin_k, weights)
        o_a = _attention(q_t, qi_t, wi_t, k, kidx, v_t, za)
        o_b = _ssd(xb, bcdt, zb,
                   conv_w[i][:, :SSM_WIDTH], conv_b[i][None, :SSM_WIDTH],
                   conv_w[i][:, SSM_WIDTH:], conv_b[i][None, SSM_WIDTH:],
                   _pad_lanes(dt_bias[i], LANES), _pad_lanes(a_log[i], LANES),
                   jnp.repeat(d_skip[i], SSM_HEAD_DIM)[None, :], ssm_norm_w[i][None, :])
        x = _merge(x.reshape(bsz * seq, D_MODEL), o_a.reshape(bsz * seq, ATTN_WIDTH),
                   o_b.reshape(bsz * seq, SSM_WIDTH), gates.reshape(bsz * seq, N_BRANCH * D_MODEL),
                   gate_bias[i][None, :], w_branch_a[i].astype(_BF16), w_branch_b[i].astype(_BF16),
                   w_out[i].astype(_BF16), final_norm_w[None, :],
                   final_norm=(i == depth - 1)).reshape(bsz, seq, D_MODEL)
    return x
```

```python
import functools

import numpy as np
import jax
import jax.numpy as jnp
from jax import lax
from jax.experimental import pallas as pl
from jax.experimental.pallas import tpu as pltpu

D_MODEL = 1024
ATTN_HEADS = 8
ATTN_KV_HEADS = 2
HEAD_DIM = 64
ATTN_GROUP = ATTN_HEADS // ATTN_KV_HEADS
ATTN_WIDTH = ATTN_HEADS * HEAD_DIM
KV_WIDTH = ATTN_KV_HEADS * HEAD_DIM
ROT_DIM = HEAD_DIM // 4
ROT_HALF = ROT_DIM // 2
ROPE_THETA = 500000.0
IDX_HEADS = 4
IDX_DIM = 64
TOPK_MAX = 256
SSM_HEADS = 16
SSM_HEAD_DIM = 64
SSM_WIDTH = SSM_HEADS * SSM_HEAD_DIM
SSM_GROUPS = 4
SSM_STATE = 64
CONV_K = 4
CHUNK = 128
BC_WIDTH = 2 * SSM_GROUPS * SSM_STATE
N_BRANCH = 2
EPS = 1e-6
SPLIT_SIZES = (ATTN_WIDTH, KV_WIDTH, KV_WIDTH, ATTN_WIDTH,
               IDX_HEADS * IDX_DIM, IDX_DIM, IDX_HEADS,
               SSM_WIDTH, SSM_WIDTH, SSM_GROUPS * SSM_STATE, SSM_GROUPS * SSM_STATE, SSM_HEADS,
               N_BRANCH * D_MODEL)
SPLIT_OFFSETS = tuple(int(o) for o in np.cumsum((0,) + SPLIT_SIZES))

LANES = 128
SUBLANES = 8
VMEM_LIMIT_BYTES = 56 * 1024 * 1024

PROJ_ROWS = 256
Q_TILE = 256
KEY_TILE = 256
COUNT_ROWS = 64
LOG2_E = 1.4426950408889634
MERGE_ROWS = 512
W_IDX_ROWS = 8
BCDT_WIDTH = BC_WIDTH + LANES
NEG_BIG = -1e30

_F32 = jnp.float32
_BF16 = jnp.bfloat16
_NT = (((1,), (1,)), ((), ()))
_TN = (((0,), (0,)), ((), ()))


def _dot(a, b):
    return jnp.dot(a, b, preferred_element_type=_F32)


def _silu(x):
    return x * jax.nn.sigmoid(x)


def _proj_kernel(x_ref, nw_ref, cos_t_ref, sin_t_ref, cos_k_ref, sin_k_ref,
                 w_t_ref, w_k_ref, w_kp_ref, w_zb_ref, w_xb_ref, w_bcdt_ref, w_g_ref, w_za_ref,
                 q_t_ref, qi_t_ref, wi_t_ref, v_t_ref, k_ref, kidx_ref,
                 zb_ref, xb_ref, bcdt_ref, g_ref, za_ref):
    x = x_ref[0]
    h = x * lax.rsqrt(jnp.mean(x * x, axis=-1, keepdims=True) + EPS) * nw_ref[...]
    h = h.astype(_BF16)

    t = lax.dot_general(w_t_ref[...], h, _NT, preferred_element_type=_F32)
    cos_t = cos_t_ref[0]
    sin_t = sin_t_ref[0]

    def rope_t(block, n_heads, scale, out_ref):
        out_ref[0] = (block * scale).astype(out_ref.dtype)
        for hd in range(n_heads):
            x1 = block[hd * HEAD_DIM: hd * HEAD_DIM + ROT_HALF]
            x2 = block[hd * HEAD_DIM + ROT_HALF: hd * HEAD_DIM + ROT_DIM]
            rot = jnp.concatenate([x1 * cos_t - x2 * sin_t, x2 * cos_t + x1 * sin_t], axis=0)
            out_ref[0, hd * HEAD_DIM: hd * HEAD_DIM + ROT_DIM, :] = (rot * scale).astype(out_ref.dtype)

    o_q, o_qi, o_v, o_w = 0, ATTN_WIDTH, ATTN_WIDTH + IDX_HEADS * IDX_DIM, ATTN_WIDTH + IDX_HEADS * IDX_DIM + KV_WIDTH
    rope_t(t[o_q:o_qi], ATTN_HEADS, LOG2_E * HEAD_DIM ** -0.5, q_t_ref)
    rope_t(t[o_qi:o_v], IDX_HEADS, IDX_DIM ** -0.5, qi_t_ref)
    v_t_ref[0] = t[o_v:o_w].astype(_BF16)
    wi_t_ref[0] = t[o_w:o_w + W_IDX_ROWS] * (IDX_HEADS ** -0.5)

    ks = _dot(h, w_k_ref[...])
    ksp = _dot(h, w_kp_ref[...])
    kr = ks * cos_k_ref[0] + ksp * sin_k_ref[0]
    k_ref[0] = kr[:, :KV_WIDTH].astype(_BF16)
    kidx_ref[0] = kr[:, KV_WIDTH:KV_WIDTH + IDX_DIM].astype(_BF16)

    zb_ref[0] = _dot(h, w_zb_ref[...])
    xb_ref[0] = _dot(h, w_xb_ref[...])
    bcdt_ref[0] = _dot(h, w_bcdt_ref[...])
    g_ref[0] = _dot(h, w_g_ref[...])
    za_ref[0] = _dot(h, w_za_ref[...])


def _const_spec(shape):
    nd = len(shape)
    return pl.BlockSpec(shape, lambda *_: (0,) * nd, pipeline_mode=pl.Buffered(1))


def _projection(x, nw, cos_t, sin_t, cos_k, sin_k, weights):
    bsz, seq, _ = x.shape
    tm = PROJ_ROWS
    tok = lambda width: pl.BlockSpec((1, tm, width), lambda b, i: (b, i, 0))
    feat = lambda rows: pl.BlockSpec((1, rows, tm), lambda b, i: (b, 0, i))
    in_specs = [tok(D_MODEL), _const_spec((1, D_MODEL)), feat(ROT_HALF), feat(ROT_HALF),
                tok(2 * LANES), tok(2 * LANES)] + [_const_spec(w.shape) for w in weights]
    out_shape = (
        jax.ShapeDtypeStruct((bsz, ATTN_WIDTH, seq), _BF16),
        jax.ShapeDtypeStruct((bsz, IDX_HEADS * IDX_DIM, seq), _BF16),
        jax.ShapeDtypeStruct((bsz, W_IDX_ROWS, seq), _F32),
        jax.ShapeDtypeStruct((bsz, KV_WIDTH, seq), _BF16),
        jax.ShapeDtypeStruct((bsz, seq, KV_WIDTH), _BF16),
        jax.ShapeDtypeStruct((bsz, seq, IDX_DIM), _BF16),
        jax.ShapeDtypeStruct((bsz, seq, SSM_WIDTH), _F32),
        jax.ShapeDtypeStruct((bsz, seq, SSM_WIDTH), _F32),
        jax.ShapeDtypeStruct((bsz, seq, BCDT_WIDTH), _F32),
        jax.ShapeDtypeStruct((bsz, seq, N_BRANCH * D_MODEL), _F32),
        jax.ShapeDtypeStruct((bsz, seq, ATTN_WIDTH), _F32),
    )
    out_specs = (feat(ATTN_WIDTH), feat(IDX_HEADS * IDX_DIM), feat(W_IDX_ROWS), feat(KV_WIDTH),
                 tok(KV_WIDTH), tok(IDX_DIM), tok(SSM_WIDTH), tok(SSM_WIDTH), tok(BCDT_WIDTH),
                 tok(N_BRANCH * D_MODEL), tok(ATTN_WIDTH))
    return pl.pallas_call(
        _proj_kernel,
        grid=(bsz, seq // tm),
        in_specs=in_specs,
        out_specs=out_specs,
        out_shape=out_shape,
        compiler_params=pltpu.CompilerParams(
            dimension_semantics=("parallel", "parallel"), vmem_limit_bytes=VMEM_LIMIT_BYTES),
        name="proj",
    )(x, nw, cos_t, sin_t, cos_k, sin_k, *weights)


def _attn_kernel(q_t_ref, qi_t_ref, wi_t_ref, k_ref, kidx_ref, v_t_ref, za_ref, o_ref,
                 score_scr, bias_scr, qpad_scr, m_scr, alpha_scr, l_scr, acc_scr, s_scr, *, top_k):
    j = pl.program_id(1)
    n_chunks = j + 1
    tq = Q_TILE
    q_pos = j * tq + lax.broadcasted_iota(jnp.int32, (1, tq), 1)
    key_iota = lax.broadcasted_iota(jnp.int32, (KEY_TILE, tq), 0)

    qi_t = qi_t_ref[0]
    qi_cat = jnp.concatenate([qi_t[hd * IDX_DIM:(hd + 1) * IDX_DIM] for hd in range(IDX_HEADS)], axis=1)
    wi_t = wi_t_ref[0]
    wi_cat = jnp.concatenate([wi_t[hd:hd + 1] for hd in range(IDX_HEADS)], axis=1)

    def score_chunk(c, carry):
        start = pl.multiple_of(c * KEY_TILE, KEY_TILE)
        logits = _dot(kidx_ref[0, pl.ds(start, KEY_TILE), :], qi_cat)
        weighted = jnp.maximum(logits, 0.0) * wi_cat
        score = weighted[:, 0:tq]
        for hd in range(1, IDX_HEADS):
            score = score + weighted[:, hd * tq:(hd + 1) * tq]
        causal = (start + key_iota) <= q_pos
        score_scr[pl.ds(start, KEY_TILE), :] = jnp.where(causal, score, -jnp.inf)
        return carry

    lax.fori_loop(0, n_chunks, score_chunk, 0)

    def count_ge(cand, strict):
        def body(c, acc):
            start = pl.multiple_of(c * KEY_TILE, KEY_TILE)
            s = score_scr[pl.ds(start, KEY_TILE), :]
            hit = (s > cand) if strict else (s >= cand)
            inc = jnp.where(hit, 1.0, 0.0)
            return acc + inc.reshape(KEY_TILE // COUNT_ROWS, COUNT_ROWS, tq).sum(axis=0)
        acc = lax.fori_loop(0, n_chunks, body, jnp.zeros((COUNT_ROWS, tq), _F32))
        return jnp.sum(acc, axis=0, keepdims=True)

    int_min = jnp.int32(-2 ** 31)

    def ordered_to_float(kbits):
        return pltpu.bitcast(kbits ^ ((kbits >> 31) & jnp.int32(0x7FFFFFFF)), _F32)

    def bit_body(i, t_bits):
        bit = jnp.where(i == 0, int_min, jnp.int32(1) << (31 - i))
        cand_bits = t_bits ^ bit
        cnt = count_ge(ordered_to_float(cand_bits), strict=False)
        return jnp.where(cnt >= top_k, cand_bits, t_bits)

    t_bits = lax.fori_loop(0, 32, bit_body, jnp.full((1, tq), int_min, jnp.int32))
    few = (q_pos + 1) <= top_k
    thr = jnp.where(few, -jnp.inf, ordered_to_float(t_bits))
    n_above = count_ge(thr, strict=True)
    n_ties_kept = jnp.where(few, 0.0, top_k - n_above)

    row = lax.broadcasted_iota(jnp.int32, (KEY_TILE, KEY_TILE), 0)
    col = lax.broadcasted_iota(jnp.int32, (KEY_TILE, KEY_TILE), 1)
    strict_lower = jnp.where(col < row, 1.0, 0.0).astype(_BF16)

    def bias_chunk(c, ties_before):
        start = pl.multiple_of(c * KEY_TILE, KEY_TILE)
        s = score_scr[pl.ds(start, KEY_TILE), :]
        tie = jnp.where(s == thr, 1.0, 0.0)
        rank = _dot(strict_lower, tie.astype(_BF16)) + ties_before
        keep = (s > thr) | ((s == thr) & (rank < n_ties_kept))
        bias_scr[pl.ds(start, KEY_TILE), :] = jnp.where(keep, 0.0, NEG_BIG)
        return ties_before + jnp.sum(tie, axis=0, keepdims=True)

    lax.fori_loop(0, n_chunks, bias_chunk, jnp.zeros((1, tq), _F32))

    n_pairs = ATTN_HEADS // 2
    zeros_half = jnp.zeros((HEAD_DIM, 2 * tq), _BF16)
    for pair in range(n_pairs):
        g = (2 * pair) // ATTN_GROUP
        q_cat = jnp.concatenate([q_t_ref[0, (2 * pair + hd) * HEAD_DIM:(2 * pair + hd + 1) * HEAD_DIM, :]
                                 for hd in range(2)], axis=1)
        qpad_scr[pair] = jnp.concatenate([q_cat, zeros_half] if g == 0 else [zeros_half, q_cat], axis=0)
    m_scr[...] = jnp.full(m_scr.shape, NEG_BIG, _F32)
    l_scr[...] = jnp.zeros(l_scr.shape, _F32)
    acc_scr[...] = jnp.zeros(acc_scr.shape, _F32)

    def logits_stage(c, pair):
        start = pl.multiple_of(c * KEY_TILE, KEY_TILE)
        b = bias_scr[pl.ds(start, KEY_TILE), :]
        s = _dot(k_ref[0, pl.ds(start, KEY_TILE), :], qpad_scr[pair]) + jnp.concatenate([b, b], axis=1)
        m_old = m_scr[pair]
        m_new = jnp.maximum(m_old, jnp.max(s, axis=0, keepdims=True))
        s_scr[pair] = s
        alpha_scr[pair] = jnp.exp2(m_old - m_new)
        m_scr[pair] = m_new

    def value_stage(c, pair):
        start = pl.multiple_of(c * KEY_TILE, KEY_TILE)
        g = (2 * pair) // ATTN_GROUP
        p = jnp.exp2(s_scr[pair] - m_scr[pair])
        alpha = alpha_scr[pair]
        l_scr[pair] = alpha * l_scr[pair] + jnp.sum(p, axis=0, keepdims=True)
        v_c = v_t_ref[0, g * HEAD_DIM:(g + 1) * HEAD_DIM, pl.ds(start, KEY_TILE)]
        acc_scr[pair] = alpha * acc_scr[pair] + _dot(v_c, p.astype(_BF16))

    for pair in range(n_pairs):
        logits_stage(0, pair)

    def attn_tile(c, carry):
        for pair in range(n_pairs):
            value_stage(c, pair)
            logits_stage(c + 1, pair)
        return carry

    lax.fori_loop(0, n_chunks - 1, attn_tile, 0)
    for pair in range(n_pairs):
        value_stage(n_chunks - 1, pair)

    for pair in range(n_pairs):
        o_t = acc_scr[pair] / l_scr[pair]
        both = jnp.concatenate([o_t[:, :tq], o_t[:, tq:]], axis=0)
        z = za_ref[0, :, pair * LANES:(pair + 1) * LANES]
        o_ref[0, :, pair * LANES:(pair + 1) * LANES] = (both.T * _silu(z)).astype(o_ref.dtype)


def _attention(q_t, qi_t, wi_t, k, kidx, v_t, za):
    bsz, seq, _ = k.shape
    tq = Q_TILE
    top_k = min(TOPK_MAX, seq // 4)
    feat = lambda rows: pl.BlockSpec((1, rows, tq), lambda b, j: (b, 0, j))
    full = lambda s1, s2: pl.BlockSpec((1, s1, s2), lambda b, j: (b, 0, 0))
    return pl.pallas_call(
        functools.partial(_attn_kernel, top_k=top_k),
        grid=(bsz, seq // tq),
        in_specs=[feat(ATTN_WIDTH), feat(IDX_HEADS * IDX_DIM), feat(W_IDX_ROWS),
                  full(seq, KV_WIDTH), full(seq, IDX_DIM), full(KV_WIDTH, seq),
                  pl.BlockSpec((1, tq, ATTN_WIDTH), lambda b, j: (b, j, 0))],
        out_specs=pl.BlockSpec((1, tq, ATTN_WIDTH), lambda b, j: (b, j, 0)),
        out_shape=jax.ShapeDtypeStruct((bsz, seq, ATTN_WIDTH), _BF16),
        scratch_shapes=[pltpu.VMEM((seq, tq), _F32), pltpu.VMEM((seq, tq), _F32),
                        pltpu.VMEM((ATTN_HEADS // 2, 2 * HEAD_DIM, 2 * tq), _BF16),
                        pltpu.VMEM((ATTN_HEADS // 2, 1, 2 * tq), _F32),
                        pltpu.VMEM((ATTN_HEADS // 2, 1, 2 * tq), _F32),
                        pltpu.VMEM((ATTN_HEADS // 2, 1, 2 * tq), _F32),
                        pltpu.VMEM((ATTN_HEADS // 2, HEAD_DIM, 2 * tq), _F32),
                        pltpu.VMEM((ATTN_HEADS // 2, KEY_TILE, 2 * tq), _F32)],
        compiler_params=pltpu.CompilerParams(
            dimension_semantics=("parallel", "arbitrary"), vmem_limit_bytes=VMEM_LIMIT_BYTES),
        name="attn",
    )(q_t, qi_t, wi_t, k, kidx, v_t, za)


def _split3(x):
    hi = x.astype(_BF16)
    r1 = x - hi.astype(_F32)
    mid = r1.astype(_BF16)
    lo = (r1 - mid.astype(_F32)).astype(_BF16)
    return hi, mid, lo


def _causal_conv_silu(x, tail_ref, w_ref, b_ref):
    rows = x.shape[0]
    tail = tail_ref[...]
    sub = lax.broadcasted_iota(jnp.int32, (SUBLANES, x.shape[1]), 0)
    w = w_ref[...]
    y = x * w[CONV_K - 1:CONV_K] + b_ref[...]
    for shift in range(1, CONV_K):
        rolled = pltpu.roll(x, shift, 0)
        head = jnp.where(sub < shift, pltpu.roll(tail, shift, 0), rolled[:SUBLANES])
        shifted = jnp.concatenate([head, rolled[SUBLANES:]], axis=0)
        y = y + shifted * w[CONV_K - 1 - shift:CONV_K - shift]
    tail_ref[...] = x[rows - SUBLANES:]
    return _silu(y)


def _ssd_kernel(xb_ref, bcdt_ref, zb_ref, cwx_ref, cbx_ref, cwbc_ref, cbbc_ref,
                dtb_ref, alog_ref, dskip_ref, nw_ref, o_ref,
                tailx_scr, tailbc_scr, state_scr, y_scr):
    @pl.when(pl.program_id(1) == 0)
    def _():
        tailx_scr[...] = jnp.zeros_like(tailx_scr)
        tailbc_scr[...] = jnp.zeros_like(tailbc_scr)
        state_scr[...] = jnp.zeros_like(state_scr)

    bcdt = bcdt_ref[0]
    xs = _causal_conv_silu(xb_ref[0], tailx_scr, cwx_ref, cbx_ref)
    bcs = _causal_conv_silu(bcdt[:, :BC_WIDTH], tailbc_scr, cwbc_ref, cbbc_ref)

    dt_in = bcdt[:, BC_WIDTH:] + dtb_ref[...]
    dt = jnp.maximum(dt_in, 0.0) + jnp.log1p(jnp.exp(-jnp.abs(dt_in)))
    da = dt * (-jnp.exp(alog_ref[...]))

    r = lax.broadcasted_iota(jnp.int32, (CHUNK, CHUNK), 0)
    c = lax.broadcasted_iota(jnp.int32, (CHUNK, CHUNK), 1)
    lower = c <= r
    lower_b = jnp.where(lower, 1.0, 0.0).astype(_BF16)
    upper_b = jnp.where(r <= c, 1.0, 0.0).astype(_BF16)
    a_cum = sum(_dot(lower_b, part) for part in _split3(da))
    a_cum_t = sum(_dot(part, upper_b) for part in _split3(da.T))

    lane_hi = lax.broadcasted_iota(jnp.int32, (1, LANES), 1) >= SSM_HEAD_DIM
    dskip = dskip_ref[...]

    for grp in range(SSM_GROUPS):
        blk, half = divmod(grp, 2)
        b_blk = bcs[:, blk * LANES:(blk + 1) * LANES]
        c_blk = bcs[:, BC_WIDTH // 2 + blk * LANES: BC_WIDTH // 2 + (blk + 1) * LANES]
        in_grp = lane_hi if half else jnp.logical_not(lane_hi)
        c_g = jnp.where(in_grp, c_blk, 0.0).astype(_BF16)
        b_blk16 = b_blk.astype(_BF16)
        cb = lax.dot_general(c_g, b_blk16, _NT, preferred_element_type=_F32)

        for pr in range(2):
            pair = grp * 2 + pr
            h_a, h_b = 2 * pair, 2 * pair + 1

            def per_pair(arr):
                return jnp.where(lane_hi, arr[:, h_b:h_b + 1], arr[:, h_a:h_a + 1])

            dt_p = per_pair(dt)
            acum_p = per_pair(a_cum)
            last_p = per_pair(a_cum[CHUNK - 1:CHUNK])
            x_p = xs[:, pair * LANES:(pair + 1) * LANES]
            xd = x_p * dt_p

            y = None
            for hh, keep_hi in ((h_a, False), (h_b, True)):
                seg = a_cum[:, hh:hh + 1] - a_cum_t[hh:hh + 1, :]
                decay = jnp.where(lower, jnp.exp(jnp.where(lower, seg, 0.0)), 0.0)
                m_h = (cb * decay).astype(_BF16)
                sel = lane_hi if keep_hi else jnp.logical_not(lane_hi)
                part = _dot(m_h, jnp.where(sel, xd, 0.0).astype(_BF16))
                y = part if y is None else y + part

            st_prev = state_scr[pair]
            y_off = _dot(c_g, st_prev.astype(_BF16)) * jnp.exp(acum_p)
            w = (xd * jnp.exp(last_p - acum_p)).astype(_BF16)
            st_new = lax.dot_general(b_blk16, w, _TN, preferred_element_type=_F32)
            state_scr[pair] = jnp.exp(last_p) * st_prev + st_new
            y_scr[:, pair * LANES:(pair + 1) * LANES] = (
                y + y_off + dskip[:, pair * LANES:(pair + 1) * LANES] * x_p)

    gw = SSM_WIDTH // SSM_GROUPS
    for grp in range(SSM_GROUPS):
        sl = slice(grp * gw, (grp + 1) * gw)
        yz = y_scr[:, sl] * _silu(zb_ref[0, :, sl])
        ms = jnp.mean(yz * yz, axis=-1, keepdims=True)
        o_ref[0, :, sl] = (yz * lax.rsqrt(ms + EPS) * nw_ref[:, sl]).astype(o_ref.dtype)


def _ssd(xb, bcdt, zb, cwx, cbx, cwbc, cbbc, dtb, alog, dskip, nw):
    bsz, seq, _ = xb.shape
    tok = lambda width: pl.BlockSpec((1, CHUNK, width), lambda b, i: (b, i, 0))
    consts = (cwx, cbx, cwbc, cbbc, dtb, alog, dskip, nw)
    return pl.pallas_call(
        _ssd_kernel,
        grid=(bsz, seq // CHUNK),
        in_specs=[tok(SSM_WIDTH), tok(BCDT_WIDTH), tok(SSM_WIDTH)] + [_const_spec(a.shape) for a in consts],
        out_specs=tok(SSM_WIDTH),
        out_shape=jax.ShapeDtypeStruct((bsz, seq, SSM_WIDTH), _BF16),
        scratch_shapes=[pltpu.VMEM((SUBLANES, SSM_WIDTH), _F32), pltpu.VMEM((SUBLANES, BC_WIDTH), _F32),
                        pltpu.VMEM((SSM_HEADS // 2, 2 * SSM_STATE, LANES), _F32),
                        pltpu.VMEM((CHUNK, SSM_WIDTH), _F32)],
        compiler_params=pltpu.CompilerParams(
            dimension_semantics=("parallel", "arbitrary"), vmem_limit_bytes=VMEM_LIMIT_BYTES),
        name="ssd",
    )(xb, bcdt, zb, *consts)


def _merge_kernel(x_ref, oa_ref, ob_ref, g_ref, gb_ref, wa_ref, wb_ref, wo_ref, fw_ref, o_ref, *, final_norm):
    gates = jax.nn.sigmoid(g_ref[...] + gb_ref[...])
    merged = (gates[:, :D_MODEL] * _dot(oa_ref[...], wa_ref[...])
              + gates[:, D_MODEL:] * _dot(ob_ref[...], wb_ref[...]))
    y = x_ref[...] + _dot(merged.astype(_BF16), wo_ref[...])
    if final_norm:
        y = y * lax.rsqrt(jnp.mean(y * y, axis=-1, keepdims=True) + EPS) * fw_ref[...]
    o_ref[...] = y


def _merge(x2, oa2, ob2, g2, gate_bias, wa, wb, wo, fw, final_norm):
    rows = x2.shape[0]
    tm = MERGE_ROWS
    tok = lambda width: pl.BlockSpec((tm, width), lambda i: (i, 0))
    consts = (gate_bias, wa, wb, wo, fw)
    return pl.pallas_call(
        functools.partial(_merge_kernel, final_norm=final_norm),
        grid=(rows // tm,),
        in_specs=[tok(D_MODEL), tok(ATTN_WIDTH), tok(SSM_WIDTH), tok(N_BRANCH * D_MODEL)]
                 + [_const_spec(a.shape) for a in consts],
        out_specs=tok(D_MODEL),
        out_shape=jax.ShapeDtypeStruct((rows, D_MODEL), _F32),
        compiler_params=pltpu.CompilerParams(
            dimension_semantics=("parallel",), vmem_limit_bytes=VMEM_LIMIT_BYTES),
        name="merge",
    )(x2, oa2, ob2, g2, *consts)


def _rotate_cols(w, n_heads):
    w3 = w.reshape(w.shape[0], n_heads, HEAD_DIM)
    rot = jnp.concatenate([-w3[..., ROT_HALF:ROT_DIM], w3[..., :ROT_HALF],
                           jnp.zeros_like(w3[..., ROT_DIM:])], axis=-1)
    return rot.reshape(w.shape)


def _layer_weights(w_in):
    o = SPLIT_OFFSETS
    seg = lambda i: w_in[:, o[i]:o[i + 1]]
    w_q, w_k, w_v, w_za, w_qi, w_ki, w_wi, w_zb, w_xb, w_b, w_c, w_dt, w_g = (seg(i) for i in range(13))
    pad = lambda w, n: jnp.pad(w, ((0, 0), (0, n - w.shape[1])))
    w_t = jnp.concatenate([w_q, w_qi, w_v, pad(w_wi, W_IDX_ROWS)], axis=1).T
    w_kside = pad(jnp.concatenate([w_k, w_ki], axis=1), 2 * LANES)
    w_kside_rot = pad(jnp.concatenate([_rotate_cols(w_k, ATTN_KV_HEADS), _rotate_cols(w_ki, 1)], axis=1), 2 * LANES)
    w_bcdt = jnp.concatenate([w_b, w_c, pad(w_dt, LANES)], axis=1)
    return tuple(w.astype(_BF16) for w in (w_t, w_kside, w_kside_rot, w_zb, w_xb, w_bcdt, w_g, w_za))


def _rope_tables(positions):
    inv_freq = ROPE_THETA ** (-jnp.arange(0, ROT_DIM, 2, dtype=_F32) / ROT_DIM)
    ang = positions.astype(_F32)[..., None] * inv_freq
    cos, sin = jnp.cos(ang), jnp.sin(ang)
    ones = jnp.ones(ang.shape[:-1] + (HEAD_DIM - ROT_DIM,), _F32)
    cos_k = jnp.concatenate([cos, cos, ones], axis=-1)
    sin_k = jnp.concatenate([sin, sin, 0.0 * ones], axis=-1)
    n_rep = 2 * LANES // HEAD_DIM
    return (jnp.swapaxes(cos, 1, 2), jnp.swapaxes(sin, 1, 2),
            jnp.tile(cos_k, (1, 1, n_rep)), jnp.tile(sin_k, (1, 1, n_rep)))


def _pad_lanes(v, n):
    return jnp.pad(v, (0, n - v.shape[0]))[None, :]


def kernel(x, positions, norm_w, w_in, gate_bias, conv_w, conv_b, dt_bias, a_log, d_skip,
           ssm_norm_w, w_branch_a, w_branch_b, w_out, final_norm_w):
    bsz, seq, _ = x.shape
    depth = norm_w.shape[0]
    cos_t, sin_t, cos_k, sin_k = _rope_tables(positions)
    for i in range(depth):
        weights = _layer_weights(w_in[i])
        (q_t, qi_t, wi_t, v_t, k, kidx, zb, xb, bcdt, gates, za) = _projection(
            x, norm_w[i][None, :], cos_t, sin_t, cos_k, sin_k, weights)
        o_a = _attention(q_t, qi_t, wi_t, k, kidx, v_t, za)
        o_b = _ssd(xb, bcdt, zb,
                   conv_w[i][:, :SSM_WIDTH], conv_b[i][None, :SSM_WIDTH],
                   conv_w[i][:, SSM_WIDTH:], conv_b[i][None, SSM_WIDTH:],
                   _pad_lanes(dt_bias[i], LANES), _pad_lanes(a_log[i], LANES),
                   jnp.repeat(d_skip[i], SSM_HEAD_DIM)[None, :], ssm_norm_w[i][None, :])
        x = _merge(x.reshape(bsz * seq, D_MODEL), o_a.reshape(bsz * seq, ATTN_WIDTH),
                   o_b.reshape(bsz * seq, SSM_WIDTH), gates.reshape(bsz * seq, N_BRANCH * D_MODEL),
                   gate_bias[i][None, :], w_branch_a[i].astype(_BF16), w_branch_b[i].astype(_BF16),
                   w_out[i].astype(_BF16), final_norm_w[None, :],
                   final_norm=(i == depth - 1)).reshape(bsz, seq, D_MODEL)
    return x
```

```python
import functools

import numpy as np
import jax
import jax.numpy as jnp
from jax import lax
from jax.experimental import pallas as pl
from jax.experimental.pallas import tpu as pltpu

D_MODEL = 1024
ATTN_HEADS = 8
ATTN_KV_HEADS = 2
HEAD_DIM = 64
ATTN_GROUP = ATTN_HEADS // ATTN_KV_HEADS
ATTN_WIDTH = ATTN_HEADS * HEAD_DIM
KV_WIDTH = ATTN_KV_HEADS * HEAD_DIM
ROT_DIM = HEAD_DIM // 4
ROT_HALF = ROT_DIM // 2
ROPE_THETA = 500000.0
IDX_HEADS = 4
IDX_DIM = 64
TOPK_MAX = 256
SSM_HEADS = 16
SSM_HEAD_DIM = 64
SSM_WIDTH = SSM_HEADS * SSM_HEAD_DIM
SSM_GROUPS = 4
SSM_STATE = 64
CONV_K = 4
CHUNK = 128
BC_WIDTH = 2 * SSM_GROUPS * SSM_STATE
N_BRANCH = 2
EPS = 1e-6
SPLIT_SIZES = (ATTN_WIDTH, KV_WIDTH, KV_WIDTH, ATTN_WIDTH,
               IDX_HEADS * IDX_DIM, IDX_DIM, IDX_HEADS,
               SSM_WIDTH, SSM_WIDTH, SSM_GROUPS * SSM_STATE, SSM_GROUPS * SSM_STATE, SSM_HEADS,
               N_BRANCH * D_MODEL)
SPLIT_OFFSETS = tuple(int(o) for o in np.cumsum((0,) + SPLIT_SIZES))

LANES = 128
SUBLANES = 8
VMEM_LIMIT_BYTES = 56 * 1024 * 1024

PROJ_ROWS = 256
CONV_COLS = 256
Q_TILE = 256
KEY_TILE = 256
COUNT_ROWS = 64
LOG2_E = 1.4426950408889634
MERGE_ROWS = 512
W_IDX_ROWS = 8
BCDT_WIDTH = BC_WIDTH + LANES
NEG_BIG = -1e30

_F32 = jnp.float32
_BF16 = jnp.bfloat16
_NT = (((1,), (1,)), ((), ()))
_TN = (((0,), (0,)), ((), ()))


def _dot(a, b):
    return jnp.dot(a, b, preferred_element_type=_F32)


def _silu(x):
    return x * jax.nn.sigmoid(x)


def _causal_conv_silu(x, tail_ref, w_ref, b_ref):
    rows = x.shape[0]
    tail = tail_ref[...]
    sub = lax.broadcasted_iota(jnp.int32, (SUBLANES, x.shape[1]), 0)
    w = w_ref[...]
    y = x * w[CONV_K - 1:CONV_K] + b_ref[...]
    for shift in range(1, CONV_K):
        rolled = pltpu.roll(x, shift, 0)
        head = jnp.where(sub < shift, pltpu.roll(tail, shift, 0), rolled[:SUBLANES])
        shifted = jnp.concatenate([head, rolled[SUBLANES:]], axis=0)
        y = y + shifted * w[CONV_K - 1 - shift:CONV_K - shift]
    tail_ref[...] = x[rows - SUBLANES:]
    return _silu(y)


def _proj_kernel(x_ref, nw_ref, cos_t_ref, sin_t_ref,
                 w_t_ref, w_zb_ref, w_xb_ref, w_bc_ref, w_g_ref, w_za_ref,
                 cwx_ref, cbx_ref, cwbc_ref, cbbc_ref,
                 q_t_ref, qi_t_ref, wi_t_ref, dt_t_ref, v_t_ref, k_ref, kidx_ref,
                 zb_ref, xs_ref, bcs_ref, g_ref, za_ref,
                 tailx_scr, tailbc_scr):
    @pl.when(pl.program_id(1) == 0)
    def _():
        tailx_scr[...] = jnp.zeros_like(tailx_scr)
        tailbc_scr[...] = jnp.zeros_like(tailbc_scr)

    x = x_ref[0]
    h = x * lax.rsqrt(jnp.mean(x * x, axis=-1, keepdims=True) + EPS) * nw_ref[...]
    h = h.astype(_BF16)

    t = lax.dot_general(w_t_ref[...], h, _NT, preferred_element_type=_F32)
    cos_t = cos_t_ref[0]
    sin_t = sin_t_ref[0]

    def rope_head(block, hd):
        x1 = block[hd * HEAD_DIM: hd * HEAD_DIM + ROT_HALF]
        x2 = block[hd * HEAD_DIM + ROT_HALF: hd * HEAD_DIM + ROT_DIM]
        return jnp.concatenate([x1 * cos_t - x2 * sin_t, x2 * cos_t + x1 * sin_t], axis=0)

    def rope_store(block, n_heads, scale, out_ref):
        out_ref[0] = (block * scale).astype(out_ref.dtype)
        for hd in range(n_heads):
            out_ref[0, hd * HEAD_DIM: hd * HEAD_DIM + ROT_DIM, :] = (
                rope_head(block, hd) * scale).astype(out_ref.dtype)

    def rope_value(block, n_heads):
        parts = []
        for hd in range(n_heads):
            parts += [rope_head(block, hd), block[hd * HEAD_DIM + ROT_DIM:(hd + 1) * HEAD_DIM]]
        if block.shape[0] > n_heads * HEAD_DIM:
            parts.append(block[n_heads * HEAD_DIM:])
        return jnp.concatenate(parts, axis=0)

    o_qi = ATTN_WIDTH
    o_k = o_qi + IDX_HEADS * IDX_DIM
    o_ki = o_k + KV_WIDTH
    o_v = o_ki + LANES
    rope_store(t[:o_qi], ATTN_HEADS, LOG2_E * HEAD_DIM ** -0.5, q_t_ref)
    rope_store(t[o_qi:o_k], IDX_HEADS, IDX_DIM ** -0.5, qi_t_ref)
    o_wi = o_ki + IDX_DIM
    o_dt = o_wi + W_IDX_ROWS
    wi_t_ref[0] = t[o_wi:o_dt] * (IDX_HEADS ** -0.5)
    dt_t_ref[0] = t[o_dt:o_dt + SSM_HEADS]
    v_t_ref[0] = t[o_v:o_v + KV_WIDTH].astype(_BF16)
    k_ref[0] = rope_value(t[o_k:o_ki], ATTN_KV_HEADS).T.astype(_BF16)
    kidx_ref[0] = rope_value(t[o_ki:o_v], 1).T[:, :IDX_DIM].astype(_BF16)

    def plain(out_ref, w_ref, c0, width):
        out_ref[0, :, c0:c0 + width] = _dot(h, w_ref[:, c0:c0 + width])

    def conv(out_ref, w_ref, tail_scr, cw_ref, cb_ref, c0, width):
        sl = slice(c0, c0 + width)
        out_ref[0, :, sl] = _causal_conv_silu(
            _dot(h, w_ref[:, sl]), tail_scr.at[:, sl], cw_ref.at[:, sl], cb_ref.at[:, sl]).astype(_BF16)

    cc = CONV_COLS
    for n in range(SSM_WIDTH // cc):
        conv(xs_ref, w_xb_ref, tailx_scr, cwx_ref, cbx_ref, n * cc, cc)
        plain(g_ref, w_g_ref, 2 * n * cc, 2 * cc)
    for n in range(BC_WIDTH // cc):
        conv(bcs_ref, w_bc_ref, tailbc_scr, cwbc_ref, cbbc_ref, n * cc, cc)
        plain(zb_ref, w_zb_ref, 2 * n * cc, 2 * cc)
    plain(za_ref, w_za_ref, 0, ATTN_WIDTH)


def _const_spec(shape):
    nd = len(shape)
    return pl.BlockSpec(shape, lambda *_: (0,) * nd, pipeline_mode=pl.Buffered(1))


def _projection(x, nw, cos_t, sin_t, weights, conv_params):
    bsz, seq, _ = x.shape
    tm = PROJ_ROWS
    tok = lambda width: pl.BlockSpec((1, tm, width), lambda b, i: (b, i, 0))
    feat = lambda rows: pl.BlockSpec((1, rows, tm), lambda b, i: (b, 0, i))
    in_specs = [tok(D_MODEL), _const_spec((1, D_MODEL)), feat(ROT_HALF), feat(ROT_HALF)
                ] + [_const_spec(w.shape) for w in weights + conv_params]
    out_shape = (
        jax.ShapeDtypeStruct((bsz, ATTN_WIDTH, seq), _BF16),
        jax.ShapeDtypeStruct((bsz, IDX_HEADS * IDX_DIM, seq), _BF16),
        jax.ShapeDtypeStruct((bsz, W_IDX_ROWS, seq), _F32),
        jax.ShapeDtypeStruct((bsz, SSM_HEADS, seq), _F32),
        jax.ShapeDtypeStruct((bsz, KV_WIDTH, seq), _BF16),
        jax.ShapeDtypeStruct((bsz, seq, KV_WIDTH), _BF16),
        jax.ShapeDtypeStruct((bsz, seq, IDX_DIM), _BF16),
        jax.ShapeDtypeStruct((bsz, seq, SSM_WIDTH), _F32),
        jax.ShapeDtypeStruct((bsz, seq, SSM_WIDTH), _BF16),
        jax.ShapeDtypeStruct((bsz, seq, BC_WIDTH), _BF16),
        jax.ShapeDtypeStruct((bsz, seq, N_BRANCH * D_MODEL), _F32),
        jax.ShapeDtypeStruct((bsz, seq, ATTN_WIDTH), _F32),
    )
    out_specs = (feat(ATTN_WIDTH), feat(IDX_HEADS * IDX_DIM), feat(W_IDX_ROWS), feat(SSM_HEADS),
                 feat(KV_WIDTH), tok(KV_WIDTH), tok(IDX_DIM), tok(SSM_WIDTH), tok(SSM_WIDTH),
                 tok(BC_WIDTH), tok(N_BRANCH * D_MODEL), tok(ATTN_WIDTH))
    return pl.pallas_call(
        _proj_kernel,
        grid=(bsz, seq // tm),
        in_specs=in_specs,
        out_specs=out_specs,
        out_shape=out_shape,
        scratch_shapes=[pltpu.VMEM((SUBLANES, SSM_WIDTH), _F32), pltpu.VMEM((SUBLANES, BC_WIDTH), _F32)],
        compiler_params=pltpu.CompilerParams(
            dimension_semantics=("parallel", "arbitrary"), vmem_limit_bytes=VMEM_LIMIT_BYTES),
        name="proj",
    )(x, nw, cos_t, sin_t, *weights, *conv_params)


def _attn_kernel(q_t_ref, qi_t_ref, wi_t_ref, k_ref, kidx_ref, v_t_ref, za_ref, o_ref,
                 score_scr, bias_scr, qpad_scr, m_scr, alpha_scr, l_scr, acc_scr, s_scr, *, top_k):
    j = pl.program_id(1)
    n_chunks = j + 1
    tq = Q_TILE
    q_pos = j * tq + lax.broadcasted_iota(jnp.int32, (1, tq), 1)
    key_iota = lax.broadcasted_iota(jnp.int32, (KEY_TILE, tq), 0)

    qi_t = qi_t_ref[0]
    qi_cat = jnp.concatenate([qi_t[hd * IDX_DIM:(hd + 1) * IDX_DIM] for hd in range(IDX_HEADS)], axis=1)
    wi_t = wi_t_ref[0]
    wi_cat = jnp.concatenate([wi_t[hd:hd + 1] for hd in range(IDX_HEADS)], axis=1)

    def score_chunk(c, carry):
        start = pl.multiple_of(c * KEY_TILE, KEY_TILE)
        logits = _dot(kidx_ref[0, pl.ds(start, KEY_TILE), :], qi_cat)
        weighted = jnp.maximum(logits, 0.0) * wi_cat
        score = weighted[:, 0:tq]
        for hd in range(1, IDX_HEADS):
            score = score + weighted[:, hd * tq:(hd + 1) * tq]
        causal = (start + key_iota) <= q_pos
        score_scr[pl.ds(start, KEY_TILE), :] = jnp.where(causal, score, -jnp.inf)
        return carry

    lax.fori_loop(0, n_chunks, score_chunk, 0)

    def count_ge(cand, strict):
        def body(c, acc):
            start = pl.multiple_of(c * KEY_TILE, KEY_TILE)
            s = score_scr[pl.ds(start, KEY_TILE), :]
            hit = (s > cand) if strict else (s >= cand)
            inc = jnp.where(hit, 1.0, 0.0)
            return acc + inc.reshape(KEY_TILE // COUNT_ROWS, COUNT_ROWS, tq).sum(axis=0)
        acc = lax.fori_loop(0, n_chunks, body, jnp.zeros((COUNT_ROWS, tq), _F32))
        return jnp.sum(acc, axis=0, keepdims=True)

    int_min = jnp.int32(-2 ** 31)

    def ordered_to_float(kbits):
        return pltpu.bitcast(kbits ^ ((kbits >> 31) & jnp.int32(0x7FFFFFFF)), _F32)

    def bit_body(i, t_bits):
        bit = jnp.where(i == 0, int_min, jnp.int32(1) << (31 - i))
        cand_bits = t_bits ^ bit
        cnt = count_ge(ordered_to_float(cand_bits), strict=False)
        return jnp.where(cnt >= top_k, cand_bits, t_bits)

    t_bits = lax.fori_loop(0, 32, bit_body, jnp.full((1, tq), int_min, jnp.int32))
    few = (q_pos + 1) <= top_k
    thr = jnp.where(few, -jnp.inf, ordered_to_float(t_bits))
    n_above = count_ge(thr, strict=True)
    n_ties_kept = jnp.where(few, 0.0, top_k - n_above)

    row = lax.broadcasted_iota(jnp.int32, (KEY_TILE, KEY_TILE), 0)
    col = lax.broadcasted_iota(jnp.int32, (KEY_TILE, KEY_TILE), 1)
    strict_lower = jnp.where(col < row, 1.0, 0.0).astype(_BF16)

    def bias_chunk(c, ties_before):
        start = pl.multiple_of(c * KEY_TILE, KEY_TILE)
        s = score_scr[pl.ds(start, KEY_TILE), :]
        tie = jnp.where(s == thr, 1.0, 0.0)
        rank = _dot(strict_lower, tie.astype(_BF16)) + ties_before
        keep = (s > thr) | ((s == thr) & (rank < n_ties_kept))
        bias_scr[pl.ds(start, KEY_TILE), :] = jnp.where(keep, 0.0, NEG_BIG)
        return ties_before + jnp.sum(tie, axis=0, keepdims=True)

    lax.fori_loop(0, n_chunks, bias_chunk, jnp.zeros((1, tq), _F32))

    n_pairs = ATTN_HEADS // 2
    zeros_half = jnp.zeros((HEAD_DIM, 2 * tq), _BF16)
    for pair in range(n_pairs):
        g = (2 * pair) // ATTN_GROUP
        q_cat = jnp.concatenate([q_t_ref[0, (2 * pair + hd) * HEAD_DIM:(2 * pair + hd + 1) * HEAD_DIM, :]
                                 for hd in range(2)], axis=1)
        qpad_scr[pair] = jnp.concatenate([q_cat, zeros_half] if g == 0 else [zeros_half, q_cat], axis=0)
    m_scr[...] = jnp.full(m_scr.shape, NEG_BIG, _F32)
    l_scr[...] = jnp.zeros(l_scr.shape, _F32)
    acc_scr[...] = jnp.zeros(acc_scr.shape, _F32)

    def logits_stage(c, pair):
        start = pl.multiple_of(c * KEY_TILE, KEY_TILE)
        b = bias_scr[pl.ds(start, KEY_TILE), :]
        s = _dot(k_ref[0, pl.ds(start, KEY_TILE), :], qpad_scr[pair]) + jnp.concatenate([b, b], axis=1)
        m_old = m_scr[pair]
        m_new = jnp.maximum(m_old, jnp.max(s, axis=0, keepdims=True))
        s_scr[pair] = s
        alpha_scr[pair] = jnp.exp2(m_old - m_new)
        m_scr[pair] = m_new

    def value_stage(c, pair):
        start = pl.multiple_of(c * KEY_TILE, KEY_TILE)
        g = (2 * pair) // ATTN_GROUP
        p = jnp.exp2(s_scr[pair] - m_scr[pair])
        alpha = alpha_scr[pair]
        l_scr[pair] = alpha * l_scr[pair] + jnp.sum(p, axis=0, keepdims=True)
        v_c = v_t_ref[0, g * HEAD_DIM:(g + 1) * HEAD_DIM, pl.ds(start, KEY_TILE)]
        acc_scr[pair] = alpha * acc_scr[pair] + _dot(v_c, p.astype(_BF16))

    for pair in range(n_pairs):
        logits_stage(0, pair)

    def attn_tile(c, carry):
        for pair in range(n_pairs):
            value_stage(c, pair)
            logits_stage(c + 1, pair)
        return carry

    lax.fori_loop(0, n_chunks - 1, attn_tile, 0)
    for pair in range(n_pairs):
        value_stage(n_chunks - 1, pair)

    for pair in range(n_pairs):
        o_t = acc_scr[pair] / l_scr[pair]
        both = jnp.concatenate([o_t[:, :tq], o_t[:, tq:]], axis=0)
        z = za_ref[0, :, pair * LANES:(pair + 1) * LANES]
        o_ref[0, :, pair * LANES:(pair + 1) * LANES] = (both.T * _silu(z)).astype(o_ref.dtype)


def _attention(q_t, qi_t, wi_t, k, kidx, v_t, za):
    bsz, seq, _ = k.shape
    tq = Q_TILE
    top_k = min(TOPK_MAX, seq // 4)
    feat = lambda rows: pl.BlockSpec((1, rows, tq), lambda b, j: (b, 0, j))
    full = lambda s1, s2: pl.BlockSpec((1, s1, s2), lambda b, j: (b, 0, 0))
    return pl.pallas_call(
        functools.partial(_attn_kernel, top_k=top_k),
        grid=(bsz, seq // tq),
        in_specs=[feat(ATTN_WIDTH), feat(IDX_HEADS * IDX_DIM), feat(W_IDX_ROWS),
                  full(seq, KV_WIDTH), full(seq, IDX_DIM), full(KV_WIDTH, seq),
                  pl.BlockSpec((1, tq, ATTN_WIDTH), lambda b, j: (b, j, 0))],
        out_specs=pl.BlockSpec((1, tq, ATTN_WIDTH), lambda b, j: (b, j, 0)),
        out_shape=jax.ShapeDtypeStruct((bsz, seq, ATTN_WIDTH), _BF16),
        scratch_shapes=[pltpu.VMEM((seq, tq), _F32), pltpu.VMEM((seq, tq), _F32),
                        pltpu.VMEM((ATTN_HEADS // 2, 2 * HEAD_DIM, 2 * tq), _BF16),
                        pltpu.VMEM((ATTN_HEADS // 2, 1, 2 * tq), _F32),
                        pltpu.VMEM((ATTN_HEADS // 2, 1, 2 * tq), _F32),
                        pltpu.VMEM((ATTN_HEADS // 2, 1, 2 * tq), _F32),
                        pltpu.VMEM((ATTN_HEADS // 2, HEAD_DIM, 2 * tq), _F32),
                        pltpu.VMEM((ATTN_HEADS // 2, KEY_TILE, 2 * tq), _F32)],
        compiler_params=pltpu.CompilerParams(
            dimension_semantics=("parallel", "arbitrary"), vmem_limit_bytes=VMEM_LIMIT_BYTES),
        name="attn",
    )(q_t, qi_t, wi_t, k, kidx, v_t, za)


def _split3(x):
    hi = x.astype(_BF16)
    r1 = x - hi.astype(_F32)
    mid = r1.astype(_BF16)
    lo = (r1 - mid.astype(_F32)).astype(_BF16)
    return hi, mid, lo


def _ssd_kernel(xs_ref, bcs_ref, dt_ref, zb_ref, dtb_ref, alog_ref, dskip_ref, nw_ref, o_ref,
                state_scr, y_scr, dt_scr, acum_scr, acum_t_scr):
    i = pl.program_id(1)
    r = lax.broadcasted_iota(jnp.int32, (CHUNK, CHUNK), 0)
    c = lax.broadcasted_iota(jnp.int32, (CHUNK, CHUNK), 1)
    lower = c <= r

    @pl.when(i == 0)
    def _():
        state_scr[...] = jnp.zeros_like(state_scr)
        upper_b = jnp.where(r <= c, 1.0, 0.0).astype(_BF16)
        dt_in_t = dt_ref[0] + dtb_ref[...]
        dt_t = jnp.maximum(dt_in_t, 0.0) + jnp.log1p(jnp.exp(-jnp.abs(dt_in_t)))
        parts = _split3(dt_t * (-jnp.exp(alog_ref[...])))
        pad = jnp.zeros((LANES - SSM_HEADS, CHUNK), _F32)
        for n in range(dt_t.shape[1] // CHUNK):
            sl = slice(n * CHUNK, (n + 1) * CHUNK)
            a_cum_t = sum(_dot(part[:, sl], upper_b) for part in parts)
            acum_t_scr[:, sl] = a_cum_t
            acum_scr[sl, :] = jnp.concatenate([a_cum_t, pad], axis=0).T
            dt_scr[sl, :] = jnp.concatenate([dt_t[:, sl], pad], axis=0).T

    start = pl.multiple_of(i * CHUNK, CHUNK)
    dt = dt_scr[pl.ds(start, CHUNK), :]
    a_cum = acum_scr[pl.ds(start, CHUNK), :]
    a_cum_t = acum_t_scr[:, pl.ds(start, CHUNK)]

    xs = xs_ref[0].astype(_F32)
    bcs = bcs_ref[0]

    lane_hi = lax.broadcasted_iota(jnp.int32, (1, LANES), 1) >= SSM_HEAD_DIM
    dskip = dskip_ref[...]

    for grp in range(SSM_GROUPS):
        blk, half = divmod(grp, 2)
        b_blk = bcs[:, blk * LANES:(blk + 1) * LANES]
        c_blk = bcs[:, BC_WIDTH // 2 + blk * LANES: BC_WIDTH // 2 + (blk + 1) * LANES]
        in_grp = lane_hi if half else jnp.logical_not(lane_hi)
        c_g = jnp.where(in_grp, c_blk, jnp.zeros_like(c_blk))
        b_blk16 = b_blk
        cb = lax.dot_general(c_g, b_blk16, _NT, preferred_element_type=_F32)

        for pr in range(2):
            pair = grp * 2 + pr
            h_a, h_b = 2 * pair, 2 * pair + 1

            def per_pair(arr):
                return jnp.where(lane_hi, arr[:, h_b:h_b + 1], arr[:, h_a:h_a + 1])

            dt_p = per_pair(dt)
            acum_p = per_pair(a_cum)
            last_p = per_pair(a_cum[CHUNK - 1:CHUNK])
            x_p = xs[:, pair * LANES:(pair + 1) * LANES]
            xd = x_p * dt_p

            y = None
            for hh, keep_hi in ((h_a, False), (h_b, True)):
                seg = a_cum[:, hh:hh + 1] - a_cum_t[hh:hh + 1, :]
                decay = jnp.where(lower, jnp.exp(jnp.where(lower, seg, 0.0)), 0.0)
                m_h = (cb * decay).astype(_BF16)
                sel = lane_hi if keep_hi else jnp.logical_not(lane_hi)
                part = _dot(m_h, jnp.where(sel, xd, 0.0).astype(_BF16))
                y = part if y is None else y + part

            st_prev = state_scr[pair]
            y_off = _dot(c_g, st_prev.astype(_BF16)) * jnp.exp(acum_p)
            w = (xd * jnp.exp(last_p - acum_p)).astype(_BF16)
            st_new = lax.dot_general(b_blk16, w, _TN, preferred_element_type=_F32)
            state_scr[pair] = jnp.exp(last_p) * st_prev + st_new
            y_scr[:, pair * LANES:(pair + 1) * LANES] = (
                y + y_off + dskip[:, pair * LANES:(pair + 1) * LANES] * x_p)

    gw = SSM_WIDTH // SSM_GROUPS
    for grp in range(SSM_GROUPS):
        sl = slice(grp * gw, (grp + 1) * gw)
        yz = y_scr[:, sl] * _silu(zb_ref[0, :, sl])
        ms = jnp.mean(yz * yz, axis=-1, keepdims=True)
        o_ref[0, :, sl] = (yz * lax.rsqrt(ms + EPS) * nw_ref[:, sl]).astype(o_ref.dtype)


def _ssd(xs, bcs, dt_t, zb, dtb, alog, dskip, nw):
    bsz, seq, _ = xs.shape
    tok = lambda width: pl.BlockSpec((1, CHUNK, width), lambda b, i: (b, i, 0))
    consts = (dtb, alog, dskip, nw)
    return pl.pallas_call(
        _ssd_kernel,
        grid=(bsz, seq // CHUNK),
        in_specs=[tok(SSM_WIDTH), tok(BC_WIDTH), pl.BlockSpec((1, SSM_HEADS, seq), lambda b, i: (b, 0, 0)),
                  tok(SSM_WIDTH)] + [_const_spec(a.shape) for a in consts],
        out_specs=tok(SSM_WIDTH),
        out_shape=jax.ShapeDtypeStruct((bsz, seq, SSM_WIDTH), _BF16),
        scratch_shapes=[pltpu.VMEM((SSM_HEADS // 2, 2 * SSM_STATE, LANES), _F32),
                        pltpu.VMEM((CHUNK, SSM_WIDTH), _F32),
                        pltpu.VMEM((seq, LANES), _F32), pltpu.VMEM((seq, LANES), _F32),
                        pltpu.VMEM((SSM_HEADS, seq), _F32)],
        compiler_params=pltpu.CompilerParams(
            dimension_semantics=("parallel", "arbitrary"), vmem_limit_bytes=VMEM_LIMIT_BYTES),
        name="ssd",
    )(xs, bcs, dt_t, zb, *consts)


def _merge_kernel(x_ref, oa_ref, ob_ref, g_ref, gb_ref, wa_ref, wb_ref, wo_ref, fw_ref, o_ref, *, final_norm):
    gates = jax.nn.sigmoid(g_ref[...] + gb_ref[...])
    merged = (gates[:, :D_MODEL] * _dot(oa_ref[...], wa_ref[...])
              + gates[:, D_MODEL:] * _dot(ob_ref[...], wb_ref[...]))
    y = x_ref[...] + _dot(merged.astype(_BF16), wo_ref[...])
    if final_norm:
        y = y * lax.rsqrt(jnp.mean(y * y, axis=-1, keepdims=True) + EPS) * fw_ref[...]
    o_ref[...] = y


def _merge(x2, oa2, ob2, g2, gate_bias, wa, wb, wo, fw, final_norm):
    rows = x2.shape[0]
    tm = MERGE_ROWS
    tok = lambda width: pl.BlockSpec((tm, width), lambda i: (i, 0))
    consts = (gate_bias, wa, wb, wo, fw)
    return pl.pallas_call(
        functools.partial(_merge_kernel, final_norm=final_norm),
        grid=(rows // tm,),
        in_specs=[tok(D_MODEL), tok(ATTN_WIDTH), tok(SSM_WIDTH), tok(N_BRANCH * D_MODEL)]
                 + [_const_spec(a.shape) for a in consts],
        out_specs=tok(D_MODEL),
        out_shape=jax.ShapeDtypeStruct((rows, D_MODEL), _F32),
        compiler_params=pltpu.CompilerParams(
            dimension_semantics=("parallel",), vmem_limit_bytes=VMEM_LIMIT_BYTES),
        name="merge",
    )(x2, oa2, ob2, g2, *consts)


def _layer_weights(w_in):
    w16 = w_in.astype(_BF16)
    o = SPLIT_OFFSETS
    seg = lambda i: w16[:, o[i]:o[i + 1]]
    w_q, w_k, w_v, w_za, w_qi, w_ki, w_wi, w_zb, w_xb, w_b, w_c, w_dt, w_g = (seg(i) for i in range(13))
    pad = lambda w, n: jnp.pad(w, ((0, 0), (0, n - w.shape[1])))
    w_misc = pad(jnp.concatenate([w_ki, pad(w_wi, W_IDX_ROWS), w_dt], axis=1), LANES)
    w_t = jnp.concatenate([w_q, w_qi, w_k, w_misc, w_v], axis=1).T
    return w_t, w_zb, w_xb, jnp.concatenate([w_b, w_c], axis=1), w_g, w_za


def _rope_tables(positions):
    inv_freq = ROPE_THETA ** (-jnp.arange(0, ROT_DIM, 2, dtype=_F32) / ROT_DIM)
    ang = jnp.swapaxes(positions.astype(_F32)[..., None] * inv_freq, 1, 2)
    return jnp.cos(ang), jnp.sin(ang)


def _pad_lanes(v, n):
    return jnp.pad(v, (0, n - v.shape[0]))[None, :]


def kernel(x, positions, norm_w, w_in, gate_bias, conv_w, conv_b, dt_bias, a_log, d_skip,
           ssm_norm_w, w_branch_a, w_branch_b, w_out, final_norm_w):
    bsz, seq, _ = x.shape
    depth = norm_w.shape[0]
    cos_t, sin_t = _rope_tables(positions)
    for i in range(depth):
        weights = _layer_weights(w_in[i])
        conv_params = (conv_w[i][:, :SSM_WIDTH], conv_b[i][None, :SSM_WIDTH],
                       conv_w[i][:, SSM_WIDTH:], conv_b[i][None, SSM_WIDTH:])
        (q_t, qi_t, wi_t, dt_t, v_t, k, kidx, zb, xs, bcs, gates, za) = _projection(
            x, norm_w[i][None, :], cos_t, sin_t, weights, conv_params)
        o_a = _attention(q_t, qi_t, wi_t, k, kidx, v_t, za)
        o_b = _ssd(xs, bcs, dt_t, zb, dt_bias[i][:, None], a_log[i][:, None],
                   jnp.repeat(d_skip[i], SSM_HEAD_DIM)[None, :], ssm_norm_w[i][None, :])
        x = _merge(x.reshape(bsz * seq, D_MODEL), o_a.reshape(bsz * seq, ATTN_WIDTH),
                   o_b.reshape(bsz * seq, SSM_WIDTH), gates.reshape(bsz * seq, N_BRANCH * D_MODEL),
                   gate_bias[i][None, :], w_branch_a[i].astype(_BF16), w_branch_b[i].astype(_BF16),
                   w_out[i].astype(_BF16), final_norm_w[None, :],
                   final_norm=(i == depth - 1)).reshape(bsz, seq, D_MODEL)
    return x
```

```python
import functools

import numpy as np
import jax
import jax.numpy as jnp
from jax import lax
from jax.experimental import pallas as pl
from jax.experimental.pallas import tpu as pltpu

D_MODEL = 1024
ATTN_HEADS = 8
ATTN_KV_HEADS = 2
HEAD_DIM = 64
ATTN_GROUP = ATTN_HEADS // ATTN_KV_HEADS
ATTN_WIDTH = ATTN_HEADS * HEAD_DIM
KV_WIDTH = ATTN_KV_HEADS * HEAD_DIM
ROT_DIM = HEAD_DIM // 4
ROT_HALF = ROT_DIM // 2
ROPE_THETA = 500000.0
IDX_HEADS = 4
IDX_DIM = 64
TOPK_MAX = 256
SSM_HEADS = 16
SSM_HEAD_DIM = 64
SSM_WIDTH = SSM_HEADS * SSM_HEAD_DIM
SSM_GROUPS = 4
SSM_STATE = 64
CONV_K = 4
CHUNK = 128
BC_WIDTH = 2 * SSM_GROUPS * SSM_STATE
N_BRANCH = 2
EPS = 1e-6
SPLIT_SIZES = (ATTN_WIDTH, KV_WIDTH, KV_WIDTH, ATTN_WIDTH,
               IDX_HEADS * IDX_DIM, IDX_DIM, IDX_HEADS,
               SSM_WIDTH, SSM_WIDTH, SSM_GROUPS * SSM_STATE, SSM_GROUPS * SSM_STATE, SSM_HEADS,
               N_BRANCH * D_MODEL)
SPLIT_OFFSETS = tuple(int(o) for o in np.cumsum((0,) + SPLIT_SIZES))

LANES = 128
SUBLANES = 8
VMEM_LIMIT_BYTES = 56 * 1024 * 1024

PROJ_ROWS = 256
CONV_COLS = 256
CAST_ROWS = 512
Q_TILE = 256
KEY_TILE = 256
COUNT_ROWS = 64
LOG2_E = 1.4426950408889634
MERGE_ROWS = 512
W_IDX_ROWS = 8
BCDT_WIDTH = BC_WIDTH + LANES
NEG_BIG = -1e30

_F32 = jnp.float32
_BF16 = jnp.bfloat16
_NT = (((1,), (1,)), ((), ()))
_TN = (((0,), (0,)), ((), ()))


def _dot(a, b):
    return jnp.dot(a, b, preferred_element_type=_F32)


def _silu(x):
    return x * jax.nn.sigmoid(x)


def _causal_conv_silu(x, tail_ref, w_ref, b_ref):
    rows = x.shape[0]
    tail = tail_ref[...]
    sub = lax.broadcasted_iota(jnp.int32, (SUBLANES, x.shape[1]), 0)
    w = w_ref[...]
    y = x * w[CONV_K - 1:CONV_K] + b_ref[...]
    for shift in range(1, CONV_K):
        rolled = pltpu.roll(x, shift, 0)
        head = jnp.where(sub < shift, pltpu.roll(tail, shift, 0), rolled[:SUBLANES])
        shifted = jnp.concatenate([head, rolled[SUBLANES:]], axis=0)
        y = y + shifted * w[CONV_K - 1 - shift:CONV_K - shift]
    tail_ref[...] = x[rows - SUBLANES:]
    return _silu(y)


def _proj_kernel(x_ref, nw_ref, cos_t_ref, sin_t_ref,
                 w_t_ref, w_zb_ref, w_xb_ref, w_bc_ref, w_g_ref, w_za_ref,
                 cwx_ref, cbx_ref, cwbc_ref, cbbc_ref,
                 q_t_ref, qi_t_ref, wi_t_ref, dt_t_ref, v_t_ref, k_ref, kidx_ref,
                 zb_ref, xs_ref, bcs_ref, g_ref, za_ref,
                 tailx_scr, tailbc_scr):
    @pl.when(pl.program_id(1) == 0)
    def _():
        tailx_scr[...] = jnp.zeros_like(tailx_scr)
        tailbc_scr[...] = jnp.zeros_like(tailbc_scr)

    x = x_ref[0]
    h = x * lax.rsqrt(jnp.mean(x * x, axis=-1, keepdims=True) + EPS) * nw_ref[...]
    h = h.astype(_BF16)

    t = lax.dot_general(w_t_ref[...], h, _NT, preferred_element_type=_F32)
    cos_t = cos_t_ref[0]
    sin_t = sin_t_ref[0]

    def rope_head(block, hd):
        x1 = block[hd * HEAD_DIM: hd * HEAD_DIM + ROT_HALF]
        x2 = block[hd * HEAD_DIM + ROT_HALF: hd * HEAD_DIM + ROT_DIM]
        return jnp.concatenate([x1 * cos_t - x2 * sin_t, x2 * cos_t + x1 * sin_t], axis=0)

    def rope_store(block, n_heads, scale, out_ref):
        out_ref[0] = (block * scale).astype(out_ref.dtype)
        for hd in range(n_heads):
            out_ref[0, hd * HEAD_DIM: hd * HEAD_DIM + ROT_DIM, :] = (
                rope_head(block, hd) * scale).astype(out_ref.dtype)

    def rope_value(block, n_heads):
        parts = []
        for hd in range(n_heads):
            parts += [rope_head(block, hd), block[hd * HEAD_DIM + ROT_DIM:(hd + 1) * HEAD_DIM]]
        if block.shape[0] > n_heads * HEAD_DIM:
            parts.append(block[n_heads * HEAD_DIM:])
        return jnp.concatenate(parts, axis=0)

    o_qi = ATTN_WIDTH
    o_k = o_qi + IDX_HEADS * IDX_DIM
    o_ki = o_k + KV_WIDTH
    o_v = o_ki + LANES
    rope_store(t[:o_qi], ATTN_HEADS, LOG2_E * HEAD_DIM ** -0.5, q_t_ref)
    rope_store(t[o_qi:o_k], IDX_HEADS, IDX_DIM ** -0.5, qi_t_ref)
    o_wi = o_ki + IDX_DIM
    o_dt = o_wi + W_IDX_ROWS
    wi_t_ref[0] = t[o_wi:o_dt] * (IDX_HEADS ** -0.5)
    dt_t_ref[0] = t[o_dt:o_dt + SSM_HEADS]
    v_t_ref[0] = t[o_v:o_v + KV_WIDTH].astype(_BF16)
    k_ref[0] = rope_value(t[o_k:o_ki], ATTN_KV_HEADS).T.astype(_BF16)
    kidx_ref[0] = rope_value(t[o_ki:o_v], 1).T[:, :IDX_DIM].astype(_BF16)

    def plain(out_ref, w_ref, c0, width):
        out_ref[0, :, c0:c0 + width] = _dot(h, w_ref[:, c0:c0 + width])

    def conv(out_ref, w_ref, tail_scr, cw_ref, cb_ref, c0, width):
        sl = slice(c0, c0 + width)
        out_ref[0, :, sl] = _causal_conv_silu(
            _dot(h, w_ref[:, sl]), tail_scr.at[:, sl], cw_ref.at[:, sl], cb_ref.at[:, sl]).astype(_BF16)

    cc = CONV_COLS
    for n in range(SSM_WIDTH // cc):
        conv(xs_ref, w_xb_ref, tailx_scr, cwx_ref, cbx_ref, n * cc, cc)
        plain(g_ref, w_g_ref, 2 * n * cc, 2 * cc)
    for n in range(BC_WIDTH // cc):
        conv(bcs_ref, w_bc_ref, tailbc_scr, cwbc_ref, cbbc_ref, n * cc, cc)
        plain(zb_ref, w_zb_ref, 2 * n * cc, 2 * cc)
    plain(za_ref, w_za_ref, 0, ATTN_WIDTH)


def _const_spec(shape):
    nd = len(shape)
    return pl.BlockSpec(shape, lambda *_: (0,) * nd, pipeline_mode=pl.Buffered(1))


def _projection(x, nw, cos_t, sin_t, weights, conv_params):
    bsz, seq, _ = x.shape
    tm = PROJ_ROWS
    tok = lambda width: pl.BlockSpec((1, tm, width), lambda b, i: (b, i, 0))
    feat = lambda rows: pl.BlockSpec((1, rows, tm), lambda b, i: (b, 0, i))
    in_specs = [tok(D_MODEL), _const_spec((1, D_MODEL)), feat(ROT_HALF), feat(ROT_HALF)
                ] + [_const_spec(w.shape) for w in weights + conv_params]
    out_shape = (
        jax.ShapeDtypeStruct((bsz, ATTN_WIDTH, seq), _BF16),
        jax.ShapeDtypeStruct((bsz, IDX_HEADS * IDX_DIM, seq), _BF16),
        jax.ShapeDtypeStruct((bsz, W_IDX_ROWS, seq), _F32),
        jax.ShapeDtypeStruct((bsz, SSM_HEADS, seq), _F32),
        jax.ShapeDtypeStruct((bsz, KV_WIDTH, seq), _BF16),
        jax.ShapeDtypeStruct((bsz, seq, KV_WIDTH), _BF16),
        jax.ShapeDtypeStruct((bsz, seq, IDX_DIM), _BF16),
        jax.ShapeDtypeStruct((bsz, seq, SSM_WIDTH), _F32),
        jax.ShapeDtypeStruct((bsz, seq, SSM_WIDTH), _BF16),
        jax.ShapeDtypeStruct((bsz, seq, BC_WIDTH), _BF16),
        jax.ShapeDtypeStruct((bsz, seq, N_BRANCH * D_MODEL), _F32),
        jax.ShapeDtypeStruct((bsz, seq, ATTN_WIDTH), _F32),
    )
    out_specs = (feat(ATTN_WIDTH), feat(IDX_HEADS * IDX_DIM), feat(W_IDX_ROWS), feat(SSM_HEADS),
                 feat(KV_WIDTH), tok(KV_WIDTH), tok(IDX_DIM), tok(SSM_WIDTH), tok(SSM_WIDTH),
                 tok(BC_WIDTH), tok(N_BRANCH * D_MODEL), tok(ATTN_WIDTH))
    return pl.pallas_call(
        _proj_kernel,
        grid=(bsz, seq // tm),
        in_specs=in_specs,
        out_specs=out_specs,
        out_shape=out_shape,
        scratch_shapes=[pltpu.VMEM((SUBLANES, SSM_WIDTH), _F32), pltpu.VMEM((SUBLANES, BC_WIDTH), _F32)],
        compiler_params=pltpu.CompilerParams(
            dimension_semantics=("parallel", "arbitrary"), vmem_limit_bytes=VMEM_LIMIT_BYTES),
        name="proj",
    )(x, nw, cos_t, sin_t, *weights, *conv_params)


def _attn_kernel(q_t_ref, qi_t_ref, wi_t_ref, k_ref, kidx_ref, v_t_ref, za_ref, o_ref,
                 score_scr, score16_scr, bias_scr, qpad_scr, m_scr, alpha_scr, l_scr, acc_scr, s_scr,
                 *, top_k):
    j = pl.program_id(1)
    n_chunks = j + 1
    tq = Q_TILE
    q_pos = j * tq + lax.broadcasted_iota(jnp.int32, (1, tq), 1)
    key_iota = lax.broadcasted_iota(jnp.int32, (KEY_TILE, tq), 0)

    qi_t = qi_t_ref[0]
    qi_cat = jnp.concatenate([qi_t[hd * IDX_DIM:(hd + 1) * IDX_DIM] for hd in range(IDX_HEADS)], axis=1)
    wi_t = wi_t_ref[0]
    wi_cat = jnp.concatenate([wi_t[hd:hd + 1] for hd in range(IDX_HEADS)], axis=1)

    def score_chunk(c, carry):
        start = pl.multiple_of(c * KEY_TILE, KEY_TILE)
        logits = _dot(kidx_ref[0, pl.ds(start, KEY_TILE), :], qi_cat)
        weighted = jnp.maximum(logits, 0.0) * wi_cat
        score = weighted[:, 0:tq]
        for hd in range(1, IDX_HEADS):
            score = score + weighted[:, hd * tq:(hd + 1) * tq]
        score = jnp.where((start + key_iota) <= q_pos, score, -jnp.inf)
        score_scr[pl.ds(start, KEY_TILE), :] = score
        score16_scr[pl.ds(start, KEY_TILE), :] = score.astype(_BF16)
        return carry

    lax.fori_loop(0, n_chunks, score_chunk, 0)

    def code_to_f32(code):
        return pltpu.bitcast(code ^ ((code >> 31) & jnp.int32(0x7FFFFFFF)), _F32)

    def count_ge(ref, cand, dtype):
        def body(c, acc):
            start = pl.multiple_of(c * KEY_TILE, KEY_TILE)
            inc = jnp.where(ref[pl.ds(start, KEY_TILE), :] >= cand, jnp.ones((), dtype), jnp.zeros((), dtype))
            for part in range(KEY_TILE // COUNT_ROWS):
                acc = acc + inc[part * COUNT_ROWS:(part + 1) * COUNT_ROWS]
            return acc
        acc = lax.fori_loop(0, n_chunks, body, jnp.zeros((COUNT_ROWS, tq), dtype))
        return jnp.sum(acc.astype(_F32), axis=0, keepdims=True)

    def bf16_code(block):
        return jnp.where(block >= 0, block, block | jnp.int32(0xFFFF))

    def coarse_bit(i, block):
        cand = block + (jnp.int32(1) << (31 - i))
        hit = count_ge(score16_scr, code_to_f32(bf16_code(cand)).astype(_BF16), _BF16) >= top_k
        return jnp.where(hit, cand, block)

    int_min = jnp.int32(-2 ** 31)
    coarse = lax.fori_loop(0, 16, coarse_bit, jnp.full((1, tq), int_min, jnp.int32))
    base = bf16_code(coarse) - jnp.int32(1 << 16)

    def fine_bit(i, code):
        cand = code + (jnp.int32(1) << (16 - i))
        hit = count_ge(score_scr, code_to_f32(cand), _F32) >= top_k
        return jnp.where(hit, cand, code)

    fine = lax.fori_loop(0, 17, fine_bit, base)
    few = (q_pos + 1) <= top_k
    thr = jnp.where(few, -jnp.inf, code_to_f32(fine))

    def above_chunk(c, acc):
        start = pl.multiple_of(c * KEY_TILE, KEY_TILE)
        inc = jnp.where(score_scr[pl.ds(start, KEY_TILE), :] > thr, 1.0, 0.0)
        return acc + inc.reshape(KEY_TILE // COUNT_ROWS, COUNT_ROWS, tq).sum(axis=0)

    n_above = jnp.sum(lax.fori_loop(0, n_chunks, above_chunk, jnp.zeros((COUNT_ROWS, tq), _F32)),
                      axis=0, keepdims=True)
    n_ties_kept = jnp.where(few, 0.0, top_k - n_above)

    row = lax.broadcasted_iota(jnp.int32, (KEY_TILE, KEY_TILE), 0)
    col = lax.broadcasted_iota(jnp.int32, (KEY_TILE, KEY_TILE), 1)
    strict_lower = jnp.where(col < row, 1.0, 0.0).astype(_BF16)

    def bias_tile(c, ties_before):
        start = pl.multiple_of(c * KEY_TILE, KEY_TILE)
        s = score_scr[pl.ds(start, KEY_TILE), :]
        tie = jnp.where(s == thr, 1.0, 0.0)
        rank = _dot(strict_lower, tie.astype(_BF16)) + ties_before
        keep = (s > thr) | ((s == thr) & (rank < n_ties_kept))
        bias_scr[...] = jnp.where(keep, 0.0, NEG_BIG)
        return ties_before + jnp.sum(tie, axis=0, keepdims=True)

    n_pairs = ATTN_HEADS // 2
    zeros_half = jnp.zeros((HEAD_DIM, 2 * tq), _BF16)
    for pair in range(n_pairs):
        g = (2 * pair) // ATTN_GROUP
        q_cat = jnp.concatenate([q_t_ref[0, (2 * pair + hd) * HEAD_DIM:(2 * pair + hd + 1) * HEAD_DIM, :]
                                 for hd in range(2)], axis=1)
        qpad_scr[pair] = jnp.concatenate([q_cat, zeros_half] if g == 0 else [zeros_half, q_cat], axis=0)
    m_scr[...] = jnp.full(m_scr.shape, NEG_BIG, _F32)
    l_scr[...] = jnp.zeros(l_scr.shape, _F32)
    acc_scr[...] = jnp.zeros(acc_scr.shape, _F32)

    def logits_stage(c, pair):
        start = pl.multiple_of(c * KEY_TILE, KEY_TILE)
        b = bias_scr[...]
        s = _dot(k_ref[0, pl.ds(start, KEY_TILE), :], qpad_scr[pair]) + jnp.concatenate([b, b], axis=1)
        m_old = m_scr[pair]
        m_new = jnp.maximum(m_old, jnp.max(s, axis=0, keepdims=True))
        s_scr[pair] = s
        alpha_scr[pair] = jnp.exp2(m_old - m_new)
        m_scr[pair] = m_new

    def value_stage(c, pair):
        start = pl.multiple_of(c * KEY_TILE, KEY_TILE)
        g = (2 * pair) // ATTN_GROUP
        p = jnp.exp2(s_scr[pair] - m_scr[pair])
        alpha = alpha_scr[pair]
        l_scr[pair] = alpha * l_scr[pair] + jnp.sum(p, axis=0, keepdims=True)
        v_c = v_t_ref[0, g * HEAD_DIM:(g + 1) * HEAD_DIM, pl.ds(start, KEY_TILE)]
        acc_scr[pair] = alpha * acc_scr[pair] + _dot(v_c, p.astype(_BF16))

    ties = bias_tile(0, jnp.zeros((1, tq), _F32))
    for pair in range(n_pairs):
        logits_stage(0, pair)

    def attn_tile(c, ties):
        ties = bias_tile(c + 1, ties)
        for pair in range(n_pairs):
            value_stage(c, pair)
            logits_stage(c + 1, pair)
        return ties

    lax.fori_loop(0, n_chunks - 1, attn_tile, ties)
    for pair in range(n_pairs):
        value_stage(n_chunks - 1, pair)

    for pair in range(n_pairs):
        o_t = acc_scr[pair] / l_scr[pair]
        both = jnp.concatenate([o_t[:, :tq], o_t[:, tq:]], axis=0)
        z = za_ref[0, :, pair * LANES:(pair + 1) * LANES]
        o_ref[0, :, pair * LANES:(pair + 1) * LANES] = (both.T * _silu(z)).astype(o_ref.dtype)


def _attention(q_t, qi_t, wi_t, k, kidx, v_t, za):
    bsz, seq, _ = k.shape
    tq = Q_TILE
    top_k = min(TOPK_MAX, seq // 4)
    feat = lambda rows: pl.BlockSpec((1, rows, tq), lambda b, j: (b, 0, j))
    full = lambda s1, s2: pl.BlockSpec((1, s1, s2), lambda b, j: (b, 0, 0))
    return pl.pallas_call(
        functools.partial(_attn_kernel, top_k=top_k),
        grid=(bsz, seq // tq),
        in_specs=[feat(ATTN_WIDTH), feat(IDX_HEADS * IDX_DIM), feat(W_IDX_ROWS),
                  full(seq, KV_WIDTH), full(seq, IDX_DIM), full(KV_WIDTH, seq),
                  pl.BlockSpec((1, tq, ATTN_WIDTH), lambda b, j: (b, j, 0))],
        out_specs=pl.BlockSpec((1, tq, ATTN_WIDTH), lambda b, j: (b, j, 0)),
        out_shape=jax.ShapeDtypeStruct((bsz, seq, ATTN_WIDTH), _BF16),
        scratch_shapes=[pltpu.VMEM((seq, tq), _F32), pltpu.VMEM((seq, tq), _BF16),
                        pltpu.VMEM((KEY_TILE, tq), _F32),
                        pltpu.VMEM((ATTN_HEADS // 2, 2 * HEAD_DIM, 2 * tq), _BF16),
                        pltpu.VMEM((ATTN_HEADS // 2, 1, 2 * tq), _F32),
                        pltpu.VMEM((ATTN_HEADS // 2, 1, 2 * tq), _F32),
                        pltpu.VMEM((ATTN_HEADS // 2, 1, 2 * tq), _F32),
                        pltpu.VMEM((ATTN_HEADS // 2, HEAD_DIM, 2 * tq), _F32),
                        pltpu.VMEM((ATTN_HEADS // 2, KEY_TILE, 2 * tq), _F32)],
        compiler_params=pltpu.CompilerParams(
            dimension_semantics=("parallel", "arbitrary"), vmem_limit_bytes=VMEM_LIMIT_BYTES),
        name="attn",
    )(q_t, qi_t, wi_t, k, kidx, v_t, za)


def _split3(x):
    hi = x.astype(_BF16)
    r1 = x - hi.astype(_F32)
    mid = r1.astype(_BF16)
    lo = (r1 - mid.astype(_F32)).astype(_BF16)
    return hi, mid, lo


def _ssd_kernel(xs_ref, bcs_ref, dt_ref, zb_ref, dtb_ref, alog_ref, dskip_ref, nw_ref, o_ref,
                state_scr, y_scr, dt_scr, acum_scr, acum_t_scr):
    i = pl.program_id(1)
    r = lax.broadcasted_iota(jnp.int32, (CHUNK, CHUNK), 0)
    c = lax.broadcasted_iota(jnp.int32, (CHUNK, CHUNK), 1)
    lower = c <= r

    @pl.when(i == 0)
    def _():
        state_scr[...] = jnp.zeros_like(state_scr)
        upper_b = jnp.where(r <= c, 1.0, 0.0).astype(_BF16)
        dt_in_t = dt_ref[0] + dtb_ref[...]
        dt_t = jnp.maximum(dt_in_t, 0.0) + jnp.log1p(jnp.exp(-jnp.abs(dt_in_t)))
        parts = _split3(dt_t * (-jnp.exp(alog_ref[...])))
        pad = jnp.zeros((LANES - SSM_HEADS, CHUNK), _F32)
        for n in range(dt_t.shape[1] // CHUNK):
            sl = slice(n * CHUNK, (n + 1) * CHUNK)
            a_cum_t = sum(_dot(part[:, sl], upper_b) for part in parts)
            acum_t_scr[:, sl] = a_cum_t
            acum_scr[sl, :] = jnp.concatenate([a_cum_t, pad], axis=0).T
            dt_scr[sl, :] = jnp.concatenate([dt_t[:, sl], pad], axis=0).T

    start = pl.multiple_of(i * CHUNK, CHUNK)
    dt = dt_scr[pl.ds(start, CHUNK), :]
    a_cum = acum_scr[pl.ds(start, CHUNK), :]
    a_cum_t = acum_t_scr[:, pl.ds(start, CHUNK)]

    xs = xs_ref[0].astype(_F32)
    bcs = bcs_ref[0]

    lane_hi = lax.broadcasted_iota(jnp.int32, (1, LANES), 1) >= SSM_HEAD_DIM
    dskip = dskip_ref[...]

    for grp in range(SSM_GROUPS):
        blk, half = divmod(grp, 2)
        b_blk = bcs[:, blk * LANES:(blk + 1) * LANES]
        c_blk = bcs[:, BC_WIDTH // 2 + blk * LANES: BC_WIDTH // 2 + (blk + 1) * LANES]
        in_grp = lane_hi if half else jnp.logical_not(lane_hi)
        c_g = jnp.where(in_grp, c_blk, jnp.zeros_like(c_blk))
        b_blk16 = b_blk
        cb = lax.dot_general(c_g, b_blk16, _NT, preferred_element_type=_F32)

        for pr in range(2):
            pair = grp * 2 + pr
            h_a, h_b = 2 * pair, 2 * pair + 1

            def per_pair(arr):
                return jnp.where(lane_hi, arr[:, h_b:h_b + 1], arr[:, h_a:h_a + 1])

            dt_p = per_pair(dt)
            acum_p = per_pair(a_cum)
            last_p = per_pair(a_cum[CHUNK - 1:CHUNK])
            x_p = xs[:, pair * LANES:(pair + 1) * LANES]
            xd = x_p * dt_p

            y = None
            for hh, keep_hi in ((h_a, False), (h_b, True)):
                seg = a_cum[:, hh:hh + 1] - a_cum_t[hh:hh + 1, :]
                decay = jnp.where(lower, jnp.exp(jnp.where(lower, seg, 0.0)), 0.0)
                m_h = (cb * decay).astype(_BF16)
                sel = lane_hi if keep_hi else jnp.logical_not(lane_hi)
                part = _dot(m_h, jnp.where(sel, xd, 0.0).astype(_BF16))
                y = part if y is None else y + part

            st_prev = state_scr[pair]
            y_off = _dot(c_g, st_prev.astype(_BF16)) * jnp.exp(acum_p)
            w = (xd * jnp.exp(last_p - acum_p)).astype(_BF16)
            st_new = lax.dot_general(b_blk16, w, _TN, preferred_element_type=_F32)
            state_scr[pair] = jnp.exp(last_p) * st_prev + st_new
            y_scr[:, pair * LANES:(pair + 1) * LANES] = (
                y + y_off + dskip[:, pair * LANES:(pair + 1) * LANES] * x_p)

    gw = SSM_WIDTH // SSM_GROUPS
    for grp in range(SSM_GROUPS):
        sl = slice(grp * gw, (grp + 1) * gw)
        yz = y_scr[:, sl] * _silu(zb_ref[0, :, sl])
        ms = jnp.mean(yz * yz, axis=-1, keepdims=True)
        o_ref[0, :, sl] = (yz * lax.rsqrt(ms + EPS) * nw_ref[:, sl]).astype(o_ref.dtype)


def _ssd(xs, bcs, dt_t, zb, dtb, alog, dskip, nw):
    bsz, seq, _ = xs.shape
    tok = lambda width: pl.BlockSpec((1, CHUNK, width), lambda b, i: (b, i, 0))
    consts = (dtb, alog, dskip, nw)
    return pl.pallas_call(
        _ssd_kernel,
        grid=(bsz, seq // CHUNK),
        in_specs=[tok(SSM_WIDTH), tok(BC_WIDTH), pl.BlockSpec((1, SSM_HEADS, seq), lambda b, i: (b, 0, 0)),
                  tok(SSM_WIDTH)] + [_const_spec(a.shape) for a in consts],
        out_specs=tok(SSM_WIDTH),
        out_shape=jax.ShapeDtypeStruct((bsz, seq, SSM_WIDTH), _BF16),
        scratch_shapes=[pltpu.VMEM((SSM_HEADS // 2, 2 * SSM_STATE, LANES), _F32),
                        pltpu.VMEM((CHUNK, SSM_WIDTH), _F32),
                        pltpu.VMEM((seq, LANES), _F32), pltpu.VMEM((seq, LANES), _F32),
                        pltpu.VMEM((SSM_HEADS, seq), _F32)],
        compiler_params=pltpu.CompilerParams(
            dimension_semantics=("parallel", "arbitrary"), vmem_limit_bytes=VMEM_LIMIT_BYTES),
        name="ssd",
    )(xs, bcs, dt_t, zb, *consts)


def _merge_kernel(x_ref, oa_ref, ob_ref, g_ref, gb_ref, wa_ref, wb_ref, wo_ref, fw_ref, o_ref, *, final_norm):
    gates = jax.nn.sigmoid(g_ref[...] + gb_ref[...])
    merged = (gates[:, :D_MODEL] * _dot(oa_ref[...], wa_ref[...])
              + gates[:, D_MODEL:] * _dot(ob_ref[...], wb_ref[...]))
    y = x_ref[...] + _dot(merged.astype(_BF16), wo_ref[...])
    if final_norm:
        y = y * lax.rsqrt(jnp.mean(y * y, axis=-1, keepdims=True) + EPS) * fw_ref[...]
    o_ref[...] = y


def _merge(x2, oa2, ob2, g2, gate_bias, wa, wb, wo, fw, final_norm):
    rows = x2.shape[0]
    tm = MERGE_ROWS
    tok = lambda width: pl.BlockSpec((tm, width), lambda i: (i, 0))
    consts = (gate_bias, wa, wb, wo, fw)
    return pl.pallas_call(
        functools.partial(_merge_kernel, final_norm=final_norm),
        grid=(rows // tm,),
        in_specs=[tok(D_MODEL), tok(ATTN_WIDTH), tok(SSM_WIDTH), tok(N_BRANCH * D_MODEL)]
                 + [_const_spec(a.shape) for a in consts],
        out_specs=tok(D_MODEL),
        out_shape=jax.ShapeDtypeStruct((rows, D_MODEL), _F32),
        compiler_params=pltpu.CompilerParams(
            dimension_semantics=("parallel",), vmem_limit_bytes=VMEM_LIMIT_BYTES),
        name="merge",
    )(x2, oa2, ob2, g2, *consts)


def _cast_kernel(w_ref, o_ref):
    o_ref[...] = w_ref[0].astype(o_ref.dtype)


def _layer_transposed_bf16(w_all, layer):
    w_t_all = jnp.swapaxes(w_all, 1, 2)
    _, rows, cols = w_t_all.shape
    tm = CAST_ROWS
    return pl.pallas_call(
        _cast_kernel,
        grid=(pl.cdiv(rows, tm),),
        in_specs=[pl.BlockSpec((1, tm, cols), lambda i: (layer, i, 0))],
        out_specs=pl.BlockSpec((tm, cols), lambda i: (i, 0)),
        out_shape=jax.ShapeDtypeStruct((rows, cols), _BF16),
        compiler_params=pltpu.CompilerParams(dimension_semantics=("parallel",)),
        name="cast",
    )(w_t_all)


def _layer_weights(w_in_all, layer):
    wt16 = _layer_transposed_bf16(w_in_all, layer)
    o = SPLIT_OFFSETS
    seg = lambda i: wt16[o[i]:o[i + 1]]
    w_q, w_k, w_v, w_za, w_qi, w_ki, w_wi, w_zb, w_xb, w_b, w_c, w_dt, w_g = (seg(i) for i in range(13))
    pad = lambda w, n: jnp.pad(w, ((0, n - w.shape[0]), (0, 0)))
    w_misc = pad(jnp.concatenate([w_ki, pad(w_wi, W_IDX_ROWS), w_dt], axis=0), LANES)
    w_t = jnp.concatenate([w_q, w_qi, w_k, w_misc, w_v], axis=0)
    return w_t, w_zb.T, w_xb.T, jnp.concatenate([w_b, w_c], axis=0).T, w_g.T, w_za.T


def _rope_tables(positions):
    inv_freq = ROPE_THETA ** (-jnp.arange(0, ROT_DIM, 2, dtype=_F32) / ROT_DIM)
    ang = jnp.swapaxes(positions.astype(_F32)[..., None] * inv_freq, 1, 2)
    return jnp.cos(ang), jnp.sin(ang)


def _pad_lanes(v, n):
    return jnp.pad(v, (0, n - v.shape[0]))[None, :]


def kernel(x, positions, norm_w, w_in, gate_bias, conv_w, conv_b, dt_bias, a_log, d_skip,
           ssm_norm_w, w_branch_a, w_branch_b, w_out, final_norm_w):
    bsz, seq, _ = x.shape
    depth = norm_w.shape[0]
    cos_t, sin_t = _rope_tables(positions)
    for i in range(depth):
        weights = _layer_weights(w_in, i)
        conv_params = (conv_w[i][:, :SSM_WIDTH], conv_b[i][None, :SSM_WIDTH],
                       conv_w[i][:, SSM_WIDTH:], conv_b[i][None, SSM_WIDTH:])
        (q_t, qi_t, wi_t, dt_t, v_t, k, kidx, zb, xs, bcs, gates, za) = _projection(
            x, norm_w[i][None, :], cos_t, sin_t, weights, conv_params)
        o_a = _attention(q_t, qi_t, wi_t, k, kidx, v_t, za)
        o_b = _ssd(xs, bcs, dt_t, zb, dt_bias[i][:, None], a_log[i][:, None],
                   jnp.repeat(d_skip[i], SSM_HEAD_DIM)[None, :], ssm_norm_w[i][None, :])
        x = _merge(x.reshape(bsz * seq, D_MODEL), o_a.reshape(bsz * seq, ATTN_WIDTH),
                   o_b.reshape(bsz * seq, SSM_WIDTH), gates.reshape(bsz * seq, N_BRANCH * D_MODEL),
                   gate_bias[i][None, :], w_branch_a[i].astype(_BF16), w_branch_b[i].astype(_BF16),
                   w_out[i].astype(_BF16), final_norm_w[None, :],
                   final_norm=(i == depth - 1)).reshape(bsz, seq, D_MODEL)
    return x
```

```python
import functools

import numpy as np
import jax
import jax.numpy as jnp
from jax import lax
from jax.experimental import pallas as pl
from jax.experimental.pallas import tpu as pltpu

D_MODEL = 1024
ATTN_HEADS = 8
ATTN_KV_HEADS = 2
HEAD_DIM = 64
ATTN_GROUP = ATTN_HEADS // ATTN_KV_HEADS
ATTN_WIDTH = ATTN_HEADS * HEAD_DIM
KV_WIDTH = ATTN_KV_HEADS * HEAD_DIM
ROT_DIM = HEAD_DIM // 4
ROT_HALF = ROT_DIM // 2
ROPE_THETA = 500000.0
IDX_HEADS = 4
IDX_DIM = 64
TOPK_MAX = 256
SSM_HEADS = 16
SSM_HEAD_DIM = 64
SSM_WIDTH = SSM_HEADS * SSM_HEAD_DIM
SSM_GROUPS = 4
SSM_STATE = 64
CONV_K = 4
CHUNK = 128
BC_WIDTH = 2 * SSM_GROUPS * SSM_STATE
N_BRANCH = 2
EPS = 1e-6
SPLIT_SIZES = (ATTN_WIDTH, KV_WIDTH, KV_WIDTH, ATTN_WIDTH,
               IDX_HEADS * IDX_DIM, IDX_DIM, IDX_HEADS,
               SSM_WIDTH, SSM_WIDTH, SSM_GROUPS * SSM_STATE, SSM_GROUPS * SSM_STATE, SSM_HEADS,
               N_BRANCH * D_MODEL)
SPLIT_OFFSETS = tuple(int(o) for o in np.cumsum((0,) + SPLIT_SIZES))

LANES = 128
SUBLANES = 8
VMEM_LIMIT_BYTES = 56 * 1024 * 1024

PROJ_ROWS = 256
CONV_COLS = 256
CAST_ROWS = 512
Q_TILE = 256
KEY_TILE = 256
COUNT_ROWS = 64
LOG2_E = 1.4426950408889634
MERGE_ROWS = 512
W_IDX_ROWS = 8
BCDT_WIDTH = BC_WIDTH + LANES
NEG_BIG = -1e30

_F32 = jnp.float32
_BF16 = jnp.bfloat16
_NT = (((1,), (1,)), ((), ()))
_TN = (((0,), (0,)), ((), ()))


def _dot(a, b):
    return jnp.dot(a, b, preferred_element_type=_F32)


def _silu(x):
    return x * jax.nn.sigmoid(x)


def _causal_conv_silu(x, tail_ref, w_ref, b_ref):
    rows = x.shape[0]
    tail = tail_ref[...]
    sub = lax.broadcasted_iota(jnp.int32, (SUBLANES, x.shape[1]), 0)
    w = w_ref[...]
    y = x * w[CONV_K - 1:CONV_K] + b_ref[...]
    for shift in range(1, CONV_K):
        rolled = pltpu.roll(x, shift, 0)
        head = jnp.where(sub < shift, pltpu.roll(tail, shift, 0), rolled[:SUBLANES])
        shifted = jnp.concatenate([head, rolled[SUBLANES:]], axis=0)
        y = y + shifted * w[CONV_K - 1 - shift:CONV_K - shift]
    tail_ref[...] = x[rows - SUBLANES:]
    return _silu(y)


def _proj_kernel(x_ref, nw_ref, cos_t_ref, sin_t_ref,
                 w_t_ref, w_zb_ref, w_xb_ref, w_bc_ref, w_g_ref, w_za_ref,
                 cwx_ref, cbx_ref, cwbc_ref, cbbc_ref,
                 q_t_ref, qi_t_ref, wi_t_ref, dt_t_ref, v_t_ref, k_ref, kidx_ref,
                 zb_ref, xs_ref, bcs_ref, g_ref, za_ref,
                 tailx_scr, tailbc_scr):
    @pl.when(pl.program_id(1) == 0)
    def _():
        tailx_scr[...] = jnp.zeros_like(tailx_scr)
        tailbc_scr[...] = jnp.zeros_like(tailbc_scr)

    x = x_ref[0]
    h = x * lax.rsqrt(jnp.mean(x * x, axis=-1, keepdims=True) + EPS) * nw_ref[...]
    h = h.astype(_BF16)

    t = lax.dot_general(w_t_ref[...], h, _NT, preferred_element_type=_F32)
    cos_t = cos_t_ref[0]
    sin_t = sin_t_ref[0]

    def rope_head(block, hd):
        x1 = block[hd * HEAD_DIM: hd * HEAD_DIM + ROT_HALF]
        x2 = block[hd * HEAD_DIM + ROT_HALF: hd * HEAD_DIM + ROT_DIM]
        return jnp.concatenate([x1 * cos_t - x2 * sin_t, x2 * cos_t + x1 * sin_t], axis=0)

    def rope_store(block, n_heads, scale, out_ref):
        out_ref[0] = (block * scale).astype(out_ref.dtype)
        for hd in range(n_heads):
            out_ref[0, hd * HEAD_DIM: hd * HEAD_DIM + ROT_DIM, :] = (
                rope_head(block, hd) * scale).astype(out_ref.dtype)

    def rope_value(block, n_heads):
        parts = []
        for hd in range(n_heads):
            parts += [rope_head(block, hd), block[hd * HEAD_DIM + ROT_DIM:(hd + 1) * HEAD_DIM]]
        if block.shape[0] > n_heads * HEAD_DIM:
            parts.append(block[n_heads * HEAD_DIM:])
        return jnp.concatenate(parts, axis=0)

    o_qi = ATTN_WIDTH
    o_k = o_qi + IDX_HEADS * IDX_DIM
    o_ki = o_k + KV_WIDTH
    o_v = o_ki + LANES
    rope_store(t[:o_qi], ATTN_HEADS, LOG2_E * HEAD_DIM ** -0.5, q_t_ref)
    rope_store(t[o_qi:o_k], IDX_HEADS, IDX_DIM ** -0.5, qi_t_ref)
    o_wi = o_ki + IDX_DIM
    o_dt = o_wi + W_IDX_ROWS
    wi_t_ref[0] = t[o_wi:o_dt] * (IDX_HEADS ** -0.5)
    dt_t_ref[0] = t[o_dt:o_dt + SSM_HEADS]
    v_t_ref[0] = t[o_v:o_v + KV_WIDTH].astype(_BF16)
    k_ref[0] = rope_value(t[o_k:o_ki], ATTN_KV_HEADS).T.astype(_BF16)
    kidx_ref[0] = rope_value(t[o_ki:o_v], 1).T[:, :IDX_DIM].astype(_BF16)

    def plain(out_ref, w_ref, c0, width):
        out_ref[0, :, c0:c0 + width] = _dot(h, w_ref[:, c0:c0 + width]).astype(out_ref.dtype)

    def conv(out_ref, w_ref, tail_scr, cw_ref, cb_ref, c0, width):
        sl = slice(c0, c0 + width)
        out_ref[0, :, sl] = _causal_conv_silu(
            _dot(h, w_ref[:, sl]), tail_scr.at[:, sl], cw_ref.at[:, sl], cb_ref.at[:, sl]).astype(_BF16)

    cc = CONV_COLS
    for n in range(SSM_WIDTH // cc):
        conv(xs_ref, w_xb_ref, tailx_scr, cwx_ref, cbx_ref, n * cc, cc)
        plain(g_ref, w_g_ref, 2 * n * cc, 2 * cc)
    for n in range(BC_WIDTH // cc):
        conv(bcs_ref, w_bc_ref, tailbc_scr, cwbc_ref, cbbc_ref, n * cc, cc)
        plain(zb_ref, w_zb_ref, 2 * n * cc, 2 * cc)
    plain(za_ref, w_za_ref, 0, ATTN_WIDTH)


def _const_spec(shape):
    nd = len(shape)
    return pl.BlockSpec(shape, lambda *_: (0,) * nd, pipeline_mode=pl.Buffered(1))


def _projection(x, nw, cos_t, sin_t, weights, conv_params):
    bsz, seq, _ = x.shape
    tm = PROJ_ROWS
    tok = lambda width: pl.BlockSpec((1, tm, width), lambda b, i: (b, i, 0))
    feat = lambda rows: pl.BlockSpec((1, rows, tm), lambda b, i: (b, 0, i))
    in_specs = [tok(D_MODEL), _const_spec((1, D_MODEL)), feat(ROT_HALF), feat(ROT_HALF)
                ] + [_const_spec(w.shape) for w in weights + conv_params]
    out_shape = (
        jax.ShapeDtypeStruct((bsz, ATTN_WIDTH, seq), _BF16),
        jax.ShapeDtypeStruct((bsz, IDX_HEADS * IDX_DIM, seq), _BF16),
        jax.ShapeDtypeStruct((bsz, W_IDX_ROWS, seq), _F32),
        jax.ShapeDtypeStruct((bsz, SSM_HEADS, seq), _F32),
        jax.ShapeDtypeStruct((bsz, KV_WIDTH, seq), _BF16),
        jax.ShapeDtypeStruct((bsz, seq, KV_WIDTH), _BF16),
        jax.ShapeDtypeStruct((bsz, seq, IDX_DIM), _BF16),
        jax.ShapeDtypeStruct((bsz, seq, SSM_WIDTH), _F32),
        jax.ShapeDtypeStruct((bsz, seq, SSM_WIDTH), _BF16),
        jax.ShapeDtypeStruct((bsz, seq, BC_WIDTH), _BF16),
        jax.ShapeDtypeStruct((bsz, seq, N_BRANCH * D_MODEL), _BF16),
        jax.ShapeDtypeStruct((bsz, seq, ATTN_WIDTH), _F32),
    )
    out_specs = (feat(ATTN_WIDTH), feat(IDX_HEADS * IDX_DIM), feat(W_IDX_ROWS), feat(SSM_HEADS),
                 feat(KV_WIDTH), tok(KV_WIDTH), tok(IDX_DIM), tok(SSM_WIDTH), tok(SSM_WIDTH),
                 tok(BC_WIDTH), tok(N_BRANCH * D_MODEL), tok(ATTN_WIDTH))
    return pl.pallas_call(
        _proj_kernel,
        grid=(bsz, seq // tm),
        in_specs=in_specs,
        out_specs=out_specs,
        out_shape=out_shape,
        scratch_shapes=[pltpu.VMEM((SUBLANES, SSM_WIDTH), _F32), pltpu.VMEM((SUBLANES, BC_WIDTH), _F32)],
        compiler_params=pltpu.CompilerParams(
            dimension_semantics=("parallel", "arbitrary"), vmem_limit_bytes=VMEM_LIMIT_BYTES),
        name="proj",
    )(x, nw, cos_t, sin_t, *weights, *conv_params)


def _attn_kernel(q_t_ref, qi_t_ref, wi_t_ref, k_ref, kidx_ref, v_t_ref, za_ref, o_ref,
                 score_scr, score16_scr, bias_scr, qpad_scr, m_scr, alpha_scr, l_scr, acc_scr, s_scr,
                 *, top_k):
    j = pl.program_id(1)
    n_chunks = j + 1
    tq = Q_TILE
    q_pos = j * tq + lax.broadcasted_iota(jnp.int32, (1, tq), 1)
    key_iota = lax.broadcasted_iota(jnp.int32, (KEY_TILE, tq), 0)

    qi_t = qi_t_ref[0]
    qi_cat = jnp.concatenate([qi_t[hd * IDX_DIM:(hd + 1) * IDX_DIM] for hd in range(IDX_HEADS)], axis=1)
    wi_t = wi_t_ref[0]
    wi_cat = jnp.concatenate([wi_t[hd:hd + 1] for hd in range(IDX_HEADS)], axis=1)

    def score_chunk(c, carry):
        start = pl.multiple_of(c * KEY_TILE, KEY_TILE)
        logits = _dot(kidx_ref[0, pl.ds(start, KEY_TILE), :], qi_cat)
        weighted = jnp.maximum(logits, 0.0) * wi_cat
        score = weighted[:, 0:tq]
        for hd in range(1, IDX_HEADS):
            score = score + weighted[:, hd * tq:(hd + 1) * tq]
        score = jnp.where((start + key_iota) <= q_pos, score, -jnp.inf)
        score_scr[pl.ds(start, KEY_TILE), :] = score
        score16_scr[pl.ds(start, KEY_TILE), :] = score.astype(_BF16)
        return carry

    lax.fori_loop(0, n_chunks, score_chunk, 0)

    def code_to_f32(code):
        return pltpu.bitcast(code ^ ((code >> 31) & jnp.int32(0x7FFFFFFF)), _F32)

    def count_ge(ref, cand, dtype):
        def body(c, acc):
            start = pl.multiple_of(c * KEY_TILE, KEY_TILE)
            inc = jnp.where(ref[pl.ds(start, KEY_TILE), :] >= cand, jnp.ones((), dtype), jnp.zeros((), dtype))
            for part in range(KEY_TILE // COUNT_ROWS):
                acc = acc + inc[part * COUNT_ROWS:(part + 1) * COUNT_ROWS]
            return acc
        acc = lax.fori_loop(0, n_chunks, body, jnp.zeros((COUNT_ROWS, tq), dtype))
        return jnp.sum(acc.astype(_F32), axis=0, keepdims=True)

    def bf16_code(block):
        return jnp.where(block >= 0, block, block | jnp.int32(0xFFFF))

    def coarse_bit(i, block):
        cand = block + (jnp.int32(1) << (31 - i))
        hit = count_ge(score16_scr, code_to_f32(bf16_code(cand)).astype(_BF16), _BF16) >= top_k
        return jnp.where(hit, cand, block)

    int_min = jnp.int32(-2 ** 31)
    coarse = lax.fori_loop(0, 16, coarse_bit, jnp.full((1, tq), int_min, jnp.int32))
    base = bf16_code(coarse) - jnp.int32(1 << 16)

    def fine_bit(i, carry):
        code, n_above = carry
        cand = code + (jnp.int32(1) << (16 - i))
        cnt = count_ge(score_scr, code_to_f32(cand), _F32)
        hit = cnt >= top_k
        return jnp.where(hit, cand, code), jnp.where(hit, n_above, cnt)

    fine, n_above = lax.fori_loop(0, 17, fine_bit, (base, jnp.zeros((1, tq), _F32)))
    few = (q_pos + 1) <= top_k
    thr = jnp.where(few, -jnp.inf, code_to_f32(fine))
    n_ties_kept = jnp.where(few, 0.0, top_k - n_above)

    row = lax.broadcasted_iota(jnp.int32, (KEY_TILE, KEY_TILE), 0)
    col = lax.broadcasted_iota(jnp.int32, (KEY_TILE, KEY_TILE), 1)
    strict_lower = jnp.where(col < row, 1.0, 0.0).astype(_BF16)

    def bias_tile(c, ties_before):
        start = pl.multiple_of(c * KEY_TILE, KEY_TILE)
        s = score_scr[pl.ds(start, KEY_TILE), :]
        tie = jnp.where(s == thr, 1.0, 0.0)
        rank = _dot(strict_lower, tie.astype(_BF16)) + ties_before
        keep = (s > thr) | ((s == thr) & (rank < n_ties_kept))
        bias_scr[...] = jnp.where(keep, 0.0, NEG_BIG)
        return ties_before + jnp.sum(tie, axis=0, keepdims=True)

    n_pairs = ATTN_HEADS // 2
    zeros_half = jnp.zeros((HEAD_DIM, 2 * tq), _BF16)
    for pair in range(n_pairs):
        g = (2 * pair) // ATTN_GROUP
        q_cat = jnp.concatenate([q_t_ref[0, (2 * pair + hd) * HEAD_DIM:(2 * pair + hd + 1) * HEAD_DIM, :]
                                 for hd in range(2)], axis=1)
        qpad_scr[pair] = jnp.concatenate([q_cat, zeros_half] if g == 0 else [zeros_half, q_cat], axis=0)
    m_scr[...] = jnp.full(m_scr.shape, NEG_BIG, _F32)
    l_scr[...] = jnp.zeros(l_scr.shape, _F32)
    acc_scr[...] = jnp.zeros(acc_scr.shape, _F32)

    def logits_stage(c, pair):
        start = pl.multiple_of(c * KEY_TILE, KEY_TILE)
        b = bias_scr[...]
        s = _dot(k_ref[0, pl.ds(start, KEY_TILE), :], qpad_scr[pair]) + jnp.concatenate([b, b], axis=1)
        m_old = m_scr[pair]
        m_new = jnp.maximum(m_old, jnp.max(s, axis=0, keepdims=True))
        s_scr[pair] = s
        alpha_scr[pair] = jnp.exp2(m_old - m_new)
        m_scr[pair] = m_new

    def value_stage(c, pair):
        start = pl.multiple_of(c * KEY_TILE, KEY_TILE)
        g = (2 * pair) // ATTN_GROUP
        p = jnp.exp2(s_scr[pair] - m_scr[pair])
        alpha = alpha_scr[pair]
        l_scr[pair] = alpha * l_scr[pair] + jnp.sum(p, axis=0, keepdims=True)
        v_c = v_t_ref[0, g * HEAD_DIM:(g + 1) * HEAD_DIM, pl.ds(start, KEY_TILE)]
        acc_scr[pair] = alpha * acc_scr[pair] + _dot(v_c, p.astype(_BF16))

    ties = bias_tile(0, jnp.zeros((1, tq), _F32))
    for pair in range(n_pairs):
        logits_stage(0, pair)

    def attn_tile(c, ties):
        ties = bias_tile(c + 1, ties)
        for pair in range(n_pairs):
            value_stage(c, pair)
            logits_stage(c + 1, pair)
        return ties

    lax.fori_loop(0, n_chunks - 1, attn_tile, ties)
    for pair in range(n_pairs):
        value_stage(n_chunks - 1, pair)

    for pair in range(n_pairs):
        o_t = acc_scr[pair] / l_scr[pair]
        both = jnp.concatenate([o_t[:, :tq], o_t[:, tq:]], axis=0)
        z = za_ref[0, :, pair * LANES:(pair + 1) * LANES]
        o_ref[0, :, pair * LANES:(pair + 1) * LANES] = (both.T * _silu(z)).astype(o_ref.dtype)


def _attention(q_t, qi_t, wi_t, k, kidx, v_t, za):
    bsz, seq, _ = k.shape
    tq = Q_TILE
    top_k = min(TOPK_MAX, seq // 4)
    feat = lambda rows: pl.BlockSpec((1, rows, tq), lambda b, j: (b, 0, j))
    full = lambda s1, s2: pl.BlockSpec((1, s1, s2), lambda b, j: (b, 0, 0))
    return pl.pallas_call(
        functools.partial(_attn_kernel, top_k=top_k),
        grid=(bsz, seq // tq),
        in_specs=[feat(ATTN_WIDTH), feat(IDX_HEADS * IDX_DIM), feat(W_IDX_ROWS),
                  full(seq, KV_WIDTH), full(seq, IDX_DIM), full(KV_WIDTH, seq),
                  pl.BlockSpec((1, tq, ATTN_WIDTH), lambda b, j: (b, j, 0))],
        out_specs=pl.BlockSpec((1, tq, ATTN_WIDTH), lambda b, j: (b, j, 0)),
        out_shape=jax.ShapeDtypeStruct((bsz, seq, ATTN_WIDTH), _BF16),
        scratch_shapes=[pltpu.VMEM((seq, tq), _F32), pltpu.VMEM((seq, tq), _BF16),
                        pltpu.VMEM((KEY_TILE, tq), _F32),
                        pltpu.VMEM((ATTN_HEADS // 2, 2 * HEAD_DIM, 2 * tq), _BF16),
                        pltpu.VMEM((ATTN_HEADS // 2, 1, 2 * tq), _F32),
                        pltpu.VMEM((ATTN_HEADS // 2, 1, 2 * tq), _F32),
                        pltpu.VMEM((ATTN_HEADS // 2, 1, 2 * tq), _F32),
                        pltpu.VMEM((ATTN_HEADS // 2, HEAD_DIM, 2 * tq), _F32),
                        pltpu.VMEM((ATTN_HEADS // 2, KEY_TILE, 2 * tq), _F32)],
        compiler_params=pltpu.CompilerParams(
            dimension_semantics=("parallel", "arbitrary"), vmem_limit_bytes=VMEM_LIMIT_BYTES),
        name="attn",
    )(q_t, qi_t, wi_t, k, kidx, v_t, za)


def _split3(x):
    hi = x.astype(_BF16)
    r1 = x - hi.astype(_F32)
    mid = r1.astype(_BF16)
    lo = (r1 - mid.astype(_F32)).astype(_BF16)
    return hi, mid, lo


def _ssd_kernel(xs_ref, bcs_ref, dt_ref, zb_ref, dtb_ref, alog_ref, dskip_ref, nw_ref, o_ref,
                state_scr, y_scr, dt_scr, acum_scr, acum_t_scr):
    i = pl.program_id(1)
    r = lax.broadcasted_iota(jnp.int32, (CHUNK, CHUNK), 0)
    c = lax.broadcasted_iota(jnp.int32, (CHUNK, CHUNK), 1)
    lower = c <= r

    @pl.when(i == 0)
    def _():
        state_scr[...] = jnp.zeros_like(state_scr)
        upper_b = jnp.where(r <= c, 1.0, 0.0).astype(_BF16)
        dt_in_t = dt_ref[0] + dtb_ref[...]
        dt_t = jnp.maximum(dt_in_t, 0.0) + jnp.log1p(jnp.exp(-jnp.abs(dt_in_t)))
        parts = _split3(dt_t * (-jnp.exp(alog_ref[...])))
        pad = jnp.zeros((LANES - SSM_HEADS, CHUNK), _F32)
        for n in range(dt_t.shape[1] // CHUNK):
            sl = slice(n * CHUNK, (n + 1) * CHUNK)
            a_cum_t = sum(_dot(part[:, sl], upper_b) for part in parts)
            acum_t_scr[:, sl] = a_cum_t
            acum_scr[sl, :] = jnp.concatenate([a_cum_t, pad], axis=0).T
            dt_scr[sl, :] = jnp.concatenate([dt_t[:, sl], pad], axis=0).T

    start = pl.multiple_of(i * CHUNK, CHUNK)
    dt = dt_scr[pl.ds(start, CHUNK), :]
    a_cum = acum_scr[pl.ds(start, CHUNK), :]
    a_cum_t = acum_t_scr[:, pl.ds(start, CHUNK)]

    lane_hi = lax.broadcasted_iota(jnp.int32, (1, LANES), 1) >= SSM_HEAD_DIM

    for grp in range(SSM_GROUPS):
        blk, half = divmod(grp, 2)
        b_blk = bcs_ref[0, :, blk * LANES:(blk + 1) * LANES]
        c_blk = bcs_ref[0, :, BC_WIDTH // 2 + blk * LANES: BC_WIDTH // 2 + (blk + 1) * LANES]
        in_grp = lane_hi if half else jnp.logical_not(lane_hi)
        c_g = jnp.where(in_grp, c_blk, jnp.zeros_like(c_blk))
        cb = lax.dot_general(c_g, b_blk, _NT, preferred_element_type=_F32)
        cb = jnp.where(lower, cb, 0.0)

        for pr in range(2):
            pair = grp * 2 + pr
            h_a, h_b = 2 * pair, 2 * pair + 1
            cols = slice(pair * LANES, (pair + 1) * LANES)

            def per_pair(arr):
                return jnp.where(lane_hi, arr[:, h_b:h_b + 1], arr[:, h_a:h_a + 1])

            dt_p = per_pair(dt)
            acum_p = per_pair(a_cum)
            last_p = per_pair(a_cum[CHUNK - 1:CHUNK])
            x_p = xs_ref[0, :, cols].astype(_F32)
            xd = x_p * dt_p

            y = None
            for hh, keep_hi in ((h_a, False), (h_b, True)):
                seg = a_cum[:, hh:hh + 1] - a_cum_t[hh:hh + 1, :]
                m_h = (cb * jnp.exp(jnp.minimum(seg, 0.0))).astype(_BF16)
                sel = lane_hi if keep_hi else jnp.logical_not(lane_hi)
                part = _dot(m_h, jnp.where(sel, xd, 0.0).astype(_BF16))
                y = part if y is None else y + part

            st_prev = state_scr[pair]
            y_off = _dot(c_g, st_prev.astype(_BF16)) * jnp.exp(acum_p)
            w = (xd * jnp.exp(last_p - acum_p)).astype(_BF16)
            st_new = lax.dot_general(b_blk, w, _TN, preferred_element_type=_F32)
            state_scr[pair] = jnp.exp(last_p) * st_prev + st_new
            y_scr[:, cols] = y + y_off + dskip_ref[:, cols] * x_p

    gw = SSM_WIDTH // SSM_GROUPS
    for grp in range(SSM_GROUPS):
        sl = slice(grp * gw, (grp + 1) * gw)
        yz = y_scr[:, sl] * _silu(zb_ref[0, :, sl])
        ms = jnp.mean(yz * yz, axis=-1, keepdims=True)
        o_ref[0, :, sl] = (yz * lax.rsqrt(ms + EPS) * nw_ref[:, sl]).astype(o_ref.dtype)


def _ssd(xs, bcs, dt_t, zb, dtb, alog, dskip, nw):
    bsz, seq, _ = xs.shape
    tok = lambda width: pl.BlockSpec((1, CHUNK, width), lambda b, i: (b, i, 0))
    consts = (dtb, alog, dskip, nw)
    return pl.pallas_call(
        _ssd_kernel,
        grid=(bsz, seq // CHUNK),
        in_specs=[tok(SSM_WIDTH), tok(BC_WIDTH), pl.BlockSpec((1, SSM_HEADS, seq), lambda b, i: (b, 0, 0)),
                  tok(SSM_WIDTH)] + [_const_spec(a.shape) for a in consts],
        out_specs=tok(SSM_WIDTH),
        out_shape=jax.ShapeDtypeStruct((bsz, seq, SSM_WIDTH), _BF16),
        scratch_shapes=[pltpu.VMEM((SSM_HEADS // 2, 2 * SSM_STATE, LANES), _F32),
                        pltpu.VMEM((CHUNK, SSM_WIDTH), _F32),
                        pltpu.VMEM((seq, LANES), _F32), pltpu.VMEM((seq, LANES), _F32),
                        pltpu.VMEM((SSM_HEADS, seq), _F32)],
        compiler_params=pltpu.CompilerParams(
            dimension_semantics=("parallel", "arbitrary"), vmem_limit_bytes=VMEM_LIMIT_BYTES),
        name="ssd",
    )(xs, bcs, dt_t, zb, *consts)


def _merge_kernel(x_ref, oa_ref, ob_ref, g_ref, gb_ref, wa_ref, wb_ref, wo_ref, fw_ref, o_ref, *, final_norm):
    gates = jax.nn.sigmoid(g_ref[...] + gb_ref[...])
    merged = (gates[:, :D_MODEL] * _dot(oa_ref[...], wa_ref[...])
              + gates[:, D_MODEL:] * _dot(ob_ref[...], wb_ref[...]))
    y = x_ref[...] + _dot(merged.astype(_BF16), wo_ref[...])
    if final_norm:
        y = y * lax.rsqrt(jnp.mean(y * y, axis=-1, keepdims=True) + EPS) * fw_ref[...]
    o_ref[...] = y


def _merge(x2, oa2, ob2, g2, gate_bias, wa, wb, wo, fw, final_norm):
    rows = x2.shape[0]
    tm = MERGE_ROWS
    tok = lambda width: pl.BlockSpec((tm, width), lambda i: (i, 0))
    consts = (gate_bias, wa, wb, wo, fw)
    return pl.pallas_call(
        functools.partial(_merge_kernel, final_norm=final_norm),
        grid=(rows // tm,),
        in_specs=[tok(D_MODEL), tok(ATTN_WIDTH), tok(SSM_WIDTH), tok(N_BRANCH * D_MODEL)]
                 + [_const_spec(a.shape) for a in consts],
        out_specs=tok(D_MODEL),
        out_shape=jax.ShapeDtypeStruct((rows, D_MODEL), _F32),
        compiler_params=pltpu.CompilerParams(
            dimension_semantics=("parallel",), vmem_limit_bytes=VMEM_LIMIT_BYTES),
        name="merge",
    )(x2, oa2, ob2, g2, *consts)


def _cast_kernel(w_ref, o_ref):
    o_ref[...] = w_ref[0].astype(o_ref.dtype)


def _layer_transposed_bf16(w_all, layer):
    w_t_all = jnp.swapaxes(w_all, 1, 2)
    _, rows, cols = w_t_all.shape
    tm = CAST_ROWS
    return pl.pallas_call(
        _cast_kernel,
        grid=(pl.cdiv(rows, tm),),
        in_specs=[pl.BlockSpec((1, tm, cols), lambda i: (layer, i, 0))],
        out_specs=pl.BlockSpec((tm, cols), lambda i: (i, 0)),
        out_shape=jax.ShapeDtypeStruct((rows, cols), _BF16),
        compiler_params=pltpu.CompilerParams(dimension_semantics=("parallel",)),
        name="cast",
    )(w_t_all)


def _layer_weights(w_in_all, layer):
    wt16 = _layer_transposed_bf16(w_in_all, layer)
    o = SPLIT_OFFSETS
    seg = lambda i: wt16[o[i]:o[i + 1]]
    w_q, w_k, w_v, w_za, w_qi, w_ki, w_wi, w_zb, w_xb, w_b, w_c, w_dt, w_g = (seg(i) for i in range(13))
    pad = lambda w, n: jnp.pad(w, ((0, n - w.shape[0]), (0, 0)))
    w_misc = pad(jnp.concatenate([w_ki, pad(w_wi, W_IDX_ROWS), w_dt], axis=0), LANES)
    w_t = jnp.concatenate([w_q, w_qi, w_k, w_misc, w_v], axis=0)
    return w_t, w_zb.T, w_xb.T, jnp.concatenate([w_b, w_c], axis=0).T, w_g.T, w_za.T


def _rope_tables(positions):
    inv_freq = ROPE_THETA ** (-jnp.arange(0, ROT_DIM, 2, dtype=_F32) / ROT_DIM)
    ang = jnp.swapaxes(positions.astype(_F32)[..., None] * inv_freq, 1, 2)
    return jnp.cos(ang), jnp.sin(ang)


def _pad_lanes(v, n):
    return jnp.pad(v, (0, n - v.shape[0]))[None, :]


def kernel(x, positions, norm_w, w_in, gate_bias, conv_w, conv_b, dt_bias, a_log, d_skip,
           ssm_norm_w, w_branch_a, w_branch_b, w_out, final_norm_w):
    bsz, seq, _ = x.shape
    depth = norm_w.shape[0]
    cos_t, sin_t = _rope_tables(positions)
    for i in range(depth):
        weights = _layer_weights(w_in, i)
        conv_params = (conv_w[i][:, :SSM_WIDTH], conv_b[i][None, :SSM_WIDTH],
                       conv_w[i][:, SSM_WIDTH:], conv_b[i][None, SSM_WIDTH:])
        (q_t, qi_t, wi_t, dt_t, v_t, k, kidx, zb, xs, bcs, gates, za) = _projection(
            x, norm_w[i][None, :], cos_t, sin_t, weights, conv_params)
        o_a = _attention(q_t, qi_t, wi_t, k, kidx, v_t, za)
        o_b = _ssd(xs, bcs, dt_t, zb, dt_bias[i][:, None], a_log[i][:, None],
                   jnp.repeat(d_skip[i], SSM_HEAD_DIM)[None, :], ssm_norm_w[i][None, :])
        x = _merge(x.reshape(bsz * seq, D_MODEL), o_a.reshape(bsz * seq, ATTN_WIDTH),
                   o_b.reshape(bsz * seq, SSM_WIDTH), gates.reshape(bsz * seq, N_BRANCH * D_MODEL),
                   gate_bias[i][None, :], w_branch_a[i].astype(_BF16), w_branch_b[i].astype(_BF16),
                   w_out[i].astype(_BF16), final_norm_w[None, :],
                   final_norm=(i == depth - 1)).reshape(bsz, seq, D_MODEL)
    return x
```

```python
import functools

import numpy as np
import jax
import jax.numpy as jnp
from jax import lax
from jax.experimental import pallas as pl
from jax.experimental.pallas import tpu as pltpu

D_MODEL = 1024
ATTN_HEADS = 8
ATTN_KV_HEADS = 2
HEAD_DIM = 64
ATTN_GROUP = ATTN_HEADS // ATTN_KV_HEADS
ATTN_WIDTH = ATTN_HEADS * HEAD_DIM
KV_WIDTH = ATTN_KV_HEADS * HEAD_DIM
ROT_DIM = HEAD_DIM // 4
ROT_HALF = ROT_DIM // 2
ROPE_THETA = 500000.0
IDX_HEADS = 4
IDX_DIM = 64
TOPK_MAX = 256
SSM_HEADS = 16
SSM_HEAD_DIM = 64
SSM_WIDTH = SSM_HEADS * SSM_HEAD_DIM
SSM_GROUPS = 4
SSM_STATE = 64
CONV_K = 4
CHUNK = 128
BC_WIDTH = 2 * SSM_GROUPS * SSM_STATE
N_BRANCH = 2
EPS = 1e-6
SPLIT_SIZES = (ATTN_WIDTH, KV_WIDTH, KV_WIDTH, ATTN_WIDTH,
               IDX_HEADS * IDX_DIM, IDX_DIM, IDX_HEADS,
               SSM_WIDTH, SSM_WIDTH, SSM_GROUPS * SSM_STATE, SSM_GROUPS * SSM_STATE, SSM_HEADS,
               N_BRANCH * D_MODEL)
SPLIT_OFFSETS = tuple(int(o) for o in np.cumsum((0,) + SPLIT_SIZES))

LANES = 128
SUBLANES = 8
VMEM_LIMIT_BYTES = 56 * 1024 * 1024

PROJ_ROWS = 256
CONV_COLS = 256
CAST_ROWS = 512
Q_TILE = 256
KEY_TILE = 256
COUNT_ROWS = 32
LOG2_E = 1.4426950408889634
MERGE_ROWS = 512
W_IDX_ROWS = 8
V_ROWS = HEAD_DIM + 16
BCDT_WIDTH = BC_WIDTH + LANES
NEG_BIG = -1e30

_F32 = jnp.float32
_BF16 = jnp.bfloat16
_NT = (((1,), (1,)), ((), ()))
_TN = (((0,), (0,)), ((), ()))


def _dot(a, b):
    return jnp.dot(a, b, preferred_element_type=_F32)


def _silu(x):
    return x * jax.nn.sigmoid(x)


def _causal_conv_silu(x, tail_ref, w_ref, b_ref):
    rows = x.shape[0]
    tail = tail_ref[...]
    sub = lax.broadcasted_iota(jnp.int32, (SUBLANES, x.shape[1]), 0)
    w = w_ref[...]
    y = x * w[CONV_K - 1:CONV_K] + b_ref[...]
    for shift in range(1, CONV_K):
        rolled = pltpu.roll(x, shift, 0)
        head = jnp.where(sub < shift, pltpu.roll(tail, shift, 0), rolled[:SUBLANES])
        shifted = jnp.concatenate([head, rolled[SUBLANES:]], axis=0)
        y = y + shifted * w[CONV_K - 1 - shift:CONV_K - shift]
    tail_ref[...] = x[rows - SUBLANES:]
    return _silu(y)


def _proj_kernel(x_ref, nw_ref, cos_t_ref, sin_t_ref,
                 w_t_ref, w_zb_ref, w_xb_ref, w_bc_ref, w_g_ref, w_za_ref,
                 cwx_ref, cbx_ref, cwbc_ref, cbbc_ref,
                 q_t_ref, qi_t_ref, wi_t_ref, dt_t_ref, v_t_ref, k_ref, kidx_ref,
                 zb_ref, xs_ref, bcs_ref, g_ref, za_ref,
                 tailx_scr, tailbc_scr):
    @pl.when(pl.program_id(1) == 0)
    def _():
        tailx_scr[...] = jnp.zeros_like(tailx_scr)
        tailbc_scr[...] = jnp.zeros_like(tailbc_scr)

    x = x_ref[0]
    h = x * lax.rsqrt(jnp.mean(x * x, axis=-1, keepdims=True) + EPS) * nw_ref[...]
    h = h.astype(_BF16)

    t = lax.dot_general(w_t_ref[...], h, _NT, preferred_element_type=_F32)
    cos_t = cos_t_ref[0]
    sin_t = sin_t_ref[0]

    def rope_head(block, hd):
        x1 = block[hd * HEAD_DIM: hd * HEAD_DIM + ROT_HALF]
        x2 = block[hd * HEAD_DIM + ROT_HALF: hd * HEAD_DIM + ROT_DIM]
        return jnp.concatenate([x1 * cos_t - x2 * sin_t, x2 * cos_t + x1 * sin_t], axis=0)

    def rope_store(block, n_heads, scale, out_ref):
        out_ref[0] = (block * scale).astype(out_ref.dtype)
        for hd in range(n_heads):
            out_ref[0, hd * HEAD_DIM: hd * HEAD_DIM + ROT_DIM, :] = (
                rope_head(block, hd) * scale).astype(out_ref.dtype)

    def rope_value(block, n_heads):
        parts = []
        for hd in range(n_heads):
            parts += [rope_head(block, hd), block[hd * HEAD_DIM + ROT_DIM:(hd + 1) * HEAD_DIM]]
        if block.shape[0] > n_heads * HEAD_DIM:
            parts.append(block[n_heads * HEAD_DIM:])
        return jnp.concatenate(parts, axis=0)

    o_qi = ATTN_WIDTH
    o_k = o_qi + IDX_HEADS * IDX_DIM
    o_ki = o_k + KV_WIDTH
    o_v = o_ki + LANES
    rope_store(t[:o_qi], ATTN_HEADS, LOG2_E * HEAD_DIM ** -0.5, q_t_ref)
    rope_store(t[o_qi:o_k], IDX_HEADS, IDX_DIM ** -0.5, qi_t_ref)
    o_wi = o_ki + IDX_DIM
    o_dt = o_wi + W_IDX_ROWS
    wi_t_ref[0] = t[o_wi:o_dt] * (IDX_HEADS ** -0.5)
    dt_t_ref[0] = t[o_dt:o_dt + SSM_HEADS]
    for g in range(ATTN_KV_HEADS):
        v_t_ref[0, g * V_ROWS:g * V_ROWS + HEAD_DIM, :] = t[o_v + g * HEAD_DIM:o_v + (g + 1) * HEAD_DIM].astype(_BF16)
        v_t_ref[0, g * V_ROWS + HEAD_DIM:(g + 1) * V_ROWS, :] = jnp.ones((V_ROWS - HEAD_DIM, x.shape[0]), _BF16)
    k_ref[0] = rope_value(t[o_k:o_ki], ATTN_KV_HEADS).T.astype(_BF16)
    kidx_ref[0] = rope_value(t[o_ki:o_v], 1).T[:, :IDX_DIM].astype(_BF16)

    def plain(out_ref, w_ref, c0, width):
        out_ref[0, :, c0:c0 + width] = _dot(h, w_ref[:, c0:c0 + width]).astype(out_ref.dtype)

    def conv(out_ref, w_ref, tail_scr, cw_ref, cb_ref, c0, width):
        sl = slice(c0, c0 + width)
        out_ref[0, :, sl] = _causal_conv_silu(
            _dot(h, w_ref[:, sl]), tail_scr.at[:, sl], cw_ref.at[:, sl], cb_ref.at[:, sl]).astype(_BF16)

    cc = CONV_COLS
    for n in range(SSM_WIDTH // cc):
        conv(xs_ref, w_xb_ref, tailx_scr, cwx_ref, cbx_ref, n * cc, cc)
        plain(g_ref, w_g_ref, 2 * n * cc, 2 * cc)
    for n in range(BC_WIDTH // cc):
        conv(bcs_ref, w_bc_ref, tailbc_scr, cwbc_ref, cbbc_ref, n * cc, cc)
        plain(zb_ref, w_zb_ref, 2 * n * cc, 2 * cc)
    plain(za_ref, w_za_ref, 0, ATTN_WIDTH)


def _const_spec(shape):
    nd = len(shape)
    return pl.BlockSpec(shape, lambda *_: (0,) * nd, pipeline_mode=pl.Buffered(1))


def _projection(x, nw, cos_t, sin_t, weights, conv_params):
    bsz, seq, _ = x.shape
    tm = PROJ_ROWS
    tok = lambda width: pl.BlockSpec((1, tm, width), lambda b, i: (b, i, 0))
    feat = lambda rows: pl.BlockSpec((1, rows, tm), lambda b, i: (b, 0, i))
    in_specs = [tok(D_MODEL), _const_spec((1, D_MODEL)), feat(ROT_HALF), feat(ROT_HALF)
                ] + [_const_spec(w.shape) for w in weights + conv_params]
    out_shape = (
        jax.ShapeDtypeStruct((bsz, ATTN_WIDTH, seq), _BF16),
        jax.ShapeDtypeStruct((bsz, IDX_HEADS * IDX_DIM, seq), _BF16),
        jax.ShapeDtypeStruct((bsz, W_IDX_ROWS, seq), _F32),
        jax.ShapeDtypeStruct((bsz, SSM_HEADS, seq), _F32),
        jax.ShapeDtypeStruct((bsz, ATTN_KV_HEADS * V_ROWS, seq), _BF16),
        jax.ShapeDtypeStruct((bsz, seq, KV_WIDTH), _BF16),
        jax.ShapeDtypeStruct((bsz, seq, IDX_DIM), _BF16),
        jax.ShapeDtypeStruct((bsz, seq, SSM_WIDTH), _F32),
        jax.ShapeDtypeStruct((bsz, seq, SSM_WIDTH), _BF16),
        jax.ShapeDtypeStruct((bsz, seq, BC_WIDTH), _BF16),
        jax.ShapeDtypeStruct((bsz, seq, N_BRANCH * D_MODEL), _BF16),
        jax.ShapeDtypeStruct((bsz, seq, ATTN_WIDTH), _F32),
    )
    out_specs = (feat(ATTN_WIDTH), feat(IDX_HEADS * IDX_DIM), feat(W_IDX_ROWS), feat(SSM_HEADS),
                 feat(ATTN_KV_HEADS * V_ROWS), tok(KV_WIDTH), tok(IDX_DIM), tok(SSM_WIDTH), tok(SSM_WIDTH),
                 tok(BC_WIDTH), tok(N_BRANCH * D_MODEL), tok(ATTN_WIDTH))
    return pl.pallas_call(
        _proj_kernel,
        grid=(bsz, seq // tm),
        in_specs=in_specs,
        out_specs=out_specs,
        out_shape=out_shape,
        scratch_shapes=[pltpu.VMEM((SUBLANES, SSM_WIDTH), _F32), pltpu.VMEM((SUBLANES, BC_WIDTH), _F32)],
        compiler_params=pltpu.CompilerParams(
            dimension_semantics=("parallel", "arbitrary"), vmem_limit_bytes=VMEM_LIMIT_BYTES),
        name="proj",
    )(x, nw, cos_t, sin_t, *weights, *conv_params)


def _attn_kernel(q_t_ref, qi_t_ref, wi_t_ref, k_ref, kidx_ref, v_t_ref, za_ref, o_ref,
                 score_scr, score16_scr, sel_scr, bias_scr, qpad_scr, m_scr, alpha_scr, acc_scr, s_scr,
                 *, top_k):
    j = pl.program_id(1)
    n_chunks = j + 1
    tq = Q_TILE
    q_pos = j * tq + lax.broadcasted_iota(jnp.int32, (1, tq), 1)
    key_iota = lax.broadcasted_iota(jnp.int32, (KEY_TILE, tq), 0)

    qi_t = qi_t_ref[0]
    qi_cat = jnp.concatenate([qi_t[hd * IDX_DIM:(hd + 1) * IDX_DIM] for hd in range(IDX_HEADS)], axis=1)
    wi_t = wi_t_ref[0]
    wi_cat = jnp.concatenate([wi_t[hd:hd + 1] for hd in range(IDX_HEADS)], axis=1)

    def score_chunk(c, carry):
        start = pl.multiple_of(c * KEY_TILE, KEY_TILE)
        logits = _dot(kidx_ref[0, pl.ds(start, KEY_TILE), :], qi_cat)
        weighted = jnp.maximum(logits, 0.0) * wi_cat
        score = weighted[:, 0:tq]
        for hd in range(1, IDX_HEADS):
            score = score + weighted[:, hd * tq:(hd + 1) * tq]
        score = jnp.where((start + key_iota) <= q_pos, score, -jnp.inf)
        score_scr[pl.ds(start, KEY_TILE), :] = score
        score16_scr[pl.ds(start, KEY_TILE), :] = score.astype(_BF16)
        return carry

    lax.fori_loop(0, n_chunks, score_chunk, 0)

    def code_to_f32(code):
        return pltpu.bitcast(code ^ ((code >> 31) & jnp.int32(0x7FFFFFFF)), _F32)

    def count_ge(ref, cand, dtype):
        def body(c, acc):
            start = pl.multiple_of(c * KEY_TILE, KEY_TILE)
            inc = jnp.where(ref[pl.ds(start, KEY_TILE), :] >= cand, jnp.ones((), dtype), jnp.zeros((), dtype))
            for part in range(KEY_TILE // COUNT_ROWS):
                acc = acc + inc[part * COUNT_ROWS:(part + 1) * COUNT_ROWS]
            return acc
        acc = lax.fori_loop(0, n_chunks, body, jnp.zeros((COUNT_ROWS, tq), dtype))
        return jnp.sum(acc.astype(_F32), axis=0, keepdims=True)

    def bf16_code(block):
        return jnp.where(block >= 0, block, block | jnp.int32(0xFFFF))

    def coarse_bit(i, block):
        cand = block + (jnp.int32(1) << (31 - i))
        hit = count_ge(score16_scr, code_to_f32(bf16_code(cand)).astype(_BF16), _BF16) >= top_k
        return jnp.where(hit, cand, block)

    def fine_bit(i, carry):
        code, n_above = carry
        cand = code + (jnp.int32(1) << (16 - i))
        cnt = count_ge(score_scr, code_to_f32(cand), _F32)
        hit = cnt >= top_k
        return jnp.where(hit, cand, code), jnp.where(hit, n_above, cnt)

    sel_scr[0:1, :] = jnp.full((1, tq), -jnp.inf, _F32)
    sel_scr[1:2, :] = jnp.zeros((1, tq), _F32)

    @pl.when((j + 1) * tq > top_k)
    def _():
        int_min = jnp.int32(-2 ** 31)
        coarse = lax.fori_loop(0, 16, coarse_bit, jnp.full((1, tq), int_min, jnp.int32))
        base = bf16_code(coarse) - jnp.int32(1 << 16)
        fine, n_above = lax.fori_loop(0, 17, fine_bit, (base, jnp.zeros((1, tq), _F32)))
        few = (q_pos + 1) <= top_k
        sel_scr[0:1, :] = jnp.where(few, -jnp.inf, code_to_f32(fine))
        sel_scr[1:2, :] = jnp.where(few, 0.0, top_k - n_above)

    thr = sel_scr[0:1, :]
    n_ties_kept = sel_scr[1:2, :]

    row = lax.broadcasted_iota(jnp.int32, (KEY_TILE, KEY_TILE), 0)
    col = lax.broadcasted_iota(jnp.int32, (KEY_TILE, KEY_TILE), 1)
    strict_lower = jnp.where(col < row, 1.0, 0.0).astype(_BF16)

    def bias_tile(c, ties_before):
        start = pl.multiple_of(c * KEY_TILE, KEY_TILE)
        s = score_scr[pl.ds(start, KEY_TILE), :]
        tie = jnp.where(s == thr, 1.0, 0.0)
        rank = _dot(strict_lower, tie.astype(_BF16)) + ties_before
        keep = (s > thr) | ((s == thr) & (rank < n_ties_kept))
        bias_scr[...] = jnp.where(keep, 0.0, NEG_BIG)
        return ties_before + jnp.sum(tie, axis=0, keepdims=True)

    n_pairs = ATTN_HEADS // 2
    zeros_half = jnp.zeros((HEAD_DIM, 2 * tq), _BF16)
    for pair in range(n_pairs):
        g = (2 * pair) // ATTN_GROUP
        q_cat = jnp.concatenate([q_t_ref[0, (2 * pair + hd) * HEAD_DIM:(2 * pair + hd + 1) * HEAD_DIM, :]
                                 for hd in range(2)], axis=1)
        qpad_scr[pair] = jnp.concatenate([q_cat, zeros_half] if g == 0 else [zeros_half, q_cat], axis=0)
    m_scr[...] = jnp.full(m_scr.shape, NEG_BIG, _F32)
    acc_scr[...] = jnp.zeros(acc_scr.shape, _F32)

    def logits_stage(c, pair):
        start = pl.multiple_of(c * KEY_TILE, KEY_TILE)
        b = bias_scr[...]
        s = _dot(k_ref[0, pl.ds(start, KEY_TILE), :], qpad_scr[pair]) + jnp.concatenate([b, b], axis=1)
        m_old = m_scr[pair]
        m_new = jnp.maximum(m_old, jnp.max(s, axis=0, keepdims=True))
        s_scr[pair] = s
        alpha_scr[pair] = jnp.exp2(m_old - m_new)
        m_scr[pair] = m_new

    def value_stage(c, pair):
        start = pl.multiple_of(c * KEY_TILE, KEY_TILE)
        g = (2 * pair) // ATTN_GROUP
        p = jnp.exp2(s_scr[pair] - m_scr[pair])
        v_c = v_t_ref[0, g * V_ROWS:(g + 1) * V_ROWS, pl.ds(start, KEY_TILE)]
        acc_scr[pair] = alpha_scr[pair] * acc_scr[pair] + _dot(v_c, p.astype(_BF16))

    ties = bias_tile(0, jnp.zeros((1, tq), _F32))
    for pair in range(n_pairs):
        logits_stage(0, pair)

    def attn_tile(c, ties):
        ties = bias_tile(c + 1, ties)
        for pair in range(n_pairs):
            value_stage(c, pair)
            logits_stage(c + 1, pair)
        return ties

    lax.fori_loop(0, n_chunks - 1, attn_tile, ties)
    for pair in range(n_pairs):
        value_stage(n_chunks - 1, pair)

    for pair in range(n_pairs):
        o_t = acc_scr[pair, :HEAD_DIM] / acc_scr[pair, HEAD_DIM:HEAD_DIM + 1]
        both = jnp.concatenate([o_t[:, :tq], o_t[:, tq:]], axis=0)
        z = za_ref[0, :, pair * LANES:(pair + 1) * LANES]
        o_ref[0, :, pair * LANES:(pair + 1) * LANES] = (both.T * _silu(z)).astype(o_ref.dtype)


def _attention(q_t, qi_t, wi_t, k, kidx, v_t, za):
    bsz, seq, _ = k.shape
    tq = Q_TILE
    top_k = min(TOPK_MAX, seq // 4)
    feat = lambda rows: pl.BlockSpec((1, rows, tq), lambda b, j: (b, 0, j))
    full = lambda s1, s2: pl.BlockSpec((1, s1, s2), lambda b, j: (b, 0, 0))
    return pl.pallas_call(
        functools.partial(_attn_kernel, top_k=top_k),
        grid=(bsz, seq // tq),
        in_specs=[feat(ATTN_WIDTH), feat(IDX_HEADS * IDX_DIM), feat(W_IDX_ROWS),
                  full(seq, KV_WIDTH), full(seq, IDX_DIM), full(ATTN_KV_HEADS * V_ROWS, seq),
                  pl.BlockSpec((1, tq, ATTN_WIDTH), lambda b, j: (b, j, 0))],
        out_specs=pl.BlockSpec((1, tq, ATTN_WIDTH), lambda b, j: (b, j, 0)),
        out_shape=jax.ShapeDtypeStruct((bsz, seq, ATTN_WIDTH), _BF16),
        scratch_shapes=[pltpu.VMEM((seq, tq), _F32), pltpu.VMEM((seq, tq), _BF16),
                        pltpu.VMEM((SUBLANES, tq), _F32), pltpu.VMEM((KEY_TILE, tq), _F32),
                        pltpu.VMEM((ATTN_HEADS // 2, 2 * HEAD_DIM, 2 * tq), _BF16),
                        pltpu.VMEM((ATTN_HEADS // 2, 1, 2 * tq), _F32),
                        pltpu.VMEM((ATTN_HEADS // 2, 1, 2 * tq), _F32),
                        pltpu.VMEM((ATTN_HEADS // 2, V_ROWS, 2 * tq), _F32),
                        pltpu.VMEM((ATTN_HEADS // 2, KEY_TILE, 2 * tq), _F32)],
        compiler_params=pltpu.CompilerParams(
            dimension_semantics=("parallel", "arbitrary"), vmem_limit_bytes=VMEM_LIMIT_BYTES),
        name="attn",
    )(q_t, qi_t, wi_t, k, kidx, v_t, za)


def _split3(x):
    hi = x.astype(_BF16)
    r1 = x - hi.astype(_F32)
    mid = r1.astype(_BF16)
    lo = (r1 - mid.astype(_F32)).astype(_BF16)
    return hi, mid, lo


def _ssd_kernel(xs_ref, bcs_ref, dt_ref, zb_ref, dtb_ref, alog_ref, dskip_ref, nw_ref, o_ref,
                state_scr, y_scr, dt_scr, acum_scr, acum_t_scr):
    i = pl.program_id(1)
    r = lax.broadcasted_iota(jnp.int32, (CHUNK, CHUNK), 0)
    c = lax.broadcasted_iota(jnp.int32, (CHUNK, CHUNK), 1)
    lower = c <= r

    @pl.when(i == 0)
    def _():
        state_scr[...] = jnp.zeros_like(state_scr)
        upper_b = jnp.where(r <= c, 1.0, 0.0).astype(_BF16)
        dt_in_t = dt_ref[0] + dtb_ref[...]
        dt_t = jnp.maximum(dt_in_t, 0.0) + jnp.log1p(jnp.exp(-jnp.abs(dt_in_t)))
        parts = _split3(dt_t * (-jnp.exp(alog_ref[...])))
        pad = jnp.zeros((LANES - SSM_HEADS, CHUNK), _F32)
        for n in range(dt_t.shape[1] // CHUNK):
            sl = slice(n * CHUNK, (n + 1) * CHUNK)
            a_cum_t = sum(_dot(part[:, sl], upper_b) for part in parts)
            acum_t_scr[:, sl] = a_cum_t
            acum_scr[sl, :] = jnp.concatenate([a_cum_t, pad], axis=0).T
            dt_scr[sl, :] = jnp.concatenate([dt_t[:, sl], pad], axis=0).T

    start = pl.multiple_of(i * CHUNK, CHUNK)
    dt = dt_scr[pl.ds(start, CHUNK), :]
    a_cum = acum_scr[pl.ds(start, CHUNK), :]
    a_cum_t = acum_t_scr[:, pl.ds(start, CHUNK)]

    lane_hi = lax.broadcasted_iota(jnp.int32, (1, LANES), 1) >= SSM_HEAD_DIM

    for grp in range(SSM_GROUPS):
        blk, half = divmod(grp, 2)
        b_blk = bcs_ref[0, :, blk * LANES:(blk + 1) * LANES]
        c_blk = bcs_ref[0, :, BC_WIDTH // 2 + blk * LANES: BC_WIDTH // 2 + (blk + 1) * LANES]
        in_grp = lane_hi if half else jnp.logical_not(lane_hi)
        c_g = jnp.where(in_grp, c_blk, jnp.zeros_like(c_blk))
        cb = lax.dot_general(c_g, b_blk, _NT, preferred_element_type=_F32)
        cb = jnp.where(lower, cb, 0.0)

        for pr in range(2):
            pair = grp * 2 + pr
            h_a, h_b = 2 * pair, 2 * pair + 1
            cols = slice(pair * LANES, (pair + 1) * LANES)

            def per_pair(arr):
                return jnp.where(lane_hi, arr[:, h_b:h_b + 1], arr[:, h_a:h_a + 1])

            acum_bc = {hh: jnp.broadcast_to(a_cum[:, hh:hh + 1], (CHUNK, LANES)) for hh in (h_a, h_b)}
            dt_p = per_pair(dt)
            acum_p = jnp.where(lane_hi, acum_bc[h_b], acum_bc[h_a])
            last_p = per_pair(a_cum[CHUNK - 1:CHUNK])
            x_p = xs_ref[0, :, cols].astype(_F32)
            xd = x_p * dt_p

            y = None
            for hh, keep_hi in ((h_a, False), (h_b, True)):
                seg = acum_bc[hh] - a_cum_t[hh:hh + 1, :]
                m_h = (cb * jnp.exp(jnp.minimum(seg, 0.0))).astype(_BF16)
                sel = lane_hi if keep_hi else jnp.logical_not(lane_hi)
                part = _dot(m_h, jnp.where(sel, xd, 0.0).astype(_BF16))
                y = part if y is None else y + part

            st_prev = state_scr[pair]
            y_off = _dot(c_g, st_prev.astype(_BF16)) * jnp.exp(acum_p)
            w = (xd * jnp.exp(last_p - acum_p)).astype(_BF16)
            st_new = lax.dot_general(b_blk, w, _TN, preferred_element_type=_F32)
            state_scr[pair] = jnp.exp(last_p) * st_prev + st_new
            y_scr[:, cols] = y + y_off + dskip_ref[:, cols] * x_p

    gw = SSM_WIDTH // SSM_GROUPS
    for grp in range(SSM_GROUPS):
        sl = slice(grp * gw, (grp + 1) * gw)
        yz = y_scr[:, sl] * _silu(zb_ref[0, :, sl])
        ms = jnp.mean(yz * yz, axis=-1, keepdims=True)
        o_ref[0, :, sl] = (yz * lax.rsqrt(ms + EPS) * nw_ref[:, sl]).astype(o_ref.dtype)


def _ssd(xs, bcs, dt_t, zb, dtb, alog, dskip, nw):
    bsz, seq, _ = xs.shape
    tok = lambda width: pl.BlockSpec((1, CHUNK, width), lambda b, i: (b, i, 0))
    consts = (dtb, alog, dskip, nw)
    return pl.pallas_call(
        _ssd_kernel,
        grid=(bsz, seq // CHUNK),
        in_specs=[tok(SSM_WIDTH), tok(BC_WIDTH), pl.BlockSpec((1, SSM_HEADS, seq), lambda b, i: (b, 0, 0)),
                  tok(SSM_WIDTH)] + [_const_spec(a.shape) for a in consts],
        out_specs=tok(SSM_WIDTH),
        out_shape=jax.ShapeDtypeStruct((bsz, seq, SSM_WIDTH), _BF16),
        scratch_shapes=[pltpu.VMEM((SSM_HEADS // 2, 2 * SSM_STATE, LANES), _F32),
                        pltpu.VMEM((CHUNK, SSM_WIDTH), _F32),
                        pltpu.VMEM((seq, LANES), _F32), pltpu.VMEM((seq, LANES), _F32),
                        pltpu.VMEM((SSM_HEADS, seq), _F32)],
        compiler_params=pltpu.CompilerParams(
            dimension_semantics=("parallel", "arbitrary"), vmem_limit_bytes=VMEM_LIMIT_BYTES),
        name="ssd",
    )(xs, bcs, dt_t, zb, *consts)


def _merge_kernel(x_ref, oa_ref, ob_ref, g_ref, gb_ref, wa_ref, wb_ref, wo_ref, fw_ref, o_ref, *, final_norm):
    gates = jax.nn.sigmoid(g_ref[...] + gb_ref[...])
    merged = (gates[:, :D_MODEL] * _dot(oa_ref[...], wa_ref[...])
              + gates[:, D_MODEL:] * _dot(ob_ref[...], wb_ref[...]))
    y = x_ref[...] + _dot(merged.astype(_BF16), wo_ref[...])
    if final_norm:
        y = y * lax.rsqrt(jnp.mean(y * y, axis=-1, keepdims=True) + EPS) * fw_ref[...]
    o_ref[...] = y


def _merge(x2, oa2, ob2, g2, gate_bias, wa, wb, wo, fw, final_norm):
    rows = x2.shape[0]
    tm = MERGE_ROWS
    tok = lambda width: pl.BlockSpec((tm, width), lambda i: (i, 0))
    consts = (gate_bias, wa, wb, wo, fw)
    return pl.pallas_call(
        functools.partial(_merge_kernel, final_norm=final_norm),
        grid=(rows // tm,),
        in_specs=[tok(D_MODEL), tok(ATTN_WIDTH), tok(SSM_WIDTH), tok(N_BRANCH * D_MODEL)]
                 + [_const_spec(a.shape) for a in consts],
        out_specs=tok(D_MODEL),
        out_shape=jax.ShapeDtypeStruct((rows, D_MODEL), _F32),
        compiler_params=pltpu.CompilerParams(
            dimension_semantics=("parallel",), vmem_limit_bytes=VMEM_LIMIT_BYTES),
        name="merge",
    )(x2, oa2, ob2, g2, *consts)


def _cast_kernel(w_ref, o_ref):
    o_ref[...] = w_ref[0].astype(o_ref.dtype)


def _layer_transposed_bf16(w_all, layer):
    w_t_all = jnp.swapaxes(w_all, 1, 2)
    _, rows, cols = w_t_all.shape
    tm = CAST_ROWS
    return pl.pallas_call(
        _cast_kernel,
        grid=(pl.cdiv(rows, tm),),
        in_specs=[pl.BlockSpec((1, tm, cols), lambda i: (layer, i, 0))],
        out_specs=pl.BlockSpec((tm, cols), lambda i: (i, 0)),
        out_shape=jax.ShapeDtypeStruct((rows, cols), _BF16),
        compiler_params=pltpu.CompilerParams(dimension_semantics=("parallel",)),
        name="cast",
    )(w_t_all)


def _layer_weights(w_in_all, layer):
    wt16 = _layer_transposed_bf16(w_in_all, layer)
    o = SPLIT_OFFSETS
    seg = lambda i: wt16[o[i]:o[i + 1]]
    w_q, w_k, w_v, w_za, w_qi, w_ki, w_wi, w_zb, w_xb, w_b, w_c, w_dt, w_g = (seg(i) for i in range(13))
    pad = lambda w, n: jnp.pad(w, ((0, n - w.shape[0]), (0, 0)))
    w_misc = pad(jnp.concatenate([w_ki, pad(w_wi, W_IDX_ROWS), w_dt], axis=0), LANES)
    w_t = jnp.concatenate([w_q, w_qi, w_k, w_misc, w_v], axis=0)
    return w_t, w_zb.T, w_xb.T, jnp.concatenate([w_b, w_c], axis=0).T, w_g.T, w_za.T


def _rope_tables(positions):
    inv_freq = ROPE_THETA ** (-jnp.arange(0, ROT_DIM, 2, dtype=_F32) / ROT_DIM)
    ang = jnp.swapaxes(positions.astype(_F32)[..., None] * inv_freq, 1, 2)
    return jnp.cos(ang), jnp.sin(ang)


def _pad_lanes(v, n):
    return jnp.pad(v, (0, n - v.shape[0]))[None, :]


def kernel(x, positions, norm_w, w_in, gate_bias, conv_w, conv_b, dt_bias, a_log, d_skip,
           ssm_norm_w, w_branch_a, w_branch_b, w_out, final_norm_w):
    bsz, seq, _ = x.shape
    depth = norm_w.shape[0]
    cos_t, sin_t = _rope_tables(positions)
    for i in range(depth):
        weights = _layer_weights(w_in, i)
        conv_params = (conv_w[i][:, :SSM_WIDTH], conv_b[i][None, :SSM_WIDTH],
                       conv_w[i][:, SSM_WIDTH:], conv_b[i][None, SSM_WIDTH:])
        (q_t, qi_t, wi_t, dt_t, v_t, k, kidx, zb, xs, bcs, gates, za) = _projection(
            x, norm_w[i][None, :], cos_t, sin_t, weights, conv_params)
        o_a = _attention(q_t, qi_t, wi_t, k, kidx, v_t, za)
        o_b = _ssd(xs, bcs, dt_t, zb, dt_bias[i][:, None], a_log[i][:, None],
                   jnp.repeat(d_skip[i], SSM_HEAD_DIM)[None, :], ssm_norm_w[i][None, :])
        x = _merge(x.reshape(bsz * seq, D_MODEL), o_a.reshape(bsz * seq, ATTN_WIDTH),
                   o_b.reshape(bsz * seq, SSM_WIDTH), gates.reshape(bsz * seq, N_BRANCH * D_MODEL),
                   gate_bias[i][None, :], w_branch_a[i].astype(_BF16), w_branch_b[i].astype(_BF16),
                   w_out[i].astype(_BF16), final_norm_w[None, :],
                   final_norm=(i == depth - 1)).reshape(bsz, seq, D_MODEL)
    return x
```

```python
import functools

import numpy as np
import jax
import jax.numpy as jnp
from jax import lax
from jax.experimental import pallas as pl
from jax.experimental.pallas import tpu as pltpu

D_MODEL = 1024
ATTN_HEADS = 8
ATTN_KV_HEADS = 2
HEAD_DIM = 64
ATTN_GROUP = ATTN_HEADS // ATTN_KV_HEADS
ATTN_WIDTH = ATTN_HEADS * HEAD_DIM
KV_WIDTH = ATTN_KV_HEADS * HEAD_DIM
ROT_DIM = HEAD_DIM // 4
ROT_HALF = ROT_DIM // 2
ROPE_THETA = 500000.0
IDX_HEADS = 4
IDX_DIM = 64
TOPK_MAX = 256
SSM_HEADS = 16
SSM_HEAD_DIM = 64
SSM_WIDTH = SSM_HEADS * SSM_HEAD_DIM
SSM_GROUPS = 4
SSM_STATE = 64
CONV_K = 4
CHUNK = 128
BC_WIDTH = 2 * SSM_GROUPS * SSM_STATE
N_BRANCH = 2
EPS = 1e-6
SPLIT_SIZES = (ATTN_WIDTH, KV_WIDTH, KV_WIDTH, ATTN_WIDTH,
               IDX_HEADS * IDX_DIM, IDX_DIM, IDX_HEADS,
               SSM_WIDTH, SSM_WIDTH, SSM_GROUPS * SSM_STATE, SSM_GROUPS * SSM_STATE, SSM_HEADS,
               N_BRANCH * D_MODEL)
SPLIT_OFFSETS = tuple(int(o) for o in np.cumsum((0,) + SPLIT_SIZES))

LANES = 128
SUBLANES = 8
VMEM_LIMIT_BYTES = 56 * 1024 * 1024

PROJ_ROWS = 256
CONV_COLS = 256
CAST_ROWS = 512
Q_TILE = 256
KEY_TILE = 256
COUNT_ROWS = 32
LOG2_E = 1.4426950408889634
MERGE_ROWS = 512
W_IDX_ROWS = 8
V_ROWS = HEAD_DIM + 16
NEG_BIG = -1e30

_F32 = jnp.float32
_BF16 = jnp.bfloat16
_NT = (((1,), (1,)), ((), ()))
_TN = (((0,), (0,)), ((), ()))


def _dot(a, b):
    return jnp.dot(a, b, preferred_element_type=_F32)


def _silu(x):
    return x * jax.nn.sigmoid(x)


def _causal_conv_silu(x, tail_ref, w_ref, b_ref):
    rows = x.shape[0]
    tail = tail_ref[...]
    sub = lax.broadcasted_iota(jnp.int32, (SUBLANES, x.shape[1]), 0)
    w = w_ref[...]
    y = x * w[CONV_K - 1:CONV_K] + b_ref[...]
    for shift in range(1, CONV_K):
        rolled = pltpu.roll(x, shift, 0)
        head = jnp.where(sub < shift, pltpu.roll(tail, shift, 0), rolled[:SUBLANES])
        shifted = jnp.concatenate([head, rolled[SUBLANES:]], axis=0)
        y = y + shifted * w[CONV_K - 1 - shift:CONV_K - shift]
    tail_ref[...] = x[rows - SUBLANES:]
    return _silu(y)


def _proj_kernel(x_ref, nw_ref, cos_t_ref, sin_t_ref,
                 w_t_ref, w_zb_ref, w_xb_ref, w_bc_ref, w_g_ref, w_za_ref,
                 cwx_ref, cbx_ref, cwbc_ref, cbbc_ref,
                 q_t_ref, qi_t_ref, wi_t_ref, dt_t_ref, v_t_ref, k_ref, kidx_ref,
                 zb_ref, xs_ref, bcs_ref, g_ref, za_ref,
                 tailx_scr, tailbc_scr):
    @pl.when(pl.program_id(1) == 0)
    def _():
        tailx_scr[...] = jnp.zeros_like(tailx_scr)
        tailbc_scr[...] = jnp.zeros_like(tailbc_scr)

    x = x_ref[0]
    h = x * lax.rsqrt(jnp.mean(x * x, axis=-1, keepdims=True) + EPS) * nw_ref[...]
    h = h.astype(_BF16)

    t = lax.dot_general(w_t_ref[...], h, _NT, preferred_element_type=_F32)
    cos_t = cos_t_ref[0]
    sin_t = sin_t_ref[0]

    def rope_head(block, hd):
        x1 = block[hd * HEAD_DIM: hd * HEAD_DIM + ROT_HALF]
        x2 = block[hd * HEAD_DIM + ROT_HALF: hd * HEAD_DIM + ROT_DIM]
        return jnp.concatenate([x1 * cos_t - x2 * sin_t, x2 * cos_t + x1 * sin_t], axis=0)

    def rope_store(block, n_heads, scale, out_ref):
        out_ref[0] = (block * scale).astype(out_ref.dtype)
        for hd in range(n_heads):
            out_ref[0, hd * HEAD_DIM: hd * HEAD_DIM + ROT_DIM, :] = (
                rope_head(block, hd) * scale).astype(out_ref.dtype)

    def rope_value(block, n_heads):
        parts = []
        for hd in range(n_heads):
            parts += [rope_head(block, hd), block[hd * HEAD_DIM + ROT_DIM:(hd + 1) * HEAD_DIM]]
        if block.shape[0] > n_heads * HEAD_DIM:
            parts.append(block[n_heads * HEAD_DIM:])
        return jnp.concatenate(parts, axis=0)

    o_qi = ATTN_WIDTH
    o_k = o_qi + IDX_HEADS * IDX_DIM
    o_ki = o_k + KV_WIDTH
    o_v = o_ki + LANES
    rope_store(t[:o_qi], ATTN_HEADS, LOG2_E * HEAD_DIM ** -0.5, q_t_ref)
    rope_store(t[o_qi:o_k], IDX_HEADS, IDX_DIM ** -0.5, qi_t_ref)
    o_wi = o_ki + IDX_DIM
    o_dt = o_wi + W_IDX_ROWS
    wi_t_ref[0] = t[o_wi:o_dt] * (IDX_HEADS ** -0.5)
    dt_t_ref[0] = t[o_dt:o_dt + SSM_HEADS]
    for g in range(ATTN_KV_HEADS):
        v_t_ref[0, g * V_ROWS:g * V_ROWS + HEAD_DIM, :] = t[o_v + g * HEAD_DIM:o_v + (g + 1) * HEAD_DIM].astype(_BF16)
        v_t_ref[0, g * V_ROWS + HEAD_DIM:(g + 1) * V_ROWS, :] = jnp.ones((V_ROWS - HEAD_DIM, x.shape[0]), _BF16)
    k_ref[0] = rope_value(t[o_k:o_ki], ATTN_KV_HEADS).T.astype(_BF16)
    kidx_ref[0] = rope_value(t[o_ki:o_v], 1).T[:, :IDX_DIM].astype(_BF16)

    def plain(out_ref, w_ref, c0, width):
        out_ref[0, :, c0:c0 + width] = _dot(h, w_ref[:, c0:c0 + width]).astype(out_ref.dtype)

    def conv(out_ref, w_ref, tail_scr, cw_ref, cb_ref, c0, width):
        sl = slice(c0, c0 + width)
        out_ref[0, :, sl] = _causal_conv_silu(
            _dot(h, w_ref[:, sl]), tail_scr.at[:, sl], cw_ref.at[:, sl], cb_ref.at[:, sl]).astype(_BF16)

    cc = CONV_COLS
    for n in range(SSM_WIDTH // cc):
        conv(xs_ref, w_xb_ref, tailx_scr, cwx_ref, cbx_ref, n * cc, cc)
        plain(g_ref, w_g_ref, 2 * n * cc, 2 * cc)
    for n in range(BC_WIDTH // cc):
        conv(bcs_ref, w_bc_ref, tailbc_scr, cwbc_ref, cbbc_ref, n * cc, cc)
        plain(zb_ref, w_zb_ref, 2 * n * cc, 2 * cc)
    plain(za_ref, w_za_ref, 0, ATTN_WIDTH)


def _const_spec(shape):
    nd = len(shape)
    return pl.BlockSpec(shape, lambda *_: (0,) * nd, pipeline_mode=pl.Buffered(1))


def _projection(x, nw, cos_t, sin_t, weights, conv_params):
    bsz, seq, _ = x.shape
    tm = PROJ_ROWS
    tok = lambda width: pl.BlockSpec((1, tm, width), lambda b, i: (b, i, 0))
    feat = lambda rows: pl.BlockSpec((1, rows, tm), lambda b, i: (b, 0, i))
    in_specs = [tok(D_MODEL), _const_spec((1, D_MODEL)), feat(ROT_HALF), feat(ROT_HALF)
                ] + [_const_spec(w.shape) for w in weights + conv_params]
    out_shape = (
        jax.ShapeDtypeStruct((bsz, ATTN_WIDTH, seq), _BF16),
        jax.ShapeDtypeStruct((bsz, IDX_HEADS * IDX_DIM, seq), _BF16),
        jax.ShapeDtypeStruct((bsz, W_IDX_ROWS, seq), _F32),
        jax.ShapeDtypeStruct((bsz, SSM_HEADS, seq), _F32),
        jax.ShapeDtypeStruct((bsz, ATTN_KV_HEADS * V_ROWS, seq), _BF16),
        jax.ShapeDtypeStruct((bsz, seq, KV_WIDTH), _BF16),
        jax.ShapeDtypeStruct((bsz, seq, IDX_DIM), _BF16),
        jax.ShapeDtypeStruct((bsz, seq, SSM_WIDTH), _F32),
        jax.ShapeDtypeStruct((bsz, seq, SSM_WIDTH), _BF16),
        jax.ShapeDtypeStruct((bsz, seq, BC_WIDTH), _BF16),
        jax.ShapeDtypeStruct((bsz, seq, N_BRANCH * D_MODEL), _BF16),
        jax.ShapeDtypeStruct((bsz, seq, ATTN_WIDTH), _F32),
    )
    out_specs = (feat(ATTN_WIDTH), feat(IDX_HEADS * IDX_DIM), feat(W_IDX_ROWS), feat(SSM_HEADS),
                 feat(ATTN_KV_HEADS * V_ROWS), tok(KV_WIDTH), tok(IDX_DIM), tok(SSM_WIDTH), tok(SSM_WIDTH),
                 tok(BC_WIDTH), tok(N_BRANCH * D_MODEL), tok(ATTN_WIDTH))
    return pl.pallas_call(
        _proj_kernel,
        grid=(bsz, seq // tm),
        in_specs=in_specs,
        out_specs=out_specs,
        out_shape=out_shape,
        scratch_shapes=[pltpu.VMEM((SUBLANES, SSM_WIDTH), _F32), pltpu.VMEM((SUBLANES, BC_WIDTH), _F32)],
        compiler_params=pltpu.CompilerParams(
            dimension_semantics=("parallel", "arbitrary"), vmem_limit_bytes=VMEM_LIMIT_BYTES),
        name="proj",
    )(x, nw, cos_t, sin_t, *weights, *conv_params)


def _attn_kernel(q_t_ref, qi_t_ref, wi_t_ref, k_ref, kidx_ref, v_t_ref, za_ref, o_ref,
                 score_scr, score16_scr, sel_scr, bias_scr, qpad_scr, m_scr, alpha_scr, acc_scr, s_scr,
                 *, top_k):
    j = pl.program_id(1)
    n_chunks = j + 1
    tq = Q_TILE
    q_pos = j * tq + lax.broadcasted_iota(jnp.int32, (1, tq), 1)
    key_iota = lax.broadcasted_iota(jnp.int32, (KEY_TILE, tq), 0)

    qi_t = qi_t_ref[0]
    qi_cat = jnp.concatenate([qi_t[hd * IDX_DIM:(hd + 1) * IDX_DIM] for hd in range(IDX_HEADS)], axis=1)
    wi_t = wi_t_ref[0]
    wi_cat = jnp.concatenate([wi_t[hd:hd + 1] for hd in range(IDX_HEADS)], axis=1)

    def score_chunk(c, carry):
        start = pl.multiple_of(c * KEY_TILE, KEY_TILE)
        logits = _dot(kidx_ref[0, pl.ds(start, KEY_TILE), :], qi_cat)
        weighted = jnp.maximum(logits, 0.0) * wi_cat
        score = weighted[:, 0:tq]
        for hd in range(1, IDX_HEADS):
            score = score + weighted[:, hd * tq:(hd + 1) * tq]
        score = jnp.where((start + key_iota) <= q_pos, score, -jnp.inf)
        score_scr[pl.ds(start, KEY_TILE), :] = score
        score16_scr[pl.ds(start, KEY_TILE), :] = score.astype(_BF16)
        return carry

    lax.fori_loop(0, n_chunks, score_chunk, 0)

    def code_to_f32(code):
        return pltpu.bitcast(code ^ ((code >> 31) & jnp.int32(0x7FFFFFFF)), _F32)

    def count_ge(ref, cand, dtype):
        def body(c, acc):
            start = pl.multiple_of(c * KEY_TILE, KEY_TILE)
            inc = jnp.where(ref[pl.ds(start, KEY_TILE), :] >= cand, jnp.ones((), dtype), jnp.zeros((), dtype))
            for part in range(KEY_TILE // COUNT_ROWS):
                acc = acc + inc[part * COUNT_ROWS:(part + 1) * COUNT_ROWS]
            return acc
        acc = lax.fori_loop(0, n_chunks, body, jnp.zeros((COUNT_ROWS, tq), dtype))
        return jnp.sum(acc.astype(_F32), axis=0, keepdims=True)

    def bf16_code(block):
        return jnp.where(block >= 0, block, block | jnp.int32(0xFFFF))

    def coarse_bit(i, block):
        cand = block + (jnp.int32(1) << (31 - i))
        hit = count_ge(score16_scr, code_to_f32(bf16_code(cand)).astype(_BF16), _BF16) >= top_k
        return jnp.where(hit, cand, block)

    def fine_bit(i, carry):
        code, n_above = carry
        cand = code + (jnp.int32(1) << (16 - i))
        cnt = count_ge(score_scr, code_to_f32(cand), _F32)
        hit = cnt >= top_k
        return jnp.where(hit, cand, code), jnp.where(hit, n_above, cnt)

    sel_scr[0:1, :] = jnp.full((1, tq), -jnp.inf, _F32)
    sel_scr[1:2, :] = jnp.zeros((1, tq), _F32)

    @pl.when((j + 1) * tq > top_k)
    def _():
        int_min = jnp.int32(-2 ** 31)
        coarse = lax.fori_loop(0, 16, coarse_bit, jnp.full((1, tq), int_min, jnp.int32))
        base = bf16_code(coarse) - jnp.int32(1 << 16)
        fine, n_above = lax.fori_loop(0, 17, fine_bit, (base, jnp.zeros((1, tq), _F32)))
        few = (q_pos + 1) <= top_k
        sel_scr[0:1, :] = jnp.where(few, -jnp.inf, code_to_f32(fine))
        sel_scr[1:2, :] = jnp.where(few, 0.0, top_k - n_above)

    thr = sel_scr[0:1, :]
    n_ties_kept = sel_scr[1:2, :]

    row = lax.broadcasted_iota(jnp.int32, (KEY_TILE, KEY_TILE), 0)
    col = lax.broadcasted_iota(jnp.int32, (KEY_TILE, KEY_TILE), 1)
    strict_lower = jnp.where(col < row, 1.0, 0.0).astype(_BF16)

    def bias_tile(c, ties_before):
        start = pl.multiple_of(c * KEY_TILE, KEY_TILE)
        s = score_scr[pl.ds(start, KEY_TILE), :]
        tie = jnp.where(s == thr, 1.0, 0.0)
        rank = _dot(strict_lower, tie.astype(_BF16)) + ties_before
        keep = (s > thr) | ((s == thr) & (rank < n_ties_kept))
        bias_scr[...] = jnp.where(keep, 0.0, NEG_BIG)
        return ties_before + jnp.sum(tie, axis=0, keepdims=True)

    n_pairs = ATTN_HEADS // 2
    zeros_half = jnp.zeros((HEAD_DIM, 2 * tq), _BF16)
    for pair in range(n_pairs):
        g = (2 * pair) // ATTN_GROUP
        q_cat = jnp.concatenate([q_t_ref[0, (2 * pair + hd) * HEAD_DIM:(2 * pair + hd + 1) * HEAD_DIM, :]
                                 for hd in range(2)], axis=1)
        qpad_scr[pair] = jnp.concatenate([q_cat, zeros_half] if g == 0 else [zeros_half, q_cat], axis=0)
    m_scr[...] = jnp.full(m_scr.shape, NEG_BIG, _F32)
    acc_scr[...] = jnp.zeros(acc_scr.shape, _F32)

    def logits_stage(c, pair):
        start = pl.multiple_of(c * KEY_TILE, KEY_TILE)
        b = bias_scr[...]
        s = _dot(k_ref[0, pl.ds(start, KEY_TILE), :], qpad_scr[pair]) + jnp.concatenate([b, b], axis=1)
        m_old = m_scr[pair]
        m_new = jnp.maximum(m_old, jnp.max(s, axis=0, keepdims=True))
        s_scr[pair] = s
        alpha_scr[pair] = jnp.exp2(m_old - m_new)
        m_scr[pair] = m_new

    def value_stage(c, pair):
        start = pl.multiple_of(c * KEY_TILE, KEY_TILE)
        g = (2 * pair) // ATTN_GROUP
        p = jnp.exp2(s_scr[pair] - m_scr[pair])
        v_c = v_t_ref[0, g * V_ROWS:(g + 1) * V_ROWS, pl.ds(start, KEY_TILE)]
        acc_scr[pair] = alpha_scr[pair] * acc_scr[pair] + _dot(v_c, p.astype(_BF16))

    ties = bias_tile(0, jnp.zeros((1, tq), _F32))
    for pair in range(n_pairs):
        logits_stage(0, pair)

    def attn_tile(c, ties):
        ties = bias_tile(c + 1, ties)
        for pair in range(n_pairs):
            value_stage(c, pair)
            logits_stage(c + 1, pair)
        return ties

    lax.fori_loop(0, n_chunks - 1, attn_tile, ties)
    for pair in range(n_pairs):
        value_stage(n_chunks - 1, pair)

    for pair in range(n_pairs):
        o_t = acc_scr[pair, :HEAD_DIM] / acc_scr[pair, HEAD_DIM:HEAD_DIM + 1]
        both = jnp.concatenate([o_t[:, :tq], o_t[:, tq:]], axis=0)
        z = za_ref[0, :, pair * LANES:(pair + 1) * LANES]
        o_ref[0, :, pair * LANES:(pair + 1) * LANES] = (both.T * _silu(z)).astype(o_ref.dtype)


def _attention(q_t, qi_t, wi_t, k, kidx, v_t, za):
    bsz, seq, _ = k.shape
    tq = Q_TILE
    top_k = min(TOPK_MAX, seq // 4)
    feat = lambda rows: pl.BlockSpec((1, rows, tq), lambda b, j: (b, 0, j))
    full = lambda s1, s2: pl.BlockSpec((1, s1, s2), lambda b, j: (b, 0, 0))
    return pl.pallas_call(
        functools.partial(_attn_kernel, top_k=top_k),
        grid=(bsz, seq // tq),
        in_specs=[feat(ATTN_WIDTH), feat(IDX_HEADS * IDX_DIM), feat(W_IDX_ROWS),
                  full(seq, KV_WIDTH), full(seq, IDX_DIM), full(ATTN_KV_HEADS * V_ROWS, seq),
                  pl.BlockSpec((1, tq, ATTN_WIDTH), lambda b, j: (b, j, 0))],
        out_specs=pl.BlockSpec((1, tq, ATTN_WIDTH), lambda b, j: (b, j, 0)),
        out_shape=jax.ShapeDtypeStruct((bsz, seq, ATTN_WIDTH), _BF16),
        scratch_shapes=[pltpu.VMEM((seq, tq), _F32), pltpu.VMEM((seq, tq), _BF16),
                        pltpu.VMEM((SUBLANES, tq), _F32), pltpu.VMEM((KEY_TILE, tq), _F32),
                        pltpu.VMEM((ATTN_HEADS // 2, 2 * HEAD_DIM, 2 * tq), _BF16),
                        pltpu.VMEM((ATTN_HEADS // 2, 1, 2 * tq), _F32),
                        pltpu.VMEM((ATTN_HEADS // 2, 1, 2 * tq), _F32),
                        pltpu.VMEM((ATTN_HEADS // 2, V_ROWS, 2 * tq), _F32),
                        pltpu.VMEM((ATTN_HEADS // 2, KEY_TILE, 2 * tq), _F32)],
        compiler_params=pltpu.CompilerParams(
            dimension_semantics=("parallel", "arbitrary"), vmem_limit_bytes=VMEM_LIMIT_BYTES),
        name="attn",
    )(q_t, qi_t, wi_t, k, kidx, v_t, za)


def _split3(x):
    hi = x.astype(_BF16)
    r1 = x - hi.astype(_F32)
    mid = r1.astype(_BF16)
    lo = (r1 - mid.astype(_F32)).astype(_BF16)
    return hi, mid, lo


def _ssd_kernel(xs_ref, bcs_ref, dt_ref, zb_ref, dtb_ref, alog_ref, dskip_ref, nw_ref, o_ref,
                state_scr, y_scr, dt_scr, acum_scr, acum_t_scr):
    i = pl.program_id(1)
    r = lax.broadcasted_iota(jnp.int32, (CHUNK, CHUNK), 0)
    c = lax.broadcasted_iota(jnp.int32, (CHUNK, CHUNK), 1)
    lower = c <= r

    @pl.when(i == 0)
    def _():
        state_scr[...] = jnp.zeros_like(state_scr)
        upper_b = jnp.where(r <= c, 1.0, 0.0).astype(_BF16)
        dt_in_t = dt_ref[0] + dtb_ref[...]
        dt_t = jnp.maximum(dt_in_t, 0.0) + jnp.log1p(jnp.exp(-jnp.abs(dt_in_t)))
        parts = _split3(dt_t * (-jnp.exp(alog_ref[...])))
        pad = jnp.zeros((LANES - SSM_HEADS, CHUNK), _F32)
        for n in range(dt_t.shape[1] // CHUNK):
            sl = slice(n * CHUNK, (n + 1) * CHUNK)
            a_cum_t = sum(_dot(part[:, sl], upper_b) for part in parts)
            acum_t_scr[:, sl] = a_cum_t
            acum_scr[sl, :] = jnp.concatenate([a_cum_t, pad], axis=0).T
            dt_scr[sl, :] = jnp.concatenate([dt_t[:, sl], pad], axis=0).T

    start = pl.multiple_of(i * CHUNK, CHUNK)
    dt = dt_scr[pl.ds(start, CHUNK), :]
    a_cum = acum_scr[pl.ds(start, CHUNK), :]
    a_cum_t = acum_t_scr[:, pl.ds(start, CHUNK)]

    lane_hi = lax.broadcasted_iota(jnp.int32, (1, LANES), 1) >= SSM_HEAD_DIM

    for grp in range(SSM_GROUPS):
        blk, half = divmod(grp, 2)
        b_blk = bcs_ref[0, :, blk * LANES:(blk + 1) * LANES]
        c_blk = bcs_ref[0, :, BC_WIDTH // 2 + blk * LANES: BC_WIDTH // 2 + (blk + 1) * LANES]
        in_grp = lane_hi if half else jnp.logical_not(lane_hi)
        c_g = jnp.where(in_grp, c_blk, jnp.zeros_like(c_blk))
        cb = lax.dot_general(c_g, b_blk, _NT, preferred_element_type=_F32)
        cb = jnp.where(lower, cb, 0.0)

        for pr in range(2):
            pair = grp * 2 + pr
            h_a, h_b = 2 * pair, 2 * pair + 1
            cols = slice(pair * LANES, (pair + 1) * LANES)

            def per_pair(arr):
                return jnp.where(lane_hi, arr[:, h_b:h_b + 1], arr[:, h_a:h_a + 1])

            acum_bc = {hh: jnp.broadcast_to(a_cum[:, hh:hh + 1], (CHUNK, LANES)) for hh in (h_a, h_b)}
            dt_p = per_pair(dt)
            acum_p = jnp.where(lane_hi, acum_bc[h_b], acum_bc[h_a])
            last_p = per_pair(a_cum[CHUNK - 1:CHUNK])
            x_p = xs_ref[0, :, cols].astype(_F32)
            xd = x_p * dt_p

            y = None
            for hh, keep_hi in ((h_a, False), (h_b, True)):
                seg = acum_bc[hh] - a_cum_t[hh:hh + 1, :]
                m_h = (cb * jnp.exp(jnp.minimum(seg, 0.0))).astype(_BF16)
                sel = lane_hi if keep_hi else jnp.logical_not(lane_hi)
                part = _dot(m_h, jnp.where(sel, xd, 0.0).astype(_BF16))
                y = part if y is None else y + part

            st_prev = state_scr[pair]
            y_off = _dot(c_g, st_prev.astype(_BF16)) * jnp.exp(acum_p)
            w = (xd * jnp.exp(last_p - acum_p)).astype(_BF16)
            st_new = lax.dot_general(b_blk, w, _TN, preferred_element_type=_F32)
            state_scr[pair] = jnp.exp(last_p) * st_prev + st_new
            y_scr[:, cols] = y + y_off + dskip_ref[:, cols] * x_p

    gw = SSM_WIDTH // SSM_GROUPS
    for grp in range(SSM_GROUPS):
        sl = slice(grp * gw, (grp + 1) * gw)
        yz = y_scr[:, sl] * _silu(zb_ref[0, :, sl])
        ms = jnp.mean(yz * yz, axis=-1, keepdims=True)
        o_ref[0, :, sl] = (yz * lax.rsqrt(ms + EPS) * nw_ref[:, sl]).astype(o_ref.dtype)


def _ssd(xs, bcs, dt_t, zb, dtb, alog, dskip, nw):
    bsz, seq, _ = xs.shape
    tok = lambda width: pl.BlockSpec((1, CHUNK, width), lambda b, i: (b, i, 0))
    consts = (dtb, alog, dskip, nw)
    return pl.pallas_call(
        _ssd_kernel,
        grid=(bsz, seq // CHUNK),
        in_specs=[tok(SSM_WIDTH), tok(BC_WIDTH), pl.BlockSpec((1, SSM_HEADS, seq), lambda b, i: (b, 0, 0)),
                  tok(SSM_WIDTH)] + [_const_spec(a.shape) for a in consts],
        out_specs=tok(SSM_WIDTH),
        out_shape=jax.ShapeDtypeStruct((bsz, seq, SSM_WIDTH), _BF16),
        scratch_shapes=[pltpu.VMEM((SSM_HEADS // 2, 2 * SSM_STATE, LANES), _F32),
                        pltpu.VMEM((CHUNK, SSM_WIDTH), _F32),
                        pltpu.VMEM((seq, LANES), _F32), pltpu.VMEM((seq, LANES), _F32),
                        pltpu.VMEM((SSM_HEADS, seq), _F32)],
        compiler_params=pltpu.CompilerParams(
            dimension_semantics=("parallel", "arbitrary"), vmem_limit_bytes=VMEM_LIMIT_BYTES),
        name="ssd",
    )(xs, bcs, dt_t, zb, *consts)


def _merge_kernel(x_ref, oa_ref, ob_ref, g_ref, gb_ref, wa_ref, wb_ref, wo_ref, fw_ref, o_ref, *, final_norm):
    gates = jax.nn.sigmoid(g_ref[...] + gb_ref[...])
    merged = (gates[:, :D_MODEL] * _dot(oa_ref[...], wa_ref[...])
              + gates[:, D_MODEL:] * _dot(ob_ref[...], wb_ref[...]))
    y = x_ref[...] + _dot(merged.astype(_BF16), wo_ref[...])
    if final_norm:
        y = y * lax.rsqrt(jnp.mean(y * y, axis=-1, keepdims=True) + EPS) * fw_ref[...]
    o_ref[...] = y


def _merge(x2, oa2, ob2, g2, gate_bias, wa, wb, wo, fw, final_norm):
    rows = x2.shape[0]
    tm = MERGE_ROWS
    tok = lambda width: pl.BlockSpec((tm, width), lambda i: (i, 0))
    consts = (gate_bias, wa, wb, wo, fw)
    return pl.pallas_call(
        functools.partial(_merge_kernel, final_norm=final_norm),
        grid=(rows // tm,),
        in_specs=[tok(D_MODEL), tok(ATTN_WIDTH), tok(SSM_WIDTH), tok(N_BRANCH * D_MODEL)]
                 + [_const_spec(a.shape) for a in consts],
        out_specs=tok(D_MODEL),
        out_shape=jax.ShapeDtypeStruct((rows, D_MODEL), _F32),
        compiler_params=pltpu.CompilerParams(
            dimension_semantics=("parallel",), vmem_limit_bytes=VMEM_LIMIT_BYTES),
        name="merge",
    )(x2, oa2, ob2, g2, *consts)


def _cast_kernel(w_ref, o_ref):
    o_ref[...] = w_ref[0].astype(o_ref.dtype)


def _layer_transposed_bf16(w_all, layer):
    w_t_all = jnp.swapaxes(w_all, 1, 2)
    _, rows, cols = w_t_all.shape
    tm = CAST_ROWS
    return pl.pallas_call(
        _cast_kernel,
        grid=(pl.cdiv(rows, tm),),
        in_specs=[pl.BlockSpec((1, tm, cols), lambda i: (layer, i, 0))],
        out_specs=pl.BlockSpec((tm, cols), lambda i: (i, 0)),
        out_shape=jax.ShapeDtypeStruct((rows, cols), _BF16),
        compiler_params=pltpu.CompilerParams(dimension_semantics=("parallel",)),
        name="cast",
    )(w_t_all)


def _layer_weights(w_in_all, layer):
    wt16 = _layer_transposed_bf16(w_in_all, layer)
    o = SPLIT_OFFSETS
    seg = lambda i: wt16[o[i]:o[i + 1]]
    w_q, w_k, w_v, w_za, w_qi, w_ki, w_wi, w_zb, w_xb, w_b, w_c, w_dt, w_g = (seg(i) for i in range(13))
    pad = lambda w, n: jnp.pad(w, ((0, n - w.shape[0]), (0, 0)))
    w_misc = pad(jnp.concatenate([w_ki, pad(w_wi, W_IDX_ROWS), w_dt], axis=0), LANES)
    w_t = jnp.concatenate([w_q, w_qi, w_k, w_misc, w_v], axis=0)
    return w_t, w_zb.T, w_xb.T, jnp.concatenate([w_b, w_c], axis=0).T, w_g.T, w_za.T


def _rope_tables(positions):
    inv_freq = ROPE_THETA ** (-jnp.arange(0, ROT_DIM, 2, dtype=_F32) / ROT_DIM)
    ang = jnp.swapaxes(positions.astype(_F32)[..., None] * inv_freq, 1, 2)
    return jnp.cos(ang), jnp.sin(ang)


def kernel(x, positions, norm_w, w_in, gate_bias, conv_w, conv_b, dt_bias, a_log, d_skip,
           ssm_norm_w, w_branch_a, w_branch_b, w_out, final_norm_w):
    bsz, seq, _ = x.shape
    depth = norm_w.shape[0]
    cos_t, sin_t = _rope_tables(positions)
    for i in range(depth):
        weights = _layer_weights(w_in, i)
        conv_params = (conv_w[i][:, :SSM_WIDTH], conv_b[i][None, :SSM_WIDTH],
                       conv_w[i][:, SSM_WIDTH:], conv_b[i][None, SSM_WIDTH:])
        (q_t, qi_t, wi_t, dt_t, v_t, k, kidx, zb, xs, bcs, gates, za) = _projection(
            x, norm_w[i][None, :], cos_t, sin_t, weights, conv_params)
        o_a = _attention(q_t, qi_t, wi_t, k, kidx, v_t, za)
        o_b = _ssd(xs, bcs, dt_t, zb, dt_bias[i][:, None], a_log[i][:, None],
                   jnp.repeat(d_skip[i], SSM_HEAD_DIM)[None, :], ssm_norm_w[i][None, :])
        x = _merge(x.reshape(bsz * seq, D_MODEL), o_a.reshape(bsz * seq, ATTN_WIDTH),
                   o_b.reshape(bsz * seq, SSM_WIDTH), gates.reshape(bsz * seq, N_BRANCH * D_MODEL),
                   gate_bias[i][None, :], w_branch_a[i].astype(_BF16), w_branch_b[i].astype(_BF16),
                   w_out[i].astype(_BF16), final_norm_w[None, :],
                   final_norm=(i == depth - 1)).reshape(bsz, seq, D_MODEL)
    return x
```

```python
import functools

import numpy as np
import jax
import jax.numpy as jnp
from jax import lax
from jax.experimental import pallas as pl
from jax.experimental.pallas import tpu as pltpu

D_MODEL = 1024
ATTN_HEADS = 8
ATTN_KV_HEADS = 2
HEAD_DIM = 64
ATTN_GROUP = ATTN_HEADS // ATTN_KV_HEADS
ATTN_WIDTH = ATTN_HEADS * HEAD_DIM
KV_WIDTH = ATTN_KV_HEADS * HEAD_DIM
ROT_DIM = HEAD_DIM // 4
ROT_HALF = ROT_DIM // 2
ROPE_THETA = 500000.0
IDX_HEADS = 4
IDX_DIM = 64
TOPK_MAX = 256
SSM_HEADS = 16
SSM_HEAD_DIM = 64
SSM_WIDTH = SSM_HEADS * SSM_HEAD_DIM
SSM_GROUPS = 4
SSM_STATE = 64
CONV_K = 4
CHUNK = 128
BC_WIDTH = 2 * SSM_GROUPS * SSM_STATE
N_BRANCH = 2
EPS = 1e-6
SPLIT_SIZES = (ATTN_WIDTH, KV_WIDTH, KV_WIDTH, ATTN_WIDTH,
               IDX_HEADS * IDX_DIM, IDX_DIM, IDX_HEADS,
               SSM_WIDTH, SSM_WIDTH, SSM_GROUPS * SSM_STATE, SSM_GROUPS * SSM_STATE, SSM_HEADS,
               N_BRANCH * D_MODEL)
SPLIT_OFFSETS = tuple(int(o) for o in np.cumsum((0,) + SPLIT_SIZES))

LANES = 128
SUBLANES = 8
VMEM_LIMIT_BYTES = 56 * 1024 * 1024

PROJ_ROWS = 256
CONV_ROWS = 256
CAST_ROWS = 512
Q_TILE = 256
KEY_TILE = 256
COUNT_ROWS = 32
LOG2_E = 1.4426950408889634
MERGE_ROWS = 512
W_IDX_ROWS = 8
V_ROWS = HEAD_DIM + 16
NEG_BIG = -1e30

_F32 = jnp.float32
_BF16 = jnp.bfloat16
_NT = (((1,), (1,)), ((), ()))
_TN = (((0,), (0,)), ((), ()))


def _dot(a, b):
    return jnp.dot(a, b, preferred_element_type=_F32)


def _silu(x):
    return x * jax.nn.sigmoid(x)


def _causal_conv_silu_t(x, tail_ref, w_ref, b_ref):
    cols = x.shape[1]
    reps = cols // LANES
    tail = tail_ref[...]
    lane = lax.broadcasted_iota(jnp.int32, (x.shape[0], LANES), 1)
    wide = lambda a: jnp.concatenate([a] * reps, axis=1)
    y = x * wide(w_ref[CONV_K - 1]) + wide(b_ref[...])
    for shift in range(1, CONV_K):
        rolled = pltpu.roll(x, shift, 1)
        head = jnp.where(lane < shift, pltpu.roll(tail, shift, 1), rolled[:, :LANES])
        shifted = jnp.concatenate([head, rolled[:, LANES:]], axis=1)
        y = y + shifted * wide(w_ref[CONV_K - 1 - shift])
    tail_ref[...] = x[:, cols - LANES:]
    return _silu(y)


def _proj_kernel(x_ref, nw_ref, cos_t_ref, sin_t_ref,
                 w_t_ref, w_zb_ref, w_xb_ref, w_bc_ref, w_g_ref, w_za_ref,
                 cwx_ref, cbx_ref, cwbc_ref, cbbc_ref,
                 q_t_ref, qi_t_ref, wi_t_ref, dt_t_ref, v_t_ref, k_ref, kidx_ref,
                 zb_t_ref, xs_t_ref, bcs_t_ref, g_ref, za_ref,
                 tailx_scr, tailbc_scr):
    @pl.when(pl.program_id(1) == 0)
    def _():
        tailx_scr[...] = jnp.zeros_like(tailx_scr)
        tailbc_scr[...] = jnp.zeros_like(tailbc_scr)

    x = x_ref[0]
    h = x * lax.rsqrt(jnp.mean(x * x, axis=-1, keepdims=True) + EPS) * nw_ref[...]
    h = h.astype(_BF16)

    t = lax.dot_general(w_t_ref[...], h, _NT, preferred_element_type=_F32)
    cos_t = cos_t_ref[0]
    sin_t = sin_t_ref[0]

    def rope_head(block, hd):
        x1 = block[hd * HEAD_DIM: hd * HEAD_DIM + ROT_HALF]
        x2 = block[hd * HEAD_DIM + ROT_HALF: hd * HEAD_DIM + ROT_DIM]
        return jnp.concatenate([x1 * cos_t - x2 * sin_t, x2 * cos_t + x1 * sin_t], axis=0)

    def rope_store(block, n_heads, scale, out_ref):
        out_ref[0] = (block * scale).astype(out_ref.dtype)
        for hd in range(n_heads):
            out_ref[0, hd * HEAD_DIM: hd * HEAD_DIM + ROT_DIM, :] = (
                rope_head(block, hd) * scale).astype(out_ref.dtype)

    def rope_value(block, n_heads):
        parts = []
        for hd in range(n_heads):
            parts += [rope_head(block, hd), block[hd * HEAD_DIM + ROT_DIM:(hd + 1) * HEAD_DIM]]
        if block.shape[0] > n_heads * HEAD_DIM:
            parts.append(block[n_heads * HEAD_DIM:])
        return jnp.concatenate(parts, axis=0)

    o_qi = ATTN_WIDTH
    o_k = o_qi + IDX_HEADS * IDX_DIM
    o_ki = o_k + KV_WIDTH
    o_v = o_ki + LANES
    rope_store(t[:o_qi], ATTN_HEADS, LOG2_E * HEAD_DIM ** -0.5, q_t_ref)
    rope_store(t[o_qi:o_k], IDX_HEADS, IDX_DIM ** -0.5, qi_t_ref)
    o_wi = o_ki + IDX_DIM
    o_dt = o_wi + W_IDX_ROWS
    wi_t_ref[0] = t[o_wi:o_dt] * (IDX_HEADS ** -0.5)
    dt_t_ref[0] = t[o_dt:o_dt + SSM_HEADS]
    for g in range(ATTN_KV_HEADS):
        v_t_ref[0, g * V_ROWS:g * V_ROWS + HEAD_DIM, :] = t[o_v + g * HEAD_DIM:o_v + (g + 1) * HEAD_DIM].astype(_BF16)
        v_t_ref[0, g * V_ROWS + HEAD_DIM:(g + 1) * V_ROWS, :] = jnp.ones((V_ROWS - HEAD_DIM, x.shape[0]), _BF16)
    k_ref[0] = rope_value(t[o_k:o_ki], ATTN_KV_HEADS).T.astype(_BF16)
    kidx_ref[0] = rope_value(t[o_ki:o_v], 1).T[:, :IDX_DIM].astype(_BF16)

    def feat_major(w_ref, r0, rows):
        return lax.dot_general(w_ref[r0:r0 + rows], h, _NT, preferred_element_type=_F32)

    rc = CONV_ROWS
    for n in range(SSM_WIDTH // rc):
        sl = slice(n * rc, (n + 1) * rc)
        xs_t_ref[0, sl, :] = _causal_conv_silu_t(
            feat_major(w_xb_ref, n * rc, rc), tailx_scr.at[sl], cwx_ref.at[:, sl], cbx_ref.at[sl]).astype(_BF16)
        zb_t_ref[0, sl, :] = feat_major(w_zb_ref, n * rc, rc)
    for n in range(BC_WIDTH // rc):
        sl = slice(n * rc, (n + 1) * rc)
        bcs_t_ref[0, sl, :] = _causal_conv_silu_t(
            feat_major(w_bc_ref, n * rc, rc), tailbc_scr.at[sl], cwbc_ref.at[:, sl], cbbc_ref.at[sl]).astype(_BF16)
    g_ref[0] = _dot(h, w_g_ref[...]).astype(_BF16)
    za_ref[0] = _dot(h, w_za_ref[...])


def _const_spec(shape):
    nd = len(shape)
    return pl.BlockSpec(shape, lambda *_: (0,) * nd, pipeline_mode=pl.Buffered(1))


def _projection(x, nw, cos_t, sin_t, weights, conv_params):
    bsz, seq, _ = x.shape
    tm = PROJ_ROWS
    tok = lambda width: pl.BlockSpec((1, tm, width), lambda b, i: (b, i, 0))
    feat = lambda rows: pl.BlockSpec((1, rows, tm), lambda b, i: (b, 0, i))
    in_specs = [tok(D_MODEL), _const_spec((1, D_MODEL)), feat(ROT_HALF), feat(ROT_HALF)
                ] + [_const_spec(w.shape) for w in weights + conv_params]
    out_shape = (
        jax.ShapeDtypeStruct((bsz, ATTN_WIDTH, seq), _BF16),
        jax.ShapeDtypeStruct((bsz, IDX_HEADS * IDX_DIM, seq), _BF16),
        jax.ShapeDtypeStruct((bsz, W_IDX_ROWS, seq), _F32),
        jax.ShapeDtypeStruct((bsz, SSM_HEADS, seq), _F32),
        jax.ShapeDtypeStruct((bsz, ATTN_KV_HEADS * V_ROWS, seq), _BF16),
        jax.ShapeDtypeStruct((bsz, seq, KV_WIDTH), _BF16),
        jax.ShapeDtypeStruct((bsz, seq, IDX_DIM), _BF16),
        jax.ShapeDtypeStruct((bsz, SSM_WIDTH, seq), _F32),
        jax.ShapeDtypeStruct((bsz, SSM_WIDTH, seq), _BF16),
        jax.ShapeDtypeStruct((bsz, BC_WIDTH, seq), _BF16),
        jax.ShapeDtypeStruct((bsz, seq, N_BRANCH * D_MODEL), _BF16),
        jax.ShapeDtypeStruct((bsz, seq, ATTN_WIDTH), _F32),
    )
    out_specs = (feat(ATTN_WIDTH), feat(IDX_HEADS * IDX_DIM), feat(W_IDX_ROWS), feat(SSM_HEADS),
                 feat(ATTN_KV_HEADS * V_ROWS), tok(KV_WIDTH), tok(IDX_DIM), feat(SSM_WIDTH), feat(SSM_WIDTH),
                 feat(BC_WIDTH), tok(N_BRANCH * D_MODEL), tok(ATTN_WIDTH))
    return pl.pallas_call(
        _proj_kernel,
        grid=(bsz, seq // tm),
        in_specs=in_specs,
        out_specs=out_specs,
        out_shape=out_shape,
        scratch_shapes=[pltpu.VMEM((SSM_WIDTH, LANES), _F32), pltpu.VMEM((BC_WIDTH, LANES), _F32)],
        compiler_params=pltpu.CompilerParams(
            dimension_semantics=("parallel", "arbitrary"), vmem_limit_bytes=VMEM_LIMIT_BYTES),
        name="proj",
    )(x, nw, cos_t, sin_t, *weights, *conv_params)


def _attn_kernel(q_t_ref, qi_t_ref, wi_t_ref, k_ref, kidx_ref, v_t_ref, za_ref, o_ref,
                 score_scr, score16_scr, sel_scr, bias_scr, qpad_scr, m_scr, alpha_scr, acc_scr, s_scr,
                 *, top_k):
    j = pl.program_id(1)
    n_chunks = j + 1
    tq = Q_TILE
    q_pos = j * tq + lax.broadcasted_iota(jnp.int32, (1, tq), 1)
    key_iota = lax.broadcasted_iota(jnp.int32, (KEY_TILE, tq), 0)

    qi_t = qi_t_ref[0]
    qi_cat = jnp.concatenate([qi_t[hd * IDX_DIM:(hd + 1) * IDX_DIM] for hd in range(IDX_HEADS)], axis=1)
    wi_t = wi_t_ref[0]
    wi_cat = jnp.concatenate([wi_t[hd:hd + 1] for hd in range(IDX_HEADS)], axis=1)

    def score_chunk(c, carry):
        start = pl.multiple_of(c * KEY_TILE, KEY_TILE)
        logits = _dot(kidx_ref[0, pl.ds(start, KEY_TILE), :], qi_cat)
        weighted = jnp.maximum(logits, 0.0) * wi_cat
        score = weighted[:, 0:tq]
        for hd in range(1, IDX_HEADS):
            score = score + weighted[:, hd * tq:(hd + 1) * tq]
        score = jnp.where((start + key_iota) <= q_pos, score, -jnp.inf)
        score_scr[pl.ds(start, KEY_TILE), :] = score
        score16_scr[pl.ds(start, KEY_TILE), :] = score.astype(_BF16)
        return carry

    lax.fori_loop(0, n_chunks, score_chunk, 0)

    def code_to_f32(code):
        return pltpu.bitcast(code ^ ((code >> 31) & jnp.int32(0x7FFFFFFF)), _F32)

    def count_ge(ref, cand, dtype):
        def body(c, acc):
            start = pl.multiple_of(c * KEY_TILE, KEY_TILE)
            inc = jnp.where(ref[pl.ds(start, KEY_TILE), :] >= cand, jnp.ones((), dtype), jnp.zeros((), dtype))
            for part in range(KEY_TILE // COUNT_ROWS):
                acc = acc + inc[part * COUNT_ROWS:(part + 1) * COUNT_ROWS]
            return acc
        acc = lax.fori_loop(0, n_chunks, body, jnp.zeros((COUNT_ROWS, tq), dtype))
        return jnp.sum(acc.astype(_F32), axis=0, keepdims=True)

    def bf16_code(block):
        return jnp.where(block >= 0, block, block | jnp.int32(0xFFFF))

    def coarse_bit(i, block):
        cand = block + (jnp.int32(1) << (31 - i))
        hit = count_ge(score16_scr, code_to_f32(bf16_code(cand)).astype(_BF16), _BF16) >= top_k
        return jnp.where(hit, cand, block)

    def fine_bit(i, carry):
        code, n_above = carry
        cand = code + (jnp.int32(1) << (16 - i))
        cnt = count_ge(score_scr, code_to_f32(cand), _F32)
        hit = cnt >= top_k
        return jnp.where(hit, cand, code), jnp.where(hit, n_above, cnt)

    sel_scr[0:1, :] = jnp.full((1, tq), -jnp.inf, _F32)
    sel_scr[1:2, :] = jnp.zeros((1, tq), _F32)

    @pl.when((j + 1) * tq > top_k)
    def _():
        int_min = jnp.int32(-2 ** 31)
        coarse = lax.fori_loop(0, 16, coarse_bit, jnp.full((1, tq), int_min, jnp.int32))
        base = bf16_code(coarse) - jnp.int32(1 << 16)
        fine, n_above = lax.fori_loop(0, 17, fine_bit, (base, jnp.zeros((1, tq), _F32)))
        few = (q_pos + 1) <= top_k
        sel_scr[0:1, :] = jnp.where(few, -jnp.inf, code_to_f32(fine))
        sel_scr[1:2, :] = jnp.where(few, 0.0, top_k - n_above)

    thr = sel_scr[0:1, :]
    n_ties_kept = sel_scr[1:2, :]

    row = lax.broadcasted_iota(jnp.int32, (KEY_TILE, KEY_TILE), 0)
    col = lax.broadcasted_iota(jnp.int32, (KEY_TILE, KEY_TILE), 1)
    strict_lower = jnp.where(col < row, 1.0, 0.0).astype(_BF16)

    def bias_tile(c, ties_before):
        start = pl.multiple_of(c * KEY_TILE, KEY_TILE)
        s = score_scr[pl.ds(start, KEY_TILE), :]
        tie = jnp.where(s == thr, 1.0, 0.0)
        rank = _dot(strict_lower, tie.astype(_BF16)) + ties_before
        keep = (s > thr) | ((s == thr) & (rank < n_ties_kept))
        bias_scr[...] = jnp.where(keep, 0.0, NEG_BIG)
        return ties_before + jnp.sum(tie, axis=0, keepdims=True)

    n_pairs = ATTN_HEADS // 2
    zeros_half = jnp.zeros((HEAD_DIM, 2 * tq), _BF16)
    for pair in range(n_pairs):
        g = (2 * pair) // ATTN_GROUP
        q_cat = jnp.concatenate([q_t_ref[0, (2 * pair + hd) * HEAD_DIM:(2 * pair + hd + 1) * HEAD_DIM, :]
                                 for hd in range(2)], axis=1)
        qpad_scr[pair] = jnp.concatenate([q_cat, zeros_half] if g == 0 else [zeros_half, q_cat], axis=0)
    m_scr[...] = jnp.full(m_scr.shape, NEG_BIG, _F32)
    acc_scr[...] = jnp.zeros(acc_scr.shape, _F32)

    def logits_stage(c, pair):
        start = pl.multiple_of(c * KEY_TILE, KEY_TILE)
        b = bias_scr[...]
        s = _dot(k_ref[0, pl.ds(start, KEY_TILE), :], qpad_scr[pair]) + jnp.concatenate([b, b], axis=1)
        m_old = m_scr[pair]
        m_new = jnp.maximum(m_old, jnp.max(s, axis=0, keepdims=True))
        s_scr[pair] = s
        alpha_scr[pair] = jnp.exp2(m_old - m_new)
        m_scr[pair] = m_new

    def value_stage(c, pair):
        start = pl.multiple_of(c * KEY_TILE, KEY_TILE)
        g = (2 * pair) // ATTN_GROUP
        p = jnp.exp2(s_scr[pair] - m_scr[pair])
        v_c = v_t_ref[0, g * V_ROWS:(g + 1) * V_ROWS, pl.ds(start, KEY_TILE)]
        acc_scr[pair] = alpha_scr[pair] * acc_scr[pair] + _dot(v_c, p.astype(_BF16))

    ties = bias_tile(0, jnp.zeros((1, tq), _F32))
    for pair in range(n_pairs):
        logits_stage(0, pair)

    def attn_tile(c, ties):
        ties = bias_tile(c + 1, ties)
        for pair in range(n_pairs):
            value_stage(c, pair)
            logits_stage(c + 1, pair)
        return ties

    lax.fori_loop(0, n_chunks - 1, attn_tile, ties)
    for pair in range(n_pairs):
        value_stage(n_chunks - 1, pair)

    for pair in range(n_pairs):
        o_t = acc_scr[pair, :HEAD_DIM] / acc_scr[pair, HEAD_DIM:HEAD_DIM + 1]
        both = jnp.concatenate([o_t[:, :tq], o_t[:, tq:]], axis=0)
        z = za_ref[0, :, pair * LANES:(pair + 1) * LANES]
        o_ref[0, :, pair * LANES:(pair + 1) * LANES] = (both.T * _silu(z)).astype(o_ref.dtype)


def _attention(q_t, qi_t, wi_t, k, kidx, v_t, za):
    bsz, seq, _ = k.shape
    tq = Q_TILE
    top_k = min(TOPK_MAX, seq // 4)
    feat = lambda rows: pl.BlockSpec((1, rows, tq), lambda b, j: (b, 0, j))
    full = lambda s1, s2: pl.BlockSpec((1, s1, s2), lambda b, j: (b, 0, 0))
    return pl.pallas_call(
        functools.partial(_attn_kernel, top_k=top_k),
        grid=(bsz, seq // tq),
        in_specs=[feat(ATTN_WIDTH), feat(IDX_HEADS * IDX_DIM), feat(W_IDX_ROWS),
                  full(seq, KV_WIDTH), full(seq, IDX_DIM), full(ATTN_KV_HEADS * V_ROWS, seq),
                  pl.BlockSpec((1, tq, ATTN_WIDTH), lambda b, j: (b, j, 0))],
        out_specs=pl.BlockSpec((1, tq, ATTN_WIDTH), lambda b, j: (b, j, 0)),
        out_shape=jax.ShapeDtypeStruct((bsz, seq, ATTN_WIDTH), _BF16),
        scratch_shapes=[pltpu.VMEM((seq, tq), _F32), pltpu.VMEM((seq, tq), _BF16),
                        pltpu.VMEM((SUBLANES, tq), _F32), pltpu.VMEM((KEY_TILE, tq), _F32),
                        pltpu.VMEM((ATTN_HEADS // 2, 2 * HEAD_DIM, 2 * tq), _BF16),
                        pltpu.VMEM((ATTN_HEADS // 2, 1, 2 * tq), _F32),
                        pltpu.VMEM((ATTN_HEADS // 2, 1, 2 * tq), _F32),
                        pltpu.VMEM((ATTN_HEADS // 2, V_ROWS, 2 * tq), _F32),
                        pltpu.VMEM((ATTN_HEADS // 2, KEY_TILE, 2 * tq), _F32)],
        compiler_params=pltpu.CompilerParams(
            dimension_semantics=("parallel", "arbitrary"), vmem_limit_bytes=VMEM_LIMIT_BYTES),
        name="attn",
    )(q_t, qi_t, wi_t, k, kidx, v_t, za)


def _split3(x):
    hi = x.astype(_BF16)
    r1 = x - hi.astype(_F32)
    mid = r1.astype(_BF16)
    lo = (r1 - mid.astype(_F32)).astype(_BF16)
    return hi, mid, lo


def _ssd_kernel(xs_t_ref, bcs_t_ref, dt_ref, zb_t_ref, dtb_ref, alog_ref, dskip_ref, nw_ref, o_t_ref,
                state_scr, y_scr, dt_scr, acum_scr, acum_t_scr):
    i = pl.program_id(1)
    r = lax.broadcasted_iota(jnp.int32, (CHUNK, CHUNK), 0)
    c = lax.broadcasted_iota(jnp.int32, (CHUNK, CHUNK), 1)
    causal_t = r <= c

    @pl.when(i == 0)
    def _():
        state_scr[...] = jnp.zeros_like(state_scr)
        upper_b = jnp.where(causal_t, 1.0, 0.0).astype(_BF16)
        dt_in_t = dt_ref[0] + dtb_ref[...]
        dt_t = jnp.maximum(dt_in_t, 0.0) + jnp.log1p(jnp.exp(-jnp.abs(dt_in_t)))
        dt_scr[...] = dt_t
        parts = _split3(dt_t * (-jnp.exp(alog_ref[...])))
        pad = jnp.zeros((LANES - SSM_HEADS, CHUNK), _F32)
        for n in range(dt_t.shape[1] // CHUNK):
            sl = slice(n * CHUNK, (n + 1) * CHUNK)
            a_cum_t = sum(_dot(part[:, sl], upper_b) for part in parts)
            acum_t_scr[:, sl] = a_cum_t
            acum_scr[sl, :] = jnp.concatenate([a_cum_t, pad], axis=0).T

    start = pl.multiple_of(i * CHUNK, CHUNK)
    dt_t = dt_scr[:, pl.ds(start, CHUNK)]
    a_cum_t = acum_t_scr[:, pl.ds(start, CHUNK)]
    a_cum = acum_scr[pl.ds(start, CHUNK), :]
    heads_per_group = SSM_HEADS // SSM_GROUPS

    for grp in range(SSM_GROUPS):
        b_t = bcs_t_ref[0, grp * SSM_STATE:(grp + 1) * SSM_STATE, :]
        c_t = bcs_t_ref[0, BC_WIDTH // 2 + grp * SSM_STATE: BC_WIDTH // 2 + (grp + 1) * SSM_STATE, :]
        cb_t = lax.dot_general(b_t, c_t, _TN, preferred_element_type=_F32)
        cb_t = jnp.where(causal_t, cb_t, 0.0)

        for hh in range(grp * heads_per_group, (grp + 1) * heads_per_group):
            rows = slice(hh * SSM_HEAD_DIM, (hh + 1) * SSM_HEAD_DIM)
            acum_row = a_cum_t[hh:hh + 1, :]
            last = acum_row[:, CHUNK - 1:CHUNK]
            seg_t = acum_row - a_cum[:, hh:hh + 1]
            m_t = (cb_t * jnp.exp(jnp.minimum(seg_t, 0.0))).astype(_BF16)
            x_h = xs_t_ref[0, rows, :].astype(_F32)
            xd = x_h * dt_t[hh:hh + 1, :]
            y = _dot(xd.astype(_BF16), m_t)
            st_prev = state_scr[hh]
            y_off = _dot(st_prev.astype(_BF16), c_t) * jnp.exp(acum_row)
            w = (xd * jnp.exp(last - acum_row)).astype(_BF16)
            st_new = lax.dot_general(w, b_t, _NT, preferred_element_type=_F32)
            state_scr[hh] = jnp.exp(last) * st_prev + st_new
            y_scr[rows, :] = y + y_off + dskip_ref[rows, :] * x_h

    gw = SSM_WIDTH // SSM_GROUPS
    for grp in range(SSM_GROUPS):
        sl = slice(grp * gw, (grp + 1) * gw)
        yz = y_scr[sl, :] * _silu(zb_t_ref[0, sl, :])
        ms = jnp.mean(yz * yz, axis=0, keepdims=True)
        o_t_ref[0, sl, :] = (yz * lax.rsqrt(ms + EPS) * nw_ref[sl, :]).astype(o_t_ref.dtype)


def _ssd(xs_t, bcs_t, dt_t, zb_t, dtb, alog, dskip, nw):
    bsz, _, seq = xs_t.shape
    feat = lambda rows: pl.BlockSpec((1, rows, CHUNK), lambda b, i: (b, 0, i))
    consts = (dtb, alog, dskip, nw)
    return pl.pallas_call(
        _ssd_kernel,
        grid=(bsz, seq // CHUNK),
        in_specs=[feat(SSM_WIDTH), feat(BC_WIDTH), pl.BlockSpec((1, SSM_HEADS, seq), lambda b, i: (b, 0, 0)),
                  feat(SSM_WIDTH)] + [_const_spec(a.shape) for a in consts],
        out_specs=feat(SSM_WIDTH),
        out_shape=jax.ShapeDtypeStruct((bsz, SSM_WIDTH, seq), _BF16),
        scratch_shapes=[pltpu.VMEM((SSM_HEADS, SSM_HEAD_DIM, SSM_STATE), _F32),
                        pltpu.VMEM((SSM_WIDTH, CHUNK), _F32),
                        pltpu.VMEM((SSM_HEADS, seq), _F32), pltpu.VMEM((seq, LANES), _F32),
                        pltpu.VMEM((SSM_HEADS, seq), _F32)],
        compiler_params=pltpu.CompilerParams(
            dimension_semantics=("parallel", "arbitrary"), vmem_limit_bytes=VMEM_LIMIT_BYTES),
        name="ssd",
    )(xs_t, bcs_t, dt_t, zb_t, *consts)


def _merge_kernel(x_ref, oa_ref, ob_t_ref, g_ref, gb_ref, wa_ref, wb_ref, wo_ref, fw_ref, o_ref, *, final_norm):
    gates = jax.nn.sigmoid(g_ref[0] + gb_ref[...])
    branch_b = lax.dot_general(ob_t_ref[0], wb_ref[...], _TN, preferred_element_type=_F32)
    merged = gates[:, :D_MODEL] * _dot(oa_ref[0], wa_ref[...]) + gates[:, D_MODEL:] * branch_b
    y = x_ref[0] + _dot(merged.astype(_BF16), wo_ref[...])
    if final_norm:
        y = y * lax.rsqrt(jnp.mean(y * y, axis=-1, keepdims=True) + EPS) * fw_ref[...]
    o_ref[0] = y


def _merge(x, o_a, o_b_t, gates, gate_bias, wa, wb, wo, fw, final_norm):
    bsz, seq, _ = x.shape
    tm = MERGE_ROWS
    tok = lambda width: pl.BlockSpec((1, tm, width), lambda b, i: (b, i, 0))
    consts = (gate_bias, wa, wb, wo, fw)
    return pl.pallas_call(
        functools.partial(_merge_kernel, final_norm=final_norm),
        grid=(bsz, seq // tm),
        in_specs=[tok(D_MODEL), tok(ATTN_WIDTH), pl.BlockSpec((1, SSM_WIDTH, tm), lambda b, i: (b, 0, i)),
                  tok(N_BRANCH * D_MODEL)] + [_const_spec(a.shape) for a in consts],
        out_specs=tok(D_MODEL),
        out_shape=jax.ShapeDtypeStruct((bsz, seq, D_MODEL), _F32),
        compiler_params=pltpu.CompilerParams(
            dimension_semantics=("parallel", "parallel"), vmem_limit_bytes=VMEM_LIMIT_BYTES),
        name="merge",
    )(x, o_a, o_b_t, gates, *consts)


def _cast_kernel(w_ref, o_ref):
    o_ref[...] = w_ref[0].astype(o_ref.dtype)


def _layer_transposed_bf16(w_all, layer):
    w_t_all = jnp.swapaxes(w_all, 1, 2)
    _, rows, cols = w_t_all.shape
    tm = CAST_ROWS
    return pl.pallas_call(
        _cast_kernel,
        grid=(pl.cdiv(rows, tm),),
        in_specs=[pl.BlockSpec((1, tm, cols), lambda i: (layer, i, 0))],
        out_specs=pl.BlockSpec((tm, cols), lambda i: (i, 0)),
        out_shape=jax.ShapeDtypeStruct((rows, cols), _BF16),
        compiler_params=pltpu.CompilerParams(dimension_semantics=("parallel",)),
        name="cast",
    )(w_t_all)


def _layer_weights(w_in_all, layer):
    wt16 = _layer_transposed_bf16(w_in_all, layer)
    o = SPLIT_OFFSETS
    seg = lambda i: wt16[o[i]:o[i + 1]]
    w_q, w_k, w_v, w_za, w_qi, w_ki, w_wi, w_zb, w_xb, w_b, w_c, w_dt, w_g = (seg(i) for i in range(13))
    pad = lambda w, n: jnp.pad(w, ((0, n - w.shape[0]), (0, 0)))
    w_misc = pad(jnp.concatenate([w_ki, pad(w_wi, W_IDX_ROWS), w_dt], axis=0), LANES)
    w_t = jnp.concatenate([w_q, w_qi, w_k, w_misc, w_v], axis=0)
    return w_t, w_zb, w_xb, jnp.concatenate([w_b, w_c], axis=0), w_g.T, w_za.T


def _rope_tables(positions):
    inv_freq = ROPE_THETA ** (-jnp.arange(0, ROT_DIM, 2, dtype=_F32) / ROT_DIM)
    ang = jnp.swapaxes(positions.astype(_F32)[..., None] * inv_freq, 1, 2)
    return jnp.cos(ang), jnp.sin(ang)


def _lane_broadcast(v):
    return jnp.broadcast_to(v[..., None], v.shape + (LANES,))


def kernel(x, positions, norm_w, w_in, gate_bias, conv_w, conv_b, dt_bias, a_log, d_skip,
           ssm_norm_w, w_branch_a, w_branch_b, w_out, final_norm_w):
    depth = norm_w.shape[0]
    cos_t, sin_t = _rope_tables(positions)
    for i in range(depth):
        weights = _layer_weights(w_in, i)
        conv_params = (_lane_broadcast(conv_w[i][:, :SSM_WIDTH]), _lane_broadcast(conv_b[i][:SSM_WIDTH]),
                       _lane_broadcast(conv_w[i][:, SSM_WIDTH:]), _lane_broadcast(conv_b[i][SSM_WIDTH:]))
        (q_t, qi_t, wi_t, dt_t, v_t, k, kidx, zb_t, xs_t, bcs_t, gates, za) = _projection(
            x, norm_w[i][None, :], cos_t, sin_t, weights, conv_params)
        o_a = _attention(q_t, qi_t, wi_t, k, kidx, v_t, za)
        o_b_t = _ssd(xs_t, bcs_t, dt_t, zb_t, dt_bias[i][:, None], a_log[i][:, None],
                     _lane_broadcast(jnp.repeat(d_skip[i], SSM_HEAD_DIM)), _lane_broadcast(ssm_norm_w[i]))
        x = _merge(x, o_a, o_b_t, gates, gate_bias[i][None, :], w_branch_a[i].astype(_BF16),
                   w_branch_b[i].astype(_BF16), w_out[i].astype(_BF16), final_norm_w[None, :],
                   final_norm=(i == depth - 1))
    return x
```

```python
import functools

import numpy as np
import jax
import jax.numpy as jnp
from jax import lax
from jax.experimental import pallas as pl
from jax.experimental.pallas import tpu as pltpu

D_MODEL = 1024
ATTN_HEADS = 8
ATTN_KV_HEADS = 2
HEAD_DIM = 64
ATTN_GROUP = ATTN_HEADS // ATTN_KV_HEADS
ATTN_WIDTH = ATTN_HEADS * HEAD_DIM
KV_WIDTH = ATTN_KV_HEADS * HEAD_DIM
ROT_DIM = HEAD_DIM // 4
ROT_HALF = ROT_DIM // 2
ROPE_THETA = 500000.0
IDX_HEADS = 4
IDX_DIM = 64
TOPK_MAX = 256
SSM_HEADS = 16
SSM_HEAD_DIM = 64
SSM_WIDTH = SSM_HEADS * SSM_HEAD_DIM
SSM_GROUPS = 4
SSM_STATE = 64
CONV_K = 4
CHUNK = 128
BC_WIDTH = 2 * SSM_GROUPS * SSM_STATE
N_BRANCH = 2
EPS = 1e-6
SPLIT_SIZES = (ATTN_WIDTH, KV_WIDTH, KV_WIDTH, ATTN_WIDTH,
               IDX_HEADS * IDX_DIM, IDX_DIM, IDX_HEADS,
               SSM_WIDTH, SSM_WIDTH, SSM_GROUPS * SSM_STATE, SSM_GROUPS * SSM_STATE, SSM_HEADS,
               N_BRANCH * D_MODEL)
SPLIT_OFFSETS = tuple(int(o) for o in np.cumsum((0,) + SPLIT_SIZES))

LANES = 128
SUBLANES = 8
VMEM_LIMIT_BYTES = 56 * 1024 * 1024

PROJ_ROWS = 256
CONV_ROWS = 256
CAST_ROWS = 512
Q_TILE = 256
KEY_TILE = 256
COUNT_ROWS = 32
LOG2_E = 1.4426950408889634
MERGE_ROWS = 512
W_IDX_ROWS = 8
V_ROWS = HEAD_DIM + 16
NEG_BIG = -1e30

_F32 = jnp.float32
_BF16 = jnp.bfloat16
_NT = (((1,), (1,)), ((), ()))
_TN = (((0,), (0,)), ((), ()))


def _dot(a, b):
    return jnp.dot(a, b, preferred_element_type=_F32)


def _silu(x):
    return x * jax.nn.sigmoid(x)


def _causal_conv_silu_t(x, tail_ref, w_ref, b_ref):
    cols = x.shape[1]
    reps = cols // LANES
    tail = tail_ref[...]
    lane = lax.broadcasted_iota(jnp.int32, (x.shape[0], LANES), 1)
    wide = lambda a: jnp.concatenate([a] * reps, axis=1)
    y = x * wide(w_ref[CONV_K - 1]) + wide(b_ref[...])
    for shift in range(1, CONV_K):
        rolled = pltpu.roll(x, shift, 1)
        head = jnp.where(lane < shift, pltpu.roll(tail, shift, 1), rolled[:, :LANES])
        shifted = jnp.concatenate([head, rolled[:, LANES:]], axis=1)
        y = y + shifted * wide(w_ref[CONV_K - 1 - shift])
    tail_ref[...] = x[:, cols - LANES:]
    return _silu(y)


def _proj_kernel(x_ref, nw_ref, cos_t_ref, sin_t_ref,
                 w_t_ref, w_zb_ref, w_xb_ref, w_bc_ref, w_g_ref, w_za_ref,
                 cwx_ref, cbx_ref, cwbc_ref, cbbc_ref,
                 q_t_ref, qi_t_ref, wi_t_ref, dt_t_ref, v_t_ref, k_ref, kidx_ref,
                 zb_t_ref, xs_t_ref, bcs_t_ref, b_tok_ref, g_ref, za_ref,
                 tailx_scr, tailbc_scr):
    @pl.when(pl.program_id(1) == 0)
    def _():
        tailx_scr[...] = jnp.zeros_like(tailx_scr)
        tailbc_scr[...] = jnp.zeros_like(tailbc_scr)

    x = x_ref[0]
    h = x * lax.rsqrt(jnp.mean(x * x, axis=-1, keepdims=True) + EPS) * nw_ref[...]
    h = h.astype(_BF16)

    t = lax.dot_general(w_t_ref[...], h, _NT, preferred_element_type=_F32)
    cos_t = cos_t_ref[0]
    sin_t = sin_t_ref[0]

    def rope_head(block, hd):
        x1 = block[hd * HEAD_DIM: hd * HEAD_DIM + ROT_HALF]
        x2 = block[hd * HEAD_DIM + ROT_HALF: hd * HEAD_DIM + ROT_DIM]
        return jnp.concatenate([x1 * cos_t - x2 * sin_t, x2 * cos_t + x1 * sin_t], axis=0)

    def rope_store(block, n_heads, scale, out_ref):
        out_ref[0] = (block * scale).astype(out_ref.dtype)
        for hd in range(n_heads):
            out_ref[0, hd * HEAD_DIM: hd * HEAD_DIM + ROT_DIM, :] = (
                rope_head(block, hd) * scale).astype(out_ref.dtype)

    def rope_value(block, n_heads):
        parts = []
        for hd in range(n_heads):
            parts += [rope_head(block, hd), block[hd * HEAD_DIM + ROT_DIM:(hd + 1) * HEAD_DIM]]
        if block.shape[0] > n_heads * HEAD_DIM:
            parts.append(block[n_heads * HEAD_DIM:])
        return jnp.concatenate(parts, axis=0)

    o_qi = ATTN_WIDTH
    o_k = o_qi + IDX_HEADS * IDX_DIM
    o_ki = o_k + KV_WIDTH
    o_v = o_ki + LANES
    rope_store(t[:o_qi], ATTN_HEADS, LOG2_E * HEAD_DIM ** -0.5, q_t_ref)
    rope_store(t[o_qi:o_k], IDX_HEADS, IDX_DIM ** -0.5, qi_t_ref)
    o_wi = o_ki + IDX_DIM
    o_dt = o_wi + W_IDX_ROWS
    wi_t_ref[0] = t[o_wi:o_dt] * (IDX_HEADS ** -0.5)
    dt_t_ref[0] = t[o_dt:o_dt + SSM_HEADS]
    for g in range(ATTN_KV_HEADS):
        v_t_ref[0, g * V_ROWS:g * V_ROWS + HEAD_DIM, :] = t[o_v + g * HEAD_DIM:o_v + (g + 1) * HEAD_DIM].astype(_BF16)
        v_t_ref[0, g * V_ROWS + HEAD_DIM:(g + 1) * V_ROWS, :] = jnp.ones((V_ROWS - HEAD_DIM, x.shape[0]), _BF16)
    k_ref[0] = rope_value(t[o_k:o_ki], ATTN_KV_HEADS).T.astype(_BF16)
    kidx_ref[0] = rope_value(t[o_ki:o_v], 1).T[:, :IDX_DIM].astype(_BF16)

    def feat_major(w_ref, r0, rows):
        return lax.dot_general(w_ref[r0:r0 + rows], h, _NT, preferred_element_type=_F32)

    rc = CONV_ROWS
    for n in range(SSM_WIDTH // rc):
        sl = slice(n * rc, (n + 1) * rc)
        xs_t_ref[0, sl, :] = _causal_conv_silu_t(
            feat_major(w_xb_ref, n * rc, rc), tailx_scr.at[sl], cwx_ref.at[:, sl], cbx_ref.at[sl]).astype(_BF16)
        zb_t_ref[0, sl, :] = feat_major(w_zb_ref, n * rc, rc)
    for n in range(BC_WIDTH // rc):
        sl = slice(n * rc, (n + 1) * rc)
        bc = _causal_conv_silu_t(
            feat_major(w_bc_ref, n * rc, rc), tailbc_scr.at[sl], cwbc_ref.at[:, sl], cbbc_ref.at[sl])
        bcs_t_ref[0, sl, :] = bc.astype(_BF16)
        if (n + 1) * rc <= BC_WIDTH // 2:
            b_tok_ref[0, :, sl] = bc.T.astype(_BF16)
    g_ref[0] = _dot(h, w_g_ref[...]).astype(_BF16)
    za_ref[0] = _dot(h, w_za_ref[...])


def _const_spec(shape):
    nd = len(shape)
    return pl.BlockSpec(shape, lambda *_: (0,) * nd, pipeline_mode=pl.Buffered(1))


def _projection(x, nw, cos_t, sin_t, weights, conv_params):
    bsz, seq, _ = x.shape
    tm = PROJ_ROWS
    tok = lambda width: pl.BlockSpec((1, tm, width), lambda b, i: (b, i, 0))
    feat = lambda rows: pl.BlockSpec((1, rows, tm), lambda b, i: (b, 0, i))
    in_specs = [tok(D_MODEL), _const_spec((1, D_MODEL)), feat(ROT_HALF), feat(ROT_HALF)
                ] + [_const_spec(w.shape) for w in weights + conv_params]
    out_shape = (
        jax.ShapeDtypeStruct((bsz, ATTN_WIDTH, seq), _BF16),
        jax.ShapeDtypeStruct((bsz, IDX_HEADS * IDX_DIM, seq), _BF16),
        jax.ShapeDtypeStruct((bsz, W_IDX_ROWS, seq), _F32),
        jax.ShapeDtypeStruct((bsz, SSM_HEADS, seq), _F32),
        jax.ShapeDtypeStruct((bsz, ATTN_KV_HEADS * V_ROWS, seq), _BF16),
        jax.ShapeDtypeStruct((bsz, seq, KV_WIDTH), _BF16),
        jax.ShapeDtypeStruct((bsz, seq, IDX_DIM), _BF16),
        jax.ShapeDtypeStruct((bsz, SSM_WIDTH, seq), _F32),
        jax.ShapeDtypeStruct((bsz, SSM_WIDTH, seq), _BF16),
        jax.ShapeDtypeStruct((bsz, BC_WIDTH, seq), _BF16),
        jax.ShapeDtypeStruct((bsz, seq, BC_WIDTH // 2), _BF16),
        jax.ShapeDtypeStruct((bsz, seq, N_BRANCH * D_MODEL), _BF16),
        jax.ShapeDtypeStruct((bsz, seq, ATTN_WIDTH), _F32),
    )
    out_specs = (feat(ATTN_WIDTH), feat(IDX_HEADS * IDX_DIM), feat(W_IDX_ROWS), feat(SSM_HEADS),
                 feat(ATTN_KV_HEADS * V_ROWS), tok(KV_WIDTH), tok(IDX_DIM), feat(SSM_WIDTH), feat(SSM_WIDTH),
                 feat(BC_WIDTH), tok(BC_WIDTH // 2), tok(N_BRANCH * D_MODEL), tok(ATTN_WIDTH))
    return pl.pallas_call(
        _proj_kernel,
        grid=(bsz, seq // tm),
        in_specs=in_specs,
        out_specs=out_specs,
        out_shape=out_shape,
        scratch_shapes=[pltpu.VMEM((SSM_WIDTH, LANES), _F32), pltpu.VMEM((BC_WIDTH, LANES), _F32)],
        compiler_params=pltpu.CompilerParams(
            dimension_semantics=("parallel", "arbitrary"), vmem_limit_bytes=VMEM_LIMIT_BYTES),
        name="proj",
    )(x, nw, cos_t, sin_t, *weights, *conv_params)


def _attn_kernel(q_t_ref, qi_t_ref, wi_t_ref, k_ref, kidx_ref, v_t_ref, za_ref, o_ref,
                 score_scr, score16_scr, sel_scr, bias_scr, qpad_scr, m_scr, alpha_scr, acc_scr, s_scr,
                 *, top_k):
    j = pl.program_id(1)
    n_chunks = j + 1
    tq = Q_TILE
    q_pos = j * tq + lax.broadcasted_iota(jnp.int32, (1, tq), 1)
    key_iota = lax.broadcasted_iota(jnp.int32, (KEY_TILE, tq), 0)

    qi_t = qi_t_ref[0]
    qi_cat = jnp.concatenate([qi_t[hd * IDX_DIM:(hd + 1) * IDX_DIM] for hd in range(IDX_HEADS)], axis=1)
    wi_t = wi_t_ref[0]
    wi_cat = jnp.concatenate([wi_t[hd:hd + 1] for hd in range(IDX_HEADS)], axis=1)

    def score_chunk(c, carry):
        start = pl.multiple_of(c * KEY_TILE, KEY_TILE)
        logits = _dot(kidx_ref[0, pl.ds(start, KEY_TILE), :], qi_cat)
        weighted = jnp.maximum(logits, 0.0) * wi_cat
        score = weighted[:, 0:tq]
        for hd in range(1, IDX_HEADS):
            score = score + weighted[:, hd * tq:(hd + 1) * tq]
        score = jnp.where((start + key_iota) <= q_pos, score, -jnp.inf)
        score_scr[pl.ds(start, KEY_TILE), :] = score
        score16_scr[pl.ds(start, KEY_TILE), :] = score.astype(_BF16)
        return carry

    lax.fori_loop(0, n_chunks, score_chunk, 0)

    def code_to_f32(code):
        return pltpu.bitcast(code ^ ((code >> 31) & jnp.int32(0x7FFFFFFF)), _F32)

    def count_ge(ref, cand, dtype):
        def body(c, acc):
            start = pl.multiple_of(c * KEY_TILE, KEY_TILE)
            inc = jnp.where(ref[pl.ds(start, KEY_TILE), :] >= cand, jnp.ones((), dtype), jnp.zeros((), dtype))
            for part in range(KEY_TILE // COUNT_ROWS):
                acc = acc + inc[part * COUNT_ROWS:(part + 1) * COUNT_ROWS]
            return acc
        acc = lax.fori_loop(0, n_chunks, body, jnp.zeros((COUNT_ROWS, tq), dtype))
        return jnp.sum(acc.astype(_F32), axis=0, keepdims=True)

    def bf16_code(block):
        return jnp.where(block >= 0, block, block | jnp.int32(0xFFFF))

    def coarse_bit(i, block):
        cand = block + (jnp.int32(1) << (31 - i))
        hit = count_ge(score16_scr, code_to_f32(bf16_code(cand)).astype(_BF16), _BF16) >= top_k
        return jnp.where(hit, cand, block)

    def fine_bit(i, carry):
        code, n_above = carry
        cand = code + (jnp.int32(1) << (16 - i))
        cnt = count_ge(score_scr, code_to_f32(cand), _F32)
        hit = cnt >= top_k
        return jnp.where(hit, cand, code), jnp.where(hit, n_above, cnt)

    sel_scr[0:1, :] = jnp.full((1, tq), -jnp.inf, _F32)
    sel_scr[1:2, :] = jnp.zeros((1, tq), _F32)

    @pl.when((j + 1) * tq > top_k)
    def _():
        int_min = jnp.int32(-2 ** 31)
        coarse = lax.fori_loop(0, 16, coarse_bit, jnp.full((1, tq), int_min, jnp.int32))
        base = bf16_code(coarse) - jnp.int32(1 << 16)
        fine, n_above = lax.fori_loop(0, 17, fine_bit, (base, jnp.zeros((1, tq), _F32)))
        few = (q_pos + 1) <= top_k
        sel_scr[0:1, :] = jnp.where(few, -jnp.inf, code_to_f32(fine))
        sel_scr[1:2, :] = jnp.where(few, 0.0, top_k - n_above)

    thr = sel_scr[0:1, :]
    n_ties_kept = sel_scr[1:2, :]

    row = lax.broadcasted_iota(jnp.int32, (KEY_TILE, KEY_TILE), 0)
    col = lax.broadcasted_iota(jnp.int32, (KEY_TILE, KEY_TILE), 1)
    strict_lower = jnp.where(col < row, 1.0, 0.0).astype(_BF16)

    def bias_tile(c, ties_before):
        start = pl.multiple_of(c * KEY_TILE, KEY_TILE)
        s = score_scr[pl.ds(start, KEY_TILE), :]
        tie = jnp.where(s == thr, 1.0, 0.0)
        rank = _dot(strict_lower, tie.astype(_BF16)) + ties_before
        keep = (s > thr) | ((s == thr) & (rank < n_ties_kept))
        bias_scr[...] = jnp.where(keep, 0.0, NEG_BIG)
        return ties_before + jnp.sum(tie, axis=0, keepdims=True)

    n_pairs = ATTN_HEADS // 2
    zeros_half = jnp.zeros((HEAD_DIM, 2 * tq), _BF16)
    for pair in range(n_pairs):
        g = (2 * pair) // ATTN_GROUP
        q_cat = jnp.concatenate([q_t_ref[0, (2 * pair + hd) * HEAD_DIM:(2 * pair + hd + 1) * HEAD_DIM, :]
                                 for hd in range(2)], axis=1)
        qpad_scr[pair] = jnp.concatenate([q_cat, zeros_half] if g == 0 else [zeros_half, q_cat], axis=0)
    m_scr[...] = jnp.full(m_scr.shape, NEG_BIG, _F32)
    acc_scr[...] = jnp.zeros(acc_scr.shape, _F32)

    def logits_stage(c, pair):
        start = pl.multiple_of(c * KEY_TILE, KEY_TILE)
        b = bias_scr[...]
        s = _dot(k_ref[0, pl.ds(start, KEY_TILE), :], qpad_scr[pair]) + jnp.concatenate([b, b], axis=1)
        m_old = m_scr[pair]
        m_new = jnp.maximum(m_old, jnp.max(s, axis=0, keepdims=True))
        s_scr[pair] = s
        alpha_scr[pair] = jnp.exp2(m_old - m_new)
        m_scr[pair] = m_new

    def value_stage(c, pair):
        start = pl.multiple_of(c * KEY_TILE, KEY_TILE)
        g = (2 * pair) // ATTN_GROUP
        p = jnp.exp2(s_scr[pair] - m_scr[pair])
        v_c = v_t_ref[0, g * V_ROWS:(g + 1) * V_ROWS, pl.ds(start, KEY_TILE)]
        acc_scr[pair] = alpha_scr[pair] * acc_scr[pair] + _dot(v_c, p.astype(_BF16))

    ties = bias_tile(0, jnp.zeros((1, tq), _F32))
    for pair in range(n_pairs):
        logits_stage(0, pair)

    def attn_tile(c, ties):
        ties = bias_tile(c + 1, ties)
        for pair in range(n_pairs):
            value_stage(c, pair)
            logits_stage(c + 1, pair)
        return ties

    lax.fori_loop(0, n_chunks - 1, attn_tile, ties)
    for pair in range(n_pairs):
        value_stage(n_chunks - 1, pair)

    for pair in range(n_pairs):
        o_t = acc_scr[pair, :HEAD_DIM] / acc_scr[pair, HEAD_DIM:HEAD_DIM + 1]
        both = jnp.concatenate([o_t[:, :tq], o_t[:, tq:]], axis=0)
        z = za_ref[0, :, pair * LANES:(pair + 1) * LANES]
        o_ref[0, :, pair * LANES:(pair + 1) * LANES] = (both.T * _silu(z)).astype(o_ref.dtype)


def _attention(q_t, qi_t, wi_t, k, kidx, v_t, za):
    bsz, seq, _ = k.shape
    tq = Q_TILE
    top_k = min(TOPK_MAX, seq // 4)
    feat = lambda rows: pl.BlockSpec((1, rows, tq), lambda b, j: (b, 0, j))
    full = lambda s1, s2: pl.BlockSpec((1, s1, s2), lambda b, j: (b, 0, 0))
    return pl.pallas_call(
        functools.partial(_attn_kernel, top_k=top_k),
        grid=(bsz, seq // tq),
        in_specs=[feat(ATTN_WIDTH), feat(IDX_HEADS * IDX_DIM), feat(W_IDX_ROWS),
                  full(seq, KV_WIDTH), full(seq, IDX_DIM), full(ATTN_KV_HEADS * V_ROWS, seq),
                  pl.BlockSpec((1, tq, ATTN_WIDTH), lambda b, j: (b, j, 0))],
        out_specs=pl.BlockSpec((1, tq, ATTN_WIDTH), lambda b, j: (b, j, 0)),
        out_shape=jax.ShapeDtypeStruct((bsz, seq, ATTN_WIDTH), _BF16),
        scratch_shapes=[pltpu.VMEM((seq, tq), _F32), pltpu.VMEM((seq, tq), _BF16),
                        pltpu.VMEM((SUBLANES, tq), _F32), pltpu.VMEM((KEY_TILE, tq), _F32),
                        pltpu.VMEM((ATTN_HEADS // 2, 2 * HEAD_DIM, 2 * tq), _BF16),
                        pltpu.VMEM((ATTN_HEADS // 2, 1, 2 * tq), _F32),
                        pltpu.VMEM((ATTN_HEADS // 2, 1, 2 * tq), _F32),
                        pltpu.VMEM((ATTN_HEADS // 2, V_ROWS, 2 * tq), _F32),
                        pltpu.VMEM((ATTN_HEADS // 2, KEY_TILE, 2 * tq), _F32)],
        compiler_params=pltpu.CompilerParams(
            dimension_semantics=("parallel", "arbitrary"), vmem_limit_bytes=VMEM_LIMIT_BYTES),
        name="attn",
    )(q_t, qi_t, wi_t, k, kidx, v_t, za)


def _split3(x):
    hi = x.astype(_BF16)
    r1 = x - hi.astype(_F32)
    mid = r1.astype(_BF16)
    lo = (r1 - mid.astype(_F32)).astype(_BF16)
    return hi, mid, lo


def _ssd_kernel(xs_t_ref, bcs_t_ref, b_tok_ref, dt_ref, zb_t_ref, dtb_ref, alog_ref, dskip_ref, nw_ref, o_t_ref,
                state_scr, y_scr, dt_scr, acum_scr, acum_t_scr, cols_even_scr, cols_odd_scr):
    i = pl.program_id(1)
    r = lax.broadcasted_iota(jnp.int32, (CHUNK, CHUNK), 0)
    c = lax.broadcasted_iota(jnp.int32, (CHUNK, CHUNK), 1)
    causal_t = r <= c

    @pl.when(i == 0)
    def _():
        state_scr[...] = jnp.zeros_like(state_scr)
        upper_b = jnp.where(causal_t, 1.0, 0.0).astype(_BF16)
        dt_in_t = dt_ref[0] + dtb_ref[...]
        dt_t = jnp.maximum(dt_in_t, 0.0) + jnp.log1p(jnp.exp(-jnp.abs(dt_in_t)))
        dt_scr[...] = dt_t
        parts = _split3(dt_t * (-jnp.exp(alog_ref[...])))
        pad = jnp.zeros((LANES - SSM_HEADS, CHUNK), _F32)
        for n in range(dt_t.shape[1] // CHUNK):
            sl = slice(n * CHUNK, (n + 1) * CHUNK)
            a_cum_t = sum(_dot(part[:, sl], upper_b) for part in parts)
            acum_t_scr[:, sl] = a_cum_t
            acum_scr[sl, :] = jnp.concatenate([a_cum_t, pad], axis=0).T

    def spread_columns(chunk, cols_ref):
        a_cum = acum_scr[pl.ds(pl.multiple_of(chunk * CHUNK, CHUNK), CHUNK), :]
        for hh in range(SSM_HEADS):
            cols_ref[:, hh * LANES:(hh + 1) * LANES] = jnp.broadcast_to(a_cum[:, hh:hh + 1], (CHUNK, LANES))

    @pl.when(i == 0)
    def _():
        spread_columns(0, cols_even_scr)

    def chunk_step(cols_ref, next_cols_ref):
        start = pl.multiple_of(i * CHUNK, CHUNK)
        dt_t = dt_scr[:, pl.ds(start, CHUNK)]
        a_cum_t = acum_t_scr[:, pl.ds(start, CHUNK)]
        heads_per_group = SSM_HEADS // SSM_GROUPS

        for grp in range(SSM_GROUPS):
            b_t = bcs_t_ref[0, grp * SSM_STATE:(grp + 1) * SSM_STATE, :]
            c_t = bcs_t_ref[0, BC_WIDTH // 2 + grp * SSM_STATE: BC_WIDTH // 2 + (grp + 1) * SSM_STATE, :]
            b_tok = b_tok_ref[0, :, grp * SSM_STATE:(grp + 1) * SSM_STATE]
            cb_t = _dot(b_tok, c_t)
            cb_t = jnp.where(causal_t, cb_t, 0.0)
            if grp == 0:
                spread_columns(jnp.minimum(i + 1, pl.num_programs(1) - 1), next_cols_ref)

            for hh in range(grp * heads_per_group, (grp + 1) * heads_per_group):
                rows = slice(hh * SSM_HEAD_DIM, (hh + 1) * SSM_HEAD_DIM)
                acum_row = a_cum_t[hh:hh + 1, :]
                last = acum_row[:, CHUNK - 1:CHUNK]
                seg_t = acum_row - cols_ref[:, hh * LANES:(hh + 1) * LANES]
                m_t = (cb_t * jnp.exp(jnp.minimum(seg_t, 0.0))).astype(_BF16)
                x_h = xs_t_ref[0, rows, :].astype(_F32)
                xd = x_h * dt_t[hh:hh + 1, :]
                y = _dot(xd.astype(_BF16), m_t)
                st_prev = state_scr[hh]
                y_off = _dot(st_prev.astype(_BF16), c_t) * jnp.exp(acum_row)
                w = (xd * jnp.exp(last - acum_row)).astype(_BF16)
                st_new = lax.dot_general(w, b_t, _NT, preferred_element_type=_F32)
                state_scr[hh] = jnp.exp(last) * st_prev + st_new
                y_scr[rows, :] = y + y_off + dskip_ref[rows, :] * x_h

        gw = SSM_WIDTH // SSM_GROUPS
        for grp in range(SSM_GROUPS):
            sl = slice(grp * gw, (grp + 1) * gw)
            yz = y_scr[sl, :] * _silu(zb_t_ref[0, sl, :])
            ms = jnp.mean(yz * yz, axis=0, keepdims=True)
            o_t_ref[0, sl, :] = (yz * lax.rsqrt(ms + EPS) * nw_ref[sl, :]).astype(o_t_ref.dtype)

    @pl.when(i % 2 == 0)
    def _():
        chunk_step(cols_even_scr, cols_odd_scr)

    @pl.when(i % 2 == 1)
    def _():
        chunk_step(cols_odd_scr, cols_even_scr)


def _ssd(xs_t, bcs_t, b_tok, dt_t, zb_t, dtb, alog, dskip, nw):
    bsz, _, seq = xs_t.shape
    feat = lambda rows: pl.BlockSpec((1, rows, CHUNK), lambda b, i: (b, 0, i))
    consts = (dtb, alog, dskip, nw)
    return pl.pallas_call(
        _ssd_kernel,
        grid=(bsz, seq // CHUNK),
        in_specs=[feat(SSM_WIDTH), feat(BC_WIDTH),
                  pl.BlockSpec((1, CHUNK, BC_WIDTH // 2), lambda b, i: (b, i, 0)),
                  pl.BlockSpec((1, SSM_HEADS, seq), lambda b, i: (b, 0, 0)),
                  feat(SSM_WIDTH)] + [_const_spec(a.shape) for a in consts],
        out_specs=feat(SSM_WIDTH),
        out_shape=jax.ShapeDtypeStruct((bsz, SSM_WIDTH, seq), _BF16),
        scratch_shapes=[pltpu.VMEM((SSM_HEADS, SSM_HEAD_DIM, SSM_STATE), _F32),
                        pltpu.VMEM((SSM_WIDTH, CHUNK), _F32),
                        pltpu.VMEM((SSM_HEADS, seq), _F32), pltpu.VMEM((seq, LANES), _F32),
                        pltpu.VMEM((SSM_HEADS, seq), _F32),
                        pltpu.VMEM((CHUNK, SSM_HEADS * LANES), _F32),
                        pltpu.VMEM((CHUNK, SSM_HEADS * LANES), _F32)],
        compiler_params=pltpu.CompilerParams(
            dimension_semantics=("parallel", "arbitrary"), vmem_limit_bytes=VMEM_LIMIT_BYTES),
        name="ssd",
    )(xs_t, bcs_t, b_tok, dt_t, zb_t, *consts)


def _merge_kernel(x_ref, oa_ref, ob_t_ref, g_ref, gb_ref, wa_ref, wb_ref, wo_ref, fw_ref, o_ref, *, final_norm):
    gates = jax.nn.sigmoid(g_ref[0] + gb_ref[...])
    branch_b = lax.dot_general(ob_t_ref[0], wb_ref[...], _TN, preferred_element_type=_F32)
    merged = gates[:, :D_MODEL] * _dot(oa_ref[0], wa_ref[...]) + gates[:, D_MODEL:] * branch_b
    y = x_ref[0] + _dot(merged.astype(_BF16), wo_ref[...])
    if final_norm:
        y = y * lax.rsqrt(jnp.mean(y * y, axis=-1, keepdims=True) + EPS) * fw_ref[...]
    o_ref[0] = y


def _merge(x, o_a, o_b_t, gates, gate_bias, wa, wb, wo, fw, final_norm):
    bsz, seq, _ = x.shape
    tm = MERGE_ROWS
    tok = lambda width: pl.BlockSpec((1, tm, width), lambda b, i: (b, i, 0))
    consts = (gate_bias, wa, wb, wo, fw)
    return pl.pallas_call(
        functools.partial(_merge_kernel, final_norm=final_norm),
        grid=(bsz, seq // tm),
        in_specs=[tok(D_MODEL), tok(ATTN_WIDTH), pl.BlockSpec((1, SSM_WIDTH, tm), lambda b, i: (b, 0, i)),
                  tok(N_BRANCH * D_MODEL)] + [_const_spec(a.shape) for a in consts],
        out_specs=tok(D_MODEL),
        out_shape=jax.ShapeDtypeStruct((bsz, seq, D_MODEL), _F32),
        compiler_params=pltpu.CompilerParams(
            dimension_semantics=("parallel", "parallel"), vmem_limit_bytes=VMEM_LIMIT_BYTES),
        name="merge",
    )(x, o_a, o_b_t, gates, *consts)


def _cast_kernel(w_ref, o_ref):
    o_ref[...] = w_ref[0].astype(o_ref.dtype)


def _layer_transposed_bf16(w_all, layer):
    w_t_all = jnp.swapaxes(w_all, 1, 2)
    _, rows, cols = w_t_all.shape
    tm = CAST_ROWS
    return pl.pallas_call(
        _cast_kernel,
        grid=(pl.cdiv(rows, tm),),
        in_specs=[pl.BlockSpec((1, tm, cols), lambda i: (layer, i, 0))],
        out_specs=pl.BlockSpec((tm, cols), lambda i: (i, 0)),
        out_shape=jax.ShapeDtypeStruct((rows, cols), _BF16),
        compiler_params=pltpu.CompilerParams(dimension_semantics=("parallel",)),
        name="cast",
    )(w_t_all)


def _layer_weights(w_in_all, layer):
    wt16 = _layer_transposed_bf16(w_in_all, layer)
    o = SPLIT_OFFSETS
    seg = lambda i: wt16[o[i]:o[i + 1]]
    w_q, w_k, w_v, w_za, w_qi, w_ki, w_wi, w_zb, w_xb, w_b, w_c, w_dt, w_g = (seg(i) for i in range(13))
    pad = lambda w, n: jnp.pad(w, ((0, n - w.shape[0]), (0, 0)))
    w_misc = pad(jnp.concatenate([w_ki, pad(w_wi, W_IDX_ROWS), w_dt], axis=0), LANES)
    w_t = jnp.concatenate([w_q, w_qi, w_k, w_misc, w_v], axis=0)
    return w_t, w_zb, w_xb, jnp.concatenate([w_b, w_c], axis=0), w_g.T, w_za.T


def _rope_tables(positions):
    inv_freq = ROPE_THETA ** (-jnp.arange(0, ROT_DIM, 2, dtype=_F32) / ROT_DIM)
    ang = jnp.swapaxes(positions.astype(_F32)[..., None] * inv_freq, 1, 2)
    return jnp.cos(ang), jnp.sin(ang)


def _lane_broadcast(v):
    return jnp.broadcast_to(v[..., None], v.shape + (LANES,))


def kernel(x, positions, norm_w, w_in, gate_bias, conv_w, conv_b, dt_bias, a_log, d_skip,
           ssm_norm_w, w_branch_a, w_branch_b, w_out, final_norm_w):
    depth = norm_w.shape[0]
    cos_t, sin_t = _rope_tables(positions)
    for i in range(depth):
        weights = _layer_weights(w_in, i)
        conv_params = (_lane_broadcast(conv_w[i][:, :SSM_WIDTH]), _lane_broadcast(conv_b[i][:SSM_WIDTH]),
                       _lane_broadcast(conv_w[i][:, SSM_WIDTH:]), _lane_broadcast(conv_b[i][SSM_WIDTH:]))
        (q_t, qi_t, wi_t, dt_t, v_t, k, kidx, zb_t, xs_t, bcs_t, b_tok, gates, za) = _projection(
            x, norm_w[i][None, :], cos_t, sin_t, weights, conv_params)
        o_a = _attention(q_t, qi_t, wi_t, k, kidx, v_t, za)
        o_b_t = _ssd(xs_t, bcs_t, b_tok, dt_t, zb_t, dt_bias[i][:, None], a_log[i][:, None],
                     _lane_broadcast(jnp.repeat(d_skip[i], SSM_HEAD_DIM)), _lane_broadcast(ssm_norm_w[i]))
        x = _merge(x, o_a, o_b_t, gates, gate_bias[i][None, :], w_branch_a[i].astype(_BF16),
                   w_branch_b[i].astype(_BF16), w_out[i].astype(_BF16), final_norm_w[None, :],
                   final_norm=(i == depth - 1))
    return x
```

```python
import functools

import numpy as np
import jax
import jax.numpy as jnp
from jax import lax
from jax.experimental import pallas as pl
from jax.experimental.pallas import tpu as pltpu

D_MODEL = 1024
ATTN_HEADS = 8
ATTN_KV_HEADS = 2
HEAD_DIM = 64
ATTN_GROUP = ATTN_HEADS // ATTN_KV_HEADS
ATTN_WIDTH = ATTN_HEADS * HEAD_DIM
KV_WIDTH = ATTN_KV_HEADS * HEAD_DIM
ROT_DIM = HEAD_DIM // 4
ROT_HALF = ROT_DIM // 2
ROPE_THETA = 500000.0
IDX_HEADS = 4
IDX_DIM = 64
TOPK_MAX = 256
SSM_HEADS = 16
SSM_HEAD_DIM = 64
SSM_WIDTH = SSM_HEADS * SSM_HEAD_DIM
SSM_GROUPS = 4
SSM_STATE = 64
CONV_K = 4
CHUNK = 128
BC_WIDTH = 2 * SSM_GROUPS * SSM_STATE
N_BRANCH = 2
EPS = 1e-6
SPLIT_SIZES = (ATTN_WIDTH, KV_WIDTH, KV_WIDTH, ATTN_WIDTH,
               IDX_HEADS * IDX_DIM, IDX_DIM, IDX_HEADS,
               SSM_WIDTH, SSM_WIDTH, SSM_GROUPS * SSM_STATE, SSM_GROUPS * SSM_STATE, SSM_HEADS,
               N_BRANCH * D_MODEL)
SPLIT_OFFSETS = tuple(int(o) for o in np.cumsum((0,) + SPLIT_SIZES))

LANES = 128
SUBLANES = 8
VMEM_LIMIT_BYTES = 56 * 1024 * 1024

PROJ_ROWS = 256
CONV_ROWS = 256
CAST_ROWS = 512
Q_TILE = 256
KEY_TILE = 256
COUNT_ROWS = 32
LOG2_E = 1.4426950408889634
MERGE_ROWS = 512
W_IDX_ROWS = 8
V_ROWS = HEAD_DIM + 16
NEG_BIG = -1e30

_F32 = jnp.float32
_BF16 = jnp.bfloat16
_NT = (((1,), (1,)), ((), ()))
_TN = (((0,), (0,)), ((), ()))


def _dot(a, b):
    return jnp.dot(a, b, preferred_element_type=_F32)


def _silu(x):
    return x * jax.nn.sigmoid(x)


def _causal_conv_silu_t(x, tail_ref, w_ref, b_ref):
    cols = x.shape[1]
    reps = cols // LANES
    tail = tail_ref[...]
    lane = lax.broadcasted_iota(jnp.int32, (x.shape[0], LANES), 1)
    wide = lambda a: jnp.concatenate([a] * reps, axis=1)
    y = x * wide(w_ref[CONV_K - 1]) + wide(b_ref[...])
    for shift in range(1, CONV_K):
        rolled = pltpu.roll(x, shift, 1)
        head = jnp.where(lane < shift, pltpu.roll(tail, shift, 1), rolled[:, :LANES])
        shifted = jnp.concatenate([head, rolled[:, LANES:]], axis=1)
        y = y + shifted * wide(w_ref[CONV_K - 1 - shift])
    tail_ref[...] = x[:, cols - LANES:]
    return _silu(y)


def _proj_kernel(x_ref, nw_ref, cos_t_ref, sin_t_ref,
                 w_t_ref, w_zb_ref, w_xb_ref, w_bc_ref, w_g_ref, w_za_ref,
                 cwx_ref, cbx_ref, cwbc_ref, cbbc_ref,
                 q_t_ref, qi_t_ref, wi_t_ref, dt_t_ref, v_t_ref, k_ref, kidx_ref,
                 zb_t_ref, xs_t_ref, bcs_t_ref, b_tok_ref, g_ref, za_ref,
                 tailx_scr, tailbc_scr):
    @pl.when(pl.program_id(1) == 0)
    def _():
        tailx_scr[...] = jnp.zeros_like(tailx_scr)
        tailbc_scr[...] = jnp.zeros_like(tailbc_scr)

    x = x_ref[0]
    h = x * lax.rsqrt(jnp.mean(x * x, axis=-1, keepdims=True) + EPS) * nw_ref[...]
    h = h.astype(_BF16)

    t = lax.dot_general(w_t_ref[...], h, _NT, preferred_element_type=_F32)
    cos_t = cos_t_ref[0]
    sin_t = sin_t_ref[0]

    def rope_head(block, hd):
        x1 = block[hd * HEAD_DIM: hd * HEAD_DIM + ROT_HALF]
        x2 = block[hd * HEAD_DIM + ROT_HALF: hd * HEAD_DIM + ROT_DIM]
        return jnp.concatenate([x1 * cos_t - x2 * sin_t, x2 * cos_t + x1 * sin_t], axis=0)

    def rope_store(block, n_heads, scale, out_ref):
        out_ref[0] = (block * scale).astype(out_ref.dtype)
        for hd in range(n_heads):
            out_ref[0, hd * HEAD_DIM: hd * HEAD_DIM + ROT_DIM, :] = (
                rope_head(block, hd) * scale).astype(out_ref.dtype)

    def rope_value(block, n_heads):
        parts = []
        for hd in range(n_heads):
            parts += [rope_head(block, hd), block[hd * HEAD_DIM + ROT_DIM:(hd + 1) * HEAD_DIM]]
        if block.shape[0] > n_heads * HEAD_DIM:
            parts.append(block[n_heads * HEAD_DIM:])
        return jnp.concatenate(parts, axis=0)

    o_qi = ATTN_WIDTH
    o_k = o_qi + IDX_HEADS * IDX_DIM
    o_ki = o_k + KV_WIDTH
    o_v = o_ki + LANES
    rope_store(t[:o_qi], ATTN_HEADS, LOG2_E * HEAD_DIM ** -0.5, q_t_ref)
    rope_store(t[o_qi:o_k], IDX_HEADS, IDX_DIM ** -0.5, qi_t_ref)
    o_wi = o_ki + IDX_DIM
    o_dt = o_wi + W_IDX_ROWS
    wi_t_ref[0] = t[o_wi:o_dt] * (IDX_HEADS ** -0.5)
    dt_t_ref[0] = t[o_dt:o_dt + SSM_HEADS]
    for g in range(ATTN_KV_HEADS):
        v_t_ref[0, g * V_ROWS:g * V_ROWS + HEAD_DIM, :] = t[o_v + g * HEAD_DIM:o_v + (g + 1) * HEAD_DIM].astype(_BF16)
        v_t_ref[0, g * V_ROWS + HEAD_DIM:(g + 1) * V_ROWS, :] = jnp.ones((V_ROWS - HEAD_DIM, x.shape[0]), _BF16)
    k_ref[0] = rope_value(t[o_k:o_ki], ATTN_KV_HEADS).T.astype(_BF16)
    kidx_ref[0] = rope_value(t[o_ki:o_v], 1).T[:, :IDX_DIM].astype(_BF16)

    def feat_major(w_ref, r0, rows):
        return lax.dot_general(w_ref[r0:r0 + rows], h, _NT, preferred_element_type=_F32)

    def store_chunks(out_ref, sl, value):
        for ck in range(value.shape[1] // CHUNK):
            out_ref[0, ck, sl, :] = value[:, ck * CHUNK:(ck + 1) * CHUNK].astype(out_ref.dtype)

    rc = CONV_ROWS
    for n in range(SSM_WIDTH // rc):
        sl = slice(n * rc, (n + 1) * rc)
        store_chunks(xs_t_ref, sl, _causal_conv_silu_t(
            feat_major(w_xb_ref, n * rc, rc), tailx_scr.at[sl], cwx_ref.at[:, sl], cbx_ref.at[sl]))
        store_chunks(zb_t_ref, sl, feat_major(w_zb_ref, n * rc, rc))
    for n in range(BC_WIDTH // rc):
        sl = slice(n * rc, (n + 1) * rc)
        bc = _causal_conv_silu_t(
            feat_major(w_bc_ref, n * rc, rc), tailbc_scr.at[sl], cwbc_ref.at[:, sl], cbbc_ref.at[sl])
        store_chunks(bcs_t_ref, sl, bc)
        if (n + 1) * rc <= BC_WIDTH // 2:
            b_tok_ref[0, :, sl] = bc.T.astype(_BF16)
    g_ref[0] = _dot(h, w_g_ref[...]).astype(_BF16)
    za_ref[0] = _dot(h, w_za_ref[...])


def _const_spec(shape):
    nd = len(shape)
    return pl.BlockSpec(shape, lambda *_: (0,) * nd, pipeline_mode=pl.Buffered(1))


def _projection(x, nw, cos_t, sin_t, weights, conv_params):
    bsz, seq, _ = x.shape
    tm = PROJ_ROWS
    tok = lambda width: pl.BlockSpec((1, tm, width), lambda b, i: (b, i, 0))
    feat = lambda rows: pl.BlockSpec((1, rows, tm), lambda b, i: (b, 0, i))
    in_specs = [tok(D_MODEL), _const_spec((1, D_MODEL)), feat(ROT_HALF), feat(ROT_HALF)
                ] + [_const_spec(w.shape) for w in weights + conv_params]
    out_shape = (
        jax.ShapeDtypeStruct((bsz, ATTN_WIDTH, seq), _BF16),
        jax.ShapeDtypeStruct((bsz, IDX_HEADS * IDX_DIM, seq), _BF16),
        jax.ShapeDtypeStruct((bsz, W_IDX_ROWS, seq), _F32),
        jax.ShapeDtypeStruct((bsz, SSM_HEADS, seq), _F32),
        jax.ShapeDtypeStruct((bsz, ATTN_KV_HEADS * V_ROWS, seq), _BF16),
        jax.ShapeDtypeStruct((bsz, seq, KV_WIDTH), _BF16),
        jax.ShapeDtypeStruct((bsz, seq, IDX_DIM), _BF16),
        jax.ShapeDtypeStruct((bsz, seq // CHUNK, SSM_WIDTH, CHUNK), _F32),
        jax.ShapeDtypeStruct((bsz, seq // CHUNK, SSM_WIDTH, CHUNK), _BF16),
        jax.ShapeDtypeStruct((bsz, seq // CHUNK, BC_WIDTH, CHUNK), _BF16),
        jax.ShapeDtypeStruct((bsz, seq, BC_WIDTH // 2), _BF16),
        jax.ShapeDtypeStruct((bsz, seq, N_BRANCH * D_MODEL), _BF16),
        jax.ShapeDtypeStruct((bsz, seq, ATTN_WIDTH), _F32),
    )
    slabs = lambda rows: pl.BlockSpec((1, tm // CHUNK, rows, CHUNK), lambda b, i: (b, i, 0, 0))
    out_specs = (feat(ATTN_WIDTH), feat(IDX_HEADS * IDX_DIM), feat(W_IDX_ROWS), feat(SSM_HEADS),
                 feat(ATTN_KV_HEADS * V_ROWS), tok(KV_WIDTH), tok(IDX_DIM), slabs(SSM_WIDTH), slabs(SSM_WIDTH),
                 slabs(BC_WIDTH), tok(BC_WIDTH // 2), tok(N_BRANCH * D_MODEL), tok(ATTN_WIDTH))
    return pl.pallas_call(
        _proj_kernel,
        grid=(bsz, seq // tm),
        in_specs=in_specs,
        out_specs=out_specs,
        out_shape=out_shape,
        scratch_shapes=[pltpu.VMEM((SSM_WIDTH, LANES), _F32), pltpu.VMEM((BC_WIDTH, LANES), _F32)],
        compiler_params=pltpu.CompilerParams(
            dimension_semantics=("parallel", "arbitrary"), vmem_limit_bytes=VMEM_LIMIT_BYTES),
        name="proj",
    )(x, nw, cos_t, sin_t, *weights, *conv_params)


def _attn_kernel(q_t_ref, qi_t_ref, wi_t_ref, k_ref, kidx_ref, v_t_ref, za_ref, o_ref,
                 score_scr, score16_scr, sel_scr, bias_scr, qpad_scr, m_scr, alpha_scr, acc_scr, s_scr,
                 *, top_k):
    j = pl.program_id(1)
    n_chunks = j + 1
    tq = Q_TILE
    q_pos = j * tq + lax.broadcasted_iota(jnp.int32, (1, tq), 1)
    key_iota = lax.broadcasted_iota(jnp.int32, (KEY_TILE, tq), 0)

    qi_t = qi_t_ref[0]
    qi_cat = jnp.concatenate([qi_t[hd * IDX_DIM:(hd + 1) * IDX_DIM] for hd in range(IDX_HEADS)], axis=1)
    wi_t = wi_t_ref[0]
    wi_cat = jnp.concatenate([wi_t[hd:hd + 1] for hd in range(IDX_HEADS)], axis=1)

    def score_chunk(c, carry):
        start = pl.multiple_of(c * KEY_TILE, KEY_TILE)
        logits = _dot(kidx_ref[0, pl.ds(start, KEY_TILE), :], qi_cat)
        weighted = jnp.maximum(logits, 0.0) * wi_cat
        score = weighted[:, 0:tq]
        for hd in range(1, IDX_HEADS):
            score = score + weighted[:, hd * tq:(hd + 1) * tq]
        score = jnp.where((start + key_iota) <= q_pos, score, -jnp.inf)
        score_scr[pl.ds(start, KEY_TILE), :] = score
        score16_scr[pl.ds(start, KEY_TILE), :] = score.astype(_BF16)
        return carry

    lax.fori_loop(0, n_chunks, score_chunk, 0)

    def code_to_f32(code):
        return pltpu.bitcast(code ^ ((code >> 31) & jnp.int32(0x7FFFFFFF)), _F32)

    def count_ge(ref, cand, dtype):
        def body(c, acc):
            start = pl.multiple_of(c * KEY_TILE, KEY_TILE)
            inc = jnp.where(ref[pl.ds(start, KEY_TILE), :] >= cand, jnp.ones((), dtype), jnp.zeros((), dtype))
            for part in range(KEY_TILE // COUNT_ROWS):
                acc = acc + inc[part * COUNT_ROWS:(part + 1) * COUNT_ROWS]
            return acc
        acc = lax.fori_loop(0, n_chunks, body, jnp.zeros((COUNT_ROWS, tq), dtype))
        return jnp.sum(acc.astype(_F32), axis=0, keepdims=True)

    def bf16_code(block):
        return jnp.where(block >= 0, block, block | jnp.int32(0xFFFF))

    def coarse_bit(i, block):
        cand = block + (jnp.int32(1) << (31 - i))
        hit = count_ge(score16_scr, code_to_f32(bf16_code(cand)).astype(_BF16), _BF16) >= top_k
        return jnp.where(hit, cand, block)

    def fine_bit(i, carry):
        code, n_above = carry
        cand = code + (jnp.int32(1) << (16 - i))
        cnt = count_ge(score_scr, code_to_f32(cand), _F32)
        hit = cnt >= top_k
        return jnp.where(hit, cand, code), jnp.where(hit, n_above, cnt)

    sel_scr[0:1, :] = jnp.full((1, tq), -jnp.inf, _F32)
    sel_scr[1:2, :] = jnp.zeros((1, tq), _F32)

    @pl.when((j + 1) * tq > top_k)
    def _():
        int_min = jnp.int32(-2 ** 31)
        coarse = lax.fori_loop(0, 16, coarse_bit, jnp.full((1, tq), int_min, jnp.int32))
        base = bf16_code(coarse) - jnp.int32(1 << 16)
        fine, n_above = lax.fori_loop(0, 17, fine_bit, (base, jnp.zeros((1, tq), _F32)))
        few = (q_pos + 1) <= top_k
        sel_scr[0:1, :] = jnp.where(few, -jnp.inf, code_to_f32(fine))
        sel_scr[1:2, :] = jnp.where(few, 0.0, top_k - n_above)

    thr = sel_scr[0:1, :]
    n_ties_kept = sel_scr[1:2, :]

    row = lax.broadcasted_iota(jnp.int32, (KEY_TILE, KEY_TILE), 0)
    col = lax.broadcasted_iota(jnp.int32, (KEY_TILE, KEY_TILE), 1)
    strict_lower = jnp.where(col < row, 1.0, 0.0).astype(_BF16)

    def bias_tile(c, ties_before):
        start = pl.multiple_of(c * KEY_TILE, KEY_TILE)
        s = score_scr[pl.ds(start, KEY_TILE), :]
        tie = jnp.where(s == thr, 1.0, 0.0)
        rank = _dot(strict_lower, tie.astype(_BF16)) + ties_before
        keep = (s > thr) | ((s == thr) & (rank < n_ties_kept))
        bias_scr[...] = jnp.where(keep, 0.0, NEG_BIG)
        return ties_before + jnp.sum(tie, axis=0, keepdims=True)

    n_pairs = ATTN_HEADS // 2
    zeros_half = jnp.zeros((HEAD_DIM, 2 * tq), _BF16)
    for pair in range(n_pairs):
        g = (2 * pair) // ATTN_GROUP
        q_cat = jnp.concatenate([q_t_ref[0, (2 * pair + hd) * HEAD_DIM:(2 * pair + hd + 1) * HEAD_DIM, :]
                                 for hd in range(2)], axis=1)
        qpad_scr[pair] = jnp.concatenate([q_cat, zeros_half] if g == 0 else [zeros_half, q_cat], axis=0)
    m_scr[...] = jnp.full(m_scr.shape, NEG_BIG, _F32)
    acc_scr[...] = jnp.zeros(acc_scr.shape, _F32)

    def logits_stage(c, pair):
        start = pl.multiple_of(c * KEY_TILE, KEY_TILE)
        b = bias_scr[...]
        s = _dot(k_ref[0, pl.ds(start, KEY_TILE), :], qpad_scr[pair]) + jnp.concatenate([b, b], axis=1)
        m_old = m_scr[pair]
        m_new = jnp.maximum(m_old, jnp.max(s, axis=0, keepdims=True))
        s_scr[pair] = s
        alpha_scr[pair] = jnp.exp2(m_old - m_new)
        m_scr[pair] = m_new

    def value_stage(c, pair):
        start = pl.multiple_of(c * KEY_TILE, KEY_TILE)
        g = (2 * pair) // ATTN_GROUP
        p = jnp.exp2(s_scr[pair] - m_scr[pair])
        v_c = v_t_ref[0, g * V_ROWS:(g + 1) * V_ROWS, pl.ds(start, KEY_TILE)]
        acc_scr[pair] = alpha_scr[pair] * acc_scr[pair] + _dot(v_c, p.astype(_BF16))

    ties = bias_tile(0, jnp.zeros((1, tq), _F32))
    for pair in range(n_pairs):
        logits_stage(0, pair)

    def attn_tile(c, ties):
        ties = bias_tile(c + 1, ties)
        for pair in range(n_pairs):
            value_stage(c, pair)
            logits_stage(c + 1, pair)
        return ties

    lax.fori_loop(0, n_chunks - 1, attn_tile, ties)
    for pair in range(n_pairs):
        value_stage(n_chunks - 1, pair)

    for pair in range(n_pairs):
        o_t = acc_scr[pair, :HEAD_DIM] / acc_scr[pair, HEAD_DIM:HEAD_DIM + 1]
        both = jnp.concatenate([o_t[:, :tq], o_t[:, tq:]], axis=0)
        z = za_ref[0, :, pair * LANES:(pair + 1) * LANES]
        o_ref[0, :, pair * LANES:(pair + 1) * LANES] = (both.T * _silu(z)).astype(o_ref.dtype)


def _attention(q_t, qi_t, wi_t, k, kidx, v_t, za):
    bsz, seq, _ = k.shape
    tq = Q_TILE
    top_k = min(TOPK_MAX, seq // 4)
    feat = lambda rows: pl.BlockSpec((1, rows, tq), lambda b, j: (b, 0, j))
    full = lambda s1, s2: pl.BlockSpec((1, s1, s2), lambda b, j: (b, 0, 0))
    return pl.pallas_call(
        functools.partial(_attn_kernel, top_k=top_k),
        grid=(bsz, seq // tq),
        in_specs=[feat(ATTN_WIDTH), feat(IDX_HEADS * IDX_DIM), feat(W_IDX_ROWS),
                  full(seq, KV_WIDTH), full(seq, IDX_DIM), full(ATTN_KV_HEADS * V_ROWS, seq),
                  pl.BlockSpec((1, tq, ATTN_WIDTH), lambda b, j: (b, j, 0))],
        out_specs=pl.BlockSpec((1, tq, ATTN_WIDTH), lambda b, j: (b, j, 0)),
        out_shape=jax.ShapeDtypeStruct((bsz, seq, ATTN_WIDTH), _BF16),
        scratch_shapes=[pltpu.VMEM((seq, tq), _F32), pltpu.VMEM((seq, tq), _BF16),
                        pltpu.VMEM((SUBLANES, tq), _F32), pltpu.VMEM((KEY_TILE, tq), _F32),
                        pltpu.VMEM((ATTN_HEADS // 2, 2 * HEAD_DIM, 2 * tq), _BF16),
                        pltpu.VMEM((ATTN_HEADS // 2, 1, 2 * tq), _F32),
                        pltpu.VMEM((ATTN_HEADS // 2, 1, 2 * tq), _F32),
                        pltpu.VMEM((ATTN_HEADS // 2, V_ROWS, 2 * tq), _F32),
                        pltpu.VMEM((ATTN_HEADS // 2, KEY_TILE, 2 * tq), _F32)],
        compiler_params=pltpu.CompilerParams(
            dimension_semantics=("parallel", "arbitrary"), vmem_limit_bytes=VMEM_LIMIT_BYTES),
        name="attn",
    )(q_t, qi_t, wi_t, k, kidx, v_t, za)


def _split3(x):
    hi = x.astype(_BF16)
    r1 = x - hi.astype(_F32)
    mid = r1.astype(_BF16)
    lo = (r1 - mid.astype(_F32)).astype(_BF16)
    return hi, mid, lo


def _ssd_kernel(xs_t_ref, bcs_t_ref, b_tok_ref, dt_ref, zb_t_ref, dtb_ref, alog_ref, dskip_ref, nw_ref, o_t_ref,
                state_scr, y_scr, dt_scr, acum_scr, acum_t_scr, cols_even_scr, cols_odd_scr):
    i = pl.program_id(1)
    r = lax.broadcasted_iota(jnp.int32, (CHUNK, CHUNK), 0)
    c = lax.broadcasted_iota(jnp.int32, (CHUNK, CHUNK), 1)
    causal_t = r <= c

    @pl.when(i == 0)
    def _():
        state_scr[...] = jnp.zeros_like(state_scr)
        upper_b = jnp.where(causal_t, 1.0, 0.0).astype(_BF16)
        dt_in_t = dt_ref[0] + dtb_ref[...]
        dt_t = jnp.maximum(dt_in_t, 0.0) + jnp.log1p(jnp.exp(-jnp.abs(dt_in_t)))
        dt_scr[...] = dt_t
        parts = _split3(dt_t * (-jnp.exp(alog_ref[...])))
        pad = jnp.zeros((LANES - SSM_HEADS, CHUNK), _F32)
        for n in range(dt_t.shape[1] // CHUNK):
            sl = slice(n * CHUNK, (n + 1) * CHUNK)
            a_cum_t = sum(_dot(part[:, sl], upper_b) for part in parts)
            acum_t_scr[:, sl] = a_cum_t
            acum_scr[sl, :] = jnp.concatenate([a_cum_t, pad], axis=0).T

    def spread_columns(chunk, cols_ref):
        a_cum = acum_scr[pl.ds(pl.multiple_of(chunk * CHUNK, CHUNK), CHUNK), :]
        for hh in range(SSM_HEADS):
            cols_ref[:, hh * LANES:(hh + 1) * LANES] = jnp.broadcast_to(a_cum[:, hh:hh + 1], (CHUNK, LANES))

    @pl.when(i == 0)
    def _():
        spread_columns(0, cols_even_scr)

    def chunk_step(cols_ref, next_cols_ref):
        start = pl.multiple_of(i * CHUNK, CHUNK)
        dt_t = dt_scr[:, pl.ds(start, CHUNK)]
        a_cum_t = acum_t_scr[:, pl.ds(start, CHUNK)]
        heads_per_group = SSM_HEADS // SSM_GROUPS

        for grp in range(SSM_GROUPS):
            b_t = bcs_t_ref[0, 0, grp * SSM_STATE:(grp + 1) * SSM_STATE, :]
            c_t = bcs_t_ref[0, 0, BC_WIDTH // 2 + grp * SSM_STATE: BC_WIDTH // 2 + (grp + 1) * SSM_STATE, :]
            b_tok = b_tok_ref[0, :, grp * SSM_STATE:(grp + 1) * SSM_STATE]
            cb_t = _dot(b_tok, c_t)
            cb_t = jnp.where(causal_t, cb_t, 0.0)
            if grp == 0:
                spread_columns(jnp.minimum(i + 1, pl.num_programs(1) - 1), next_cols_ref)

            for hh in range(grp * heads_per_group, (grp + 1) * heads_per_group):
                rows = slice(hh * SSM_HEAD_DIM, (hh + 1) * SSM_HEAD_DIM)
                acum_row = a_cum_t[hh:hh + 1, :]
                last = acum_row[:, CHUNK - 1:CHUNK]
                seg_t = acum_row - cols_ref[:, hh * LANES:(hh + 1) * LANES]
                m_t = (cb_t * jnp.exp(jnp.minimum(seg_t, 0.0))).astype(_BF16)
                x_h = xs_t_ref[0, 0, rows, :].astype(_F32)
                xd = x_h * dt_t[hh:hh + 1, :]
                y = _dot(xd.astype(_BF16), m_t)
                st_prev = state_scr[hh]
                y_off = _dot(st_prev.astype(_BF16), c_t) * jnp.exp(acum_row)
                w = (xd * jnp.exp(last - acum_row)).astype(_BF16)
                st_new = lax.dot_general(w, b_t, _NT, preferred_element_type=_F32)
                state_scr[hh] = jnp.exp(last) * st_prev + st_new
                y_scr[rows, :] = y + y_off + dskip_ref[rows, :] * x_h

        gw = SSM_WIDTH // SSM_GROUPS
        for grp in range(SSM_GROUPS):
            sl = slice(grp * gw, (grp + 1) * gw)
            yz = y_scr[sl, :] * _silu(zb_t_ref[0, 0, sl, :])
            ms = jnp.mean(yz * yz, axis=0, keepdims=True)
            o_t_ref[0, 0, sl, :] = (yz * lax.rsqrt(ms + EPS) * nw_ref[sl, :]).astype(o_t_ref.dtype)

    @pl.when(i % 2 == 0)
    def _():
        chunk_step(cols_even_scr, cols_odd_scr)

    @pl.when(i % 2 == 1)
    def _():
        chunk_step(cols_odd_scr, cols_even_scr)


def _ssd(xs_t, bcs_t, b_tok, dt_t, zb_t, dtb, alog, dskip, nw):
    bsz, seq = b_tok.shape[:2]
    feat = lambda rows: pl.BlockSpec((1, 1, rows, CHUNK), lambda b, i: (b, i, 0, 0))
    consts = (dtb, alog, dskip, nw)
    return pl.pallas_call(
        _ssd_kernel,
        grid=(bsz, seq // CHUNK),
        in_specs=[feat(SSM_WIDTH), feat(BC_WIDTH),
                  pl.BlockSpec((1, CHUNK, BC_WIDTH // 2), lambda b, i: (b, i, 0)),
                  pl.BlockSpec((1, SSM_HEADS, seq), lambda b, i: (b, 0, 0)),
                  feat(SSM_WIDTH)] + [_const_spec(a.shape) for a in consts],
        out_specs=feat(SSM_WIDTH),
        out_shape=jax.ShapeDtypeStruct((bsz, seq // CHUNK, SSM_WIDTH, CHUNK), _BF16),
        scratch_shapes=[pltpu.VMEM((SSM_HEADS, SSM_HEAD_DIM, SSM_STATE), _F32),
                        pltpu.VMEM((SSM_WIDTH, CHUNK), _F32),
                        pltpu.VMEM((SSM_HEADS, seq), _F32), pltpu.VMEM((seq, LANES), _F32),
                        pltpu.VMEM((SSM_HEADS, seq), _F32),
                        pltpu.VMEM((CHUNK, SSM_HEADS * LANES), _F32),
                        pltpu.VMEM((CHUNK, SSM_HEADS * LANES), _F32)],
        compiler_params=pltpu.CompilerParams(
            dimension_semantics=("parallel", "arbitrary"), vmem_limit_bytes=VMEM_LIMIT_BYTES),
        name="ssd",
    )(xs_t, bcs_t, b_tok, dt_t, zb_t, *consts)


def _merge_kernel(x_ref, oa_ref, ob_t_ref, g_ref, gb_ref, wa_ref, wb_ref, wo_ref, fw_ref, o_ref, *, final_norm):
    gates = jax.nn.sigmoid(g_ref[0] + gb_ref[...])
    ob_t = jnp.concatenate([ob_t_ref[0, ck] for ck in range(ob_t_ref.shape[1])], axis=1)
    branch_b = lax.dot_general(ob_t, wb_ref[...], _TN, preferred_element_type=_F32)
    merged = gates[:, :D_MODEL] * _dot(oa_ref[0], wa_ref[...]) + gates[:, D_MODEL:] * branch_b
    y = x_ref[0] + _dot(merged.astype(_BF16), wo_ref[...])
    if final_norm:
        y = y * lax.rsqrt(jnp.mean(y * y, axis=-1, keepdims=True) + EPS) * fw_ref[...]
    o_ref[0] = y


def _merge(x, o_a, o_b_t, gates, gate_bias, wa, wb, wo, fw, final_norm):
    bsz, seq, _ = x.shape
    tm = MERGE_ROWS
    tok = lambda width: pl.BlockSpec((1, tm, width), lambda b, i: (b, i, 0))
    consts = (gate_bias, wa, wb, wo, fw)
    return pl.pallas_call(
        functools.partial(_merge_kernel, final_norm=final_norm),
        grid=(bsz, seq // tm),
        in_specs=[tok(D_MODEL), tok(ATTN_WIDTH), pl.BlockSpec((1, tm // CHUNK, SSM_WIDTH, CHUNK), lambda b, i: (b, i, 0, 0)),
                  tok(N_BRANCH * D_MODEL)] + [_const_spec(a.shape) for a in consts],
        out_specs=tok(D_MODEL),
        out_shape=jax.ShapeDtypeStruct((bsz, seq, D_MODEL), _F32),
        compiler_params=pltpu.CompilerParams(
            dimension_semantics=("parallel", "parallel"), vmem_limit_bytes=VMEM_LIMIT_BYTES),
        name="merge",
    )(x, o_a, o_b_t, gates, *consts)


def _cast_kernel(w_ref, o_ref):
    o_ref[...] = w_ref[0].astype(o_ref.dtype)


def _layer_transposed_bf16(w_all, layer):
    w_t_all = jnp.swapaxes(w_all, 1, 2)
    _, rows, cols = w_t_all.shape
    tm = CAST_ROWS
    return pl.pallas_call(
        _cast_kernel,
        grid=(pl.cdiv(rows, tm),),
        in_specs=[pl.BlockSpec((1, tm, cols), lambda i: (layer, i, 0))],
        out_specs=pl.BlockSpec((tm, cols), lambda i: (i, 0)),
        out_shape=jax.ShapeDtypeStruct((rows, cols), _BF16),
        compiler_params=pltpu.CompilerParams(dimension_semantics=("parallel",)),
        name="cast",
    )(w_t_all)


def _layer_weights(w_in_all, layer):
    wt16 = _layer_transposed_bf16(w_in_all, layer)
    o = SPLIT_OFFSETS
    seg = lambda i: wt16[o[i]:o[i + 1]]
    w_q, w_k, w_v, w_za, w_qi, w_ki, w_wi, w_zb, w_xb, w_b, w_c, w_dt, w_g = (seg(i) for i in range(13))
    pad = lambda w, n: jnp.pad(w, ((0, n - w.shape[0]), (0, 0)))
    w_misc = pad(jnp.concatenate([w_ki, pad(w_wi, W_IDX_ROWS), w_dt], axis=0), LANES)
    w_t = jnp.concatenate([w_q, w_qi, w_k, w_misc, w_v], axis=0)
    return w_t, w_zb, w_xb, jnp.concatenate([w_b, w_c], axis=0), w_g.T, w_za.T


def _rope_tables(positions):
    inv_freq = ROPE_THETA ** (-jnp.arange(0, ROT_DIM, 2, dtype=_F32) / ROT_DIM)
    ang = jnp.swapaxes(positions.astype(_F32)[..., None] * inv_freq, 1, 2)
    return jnp.cos(ang), jnp.sin(ang)


def _lane_broadcast(v):
    return jnp.broadcast_to(v[..., None], v.shape + (LANES,))


def kernel(x, positions, norm_w, w_in, gate_bias, conv_w, conv_b, dt_bias, a_log, d_skip,
           ssm_norm_w, w_branch_a, w_branch_b, w_out, final_norm_w):
    depth = norm_w.shape[0]
    cos_t, sin_t = _rope_tables(positions)
    for i in range(depth):
        weights = _layer_weights(w_in, i)
        conv_params = (_lane_broadcast(conv_w[i][:, :SSM_WIDTH]), _lane_broadcast(conv_b[i][:SSM_WIDTH]),
                       _lane_broadcast(conv_w[i][:, SSM_WIDTH:]), _lane_broadcast(conv_b[i][SSM_WIDTH:]))
        (q_t, qi_t, wi_t, dt_t, v_t, k, kidx, zb_t, xs_t, bcs_t, b_tok, gates, za) = _projection(
            x, norm_w[i][None, :], cos_t, sin_t, weights, conv_params)
        o_a = _attention(q_t, qi_t, wi_t, k, kidx, v_t, za)
        o_b_t = _ssd(xs_t, bcs_t, b_tok, dt_t, zb_t, dt_bias[i][:, None], a_log[i][:, None],
                     _lane_broadcast(jnp.repeat(d_skip[i], SSM_HEAD_DIM)), _lane_broadcast(ssm_norm_w[i]))
        x = _merge(x, o_a, o_b_t, gates, gate_bias[i][None, :], w_branch_a[i].astype(_BF16),
                   w_branch_b[i].astype(_BF16), w_out[i].astype(_BF16), final_norm_w[None, :],
                   final_norm=(i == depth - 1))
    return x
```

```python
import functools

import numpy as np
import jax
import jax.numpy as jnp
from jax import lax
from jax.experimental import pallas as pl
from jax.experimental.pallas import tpu as pltpu

D_MODEL = 1024
ATTN_HEADS = 8
ATTN_KV_HEADS = 2
HEAD_DIM = 64
ATTN_GROUP = ATTN_HEADS // ATTN_KV_HEADS
ATTN_WIDTH = ATTN_HEADS * HEAD_DIM
KV_WIDTH = ATTN_KV_HEADS * HEAD_DIM
ROT_DIM = HEAD_DIM // 4
ROT_HALF = ROT_DIM // 2
ROPE_THETA = 500000.0
IDX_HEADS = 4
IDX_DIM = 64
TOPK_MAX = 256
SSM_HEADS = 16
SSM_HEAD_DIM = 64
SSM_WIDTH = SSM_HEADS * SSM_HEAD_DIM
SSM_GROUPS = 4
SSM_STATE = 64
CONV_K = 4
CHUNK = 128
BC_WIDTH = 2 * SSM_GROUPS * SSM_STATE
N_BRANCH = 2
EPS = 1e-6
SPLIT_SIZES = (ATTN_WIDTH, KV_WIDTH, KV_WIDTH, ATTN_WIDTH,
               IDX_HEADS * IDX_DIM, IDX_DIM, IDX_HEADS,
               SSM_WIDTH, SSM_WIDTH, SSM_GROUPS * SSM_STATE, SSM_GROUPS * SSM_STATE, SSM_HEADS,
               N_BRANCH * D_MODEL)
SPLIT_OFFSETS = tuple(int(o) for o in np.cumsum((0,) + SPLIT_SIZES))

LANES = 128
SUBLANES = 8
VMEM_LIMIT_BYTES = 56 * 1024 * 1024

PROJ_ROWS = 256
CONV_ROWS = 256
CAST_ROWS = 512
Q_TILE = 256
KEY_TILE = 256
COUNT_ROWS = 32
LOG2_E = 1.4426950408889634
MERGE_ROWS = 512
SSD_STEP_CHUNKS = 4
W_IDX_ROWS = 8
V_ROWS = HEAD_DIM + 16
NEG_BIG = -1e30

_F32 = jnp.float32
_BF16 = jnp.bfloat16
_NT = (((1,), (1,)), ((), ()))
_TN = (((0,), (0,)), ((), ()))


def _dot(a, b):
    return jnp.dot(a, b, preferred_element_type=_F32)


def _silu(x):
    return x * jax.nn.sigmoid(x)


def _causal_conv_silu_t(x, tail_ref, w_ref, b_ref):
    cols = x.shape[1]
    reps = cols // LANES
    tail = tail_ref[...]
    lane = lax.broadcasted_iota(jnp.int32, (x.shape[0], LANES), 1)
    wide = lambda a: jnp.concatenate([a] * reps, axis=1)
    y = x * wide(w_ref[CONV_K - 1]) + wide(b_ref[...])
    for shift in range(1, CONV_K):
        rolled = pltpu.roll(x, shift, 1)
        head = jnp.where(lane < shift, pltpu.roll(tail, shift, 1), rolled[:, :LANES])
        shifted = jnp.concatenate([head, rolled[:, LANES:]], axis=1)
        y = y + shifted * wide(w_ref[CONV_K - 1 - shift])
    tail_ref[...] = x[:, cols - LANES:]
    return _silu(y)


def _proj_kernel(x_ref, nw_ref, cos_t_ref, sin_t_ref,
                 w_t_ref, w_zb_ref, w_xb_ref, w_bc_ref, w_g_ref, w_za_ref,
                 cwx_ref, cbx_ref, cwbc_ref, cbbc_ref,
                 q_t_ref, qi_t_ref, wi_t_ref, dt_t_ref, v_t_ref, k_ref, kidx_ref,
                 zb_t_ref, xs_t_ref, bcs_t_ref, b_tok_ref, g_ref, za_ref,
                 tailx_scr, tailbc_scr):
    @pl.when(pl.program_id(1) == 0)
    def _():
        tailx_scr[...] = jnp.zeros_like(tailx_scr)
        tailbc_scr[...] = jnp.zeros_like(tailbc_scr)

    x = x_ref[0]
    h = x * lax.rsqrt(jnp.mean(x * x, axis=-1, keepdims=True) + EPS) * nw_ref[...]
    h = h.astype(_BF16)

    t = lax.dot_general(w_t_ref[...], h, _NT, preferred_element_type=_F32)
    cos_t = cos_t_ref[0]
    sin_t = sin_t_ref[0]

    def rope_head(block, hd):
        x1 = block[hd * HEAD_DIM: hd * HEAD_DIM + ROT_HALF]
        x2 = block[hd * HEAD_DIM + ROT_HALF: hd * HEAD_DIM + ROT_DIM]
        return jnp.concatenate([x1 * cos_t - x2 * sin_t, x2 * cos_t + x1 * sin_t], axis=0)

    def rope_store(block, n_heads, scale, out_ref):
        out_ref[0] = (block * scale).astype(out_ref.dtype)
        for hd in range(n_heads):
            out_ref[0, hd * HEAD_DIM: hd * HEAD_DIM + ROT_DIM, :] = (
                rope_head(block, hd) * scale).astype(out_ref.dtype)

    def rope_value(block, n_heads):
        parts = []
        for hd in range(n_heads):
            parts += [rope_head(block, hd), block[hd * HEAD_DIM + ROT_DIM:(hd + 1) * HEAD_DIM]]
        if block.shape[0] > n_heads * HEAD_DIM:
            parts.append(block[n_heads * HEAD_DIM:])
        return jnp.concatenate(parts, axis=0)

    o_qi = ATTN_WIDTH
    o_k = o_qi + IDX_HEADS * IDX_DIM
    o_ki = o_k + KV_WIDTH
    o_v = o_ki + LANES
    rope_store(t[:o_qi], ATTN_HEADS, LOG2_E * HEAD_DIM ** -0.5, q_t_ref)
    rope_store(t[o_qi:o_k], IDX_HEADS, IDX_DIM ** -0.5, qi_t_ref)
    o_wi = o_ki + IDX_DIM
    o_dt = o_wi + W_IDX_ROWS
    wi_t_ref[0] = t[o_wi:o_dt] * (IDX_HEADS ** -0.5)
    dt_t_ref[0] = t[o_dt:o_dt + SSM_HEADS]
    for g in range(ATTN_KV_HEADS):
        v_t_ref[0, g * V_ROWS:g * V_ROWS + HEAD_DIM, :] = t[o_v + g * HEAD_DIM:o_v + (g + 1) * HEAD_DIM].astype(_BF16)
        v_t_ref[0, g * V_ROWS + HEAD_DIM:(g + 1) * V_ROWS, :] = jnp.ones((V_ROWS - HEAD_DIM, x.shape[0]), _BF16)
    k_ref[0] = rope_value(t[o_k:o_ki], ATTN_KV_HEADS).T.astype(_BF16)
    kidx_ref[0] = rope_value(t[o_ki:o_v], 1).T[:, :IDX_DIM].astype(_BF16)

    def feat_major(w_ref, r0, rows):
        return lax.dot_general(w_ref[r0:r0 + rows], h, _NT, preferred_element_type=_F32)

    def store_chunks(out_ref, sl, value):
        for ck in range(value.shape[1] // CHUNK):
            out_ref[0, ck, sl, :] = value[:, ck * CHUNK:(ck + 1) * CHUNK].astype(out_ref.dtype)

    rc = CONV_ROWS
    for n in range(SSM_WIDTH // rc):
        sl = slice(n * rc, (n + 1) * rc)
        store_chunks(xs_t_ref, sl, _causal_conv_silu_t(
            feat_major(w_xb_ref, n * rc, rc), tailx_scr.at[sl], cwx_ref.at[:, sl], cbx_ref.at[sl]))
        store_chunks(zb_t_ref, sl, feat_major(w_zb_ref, n * rc, rc))
    for n in range(BC_WIDTH // rc):
        sl = slice(n * rc, (n + 1) * rc)
        bc = _causal_conv_silu_t(
            feat_major(w_bc_ref, n * rc, rc), tailbc_scr.at[sl], cwbc_ref.at[:, sl], cbbc_ref.at[sl])
        store_chunks(bcs_t_ref, sl, bc)
        if (n + 1) * rc <= BC_WIDTH // 2:
            b_tok_ref[0, :, sl] = bc.T.astype(_BF16)
    g_ref[0] = _dot(h, w_g_ref[...]).astype(_BF16)
    za_ref[0] = _dot(h, w_za_ref[...])


def _const_spec(shape):
    nd = len(shape)
    return pl.BlockSpec(shape, lambda *_: (0,) * nd, pipeline_mode=pl.Buffered(1))


def _projection(x, nw, cos_t, sin_t, weights, conv_params):
    bsz, seq, _ = x.shape
    tm = PROJ_ROWS
    tok = lambda width: pl.BlockSpec((1, tm, width), lambda b, i: (b, i, 0))
    feat = lambda rows: pl.BlockSpec((1, rows, tm), lambda b, i: (b, 0, i))
    in_specs = [tok(D_MODEL), _const_spec((1, D_MODEL)), feat(ROT_HALF), feat(ROT_HALF)
                ] + [_const_spec(w.shape) for w in weights + conv_params]
    out_shape = (
        jax.ShapeDtypeStruct((bsz, ATTN_WIDTH, seq), _BF16),
        jax.ShapeDtypeStruct((bsz, IDX_HEADS * IDX_DIM, seq), _BF16),
        jax.ShapeDtypeStruct((bsz, W_IDX_ROWS, seq), _F32),
        jax.ShapeDtypeStruct((bsz, SSM_HEADS, seq), _F32),
        jax.ShapeDtypeStruct((bsz, ATTN_KV_HEADS * V_ROWS, seq), _BF16),
        jax.ShapeDtypeStruct((bsz, seq, KV_WIDTH), _BF16),
        jax.ShapeDtypeStruct((bsz, seq, IDX_DIM), _BF16),
        jax.ShapeDtypeStruct((bsz, seq // CHUNK, SSM_WIDTH, CHUNK), _F32),
        jax.ShapeDtypeStruct((bsz, seq // CHUNK, SSM_WIDTH, CHUNK), _BF16),
        jax.ShapeDtypeStruct((bsz, seq // CHUNK, BC_WIDTH, CHUNK), _BF16),
        jax.ShapeDtypeStruct((bsz, seq, BC_WIDTH // 2), _BF16),
        jax.ShapeDtypeStruct((bsz, seq, N_BRANCH * D_MODEL), _BF16),
        jax.ShapeDtypeStruct((bsz, seq, ATTN_WIDTH), _F32),
    )
    slabs = lambda rows: pl.BlockSpec((1, tm // CHUNK, rows, CHUNK), lambda b, i: (b, i, 0, 0))
    out_specs = (feat(ATTN_WIDTH), feat(IDX_HEADS * IDX_DIM), feat(W_IDX_ROWS), feat(SSM_HEADS),
                 feat(ATTN_KV_HEADS * V_ROWS), tok(KV_WIDTH), tok(IDX_DIM), slabs(SSM_WIDTH), slabs(SSM_WIDTH),
                 slabs(BC_WIDTH), tok(BC_WIDTH // 2), tok(N_BRANCH * D_MODEL), tok(ATTN_WIDTH))
    return pl.pallas_call(
        _proj_kernel,
        grid=(bsz, seq // tm),
        in_specs=in_specs,
        out_specs=out_specs,
        out_shape=out_shape,
        scratch_shapes=[pltpu.VMEM((SSM_WIDTH, LANES), _F32), pltpu.VMEM((BC_WIDTH, LANES), _F32)],
        compiler_params=pltpu.CompilerParams(
            dimension_semantics=("parallel", "arbitrary"), vmem_limit_bytes=VMEM_LIMIT_BYTES),
        name="proj",
    )(x, nw, cos_t, sin_t, *weights, *conv_params)


def _attn_kernel(q_t_ref, qi_t_ref, wi_t_ref, k_ref, kidx_ref, v_t_ref, za_ref, o_ref,
                 score_scr, score16_scr, sel_scr, bias_scr, qpad_scr, m_scr, alpha_scr, acc_scr, s_scr,
                 *, top_k):
    j = pl.program_id(1)
    n_chunks = j + 1
    tq = Q_TILE
    q_pos = j * tq + lax.broadcasted_iota(jnp.int32, (1, tq), 1)
    key_iota = lax.broadcasted_iota(jnp.int32, (KEY_TILE, tq), 0)

    qi_t = qi_t_ref[0]
    qi_cat = jnp.concatenate([qi_t[hd * IDX_DIM:(hd + 1) * IDX_DIM] for hd in range(IDX_HEADS)], axis=1)
    wi_t = wi_t_ref[0]
    wi_cat = jnp.concatenate([wi_t[hd:hd + 1] for hd in range(IDX_HEADS)], axis=1)

    def score_chunk(c, carry):
        start = pl.multiple_of(c * KEY_TILE, KEY_TILE)
        logits = _dot(kidx_ref[0, pl.ds(start, KEY_TILE), :], qi_cat)
        weighted = jnp.maximum(logits, 0.0) * wi_cat
        score = weighted[:, 0:tq]
        for hd in range(1, IDX_HEADS):
            score = score + weighted[:, hd * tq:(hd + 1) * tq]
        score = jnp.where((start + key_iota) <= q_pos, score, -jnp.inf)
        score_scr[pl.ds(start, KEY_TILE), :] = score
        score16_scr[pl.ds(start, KEY_TILE), :] = score.astype(_BF16)
        return carry

    lax.fori_loop(0, n_chunks, score_chunk, 0)

    def code_to_f32(code):
        return pltpu.bitcast(code ^ ((code >> 31) & jnp.int32(0x7FFFFFFF)), _F32)

    def count_ge(ref, cand, dtype):
        def body(c, acc):
            start = pl.multiple_of(c * KEY_TILE, KEY_TILE)
            inc = jnp.where(ref[pl.ds(start, KEY_TILE), :] >= cand, jnp.ones((), dtype), jnp.zeros((), dtype))
            for part in range(KEY_TILE // COUNT_ROWS):
                acc = acc + inc[part * COUNT_ROWS:(part + 1) * COUNT_ROWS]
            return acc
        acc = lax.fori_loop(0, n_chunks, body, jnp.zeros((COUNT_ROWS, tq), dtype))
        return jnp.sum(acc.astype(_F32), axis=0, keepdims=True)

    def bf16_code(block):
        return jnp.where(block >= 0, block, block | jnp.int32(0xFFFF))

    def coarse_bit(i, block):
        cand = block + (jnp.int32(1) << (31 - i))
        hit = count_ge(score16_scr, code_to_f32(bf16_code(cand)).astype(_BF16), _BF16) >= top_k
        return jnp.where(hit, cand, block)

    def fine_bit(i, carry):
        code, n_above = carry
        cand = code + (jnp.int32(1) << (16 - i))
        cnt = count_ge(score_scr, code_to_f32(cand), _F32)
        hit = cnt >= top_k
        return jnp.where(hit, cand, code), jnp.where(hit, n_above, cnt)

    sel_scr[0:1, :] = jnp.full((1, tq), -jnp.inf, _F32)
    sel_scr[1:2, :] = jnp.zeros((1, tq), _F32)

    @pl.when((j + 1) * tq > top_k)
    def _():
        int_min = jnp.int32(-2 ** 31)
        coarse = lax.fori_loop(0, 16, coarse_bit, jnp.full((1, tq), int_min, jnp.int32))
        base = bf16_code(coarse) - jnp.int32(1 << 16)
        fine, n_above = lax.fori_loop(0, 17, fine_bit, (base, jnp.zeros((1, tq), _F32)))
        few = (q_pos + 1) <= top_k
        sel_scr[0:1, :] = jnp.where(few, -jnp.inf, code_to_f32(fine))
        sel_scr[1:2, :] = jnp.where(few, 0.0, top_k - n_above)

    thr = sel_scr[0:1, :]
    n_ties_kept = sel_scr[1:2, :]

    row = lax.broadcasted_iota(jnp.int32, (KEY_TILE, KEY_TILE), 0)
    col = lax.broadcasted_iota(jnp.int32, (KEY_TILE, KEY_TILE), 1)
    strict_lower = jnp.where(col < row, 1.0, 0.0).astype(_BF16)

    def bias_tile(c, ties_before):
        start = pl.multiple_of(c * KEY_TILE, KEY_TILE)
        s = score_scr[pl.ds(start, KEY_TILE), :]
        tie = jnp.where(s == thr, 1.0, 0.0)
        rank = _dot(strict_lower, tie.astype(_BF16)) + ties_before
        keep = (s > thr) | ((s == thr) & (rank < n_ties_kept))
        bias_scr[...] = jnp.where(keep, 0.0, NEG_BIG)
        return ties_before + jnp.sum(tie, axis=0, keepdims=True)

    n_pairs = ATTN_HEADS // 2
    zeros_half = jnp.zeros((HEAD_DIM, 2 * tq), _BF16)
    for pair in range(n_pairs):
        g = (2 * pair) // ATTN_GROUP
        q_cat = jnp.concatenate([q_t_ref[0, (2 * pair + hd) * HEAD_DIM:(2 * pair + hd + 1) * HEAD_DIM, :]
                                 for hd in range(2)], axis=1)
        qpad_scr[pair] = jnp.concatenate([q_cat, zeros_half] if g == 0 else [zeros_half, q_cat], axis=0)
    m_scr[...] = jnp.full(m_scr.shape, NEG_BIG, _F32)
    acc_scr[...] = jnp.zeros(acc_scr.shape, _F32)

    def logits_stage(c, pair):
        start = pl.multiple_of(c * KEY_TILE, KEY_TILE)
        b = bias_scr[...]
        s = _dot(k_ref[0, pl.ds(start, KEY_TILE), :], qpad_scr[pair]) + jnp.concatenate([b, b], axis=1)
        m_old = m_scr[pair]
        m_new = jnp.maximum(m_old, jnp.max(s, axis=0, keepdims=True))
        s_scr[pair] = s
        alpha_scr[pair] = jnp.exp2(m_old - m_new)
        m_scr[pair] = m_new

    def value_stage(c, pair):
        start = pl.multiple_of(c * KEY_TILE, KEY_TILE)
        g = (2 * pair) // ATTN_GROUP
        p = jnp.exp2(s_scr[pair] - m_scr[pair])
        v_c = v_t_ref[0, g * V_ROWS:(g + 1) * V_ROWS, pl.ds(start, KEY_TILE)]
        acc_scr[pair] = alpha_scr[pair] * acc_scr[pair] + _dot(v_c, p.astype(_BF16))

    ties = bias_tile(0, jnp.zeros((1, tq), _F32))
    for pair in range(n_pairs):
        logits_stage(0, pair)

    def attn_tile(c, ties):
        ties = bias_tile(c + 1, ties)
        for pair in range(n_pairs):
            value_stage(c, pair)
            logits_stage(c + 1, pair)
        return ties

    lax.fori_loop(0, n_chunks - 1, attn_tile, ties)
    for pair in range(n_pairs):
        value_stage(n_chunks - 1, pair)

    for pair in range(n_pairs):
        o_t = acc_scr[pair, :HEAD_DIM] / acc_scr[pair, HEAD_DIM:HEAD_DIM + 1]
        both = jnp.concatenate([o_t[:, :tq], o_t[:, tq:]], axis=0)
        z = za_ref[0, :, pair * LANES:(pair + 1) * LANES]
        o_ref[0, :, pair * LANES:(pair + 1) * LANES] = (both.T * _silu(z)).astype(o_ref.dtype)


def _attention(q_t, qi_t, wi_t, k, kidx, v_t, za):
    bsz, seq, _ = k.shape
    tq = Q_TILE
    top_k = min(TOPK_MAX, seq // 4)
    feat = lambda rows: pl.BlockSpec((1, rows, tq), lambda b, j: (b, 0, j))
    full = lambda s1, s2: pl.BlockSpec((1, s1, s2), lambda b, j: (b, 0, 0))
    return pl.pallas_call(
        functools.partial(_attn_kernel, top_k=top_k),
        grid=(bsz, seq // tq),
        in_specs=[feat(ATTN_WIDTH), feat(IDX_HEADS * IDX_DIM), feat(W_IDX_ROWS),
                  full(seq, KV_WIDTH), full(seq, IDX_DIM), full(ATTN_KV_HEADS * V_ROWS, seq),
                  pl.BlockSpec((1, tq, ATTN_WIDTH), lambda b, j: (b, j, 0))],
        out_specs=pl.BlockSpec((1, tq, ATTN_WIDTH), lambda b, j: (b, j, 0)),
        out_shape=jax.ShapeDtypeStruct((bsz, seq, ATTN_WIDTH), _BF16),
        scratch_shapes=[pltpu.VMEM((seq, tq), _F32), pltpu.VMEM((seq, tq), _BF16),
                        pltpu.VMEM((SUBLANES, tq), _F32), pltpu.VMEM((KEY_TILE, tq), _F32),
                        pltpu.VMEM((ATTN_HEADS // 2, 2 * HEAD_DIM, 2 * tq), _BF16),
                        pltpu.VMEM((ATTN_HEADS // 2, 1, 2 * tq), _F32),
                        pltpu.VMEM((ATTN_HEADS // 2, 1, 2 * tq), _F32),
                        pltpu.VMEM((ATTN_HEADS // 2, V_ROWS, 2 * tq), _F32),
                        pltpu.VMEM((ATTN_HEADS // 2, KEY_TILE, 2 * tq), _F32)],
        compiler_params=pltpu.CompilerParams(
            dimension_semantics=("parallel", "arbitrary"), vmem_limit_bytes=VMEM_LIMIT_BYTES),
        name="attn",
    )(q_t, qi_t, wi_t, k, kidx, v_t, za)


def _split3(x):
    hi = x.astype(_BF16)
    r1 = x - hi.astype(_F32)
    mid = r1.astype(_BF16)
    lo = (r1 - mid.astype(_F32)).astype(_BF16)
    return hi, mid, lo


def _ssd_kernel(xs_t_ref, bcs_t_ref, b_tok_ref, dt_ref, zb_t_ref, dtb_ref, alog_ref, dskip_ref, nw_ref, o_t_ref,
                state_scr, y_scr, dt_scr, acum_scr, acum_t_scr, cols_even_scr, cols_odd_scr):
    i = pl.program_id(1)
    r = lax.broadcasted_iota(jnp.int32, (CHUNK, CHUNK), 0)
    c = lax.broadcasted_iota(jnp.int32, (CHUNK, CHUNK), 1)
    causal_t = r <= c

    @pl.when(i == 0)
    def _():
        state_scr[...] = jnp.zeros_like(state_scr)
        upper_b = jnp.where(causal_t, 1.0, 0.0).astype(_BF16)
        dt_in_t = dt_ref[0] + dtb_ref[...]
        dt_t = jnp.maximum(dt_in_t, 0.0) + jnp.log1p(jnp.exp(-jnp.abs(dt_in_t)))
        dt_scr[...] = dt_t
        parts = _split3(dt_t * (-jnp.exp(alog_ref[...])))
        pad = jnp.zeros((LANES - SSM_HEADS, CHUNK), _F32)
        for n in range(dt_t.shape[1] // CHUNK):
            sl = slice(n * CHUNK, (n + 1) * CHUNK)
            a_cum_t = sum(_dot(part[:, sl], upper_b) for part in parts)
            acum_t_scr[:, sl] = a_cum_t
            acum_scr[sl, :] = jnp.concatenate([a_cum_t, pad], axis=0).T

    def spread_columns(chunk, cols_ref):
        a_cum = acum_scr[pl.ds(pl.multiple_of(chunk * CHUNK, CHUNK), CHUNK), :]
        for hh in range(SSM_HEADS):
            cols_ref[:, hh * LANES:(hh + 1) * LANES] = jnp.broadcast_to(a_cum[:, hh:hh + 1], (CHUNK, LANES))

    @pl.when(i == 0)
    def _():
        spread_columns(0, cols_even_scr)

    n_chunks = pl.num_programs(1) * SSD_STEP_CHUNKS

    def chunk_step(ck, cols_ref, next_cols_ref):
        chunk = i * SSD_STEP_CHUNKS + ck
        start = pl.multiple_of(chunk * CHUNK, CHUNK)
        dt_t = dt_scr[:, pl.ds(start, CHUNK)]
        a_cum_t = acum_t_scr[:, pl.ds(start, CHUNK)]
        heads_per_group = SSM_HEADS // SSM_GROUPS

        for grp in range(SSM_GROUPS):
            b_t = bcs_t_ref[0, ck, grp * SSM_STATE:(grp + 1) * SSM_STATE, :]
            c_t = bcs_t_ref[0, ck, BC_WIDTH // 2 + grp * SSM_STATE: BC_WIDTH // 2 + (grp + 1) * SSM_STATE, :]
            b_tok = b_tok_ref[0, ck * CHUNK:(ck + 1) * CHUNK, grp * SSM_STATE:(grp + 1) * SSM_STATE]
            cb_t = _dot(b_tok, c_t)
            cb_t = jnp.where(causal_t, cb_t, 0.0)
            if grp == 0:
                spread_columns(jnp.minimum(chunk + 1, n_chunks - 1), next_cols_ref)

            for hh in range(grp * heads_per_group, (grp + 1) * heads_per_group):
                rows = slice(hh * SSM_HEAD_DIM, (hh + 1) * SSM_HEAD_DIM)
                acum_row = a_cum_t[hh:hh + 1, :]
                last = acum_row[:, CHUNK - 1:CHUNK]
                seg_t = acum_row - cols_ref[:, hh * LANES:(hh + 1) * LANES]
                m_t = (cb_t * jnp.exp(jnp.minimum(seg_t, 0.0))).astype(_BF16)
                x_h = xs_t_ref[0, ck, rows, :].astype(_F32)
                xd = x_h * dt_t[hh:hh + 1, :]
                y = _dot(xd.astype(_BF16), m_t)
                st_prev = state_scr[hh]
                y_off = _dot(st_prev.astype(_BF16), c_t) * jnp.exp(acum_row)
                w = (xd * jnp.exp(last - acum_row)).astype(_BF16)
                st_new = lax.dot_general(w, b_t, _NT, preferred_element_type=_F32)
                state_scr[hh] = jnp.exp(last) * st_prev + st_new
                y_scr[rows, :] = y + y_off + dskip_ref[rows, :] * x_h

        gw = SSM_WIDTH // SSM_GROUPS
        for grp in range(SSM_GROUPS):
            sl = slice(grp * gw, (grp + 1) * gw)
            yz = y_scr[sl, :] * _silu(zb_t_ref[0, ck, sl, :])
            ms = jnp.mean(yz * yz, axis=0, keepdims=True)
            o_t_ref[0, ck, sl, :] = (yz * lax.rsqrt(ms + EPS) * nw_ref[sl, :]).astype(o_t_ref.dtype)

    for ck in range(SSD_STEP_CHUNKS):
        if ck % 2 == 0:
            chunk_step(ck, cols_even_scr, cols_odd_scr)
        else:
            chunk_step(ck, cols_odd_scr, cols_even_scr)


def _ssd(xs_t, bcs_t, b_tok, dt_t, zb_t, dtb, alog, dskip, nw):
    bsz, seq = b_tok.shape[:2]
    nck = SSD_STEP_CHUNKS
    feat = lambda rows: pl.BlockSpec((1, nck, rows, CHUNK), lambda b, i: (b, i, 0, 0))
    consts = (dtb, alog, dskip, nw)
    return pl.pallas_call(
        _ssd_kernel,
        grid=(bsz, seq // (nck * CHUNK)),
        in_specs=[feat(SSM_WIDTH), feat(BC_WIDTH),
                  pl.BlockSpec((1, nck * CHUNK, BC_WIDTH // 2), lambda b, i: (b, i, 0)),
                  pl.BlockSpec((1, SSM_HEADS, seq), lambda b, i: (b, 0, 0)),
                  feat(SSM_WIDTH)] + [_const_spec(a.shape) for a in consts],
        out_specs=feat(SSM_WIDTH),
        out_shape=jax.ShapeDtypeStruct((bsz, seq // CHUNK, SSM_WIDTH, CHUNK), _BF16),
        scratch_shapes=[pltpu.VMEM((SSM_HEADS, SSM_HEAD_DIM, SSM_STATE), _F32),
                        pltpu.VMEM((SSM_WIDTH, CHUNK), _F32),
                        pltpu.VMEM((SSM_HEADS, seq), _F32), pltpu.VMEM((seq, LANES), _F32),
                        pltpu.VMEM((SSM_HEADS, seq), _F32),
                        pltpu.VMEM((CHUNK, SSM_HEADS * LANES), _F32),
                        pltpu.VMEM((CHUNK, SSM_HEADS * LANES), _F32)],
        compiler_params=pltpu.CompilerParams(
            dimension_semantics=("parallel", "arbitrary"), vmem_limit_bytes=VMEM_LIMIT_BYTES),
        name="ssd",
    )(xs_t, bcs_t, b_tok, dt_t, zb_t, *consts)


def _merge_kernel(x_ref, oa_ref, ob_t_ref, g_ref, gb_ref, wa_ref, wb_ref, wo_ref, fw_ref, o_ref, *, final_norm):
    gates = jax.nn.sigmoid(g_ref[0] + gb_ref[...])
    ob_t = jnp.concatenate([ob_t_ref[0, ck] for ck in range(ob_t_ref.shape[1])], axis=1)
    branch_b = lax.dot_general(ob_t, wb_ref[...], _TN, preferred_element_type=_F32)
    merged = gates[:, :D_MODEL] * _dot(oa_ref[0], wa_ref[...]) + gates[:, D_MODEL:] * branch_b
    y = x_ref[0] + _dot(merged.astype(_BF16), wo_ref[...])
    if final_norm:
        y = y * lax.rsqrt(jnp.mean(y * y, axis=-1, keepdims=True) + EPS) * fw_ref[...]
    o_ref[0] = y


def _merge(x, o_a, o_b_t, gates, gate_bias, wa, wb, wo, fw, final_norm):
    bsz, seq, _ = x.shape
    tm = MERGE_ROWS
    tok = lambda width: pl.BlockSpec((1, tm, width), lambda b, i: (b, i, 0))
    consts = (gate_bias, wa, wb, wo, fw)
    return pl.pallas_call(
        functools.partial(_merge_kernel, final_norm=final_norm),
        grid=(bsz, seq // tm),
        in_specs=[tok(D_MODEL), tok(ATTN_WIDTH), pl.BlockSpec((1, tm // CHUNK, SSM_WIDTH, CHUNK), lambda b, i: (b, i, 0, 0)),
                  tok(N_BRANCH * D_MODEL)] + [_const_spec(a.shape) for a in consts],
        out_specs=tok(D_MODEL),
        out_shape=jax.ShapeDtypeStruct((bsz, seq, D_MODEL), _F32),
        compiler_params=pltpu.CompilerParams(
            dimension_semantics=("parallel", "parallel"), vmem_limit_bytes=VMEM_LIMIT_BYTES),
        name="merge",
    )(x, o_a, o_b_t, gates, *consts)


def _cast_kernel(w_ref, o_ref):
    o_ref[...] = w_ref[0].astype(o_ref.dtype)


def _layer_transposed_bf16(w_all, layer):
    w_t_all = jnp.swapaxes(w_all, 1, 2)
    _, rows, cols = w_t_all.shape
    tm = CAST_ROWS
    return pl.pallas_call(
        _cast_kernel,
        grid=(pl.cdiv(rows, tm),),
        in_specs=[pl.BlockSpec((1, tm, cols), lambda i: (layer, i, 0))],
        out_specs=pl.BlockSpec((tm, cols), lambda i: (i, 0)),
        out_shape=jax.ShapeDtypeStruct((rows, cols), _BF16),
        compiler_params=pltpu.CompilerParams(dimension_semantics=("parallel",)),
        name="cast",
    )(w_t_all)


def _layer_weights(w_in_all, layer):
    wt16 = _layer_transposed_bf16(w_in_all, layer)
    o = SPLIT_OFFSETS
    seg = lambda i: wt16[o[i]:o[i + 1]]
    w_q, w_k, w_v, w_za, w_qi, w_ki, w_wi, w_zb, w_xb, w_b, w_c, w_dt, w_g = (seg(i) for i in range(13))
    pad = lambda w, n: jnp.pad(w, ((0, n - w.shape[0]), (0, 0)))
    w_misc = pad(jnp.concatenate([w_ki, pad(w_wi, W_IDX_ROWS), w_dt], axis=0), LANES)
    w_t = jnp.concatenate([w_q, w_qi, w_k, w_misc, w_v], axis=0)
    return w_t, w_zb, w_xb, jnp.concatenate([w_b, w_c], axis=0), w_g.T, w_za.T


def _rope_tables(positions):
    inv_freq = ROPE_THETA ** (-jnp.arange(0, ROT_DIM, 2, dtype=_F32) / ROT_DIM)
    ang = jnp.swapaxes(positions.astype(_F32)[..., None] * inv_freq, 1, 2)
    return jnp.cos(ang), jnp.sin(ang)


def _lane_broadcast(v):
    return jnp.broadcast_to(v[..., None], v.shape + (LANES,))


def kernel(x, positions, norm_w, w_in, gate_bias, conv_w, conv_b, dt_bias, a_log, d_skip,
           ssm_norm_w, w_branch_a, w_branch_b, w_out, final_norm_w):
    depth = norm_w.shape[0]
    cos_t, sin_t = _rope_tables(positions)
    for i in range(depth):
        weights = _layer_weights(w_in, i)
        conv_params = (_lane_broadcast(conv_w[i][:, :SSM_WIDTH]), _lane_broadcast(conv_b[i][:SSM_WIDTH]),
                       _lane_broadcast(conv_w[i][:, SSM_WIDTH:]), _lane_broadcast(conv_b[i][SSM_WIDTH:]))
        (q_t, qi_t, wi_t, dt_t, v_t, k, kidx, zb_t, xs_t, bcs_t, b_tok, gates, za) = _projection(
            x, norm_w[i][None, :], cos_t, sin_t, weights, conv_params)
        o_a = _attention(q_t, qi_t, wi_t, k, kidx, v_t, za)
        o_b_t = _ssd(xs_t, bcs_t, b_tok, dt_t, zb_t, dt_bias[i][:, None], a_log[i][:, None],
                     _lane_broadcast(jnp.repeat(d_skip[i], SSM_HEAD_DIM)), _lane_broadcast(ssm_norm_w[i]))
        x = _merge(x, o_a, o_b_t, gates, gate_bias[i][None, :], w_branch_a[i].astype(_BF16),
                   w_branch_b[i].astype(_BF16), w_out[i].astype(_BF16), final_norm_w[None, :],
                   final_norm=(i == depth - 1))
    return x
```

```python
import functools

import numpy as np
import jax
import jax.numpy as jnp
from jax import lax
from jax.experimental import pallas as pl
from jax.experimental.pallas import tpu as pltpu

D_MODEL = 1024
ATTN_HEADS = 8
ATTN_KV_HEADS = 2
HEAD_DIM = 64
ATTN_GROUP = ATTN_HEADS // ATTN_KV_HEADS
ATTN_WIDTH = ATTN_HEADS * HEAD_DIM
KV_WIDTH = ATTN_KV_HEADS * HEAD_DIM
ROT_DIM = HEAD_DIM // 4
ROT_HALF = ROT_DIM // 2
ROPE_THETA = 500000.0
IDX_HEADS = 4
IDX_DIM = 64
TOPK_MAX = 256
SSM_HEADS = 16
SSM_HEAD_DIM = 64
SSM_WIDTH = SSM_HEADS * SSM_HEAD_DIM
SSM_GROUPS = 4
SSM_STATE = 64
CONV_K = 4
CHUNK = 128
BC_WIDTH = 2 * SSM_GROUPS * SSM_STATE
N_BRANCH = 2
EPS = 1e-6
SPLIT_SIZES = (ATTN_WIDTH, KV_WIDTH, KV_WIDTH, ATTN_WIDTH,
               IDX_HEADS * IDX_DIM, IDX_DIM, IDX_HEADS,
               SSM_WIDTH, SSM_WIDTH, SSM_GROUPS * SSM_STATE, SSM_GROUPS * SSM_STATE, SSM_HEADS,
               N_BRANCH * D_MODEL)
SPLIT_OFFSETS = tuple(int(o) for o in np.cumsum((0,) + SPLIT_SIZES))

LANES = 128
SUBLANES = 8
VMEM_LIMIT_BYTES = 56 * 1024 * 1024

PROJ_ROWS = 512
CONV_ROWS = 256
CAST_ROWS = 512
Q_TILE = 256
KEY_TILE = 256
COUNT_ROWS = 32
LOG2_E = 1.4426950408889634
MERGE_ROWS = 1024
SSD_STEP_CHUNKS = 4
W_IDX_ROWS = 8
V_ROWS = HEAD_DIM + 16
NEG_BIG = -1e30

_F32 = jnp.float32
_BF16 = jnp.bfloat16
_NT = (((1,), (1,)), ((), ()))
_TN = (((0,), (0,)), ((), ()))


def _dot(a, b):
    return jnp.dot(a, b, preferred_element_type=_F32)


def _silu(x):
    return x * jax.nn.sigmoid(x)


def _causal_conv_silu_t(x, tail_ref, w_ref, b_ref):
    cols = x.shape[1]
    reps = cols // LANES
    tail = tail_ref[...]
    lane = lax.broadcasted_iota(jnp.int32, (x.shape[0], LANES), 1)
    wide = lambda a: jnp.concatenate([a] * reps, axis=1)
    y = x * wide(w_ref[CONV_K - 1]) + wide(b_ref[...])
    for shift in range(1, CONV_K):
        rolled = pltpu.roll(x, shift, 1)
        head = jnp.where(lane < shift, pltpu.roll(tail, shift, 1), rolled[:, :LANES])
        shifted = jnp.concatenate([head, rolled[:, LANES:]], axis=1)
        y = y + shifted * wide(w_ref[CONV_K - 1 - shift])
    tail_ref[...] = x[:, cols - LANES:]
    return _silu(y)


def _proj_kernel(x_ref, nw_ref, cos_t_ref, sin_t_ref,
                 w_t_ref, w_zb_ref, w_xb_ref, w_bc_ref, w_g_ref, w_za_ref,
                 cwx_ref, cbx_ref, cwbc_ref, cbbc_ref,
                 q_t_ref, qi_t_ref, wi_t_ref, dt_t_ref, v_t_ref, k_ref, kidx_ref,
                 zb_t_ref, xs_t_ref, bcs_t_ref, b_tok_ref, g_ref, za_ref,
                 tailx_scr, tailbc_scr):
    @pl.when(pl.program_id(1) == 0)
    def _():
        tailx_scr[...] = jnp.zeros_like(tailx_scr)
        tailbc_scr[...] = jnp.zeros_like(tailbc_scr)

    x = x_ref[0]
    h = x * lax.rsqrt(jnp.mean(x * x, axis=-1, keepdims=True) + EPS) * nw_ref[...]
    h = h.astype(_BF16)

    t = lax.dot_general(w_t_ref[...], h, _NT, preferred_element_type=_F32)
    cos_t = cos_t_ref[0]
    sin_t = sin_t_ref[0]

    def rope_head(block, hd):
        x1 = block[hd * HEAD_DIM: hd * HEAD_DIM + ROT_HALF]
        x2 = block[hd * HEAD_DIM + ROT_HALF: hd * HEAD_DIM + ROT_DIM]
        return jnp.concatenate([x1 * cos_t - x2 * sin_t, x2 * cos_t + x1 * sin_t], axis=0)

    def rope_store(block, n_heads, scale, out_ref):
        out_ref[0] = (block * scale).astype(out_ref.dtype)
        for hd in range(n_heads):
            out_ref[0, hd * HEAD_DIM: hd * HEAD_DIM + ROT_DIM, :] = (
                rope_head(block, hd) * scale).astype(out_ref.dtype)

    def rope_value(block, n_heads):
        parts = []
        for hd in range(n_heads):
            parts += [rope_head(block, hd), block[hd * HEAD_DIM + ROT_DIM:(hd + 1) * HEAD_DIM]]
        if block.shape[0] > n_heads * HEAD_DIM:
            parts.append(block[n_heads * HEAD_DIM:])
        return jnp.concatenate(parts, axis=0)

    o_qi = ATTN_WIDTH
    o_k = o_qi + IDX_HEADS * IDX_DIM
    o_ki = o_k + KV_WIDTH
    o_v = o_ki + LANES
    rope_store(t[:o_qi], ATTN_HEADS, LOG2_E * HEAD_DIM ** -0.5, q_t_ref)
    rope_store(t[o_qi:o_k], IDX_HEADS, IDX_DIM ** -0.5, qi_t_ref)
    o_wi = o_ki + IDX_DIM
    o_dt = o_wi + W_IDX_ROWS
    wi_t_ref[0] = t[o_wi:o_dt] * (IDX_HEADS ** -0.5)
    dt_t_ref[0] = t[o_dt:o_dt + SSM_HEADS]
    for g in range(ATTN_KV_HEADS):
        v_t_ref[0, g * V_ROWS:g * V_ROWS + HEAD_DIM, :] = t[o_v + g * HEAD_DIM:o_v + (g + 1) * HEAD_DIM].astype(_BF16)
        v_t_ref[0, g * V_ROWS + HEAD_DIM:(g + 1) * V_ROWS, :] = jnp.ones((V_ROWS - HEAD_DIM, x.shape[0]), _BF16)
    k_ref[0] = rope_value(t[o_k:o_ki], ATTN_KV_HEADS).T.astype(_BF16)
    kidx_ref[0] = rope_value(t[o_ki:o_v], 1).T[:, :IDX_DIM].astype(_BF16)

    def feat_major(w_ref, r0, rows):
        return lax.dot_general(w_ref[r0:r0 + rows], h, _NT, preferred_element_type=_F32)

    def store_chunks(out_ref, sl, value):
        for ck in range(value.shape[1] // CHUNK):
            out_ref[0, ck, sl, :] = value[:, ck * CHUNK:(ck + 1) * CHUNK].astype(out_ref.dtype)

    rc = CONV_ROWS
    for n in range(SSM_WIDTH // rc):
        sl = slice(n * rc, (n + 1) * rc)
        store_chunks(xs_t_ref, sl, _causal_conv_silu_t(
            feat_major(w_xb_ref, n * rc, rc), tailx_scr.at[sl], cwx_ref.at[:, sl], cbx_ref.at[sl]))
        store_chunks(zb_t_ref, sl, feat_major(w_zb_ref, n * rc, rc))
    for n in range(BC_WIDTH // rc):
        sl = slice(n * rc, (n + 1) * rc)
        bc = _causal_conv_silu_t(
            feat_major(w_bc_ref, n * rc, rc), tailbc_scr.at[sl], cwbc_ref.at[:, sl], cbbc_ref.at[sl])
        store_chunks(bcs_t_ref, sl, bc)
        if (n + 1) * rc <= BC_WIDTH // 2:
            b_tok_ref[0, :, sl] = bc.T.astype(_BF16)
    g_ref[0] = _dot(h, w_g_ref[...]).astype(_BF16)
    za_ref[0] = _dot(h, w_za_ref[...])


def _const_spec(shape):
    nd = len(shape)
    return pl.BlockSpec(shape, lambda *_: (0,) * nd, pipeline_mode=pl.Buffered(1))


def _projection(x, nw, cos_t, sin_t, weights, conv_params):
    bsz, seq, _ = x.shape
    tm = PROJ_ROWS
    tok = lambda width: pl.BlockSpec((1, tm, width), lambda b, i: (b, i, 0))
    feat = lambda rows: pl.BlockSpec((1, rows, tm), lambda b, i: (b, 0, i))
    in_specs = [tok(D_MODEL), _const_spec((1, D_MODEL)), feat(ROT_HALF), feat(ROT_HALF)
                ] + [_const_spec(w.shape) for w in weights + conv_params]
    out_shape = (
        jax.ShapeDtypeStruct((bsz, ATTN_WIDTH, seq), _BF16),
        jax.ShapeDtypeStruct((bsz, IDX_HEADS * IDX_DIM, seq), _BF16),
        jax.ShapeDtypeStruct((bsz, W_IDX_ROWS, seq), _F32),
        jax.ShapeDtypeStruct((bsz, SSM_HEADS, seq), _F32),
        jax.ShapeDtypeStruct((bsz, ATTN_KV_HEADS * V_ROWS, seq), _BF16),
        jax.ShapeDtypeStruct((bsz, seq, KV_WIDTH), _BF16),
        jax.ShapeDtypeStruct((bsz, seq, IDX_DIM), _BF16),
        jax.ShapeDtypeStruct((bsz, seq // CHUNK, SSM_WIDTH, CHUNK), _F32),
        jax.ShapeDtypeStruct((bsz, seq // CHUNK, SSM_WIDTH, CHUNK), _BF16),
        jax.ShapeDtypeStruct((bsz, seq // CHUNK, BC_WIDTH, CHUNK), _BF16),
        jax.ShapeDtypeStruct((bsz, seq, BC_WIDTH // 2), _BF16),
        jax.ShapeDtypeStruct((bsz, seq, N_BRANCH * D_MODEL), _BF16),
        jax.ShapeDtypeStruct((bsz, seq, ATTN_WIDTH), _F32),
    )
    slabs = lambda rows: pl.BlockSpec((1, tm // CHUNK, rows, CHUNK), lambda b, i: (b, i, 0, 0))
    out_specs = (feat(ATTN_WIDTH), feat(IDX_HEADS * IDX_DIM), feat(W_IDX_ROWS), feat(SSM_HEADS),
                 feat(ATTN_KV_HEADS * V_ROWS), tok(KV_WIDTH), tok(IDX_DIM), slabs(SSM_WIDTH), slabs(SSM_WIDTH),
                 slabs(BC_WIDTH), tok(BC_WIDTH // 2), tok(N_BRANCH * D_MODEL), tok(ATTN_WIDTH))
    return pl.pallas_call(
        _proj_kernel,
        grid=(bsz, seq // tm),
        in_specs=in_specs,
        out_specs=out_specs,
        out_shape=out_shape,
        scratch_shapes=[pltpu.VMEM((SSM_WIDTH, LANES), _F32), pltpu.VMEM((BC_WIDTH, LANES), _F32)],
        compiler_params=pltpu.CompilerParams(
            dimension_semantics=("parallel", "arbitrary"), vmem_limit_bytes=VMEM_LIMIT_BYTES),
        name="proj",
    )(x, nw, cos_t, sin_t, *weights, *conv_params)


def _attn_kernel(q_t_ref, qi_t_ref, wi_t_ref, k_ref, kidx_ref, v_t_ref, za_ref, o_ref,
                 score_scr, score16_scr, sel_scr, bias_scr, qpad_scr, m_scr, alpha_scr, acc_scr, s_scr,
                 *, top_k):
    j = pl.program_id(1)
    n_chunks = j + 1
    tq = Q_TILE
    q_pos = j * tq + lax.broadcasted_iota(jnp.int32, (1, tq), 1)
    key_iota = lax.broadcasted_iota(jnp.int32, (KEY_TILE, tq), 0)

    qi_t = qi_t_ref[0]
    qi_cat = jnp.concatenate([qi_t[hd * IDX_DIM:(hd + 1) * IDX_DIM] for hd in range(IDX_HEADS)], axis=1)
    wi_t = wi_t_ref[0]
    wi_cat = jnp.concatenate([wi_t[hd:hd + 1] for hd in range(IDX_HEADS)], axis=1)

    def score_chunk(c, carry):
        start = pl.multiple_of(c * KEY_TILE, KEY_TILE)
        logits = _dot(kidx_ref[0, pl.ds(start, KEY_TILE), :], qi_cat)
        weighted = jnp.maximum(logits, 0.0) * wi_cat
        score = weighted[:, 0:tq]
        for hd in range(1, IDX_HEADS):
            score = score + weighted[:, hd * tq:(hd + 1) * tq]
        score = jnp.where((start + key_iota) <= q_pos, score, -jnp.inf)
        score_scr[pl.ds(start, KEY_TILE), :] = score
        score16_scr[pl.ds(start, KEY_TILE), :] = score.astype(_BF16)
        return carry

    lax.fori_loop(0, n_chunks, score_chunk, 0)

    def code_to_f32(code):
        return pltpu.bitcast(code ^ ((code >> 31) & jnp.int32(0x7FFFFFFF)), _F32)

    def count_ge(ref, cand, dtype):
        def body(c, acc):
            start = pl.multiple_of(c * KEY_TILE, KEY_TILE)
            inc = jnp.where(ref[pl.ds(start, KEY_TILE), :] >= cand, jnp.ones((), dtype), jnp.zeros((), dtype))
            for part in range(KEY_TILE // COUNT_ROWS):
                acc = acc + inc[part * COUNT_ROWS:(part + 1) * COUNT_ROWS]
            return acc
        acc = lax.fori_loop(0, n_chunks, body, jnp.zeros((COUNT_ROWS, tq), dtype))
        return jnp.sum(acc.astype(_F32), axis=0, keepdims=True)

    def bf16_code(block):
        return jnp.where(block >= 0, block, block | jnp.int32(0xFFFF))

    def coarse_bit(i, block):
        cand = block + (jnp.int32(1) << (31 - i))
        hit = count_ge(score16_scr, code_to_f32(bf16_code(cand)).astype(_BF16), _BF16) >= top_k
        return jnp.where(hit, cand, block)

    def fine_bit(i, carry):
        code, n_above = carry
        cand = code + (jnp.int32(1) << (16 - i))
        cnt = count_ge(score_scr, code_to_f32(cand), _F32)
        hit = cnt >= top_k
        return jnp.where(hit, cand, code), jnp.where(hit, n_above, cnt)

    sel_scr[0:1, :] = jnp.full((1, tq), -jnp.inf, _F32)
    sel_scr[1:2, :] = jnp.zeros((1, tq), _F32)

    @pl.when((j + 1) * tq > top_k)
    def _():
        int_min = jnp.int32(-2 ** 31)
        coarse = lax.fori_loop(0, 16, coarse_bit, jnp.full((1, tq), int_min, jnp.int32))
        base = bf16_code(coarse) - jnp.int32(1 << 16)
        fine, n_above = lax.fori_loop(0, 17, fine_bit, (base, jnp.zeros((1, tq), _F32)))
        few = (q_pos + 1) <= top_k
        sel_scr[0:1, :] = jnp.where(few, -jnp.inf, code_to_f32(fine))
        sel_scr[1:2, :] = jnp.where(few, 0.0, top_k - n_above)

    thr = sel_scr[0:1, :]
    n_ties_kept = sel_scr[1:2, :]

    row = lax.broadcasted_iota(jnp.int32, (KEY_TILE, KEY_TILE), 0)
    col = lax.broadcasted_iota(jnp.int32, (KEY_TILE, KEY_TILE), 1)
    strict_lower = jnp.where(col < row, 1.0, 0.0).astype(_BF16)

    def bias_tile(c, ties_before):
        start = pl.multiple_of(c * KEY_TILE, KEY_TILE)
        s = score_scr[pl.ds(start, KEY_TILE), :]
        tie = jnp.where(s == thr, 1.0, 0.0)
        rank = _dot(strict_lower, tie.astype(_BF16)) + ties_before
        keep = (s > thr) | ((s == thr) & (rank < n_ties_kept))
        bias_scr[...] = jnp.where(keep, 0.0, NEG_BIG)
        return ties_before + jnp.sum(tie, axis=0, keepdims=True)

    n_pairs = ATTN_HEADS // 2
    zeros_half = jnp.zeros((HEAD_DIM, 2 * tq), _BF16)
    for pair in range(n_pairs):
        g = (2 * pair) // ATTN_GROUP
        q_cat = jnp.concatenate([q_t_ref[0, (2 * pair + hd) * HEAD_DIM:(2 * pair + hd + 1) * HEAD_DIM, :]
                                 for hd in range(2)], axis=1)
        qpad_scr[pair] = jnp.concatenate([q_cat, zeros_half] if g == 0 else [zeros_half, q_cat], axis=0)
    m_scr[...] = jnp.full(m_scr.shape, NEG_BIG, _F32)
    acc_scr[...] = jnp.zeros(acc_scr.shape, _F32)

    def logits_stage(c, pair):
        start = pl.multiple_of(c * KEY_TILE, KEY_TILE)
        b = bias_scr[...]
        s = _dot(k_ref[0, pl.ds(start, KEY_TILE), :], qpad_scr[pair]) + jnp.concatenate([b, b], axis=1)
        m_old = m_scr[pair]
        m_new = jnp.maximum(m_old, jnp.max(s, axis=0, keepdims=True))
        s_scr[pair] = s
        alpha_scr[pair] = jnp.exp2(m_old - m_new)
        m_scr[pair] = m_new

    def value_stage(c, pair):
        start = pl.multiple_of(c * KEY_TILE, KEY_TILE)
        g = (2 * pair) // ATTN_GROUP
        p = jnp.exp2(s_scr[pair] - m_scr[pair])
        v_c = v_t_ref[0, g * V_ROWS:(g + 1) * V_ROWS, pl.ds(start, KEY_TILE)]
        acc_scr[pair] = alpha_scr[pair] * acc_scr[pair] + _dot(v_c, p.astype(_BF16))

    ties = bias_tile(0, jnp.zeros((1, tq), _F32))
    for pair in range(n_pairs):
        logits_stage(0, pair)

    def attn_tile(c, ties):
        ties = bias_tile(c + 1, ties)
        for pair in range(n_pairs):
            value_stage(c, pair)
            logits_stage(c + 1, pair)
        return ties

    lax.fori_loop(0, n_chunks - 1, attn_tile, ties)
    for pair in range(n_pairs):
        value_stage(n_chunks - 1, pair)

    for pair in range(n_pairs):
        o_t = acc_scr[pair, :HEAD_DIM] / acc_scr[pair, HEAD_DIM:HEAD_DIM + 1]
        both = jnp.concatenate([o_t[:, :tq], o_t[:, tq:]], axis=0)
        z = za_ref[0, :, pair * LANES:(pair + 1) * LANES]
        o_ref[0, :, pair * LANES:(pair + 1) * LANES] = (both.T * _silu(z)).astype(o_ref.dtype)


def _attention(q_t, qi_t, wi_t, k, kidx, v_t, za):
    bsz, seq, _ = k.shape
    tq = Q_TILE
    top_k = min(TOPK_MAX, seq // 4)
    feat = lambda rows: pl.BlockSpec((1, rows, tq), lambda b, j: (b, 0, j))
    full = lambda s1, s2: pl.BlockSpec((1, s1, s2), lambda b, j: (b, 0, 0))
    return pl.pallas_call(
        functools.partial(_attn_kernel, top_k=top_k),
        grid=(bsz, seq // tq),
        in_specs=[feat(ATTN_WIDTH), feat(IDX_HEADS * IDX_DIM), feat(W_IDX_ROWS),
                  full(seq, KV_WIDTH), full(seq, IDX_DIM), full(ATTN_KV_HEADS * V_ROWS, seq),
                  pl.BlockSpec((1, tq, ATTN_WIDTH), lambda b, j: (b, j, 0))],
        out_specs=pl.BlockSpec((1, tq, ATTN_WIDTH), lambda b, j: (b, j, 0)),
        out_shape=jax.ShapeDtypeStruct((bsz, seq, ATTN_WIDTH), _BF16),
        scratch_shapes=[pltpu.VMEM((seq, tq), _F32), pltpu.VMEM((seq, tq), _BF16),
                        pltpu.VMEM((SUBLANES, tq), _F32), pltpu.VMEM((KEY_TILE, tq), _F32),
                        pltpu.VMEM((ATTN_HEADS // 2, 2 * HEAD_DIM, 2 * tq), _BF16),
                        pltpu.VMEM((ATTN_HEADS // 2, 1, 2 * tq), _F32),
                        pltpu.VMEM((ATTN_HEADS // 2, 1, 2 * tq), _F32),
                        pltpu.VMEM((ATTN_HEADS // 2, V_ROWS, 2 * tq), _F32),
                        pltpu.VMEM((ATTN_HEADS // 2, KEY_TILE, 2 * tq), _F32)],
        compiler_params=pltpu.CompilerParams(
            dimension_semantics=("parallel", "arbitrary"), vmem_limit_bytes=VMEM_LIMIT_BYTES),
        name="attn",
    )(q_t, qi_t, wi_t, k, kidx, v_t, za)


def _split3(x):
    hi = x.astype(_BF16)
    r1 = x - hi.astype(_F32)
    mid = r1.astype(_BF16)
    lo = (r1 - mid.astype(_F32)).astype(_BF16)
    return hi, mid, lo


def _ssd_kernel(xs_t_ref, bcs_t_ref, b_tok_ref, dt_ref, zb_t_ref, dtb_ref, alog_ref, dskip_ref, nw_ref, o_t_ref,
                state_scr, y_scr, dt_scr, acum_scr, acum_t_scr, cols_even_scr, cols_odd_scr):
    i = pl.program_id(1)
    r = lax.broadcasted_iota(jnp.int32, (CHUNK, CHUNK), 0)
    c = lax.broadcasted_iota(jnp.int32, (CHUNK, CHUNK), 1)
    causal_t = r <= c

    @pl.when(i == 0)
    def _():
        state_scr[...] = jnp.zeros_like(state_scr)
        upper_b = jnp.where(causal_t, 1.0, 0.0).astype(_BF16)
        dt_in_t = dt_ref[0] + dtb_ref[...]
        dt_t = jnp.maximum(dt_in_t, 0.0) + jnp.log1p(jnp.exp(-jnp.abs(dt_in_t)))
        dt_scr[...] = dt_t
        parts = _split3(dt_t * (-jnp.exp(alog_ref[...])))
        pad = jnp.zeros((LANES - SSM_HEADS, CHUNK), _F32)
        for n in range(dt_t.shape[1] // CHUNK):
            sl = slice(n * CHUNK, (n + 1) * CHUNK)
            a_cum_t = sum(_dot(part[:, sl], upper_b) for part in parts)
            acum_t_scr[:, sl] = a_cum_t
            acum_scr[sl, :] = jnp.concatenate([a_cum_t, pad], axis=0).T

    def spread_columns(chunk, cols_ref):
        a_cum = acum_scr[pl.ds(pl.multiple_of(chunk * CHUNK, CHUNK), CHUNK), :]
        for hh in range(SSM_HEADS):
            cols_ref[:, hh * LANES:(hh + 1) * LANES] = jnp.broadcast_to(a_cum[:, hh:hh + 1], (CHUNK, LANES))

    @pl.when(i == 0)
    def _():
        spread_columns(0, cols_even_scr)

    n_chunks = pl.num_programs(1) * SSD_STEP_CHUNKS

    def chunk_step(ck, cols_ref, next_cols_ref):
        chunk = i * SSD_STEP_CHUNKS + ck
        start = pl.multiple_of(chunk * CHUNK, CHUNK)
        dt_t = dt_scr[:, pl.ds(start, CHUNK)]
        a_cum_t = acum_t_scr[:, pl.ds(start, CHUNK)]
        heads_per_group = SSM_HEADS // SSM_GROUPS

        for grp in range(SSM_GROUPS):
            b_t = bcs_t_ref[0, ck, grp * SSM_STATE:(grp + 1) * SSM_STATE, :]
            c_t = bcs_t_ref[0, ck, BC_WIDTH // 2 + grp * SSM_STATE: BC_WIDTH // 2 + (grp + 1) * SSM_STATE, :]
            b_tok = b_tok_ref[0, ck * CHUNK:(ck + 1) * CHUNK, grp * SSM_STATE:(grp + 1) * SSM_STATE]
            cb_t = _dot(b_tok, c_t)
            cb_t = jnp.where(causal_t, cb_t, 0.0)
            if grp == 0:
                spread_columns(jnp.minimum(chunk + 1, n_chunks - 1), next_cols_ref)

            for hh in range(grp * heads_per_group, (grp + 1) * heads_per_group):
                rows = slice(hh * SSM_HEAD_DIM, (hh + 1) * SSM_HEAD_DIM)
                acum_row = a_cum_t[hh:hh + 1, :]
                last = acum_row[:, CHUNK - 1:CHUNK]
                seg_t = acum_row - cols_ref[:, hh * LANES:(hh + 1) * LANES]
                m_t = (cb_t * jnp.exp(jnp.minimum(seg_t, 0.0))).astype(_BF16)
                x_h = xs_t_ref[0, ck, rows, :].astype(_F32)
                xd = x_h * dt_t[hh:hh + 1, :]
                y = _dot(xd.astype(_BF16), m_t)
                st_prev = state_scr[hh]
                y_off = _dot(st_prev.astype(_BF16), c_t) * jnp.exp(acum_row)
                w = (xd * jnp.exp(last - acum_row)).astype(_BF16)
                st_new = lax.dot_general(w, b_t, _NT, preferred_element_type=_F32)
                state_scr[hh] = jnp.exp(last) * st_prev + st_new
                y_scr[rows, :] = y + y_off + dskip_ref[rows, :] * x_h

        gw = SSM_WIDTH // SSM_GROUPS
        for grp in range(SSM_GROUPS):
            sl = slice(grp * gw, (grp + 1) * gw)
            yz = y_scr[sl, :] * _silu(zb_t_ref[0, ck, sl, :])
            ms = jnp.mean(yz * yz, axis=0, keepdims=True)
            o_t_ref[0, ck, sl, :] = (yz * lax.rsqrt(ms + EPS) * nw_ref[sl, :]).astype(o_t_ref.dtype)

    for ck in range(SSD_STEP_CHUNKS):
        if ck % 2 == 0:
            chunk_step(ck, cols_even_scr, cols_odd_scr)
        else:
            chunk_step(ck, cols_odd_scr, cols_even_scr)


def _ssd(xs_t, bcs_t, b_tok, dt_t, zb_t, dtb, alog, dskip, nw):
    bsz, seq = b_tok.shape[:2]
    nck = SSD_STEP_CHUNKS
    feat = lambda rows: pl.BlockSpec((1, nck, rows, CHUNK), lambda b, i: (b, i, 0, 0))
    consts = (dtb, alog, dskip, nw)
    return pl.pallas_call(
        _ssd_kernel,
        grid=(bsz, seq // (nck * CHUNK)),
        in_specs=[feat(SSM_WIDTH), feat(BC_WIDTH),
                  pl.BlockSpec((1, nck * CHUNK, BC_WIDTH // 2), lambda b, i: (b, i, 0)),
                  pl.BlockSpec((1, SSM_HEADS, seq), lambda b, i: (b, 0, 0)),
                  feat(SSM_WIDTH)] + [_const_spec(a.shape) for a in consts],
        out_specs=feat(SSM_WIDTH),
        out_shape=jax.ShapeDtypeStruct((bsz, seq // CHUNK, SSM_WIDTH, CHUNK), _BF16),
        scratch_shapes=[pltpu.VMEM((SSM_HEADS, SSM_HEAD_DIM, SSM_STATE), _F32),
                        pltpu.VMEM((SSM_WIDTH, CHUNK), _F32),
                        pltpu.VMEM((SSM_HEADS, seq), _F32), pltpu.VMEM((seq, LANES), _F32),
                        pltpu.VMEM((SSM_HEADS, seq), _F32),
                        pltpu.VMEM((CHUNK, SSM_HEADS * LANES), _F32),
                        pltpu.VMEM((CHUNK, SSM_HEADS * LANES), _F32)],
        compiler_params=pltpu.CompilerParams(
            dimension_semantics=("parallel", "arbitrary"), vmem_limit_bytes=VMEM_LIMIT_BYTES),
        name="ssd",
    )(xs_t, bcs_t, b_tok, dt_t, zb_t, *consts)


def _merge_kernel(x_ref, oa_ref, ob_t_ref, g_ref, gb_ref, wa_ref, wb_ref, wo_ref, fw_ref, o_ref, *, final_norm):
    gates = jax.nn.sigmoid(g_ref[0] + gb_ref[...])
    ob_t = jnp.concatenate([ob_t_ref[0, ck] for ck in range(ob_t_ref.shape[1])], axis=1)
    branch_b = lax.dot_general(ob_t, wb_ref[...], _TN, preferred_element_type=_F32)
    merged = gates[:, :D_MODEL] * _dot(oa_ref[0], wa_ref[...]) + gates[:, D_MODEL:] * branch_b
    y = x_ref[0] + _dot(merged.astype(_BF16), wo_ref[...])
    if final_norm:
        y = y * lax.rsqrt(jnp.mean(y * y, axis=-1, keepdims=True) + EPS) * fw_ref[...]
    o_ref[0] = y


def _merge(x, o_a, o_b_t, gates, gate_bias, wa, wb, wo, fw, final_norm):
    bsz, seq, _ = x.shape
    tm = MERGE_ROWS
    tok = lambda width: pl.BlockSpec((1, tm, width), lambda b, i: (b, i, 0))
    consts = (gate_bias, wa, wb, wo, fw)
    return pl.pallas_call(
        functools.partial(_merge_kernel, final_norm=final_norm),
        grid=(bsz, seq // tm),
        in_specs=[tok(D_MODEL), tok(ATTN_WIDTH), pl.BlockSpec((1, tm // CHUNK, SSM_WIDTH, CHUNK), lambda b, i: (b, i, 0, 0)),
                  tok(N_BRANCH * D_MODEL)] + [_const_spec(a.shape) for a in consts],
        out_specs=tok(D_MODEL),
        out_shape=jax.ShapeDtypeStruct((bsz, seq, D_MODEL), _F32),
        compiler_params=pltpu.CompilerParams(
            dimension_semantics=("parallel", "parallel"), vmem_limit_bytes=VMEM_LIMIT_BYTES),
        name="merge",
    )(x, o_a, o_b_t, gates, *consts)


def _cast_kernel(w_ref, o_ref):
    o_ref[...] = w_ref[0].astype(o_ref.dtype)


def _layer_transposed_bf16(w_all, layer):
    w_t_all = jnp.swapaxes(w_all, 1, 2)
    _, rows, cols = w_t_all.shape
    tm = CAST_ROWS
    return pl.pallas_call(
        _cast_kernel,
        grid=(pl.cdiv(rows, tm),),
        in_specs=[pl.BlockSpec((1, tm, cols), lambda i: (layer, i, 0))],
        out_specs=pl.BlockSpec((tm, cols), lambda i: (i, 0)),
        out_shape=jax.ShapeDtypeStruct((rows, cols), _BF16),
        compiler_params=pltpu.CompilerParams(dimension_semantics=("parallel",)),
        name="cast",
    )(w_t_all)


def _layer_weights(w_in_all, layer):
    wt16 = _layer_transposed_bf16(w_in_all, layer)
    o = SPLIT_OFFSETS
    seg = lambda i: wt16[o[i]:o[i + 1]]
    w_q, w_k, w_v, w_za, w_qi, w_ki, w_wi, w_zb, w_xb, w_b, w_c, w_dt, w_g = (seg(i) for i in range(13))
    pad = lambda w, n: jnp.pad(w, ((0, n - w.shape[0]), (0, 0)))
    w_misc = pad(jnp.concatenate([w_ki, pad(w_wi, W_IDX_ROWS), w_dt], axis=0), LANES)
    w_t = jnp.concatenate([w_q, w_qi, w_k, w_misc, w_v], axis=0)
    return w_t, w_zb, w_xb, jnp.concatenate([w_b, w_c], axis=0), w_g.T, w_za.T


def _rope_tables(positions):
    inv_freq = ROPE_THETA ** (-jnp.arange(0, ROT_DIM, 2, dtype=_F32) / ROT_DIM)
    ang = jnp.swapaxes(positions.astype(_F32)[..., None] * inv_freq, 1, 2)
    return jnp.cos(ang), jnp.sin(ang)


def _lane_broadcast(v):
    return jnp.broadcast_to(v[..., None], v.shape + (LANES,))


def kernel(x, positions, norm_w, w_in, gate_bias, conv_w, conv_b, dt_bias, a_log, d_skip,
           ssm_norm_w, w_branch_a, w_branch_b, w_out, final_norm_w):
    depth = norm_w.shape[0]
    cos_t, sin_t = _rope_tables(positions)
    for i in range(depth):
        weights = _layer_weights(w_in, i)
        conv_params = (_lane_broadcast(conv_w[i][:, :SSM_WIDTH]), _lane_broadcast(conv_b[i][:SSM_WIDTH]),
                       _lane_broadcast(conv_w[i][:, SSM_WIDTH:]), _lane_broadcast(conv_b[i][SSM_WIDTH:]))
        (q_t, qi_t, wi_t, dt_t, v_t, k, kidx, zb_t, xs_t, bcs_t, b_tok, gates, za) = _projection(
            x, norm_w[i][None, :], cos_t, sin_t, weights, conv_params)
        o_a = _attention(q_t, qi_t, wi_t, k, kidx, v_t, za)
        o_b_t = _ssd(xs_t, bcs_t, b_tok, dt_t, zb_t, dt_bias[i][:, None], a_log[i][:, None],
                     _lane_broadcast(jnp.repeat(d_skip[i], SSM_HEAD_DIM)), _lane_broadcast(ssm_norm_w[i]))
        x = _merge(x, o_a, o_b_t, gates, gate_bias[i][None, :], w_branch_a[i].astype(_BF16),
                   w_branch_b[i].astype(_BF16), w_out[i].astype(_BF16), final_norm_w[None, :],
                   final_norm=(i == depth - 1))
    return x
```

```python
import functools

import numpy as np
import jax
import jax.numpy as jnp
from jax import lax
from jax.experimental import pallas as pl
from jax.experimental.pallas import tpu as pltpu

D_MODEL = 1024
ATTN_HEADS = 8
ATTN_KV_HEADS = 2
HEAD_DIM = 64
ATTN_GROUP = ATTN_HEADS // ATTN_KV_HEADS
ATTN_WIDTH = ATTN_HEADS * HEAD_DIM
KV_WIDTH = ATTN_KV_HEADS * HEAD_DIM
ROT_DIM = HEAD_DIM // 4
ROT_HALF = ROT_DIM // 2
ROPE_THETA = 500000.0
IDX_HEADS = 4
IDX_DIM = 64
TOPK_MAX = 256
SSM_HEADS = 16
SSM_HEAD_DIM = 64
SSM_WIDTH = SSM_HEADS * SSM_HEAD_DIM
SSM_GROUPS = 4
SSM_STATE = 64
CONV_K = 4
CHUNK = 128
BC_WIDTH = 2 * SSM_GROUPS * SSM_STATE
N_BRANCH = 2
EPS = 1e-6
SPLIT_SIZES = (ATTN_WIDTH, KV_WIDTH, KV_WIDTH, ATTN_WIDTH,
               IDX_HEADS * IDX_DIM, IDX_DIM, IDX_HEADS,
               SSM_WIDTH, SSM_WIDTH, SSM_GROUPS * SSM_STATE, SSM_GROUPS * SSM_STATE, SSM_HEADS,
               N_BRANCH * D_MODEL)
SPLIT_OFFSETS = tuple(int(o) for o in np.cumsum((0,) + SPLIT_SIZES))

LANES = 128
SUBLANES = 8
VMEM_LIMIT_BYTES = 56 * 1024 * 1024

PROJ_ROWS = 512
CONV_ROWS = 256
CAST_ROWS = 512
Q_TILE = 256
KEY_TILE = 256
COUNT_ROWS = 32
LOG2_E = 1.4426950408889634
MERGE_ROWS = 512
SSD_STEP_CHUNKS = 4
W_IDX_ROWS = 8
V_ROWS = HEAD_DIM + 16
NEG_BIG = -1e30

_F32 = jnp.float32
_BF16 = jnp.bfloat16
_NT = (((1,), (1,)), ((), ()))
_TN = (((0,), (0,)), ((), ()))


def _dot(a, b):
    return jnp.dot(a, b, preferred_element_type=_F32)


def _silu(x):
    return x * jax.nn.sigmoid(x)


def _causal_conv_silu_t(x, tail_ref, w_ref, b_ref):
    cols = x.shape[1]
    reps = cols // LANES
    tail = tail_ref[...]
    lane = lax.broadcasted_iota(jnp.int32, (x.shape[0], LANES), 1)
    wide = lambda a: jnp.concatenate([a] * reps, axis=1)
    y = x * wide(w_ref[CONV_K - 1]) + wide(b_ref[...])
    for shift in range(1, CONV_K):
        rolled = pltpu.roll(x, shift, 1)
        head = jnp.where(lane < shift, pltpu.roll(tail, shift, 1), rolled[:, :LANES])
        shifted = jnp.concatenate([head, rolled[:, LANES:]], axis=1)
        y = y + shifted * wide(w_ref[CONV_K - 1 - shift])
    tail_ref[...] = x[:, cols - LANES:]
    return _silu(y)


def _proj_kernel(x_ref, nw_ref, cos_t_ref, sin_t_ref,
                 w_t_ref, w_zb_ref, w_xb_ref, w_bc_ref, w_g_ref, w_za_ref,
                 cwx_ref, cbx_ref, cwbc_ref, cbbc_ref,
                 q_t_ref, qi_t_ref, wi_t_ref, dt_t_ref, v_t_ref, k_ref, kidx_ref,
                 zb_t_ref, xs_t_ref, bcs_t_ref, b_tok_ref, g_ref, za_ref,
                 tailx_scr, tailbc_scr):
    @pl.when(pl.program_id(1) == 0)
    def _():
        tailx_scr[...] = jnp.zeros_like(tailx_scr)
        tailbc_scr[...] = jnp.zeros_like(tailbc_scr)

    x = x_ref[0]
    h = x * lax.rsqrt(jnp.mean(x * x, axis=-1, keepdims=True) + EPS) * nw_ref[...]
    h = h.astype(_BF16)

    t = lax.dot_general(w_t_ref[...], h, _NT, preferred_element_type=_F32)
    cos_t = cos_t_ref[0]
    sin_t = sin_t_ref[0]

    def rope_head(block, hd):
        x1 = block[hd * HEAD_DIM: hd * HEAD_DIM + ROT_HALF]
        x2 = block[hd * HEAD_DIM + ROT_HALF: hd * HEAD_DIM + ROT_DIM]
        return jnp.concatenate([x1 * cos_t - x2 * sin_t, x2 * cos_t + x1 * sin_t], axis=0)

    def rope_store(block, n_heads, scale, out_ref):
        out_ref[0] = (block * scale).astype(out_ref.dtype)
        for hd in range(n_heads):
            out_ref[0, hd * HEAD_DIM: hd * HEAD_DIM + ROT_DIM, :] = (
                rope_head(block, hd) * scale).astype(out_ref.dtype)

    def rope_value(block, n_heads):
        parts = []
        for hd in range(n_heads):
            parts += [rope_head(block, hd), block[hd * HEAD_DIM + ROT_DIM:(hd + 1) * HEAD_DIM]]
        if block.shape[0] > n_heads * HEAD_DIM:
            parts.append(block[n_heads * HEAD_DIM:])
        return jnp.concatenate(parts, axis=0)

    o_qi = ATTN_WIDTH
    o_k = o_qi + IDX_HEADS * IDX_DIM
    o_ki = o_k + KV_WIDTH
    o_v = o_ki + LANES
    rope_store(t[:o_qi], ATTN_HEADS, LOG2_E * HEAD_DIM ** -0.5, q_t_ref)
    rope_store(t[o_qi:o_k], IDX_HEADS, IDX_DIM ** -0.5, qi_t_ref)
    o_wi = o_ki + IDX_DIM
    o_dt = o_wi + W_IDX_ROWS
    wi_t_ref[0] = t[o_wi:o_dt] * (IDX_HEADS ** -0.5)
    dt_t_ref[0] = t[o_dt:o_dt + SSM_HEADS]
    for g in range(ATTN_KV_HEADS):
        v_t_ref[0, g * V_ROWS:g * V_ROWS + HEAD_DIM, :] = t[o_v + g * HEAD_DIM:o_v + (g + 1) * HEAD_DIM].astype(_BF16)
        v_t_ref[0, g * V_ROWS + HEAD_DIM:(g + 1) * V_ROWS, :] = jnp.ones((V_ROWS - HEAD_DIM, x.shape[0]), _BF16)
    k_ref[0] = rope_value(t[o_k:o_ki], ATTN_KV_HEADS).T.astype(_BF16)
    kidx_ref[0] = rope_value(t[o_ki:o_v], 1).T[:, :IDX_DIM].astype(_BF16)

    def feat_major(w_ref, r0, rows):
        return lax.dot_general(w_ref[r0:r0 + rows], h, _NT, preferred_element_type=_F32)

    def store_chunks(out_ref, sl, value):
        for ck in range(value.shape[1] // CHUNK):
            out_ref[0, ck, sl, :] = value[:, ck * CHUNK:(ck + 1) * CHUNK].astype(out_ref.dtype)

    rc = CONV_ROWS
    for n in range(SSM_WIDTH // rc):
        sl = slice(n * rc, (n + 1) * rc)
        store_chunks(xs_t_ref, sl, _causal_conv_silu_t(
            feat_major(w_xb_ref, n * rc, rc), tailx_scr.at[sl], cwx_ref.at[:, sl], cbx_ref.at[sl]))
        store_chunks(zb_t_ref, sl, feat_major(w_zb_ref, n * rc, rc))
    for n in range(BC_WIDTH // rc):
        sl = slice(n * rc, (n + 1) * rc)
        bc = _causal_conv_silu_t(
            feat_major(w_bc_ref, n * rc, rc), tailbc_scr.at[sl], cwbc_ref.at[:, sl], cbbc_ref.at[sl])
        store_chunks(bcs_t_ref, sl, bc)
        if (n + 1) * rc <= BC_WIDTH // 2:
            b_tok_ref[0, :, sl] = bc.T.astype(_BF16)
    g_ref[0] = _dot(h, w_g_ref[...]).astype(_BF16)
    za_ref[0] = _dot(h, w_za_ref[...])


def _const_spec(shape):
    nd = len(shape)
    return pl.BlockSpec(shape, lambda *_: (0,) * nd, pipeline_mode=pl.Buffered(1))


def _projection(x, nw, cos_t, sin_t, weights, conv_params):
    bsz, seq, _ = x.shape
    tm = PROJ_ROWS
    tok = lambda width: pl.BlockSpec((1, tm, width), lambda b, i: (b, i, 0))
    feat = lambda rows: pl.BlockSpec((1, rows, tm), lambda b, i: (b, 0, i))
    in_specs = [tok(D_MODEL), _const_spec((1, D_MODEL)), feat(ROT_HALF), feat(ROT_HALF)
                ] + [_const_spec(w.shape) for w in weights + conv_params]
    out_shape = (
        jax.ShapeDtypeStruct((bsz, ATTN_WIDTH, seq), _BF16),
        jax.ShapeDtypeStruct((bsz, IDX_HEADS * IDX_DIM, seq), _BF16),
        jax.ShapeDtypeStruct((bsz, W_IDX_ROWS, seq), _F32),
        jax.ShapeDtypeStruct((bsz, SSM_HEADS, seq), _F32),
        jax.ShapeDtypeStruct((bsz, ATTN_KV_HEADS * V_ROWS, seq), _BF16),
        jax.ShapeDtypeStruct((bsz, seq, KV_WIDTH), _BF16),
        jax.ShapeDtypeStruct((bsz, seq, IDX_DIM), _BF16),
        jax.ShapeDtypeStruct((bsz, seq // CHUNK, SSM_WIDTH, CHUNK), _F32),
        jax.ShapeDtypeStruct((bsz, seq // CHUNK, SSM_WIDTH, CHUNK), _BF16),
        jax.ShapeDtypeStruct((bsz, seq // CHUNK, BC_WIDTH, CHUNK), _BF16),
        jax.ShapeDtypeStruct((bsz, seq, BC_WIDTH // 2), _BF16),
        jax.ShapeDtypeStruct((bsz, seq, N_BRANCH * D_MODEL), _BF16),
        jax.ShapeDtypeStruct((bsz, seq, ATTN_WIDTH), _F32),
    )
    slabs = lambda rows: pl.BlockSpec((1, tm // CHUNK, rows, CHUNK), lambda b, i: (b, i, 0, 0))
    out_specs = (feat(ATTN_WIDTH), feat(IDX_HEADS * IDX_DIM), feat(W_IDX_ROWS), feat(SSM_HEADS),
                 feat(ATTN_KV_HEADS * V_ROWS), tok(KV_WIDTH), tok(IDX_DIM), slabs(SSM_WIDTH), slabs(SSM_WIDTH),
                 slabs(BC_WIDTH), tok(BC_WIDTH // 2), tok(N_BRANCH * D_MODEL), tok(ATTN_WIDTH))
    return pl.pallas_call(
        _proj_kernel,
        grid=(bsz, seq // tm),
        in_specs=in_specs,
        out_specs=out_specs,
        out_shape=out_shape,
        scratch_shapes=[pltpu.VMEM((SSM_WIDTH, LANES), _F32), pltpu.VMEM((BC_WIDTH, LANES), _F32)],
        compiler_params=pltpu.CompilerParams(
            dimension_semantics=("parallel", "arbitrary"), vmem_limit_bytes=VMEM_LIMIT_BYTES),
        name="proj",
    )(x, nw, cos_t, sin_t, *weights, *conv_params)


def _attn_kernel(q_t_ref, qi_t_ref, wi_t_ref, k_ref, kidx_ref, v_t_ref, za_ref, o_ref,
                 score_scr, score16_scr, sel_scr, bias_scr, qpad_scr, m_scr, alpha_scr, acc_scr, s_scr,
                 *, top_k):
    j = pl.program_id(1)
    n_chunks = j + 1
    tq = Q_TILE
    q_pos = j * tq + lax.broadcasted_iota(jnp.int32, (1, tq), 1)
    key_iota = lax.broadcasted_iota(jnp.int32, (KEY_TILE, tq), 0)

    qi_t = qi_t_ref[0]
    qi_cat = jnp.concatenate([qi_t[hd * IDX_DIM:(hd + 1) * IDX_DIM] for hd in range(IDX_HEADS)], axis=1)
    wi_t = wi_t_ref[0]
    wi_cat = jnp.concatenate([wi_t[hd:hd + 1] for hd in range(IDX_HEADS)], axis=1)

    def score_chunk(c, carry):
        start = pl.multiple_of(c * KEY_TILE, KEY_TILE)
        logits = _dot(kidx_ref[0, pl.ds(start, KEY_TILE), :], qi_cat)
        weighted = jnp.maximum(logits, 0.0) * wi_cat
        score = weighted[:, 0:tq]
        for hd in range(1, IDX_HEADS):
            score = score + weighted[:, hd * tq:(hd + 1) * tq]
        score = jnp.where((start + key_iota) <= q_pos, score, -jnp.inf)
        score_scr[pl.ds(start, KEY_TILE), :] = score
        score16_scr[pl.ds(start, KEY_TILE), :] = score.astype(_BF16)
        return carry

    lax.fori_loop(0, n_chunks, score_chunk, 0)

    def code_to_f32(code):
        return pltpu.bitcast(code ^ ((code >> 31) & jnp.int32(0x7FFFFFFF)), _F32)

    def count_ge(ref, cand, dtype):
        def body(c, acc):
            start = pl.multiple_of(c * KEY_TILE, KEY_TILE)
            inc = jnp.where(ref[pl.ds(start, KEY_TILE), :] >= cand, jnp.ones((), dtype), jnp.zeros((), dtype))
            for part in range(KEY_TILE // COUNT_ROWS):
                acc = acc + inc[part * COUNT_ROWS:(part + 1) * COUNT_ROWS]
            return acc
        acc = lax.fori_loop(0, n_chunks, body, jnp.zeros((COUNT_ROWS, tq), dtype))
        return jnp.sum(acc.astype(_F32), axis=0, keepdims=True)

    def bf16_code(block):
        return jnp.where(block >= 0, block, block | jnp.int32(0xFFFF))

    def coarse_bit(i, block):
        cand = block + (jnp.int32(1) << (31 - i))
        hit = count_ge(score16_scr, code_to_f32(bf16_code(cand)).astype(_BF16), _BF16) >= top_k
        return jnp.where(hit, cand, block)

    def fine_bit(i, carry):
        code, n_above = carry
        cand = code + (jnp.int32(1) << (16 - i))
        cnt = count_ge(score_scr, code_to_f32(cand), _F32)
        hit = cnt >= top_k
        return jnp.where(hit, cand, code), jnp.where(hit, n_above, cnt)

    sel_scr[0:1, :] = jnp.full((1, tq), -jnp.inf, _F32)
    sel_scr[1:2, :] = jnp.zeros((1, tq), _F32)

    @pl.when((j + 1) * tq > top_k)
    def _():
        int_min = jnp.int32(-2 ** 31)
        coarse = lax.fori_loop(0, 16, coarse_bit, jnp.full((1, tq), int_min, jnp.int32))
        base = bf16_code(coarse) - jnp.int32(1 << 16)
        fine, n_above = lax.fori_loop(0, 17, fine_bit, (base, jnp.zeros((1, tq), _F32)))
        few = (q_pos + 1) <= top_k
        sel_scr[0:1, :] = jnp.where(few, -jnp.inf, code_to_f32(fine))
        sel_scr[1:2, :] = jnp.where(few, 0.0, top_k - n_above)

    thr = sel_scr[0:1, :]
    n_ties_kept = sel_scr[1:2, :]

    row = lax.broadcasted_iota(jnp.int32, (KEY_TILE, KEY_TILE), 0)
    col = lax.broadcasted_iota(jnp.int32, (KEY_TILE, KEY_TILE), 1)
    strict_lower = jnp.where(col < row, 1.0, 0.0).astype(_BF16)

    def bias_tile(c, ties_before):
        start = pl.multiple_of(c * KEY_TILE, KEY_TILE)
        s = score_scr[pl.ds(start, KEY_TILE), :]
        tie = jnp.where(s == thr, 1.0, 0.0)
        rank = _dot(strict_lower, tie.astype(_BF16)) + ties_before
        keep = (s > thr) | ((s == thr) & (rank < n_ties_kept))
        bias_scr[...] = jnp.where(keep, 0.0, NEG_BIG)
        return ties_before + jnp.sum(tie, axis=0, keepdims=True)

    n_pairs = ATTN_HEADS // 2
    zeros_half = jnp.zeros((HEAD_DIM, 2 * tq), _BF16)
    for pair in range(n_pairs):
        g = (2 * pair) // ATTN_GROUP
        q_cat = jnp.concatenate([q_t_ref[0, (2 * pair + hd) * HEAD_DIM:(2 * pair + hd + 1) * HEAD_DIM, :]
                                 for hd in range(2)], axis=1)
        qpad_scr[pair] = jnp.concatenate([q_cat, zeros_half] if g == 0 else [zeros_half, q_cat], axis=0)
    m_scr[...] = jnp.full(m_scr.shape, NEG_BIG, _F32)
    acc_scr[...] = jnp.zeros(acc_scr.shape, _F32)

    def logits_stage(c, pair):
        start = pl.multiple_of(c * KEY_TILE, KEY_TILE)
        b = bias_scr[...]
        s = _dot(k_ref[0, pl.ds(start, KEY_TILE), :], qpad_scr[pair]) + jnp.concatenate([b, b], axis=1)
        m_old = m_scr[pair]
        m_new = jnp.maximum(m_old, jnp.max(s, axis=0, keepdims=True))
        s_scr[pair] = s
        alpha_scr[pair] = jnp.exp2(m_old - m_new)
        m_scr[pair] = m_new

    def value_stage(c, pair):
        start = pl.multiple_of(c * KEY_TILE, KEY_TILE)
        g = (2 * pair) // ATTN_GROUP
        p = jnp.exp2(s_scr[pair] - m_scr[pair])
        v_c = v_t_ref[0, g * V_ROWS:(g + 1) * V_ROWS, pl.ds(start, KEY_TILE)]
        acc_scr[pair] = alpha_scr[pair] * acc_scr[pair] + _dot(v_c, p.astype(_BF16))

    ties = bias_tile(0, jnp.zeros((1, tq), _F32))
    for pair in range(n_pairs):
        logits_stage(0, pair)

    def attn_tile(c, ties):
        ties = bias_tile(c + 1, ties)
        for pair in range(n_pairs):
            value_stage(c, pair)
            logits_stage(c + 1, pair)
        return ties

    lax.fori_loop(0, n_chunks - 1, attn_tile, ties)
    for pair in range(n_pairs):
        value_stage(n_chunks - 1, pair)

    for pair in range(n_pairs):
        o_t = acc_scr[pair, :HEAD_DIM] / acc_scr[pair, HEAD_DIM:HEAD_DIM + 1]
        both = jnp.concatenate([o_t[:, :tq], o_t[:, tq:]], axis=0)
        z = za_ref[0, :, pair * LANES:(pair + 1) * LANES]
        o_ref[0, :, pair * LANES:(pair + 1) * LANES] = (both.T * _silu(z)).astype(o_ref.dtype)


def _attention(q_t, qi_t, wi_t, k, kidx, v_t, za):
    bsz, seq, _ = k.shape
    tq = Q_TILE
    top_k = min(TOPK_MAX, seq // 4)
    feat = lambda rows: pl.BlockSpec((1, rows, tq), lambda b, j: (b, 0, j))
    full = lambda s1, s2: pl.BlockSpec((1, s1, s2), lambda b, j: (b, 0, 0))
    return pl.pallas_call(
        functools.partial(_attn_kernel, top_k=top_k),
        grid=(bsz, seq // tq),
        in_specs=[feat(ATTN_WIDTH), feat(IDX_HEADS * IDX_DIM), feat(W_IDX_ROWS),
                  full(seq, KV_WIDTH), full(seq, IDX_DIM), full(ATTN_KV_HEADS * V_ROWS, seq),
                  pl.BlockSpec((1, tq, ATTN_WIDTH), lambda b, j: (b, j, 0))],
        out_specs=pl.BlockSpec((1, tq, ATTN_WIDTH), lambda b, j: (b, j, 0)),
        out_shape=jax.ShapeDtypeStruct((bsz, seq, ATTN_WIDTH), _BF16),
        scratch_shapes=[pltpu.VMEM((seq, tq), _F32), pltpu.VMEM((seq, tq), _BF16),
                        pltpu.VMEM((SUBLANES, tq), _F32), pltpu.VMEM((KEY_TILE, tq), _F32),
                        pltpu.VMEM((ATTN_HEADS // 2, 2 * HEAD_DIM, 2 * tq), _BF16),
                        pltpu.VMEM((ATTN_HEADS // 2, 1, 2 * tq), _F32),
                        pltpu.VMEM((ATTN_HEADS // 2, 1, 2 * tq), _F32),
                        pltpu.VMEM((ATTN_HEADS // 2, V_ROWS, 2 * tq), _F32),
                        pltpu.VMEM((ATTN_HEADS // 2, KEY_TILE, 2 * tq), _F32)],
        compiler_params=pltpu.CompilerParams(
            dimension_semantics=("parallel", "arbitrary"), vmem_limit_bytes=VMEM_LIMIT_BYTES),
        name="attn",
    )(q_t, qi_t, wi_t, k, kidx, v_t, za)


def _split3(x):
    hi = x.astype(_BF16)
    r1 = x - hi.astype(_F32)
    mid = r1.astype(_BF16)
    lo = (r1 - mid.astype(_F32)).astype(_BF16)
    return hi, mid, lo


def _ssd_kernel(xs_t_ref, bcs_t_ref, b_tok_ref, dt_ref, zb_t_ref, dtb_ref, alog_ref, dskip_ref, nw_ref, o_t_ref,
                state_scr, y_scr, dt_scr, acum_scr, acum_t_scr, cols_even_scr, cols_odd_scr):
    i = pl.program_id(1)
    r = lax.broadcasted_iota(jnp.int32, (CHUNK, CHUNK), 0)
    c = lax.broadcasted_iota(jnp.int32, (CHUNK, CHUNK), 1)
    causal_t = r <= c

    @pl.when(i == 0)
    def _():
        state_scr[...] = jnp.zeros_like(state_scr)
        upper_b = jnp.where(causal_t, 1.0, 0.0).astype(_BF16)
        dt_in_t = dt_ref[0] + dtb_ref[...]
        dt_t = jnp.maximum(dt_in_t, 0.0) + jnp.log1p(jnp.exp(-jnp.abs(dt_in_t)))
        dt_scr[...] = dt_t
        parts = _split3(dt_t * (-jnp.exp(alog_ref[...])))
        pad = jnp.zeros((LANES - SSM_HEADS, CHUNK), _F32)
        for n in range(dt_t.shape[1] // CHUNK):
            sl = slice(n * CHUNK, (n + 1) * CHUNK)
            a_cum_t = sum(_dot(part[:, sl], upper_b) for part in parts)
            acum_t_scr[:, sl] = a_cum_t
            acum_scr[sl, :] = jnp.concatenate([a_cum_t, pad], axis=0).T

    def spread_columns(chunk, cols_ref):
        a_cum = acum_scr[pl.ds(pl.multiple_of(chunk * CHUNK, CHUNK), CHUNK), :]
        for hh in range(SSM_HEADS):
            cols_ref[:, hh * LANES:(hh + 1) * LANES] = jnp.broadcast_to(a_cum[:, hh:hh + 1], (CHUNK, LANES))

    @pl.when(i == 0)
    def _():
        spread_columns(0, cols_even_scr)

    n_chunks = pl.num_programs(1) * SSD_STEP_CHUNKS

    def chunk_step(ck, cols_ref, next_cols_ref):
        chunk = i * SSD_STEP_CHUNKS + ck
        start = pl.multiple_of(chunk * CHUNK, CHUNK)
        dt_t = dt_scr[:, pl.ds(start, CHUNK)]
        a_cum_t = acum_t_scr[:, pl.ds(start, CHUNK)]
        heads_per_group = SSM_HEADS // SSM_GROUPS

        for grp in range(SSM_GROUPS):
            b_t = bcs_t_ref[0, ck, grp * SSM_STATE:(grp + 1) * SSM_STATE, :]
            c_t = bcs_t_ref[0, ck, BC_WIDTH // 2 + grp * SSM_STATE: BC_WIDTH // 2 + (grp + 1) * SSM_STATE, :]
            b_tok = b_tok_ref[0, ck * CHUNK:(ck + 1) * CHUNK, grp * SSM_STATE:(grp + 1) * SSM_STATE]
            cb_t = _dot(b_tok, c_t)
            cb_t = jnp.where(causal_t, cb_t, 0.0)
            if grp == 0:
                spread_columns(jnp.minimum(chunk + 1, n_chunks - 1), next_cols_ref)

            for hh in range(grp * heads_per_group, (grp + 1) * heads_per_group):
                rows = slice(hh * SSM_HEAD_DIM, (hh + 1) * SSM_HEAD_DIM)
                acum_row = a_cum_t[hh:hh + 1, :]
                last = acum_row[:, CHUNK - 1:CHUNK]
                seg_t = acum_row - cols_ref[:, hh * LANES:(hh + 1) * LANES]
                m_t = (cb_t * jnp.exp(jnp.minimum(seg_t, 0.0))).astype(_BF16)
                x_h = xs_t_ref[0, ck, rows, :].astype(_F32)
                xd = x_h * dt_t[hh:hh + 1, :]
                y = _dot(xd.astype(_BF16), m_t)
                st_prev = state_scr[hh]
                y_off = _dot(st_prev.astype(_BF16), c_t) * jnp.exp(acum_row)
                w = (xd * jnp.exp(last - acum_row)).astype(_BF16)
                st_new = lax.dot_general(w, b_t, _NT, preferred_element_type=_F32)
                state_scr[hh] = jnp.exp(last) * st_prev + st_new
                y_scr[rows, :] = y + y_off + dskip_ref[rows, :] * x_h

        gw = SSM_WIDTH // SSM_GROUPS
        for grp in range(SSM_GROUPS):
            sl = slice(grp * gw, (grp + 1) * gw)
            yz = y_scr[sl, :] * _silu(zb_t_ref[0, ck, sl, :])
            ms = jnp.mean(yz * yz, axis=0, keepdims=True)
            o_t_ref[0, ck, sl, :] = (yz * lax.rsqrt(ms + EPS) * nw_ref[sl, :]).astype(o_t_ref.dtype)

    for ck in range(SSD_STEP_CHUNKS):
        if ck % 2 == 0:
            chunk_step(ck, cols_even_scr, cols_odd_scr)
        else:
            chunk_step(ck, cols_odd_scr, cols_even_scr)


def _ssd(xs_t, bcs_t, b_tok, dt_t, zb_t, dtb, alog, dskip, nw):
    bsz, seq = b_tok.shape[:2]
    nck = SSD_STEP_CHUNKS
    feat = lambda rows: pl.BlockSpec((1, nck, rows, CHUNK), lambda b, i: (b, i, 0, 0))
    consts = (dtb, alog, dskip, nw)
    return pl.pallas_call(
        _ssd_kernel,
        grid=(bsz, seq // (nck * CHUNK)),
        in_specs=[feat(SSM_WIDTH), feat(BC_WIDTH),
                  pl.BlockSpec((1, nck * CHUNK, BC_WIDTH // 2), lambda b, i: (b, i, 0)),
                  pl.BlockSpec((1, SSM_HEADS, seq), lambda b, i: (b, 0, 0)),
                  feat(SSM_WIDTH)] + [_const_spec(a.shape) for a in consts],
        out_specs=feat(SSM_WIDTH),
        out_shape=jax.ShapeDtypeStruct((bsz, seq // CHUNK, SSM_WIDTH, CHUNK), _BF16),
        scratch_shapes=[pltpu.VMEM((SSM_HEADS, SSM_HEAD_DIM, SSM_STATE), _F32),
                        pltpu.VMEM((SSM_WIDTH, CHUNK), _F32),
                        pltpu.VMEM((SSM_HEADS, seq), _F32), pltpu.VMEM((seq, LANES), _F32),
                        pltpu.VMEM((SSM_HEADS, seq), _F32),
                        pltpu.VMEM((CHUNK, SSM_HEADS * LANES), _F32),
                        pltpu.VMEM((CHUNK, SSM_HEADS * LANES), _F32)],
        compiler_params=pltpu.CompilerParams(
            dimension_semantics=("parallel", "arbitrary"), vmem_limit_bytes=VMEM_LIMIT_BYTES),
        name="ssd",
    )(xs_t, bcs_t, b_tok, dt_t, zb_t, *consts)


def _merge_kernel(x_ref, oa_ref, ob_t_ref, g_ref, gb_ref, wa_ref, wb_ref, wo_ref, fw_ref, o_ref, *, final_norm):
    gates = jax.nn.sigmoid(g_ref[0] + gb_ref[...])
    ob_t = jnp.concatenate([ob_t_ref[0, ck] for ck in range(ob_t_ref.shape[1])], axis=1)
    branch_b = lax.dot_general(ob_t, wb_ref[...], _TN, preferred_element_type=_F32)
    merged = gates[:, :D_MODEL] * _dot(oa_ref[0], wa_ref[...]) + gates[:, D_MODEL:] * branch_b
    y = x_ref[0] + _dot(merged.astype(_BF16), wo_ref[...])
    if final_norm:
        y = y * lax.rsqrt(jnp.mean(y * y, axis=-1, keepdims=True) + EPS) * fw_ref[...]
    o_ref[0] = y


def _merge(x, o_a, o_b_t, gates, gate_bias, wa, wb, wo, fw, final_norm):
    bsz, seq, _ = x.shape
    tm = MERGE_ROWS
    tok = lambda width: pl.BlockSpec((1, tm, width), lambda b, i: (b, i, 0))
    consts = (gate_bias, wa, wb, wo, fw)
    return pl.pallas_call(
        functools.partial(_merge_kernel, final_norm=final_norm),
        grid=(bsz, seq // tm),
        in_specs=[tok(D_MODEL), tok(ATTN_WIDTH), pl.BlockSpec((1, tm // CHUNK, SSM_WIDTH, CHUNK), lambda b, i: (b, i, 0, 0)),
                  tok(N_BRANCH * D_MODEL)] + [_const_spec(a.shape) for a in consts],
        out_specs=tok(D_MODEL),
        out_shape=jax.ShapeDtypeStruct((bsz, seq, D_MODEL), _F32),
        compiler_params=pltpu.CompilerParams(
            dimension_semantics=("parallel", "parallel"), vmem_limit_bytes=VMEM_LIMIT_BYTES),
        name="merge",
    )(x, o_a, o_b_t, gates, *consts)


def _cast_kernel(w_ref, o_ref):
    segs = o_ref.shape[1] // LANES
    for k in range(segs):
        o_ref[:, k * LANES:(k + 1) * LANES] = w_ref[0, pl.ds(k, o_ref.shape[0], stride=segs), :].astype(o_ref.dtype)


def _layer_transposed_bf16(w_all, layer):
    depth, cols, rows = w_all.shape
    segs = cols // LANES
    w_rows = jnp.swapaxes(w_all, 1, 2).reshape(depth, rows * segs, LANES)
    tm = CAST_ROWS
    return pl.pallas_call(
        _cast_kernel,
        grid=(pl.cdiv(rows, tm),),
        in_specs=[pl.BlockSpec((1, tm * segs, LANES), lambda i: (layer, i, 0))],
        out_specs=pl.BlockSpec((tm, cols), lambda i: (i, 0)),
        out_shape=jax.ShapeDtypeStruct((rows, cols), _BF16),
        compiler_params=pltpu.CompilerParams(dimension_semantics=("parallel",)),
        name="cast",
    )(w_rows)


def _layer_weights(w_in_all, layer):
    wt16 = _layer_transposed_bf16(w_in_all, layer)
    o = SPLIT_OFFSETS
    seg = lambda i: wt16[o[i]:o[i + 1]]
    w_q, w_k, w_v, w_za, w_qi, w_ki, w_wi, w_zb, w_xb, w_b, w_c, w_dt, w_g = (seg(i) for i in range(13))
    pad = lambda w, n: jnp.pad(w, ((0, n - w.shape[0]), (0, 0)))
    w_misc = pad(jnp.concatenate([w_ki, pad(w_wi, W_IDX_ROWS), w_dt], axis=0), LANES)
    w_t = jnp.concatenate([w_q, w_qi, w_k, w_misc, w_v], axis=0)
    return w_t, w_zb, w_xb, jnp.concatenate([w_b, w_c], axis=0), w_g.T, w_za.T


def _rope_tables(positions):
    inv_freq = ROPE_THETA ** (-jnp.arange(0, ROT_DIM, 2, dtype=_F32) / ROT_DIM)
    ang = jnp.swapaxes(positions.astype(_F32)[..., None] * inv_freq, 1, 2)
    return jnp.cos(ang), jnp.sin(ang)


def _lane_broadcast(v):
    return jnp.broadcast_to(v[..., None], v.shape + (LANES,))


def kernel(x, positions, norm_w, w_in, gate_bias, conv_w, conv_b, dt_bias, a_log, d_skip,
           ssm_norm_w, w_branch_a, w_branch_b, w_out, final_norm_w):
    depth = norm_w.shape[0]
    cos_t, sin_t = _rope_tables(positions)
    for i in range(depth):
        weights = _layer_weights(w_in, i)
        conv_params = (_lane_broadcast(conv_w[i][:, :SSM_WIDTH]), _lane_broadcast(conv_b[i][:SSM_WIDTH]),
                       _lane_broadcast(conv_w[i][:, SSM_WIDTH:]), _lane_broadcast(conv_b[i][SSM_WIDTH:]))
        (q_t, qi_t, wi_t, dt_t, v_t, k, kidx, zb_t, xs_t, bcs_t, b_tok, gates, za) = _projection(
            x, norm_w[i][None, :], cos_t, sin_t, weights, conv_params)
        o_a = _attention(q_t, qi_t, wi_t, k, kidx, v_t, za)
        o_b_t = _ssd(xs_t, bcs_t, b_tok, dt_t, zb_t, dt_bias[i][:, None], a_log[i][:, None],
                     _lane_broadcast(jnp.repeat(d_skip[i], SSM_HEAD_DIM)), _lane_broadcast(ssm_norm_w[i]))
        x = _merge(x, o_a, o_b_t, gates, gate_bias[i][None, :], w_branch_a[i].astype(_BF16),
                   w_branch_b[i].astype(_BF16), w_out[i].astype(_BF16), final_norm_w[None, :],
                   final_norm=(i == depth - 1))
    return x
```

```python
import functools

import numpy as np
import jax
import jax.numpy as jnp
from jax import lax
from jax.experimental import pallas as pl
from jax.experimental.pallas import tpu as pltpu

D_MODEL = 1024
ATTN_HEADS = 8
ATTN_KV_HEADS = 2
HEAD_DIM = 64
ATTN_GROUP = ATTN_HEADS // ATTN_KV_HEADS
ATTN_WIDTH = ATTN_HEADS * HEAD_DIM
KV_WIDTH = ATTN_KV_HEADS * HEAD_DIM
ROT_DIM = HEAD_DIM // 4
ROT_HALF = ROT_DIM // 2
ROPE_THETA = 500000.0
IDX_HEADS = 4
IDX_DIM = 64
TOPK_MAX = 256
SSM_HEADS = 16
SSM_HEAD_DIM = 64
SSM_WIDTH = SSM_HEADS * SSM_HEAD_DIM
SSM_GROUPS = 4
SSM_STATE = 64
CONV_K = 4
CHUNK = 128
BC_WIDTH = 2 * SSM_GROUPS * SSM_STATE
N_BRANCH = 2
EPS = 1e-6
SPLIT_SIZES = (ATTN_WIDTH, KV_WIDTH, KV_WIDTH, ATTN_WIDTH,
               IDX_HEADS * IDX_DIM, IDX_DIM, IDX_HEADS,
               SSM_WIDTH, SSM_WIDTH, SSM_GROUPS * SSM_STATE, SSM_GROUPS * SSM_STATE, SSM_HEADS,
               N_BRANCH * D_MODEL)
SPLIT_OFFSETS = tuple(int(o) for o in np.cumsum((0,) + SPLIT_SIZES))

LANES = 128
SUBLANES = 8
VMEM_LIMIT_BYTES = 56 * 1024 * 1024

PROJ_ROWS = 512
CONV_ROWS = 256
CAST_ROWS = 512
Q_TILE = 256
KEY_TILE = 256
COUNT_ROWS = 32
LOG2_E = 1.4426950408889634
MERGE_ROWS = 512
SSD_STEP_CHUNKS = 4
W_IDX_ROWS = 8
V_ROWS = HEAD_DIM + 16
NEG_BIG = -1e30

_F32 = jnp.float32
_BF16 = jnp.bfloat16
_NT = (((1,), (1,)), ((), ()))
_TN = (((0,), (0,)), ((), ()))


def _dot(a, b):
    return jnp.dot(a, b, preferred_element_type=_F32)


def _silu(x):
    return x * jax.nn.sigmoid(x)


def _causal_conv_silu_t(x, tail_ref, w_ref, b_ref):
    cols = x.shape[1]
    reps = cols // LANES
    tail = tail_ref[...]
    lane = lax.broadcasted_iota(jnp.int32, (x.shape[0], LANES), 1)
    wide = lambda a: jnp.concatenate([a] * reps, axis=1)
    y = x * wide(w_ref[CONV_K - 1]) + wide(b_ref[...])
    for shift in range(1, CONV_K):
        rolled = pltpu.roll(x, shift, 1)
        head = jnp.where(lane < shift, pltpu.roll(tail, shift, 1), rolled[:, :LANES])
        shifted = jnp.concatenate([head, rolled[:, LANES:]], axis=1)
        y = y + shifted * wide(w_ref[CONV_K - 1 - shift])
    tail_ref[...] = x[:, cols - LANES:]
    return _silu(y)


def _proj_kernel(x_ref, nw_ref, cos_t_ref, sin_t_ref,
                 w_t_ref, w_zb_ref, w_xb_ref, w_bc_ref, w_g_ref, w_za_ref,
                 cwx_ref, cbx_ref, cwbc_ref, cbbc_ref,
                 q_t_ref, qi_t_ref, wi_t_ref, dt_t_ref, v_t_ref, k_ref, kidx_ref,
                 zb_t_ref, xs_t_ref, bcs_t_ref, b_tok_ref, g_ref, za_ref,
                 tailx_scr, tailbc_scr):
    @pl.when(pl.program_id(1) == 0)
    def _():
        tailx_scr[...] = jnp.zeros_like(tailx_scr)
        tailbc_scr[...] = jnp.zeros_like(tailbc_scr)

    x = x_ref[0]
    h = x * lax.rsqrt(jnp.mean(x * x, axis=-1, keepdims=True) + EPS) * nw_ref[...]
    h = h.astype(_BF16)

    t = lax.dot_general(w_t_ref[...], h, _NT, preferred_element_type=_F32)
    cos_t = cos_t_ref[0]
    sin_t = sin_t_ref[0]

    def rope_head(block, hd):
        x1 = block[hd * HEAD_DIM: hd * HEAD_DIM + ROT_HALF]
        x2 = block[hd * HEAD_DIM + ROT_HALF: hd * HEAD_DIM + ROT_DIM]
        return jnp.concatenate([x1 * cos_t - x2 * sin_t, x2 * cos_t + x1 * sin_t], axis=0)

    def rope_store(block, n_heads, scale, out_ref):
        out_ref[0] = (block * scale).astype(out_ref.dtype)
        for hd in range(n_heads):
            out_ref[0, hd * HEAD_DIM: hd * HEAD_DIM + ROT_DIM, :] = (
                rope_head(block, hd) * scale).astype(out_ref.dtype)

    def rope_value(block, n_heads):
        parts = []
        for hd in range(n_heads):
            parts += [rope_head(block, hd), block[hd * HEAD_DIM + ROT_DIM:(hd + 1) * HEAD_DIM]]
        if block.shape[0] > n_heads * HEAD_DIM:
            parts.append(block[n_heads * HEAD_DIM:])
        return jnp.concatenate(parts, axis=0)

    o_qi = ATTN_WIDTH
    o_k = o_qi + IDX_HEADS * IDX_DIM
    o_ki = o_k + KV_WIDTH
    o_v = o_ki + LANES
    rope_store(t[:o_qi], ATTN_HEADS, LOG2_E * HEAD_DIM ** -0.5, q_t_ref)
    rope_store(t[o_qi:o_k], IDX_HEADS, IDX_DIM ** -0.5, qi_t_ref)
    o_wi = o_ki + IDX_DIM
    o_dt = o_wi + W_IDX_ROWS
    wi_t_ref[0] = t[o_wi:o_dt] * (IDX_HEADS ** -0.5)
    dt_t_ref[0] = t[o_dt:o_dt + SSM_HEADS]
    for g in range(ATTN_KV_HEADS):
        v_t_ref[0, g * V_ROWS:g * V_ROWS + HEAD_DIM, :] = t[o_v + g * HEAD_DIM:o_v + (g + 1) * HEAD_DIM].astype(_BF16)
        v_t_ref[0, g * V_ROWS + HEAD_DIM:(g + 1) * V_ROWS, :] = jnp.ones((V_ROWS - HEAD_DIM, x.shape[0]), _BF16)
    k_ref[0] = rope_value(t[o_k:o_ki], ATTN_KV_HEADS).T.astype(_BF16)
    kidx_ref[0] = rope_value(t[o_ki:o_v], 1).T[:, :IDX_DIM].astype(_BF16)

    def feat_major(w_ref, r0, rows):
        return lax.dot_general(w_ref[r0:r0 + rows], h, _NT, preferred_element_type=_F32)

    def store_chunks(out_ref, sl, value):
        for ck in range(value.shape[1] // CHUNK):
            out_ref[0, ck, sl, :] = value[:, ck * CHUNK:(ck + 1) * CHUNK].astype(out_ref.dtype)

    rc = CONV_ROWS
    for n in range(SSM_WIDTH // rc):
        sl = slice(n * rc, (n + 1) * rc)
        store_chunks(xs_t_ref, sl, _causal_conv_silu_t(
            feat_major(w_xb_ref, n * rc, rc), tailx_scr.at[sl], cwx_ref.at[:, sl], cbx_ref.at[sl]))
        store_chunks(zb_t_ref, sl, feat_major(w_zb_ref, n * rc, rc))
    for n in range(BC_WIDTH // rc):
        sl = slice(n * rc, (n + 1) * rc)
        bc = _causal_conv_silu_t(
            feat_major(w_bc_ref, n * rc, rc), tailbc_scr.at[sl], cwbc_ref.at[:, sl], cbbc_ref.at[sl])
        store_chunks(bcs_t_ref, sl, bc)
        if (n + 1) * rc <= BC_WIDTH // 2:
            b_tok_ref[0, :, sl] = bc.T.astype(_BF16)
    g_ref[0] = _dot(h, w_g_ref[...]).astype(_BF16)
    za_ref[0] = _dot(h, w_za_ref[...])


def _const_spec(shape):
    nd = len(shape)
    return pl.BlockSpec(shape, lambda *_: (0,) * nd, pipeline_mode=pl.Buffered(1))


def _projection(x, nw, cos_t, sin_t, weights, conv_params):
    bsz, seq, _ = x.shape
    tm = PROJ_ROWS
    tok = lambda width: pl.BlockSpec((1, tm, width), lambda b, i: (b, i, 0))
    feat = lambda rows: pl.BlockSpec((1, rows, tm), lambda b, i: (b, 0, i))
    in_specs = [tok(D_MODEL), _const_spec((1, D_MODEL)), feat(ROT_HALF), feat(ROT_HALF)
                ] + [_const_spec(w.shape) for w in weights + conv_params]
    out_shape = (
        jax.ShapeDtypeStruct((bsz, ATTN_WIDTH, seq), _BF16),
        jax.ShapeDtypeStruct((bsz, IDX_HEADS * IDX_DIM, seq), _BF16),
        jax.ShapeDtypeStruct((bsz, W_IDX_ROWS, seq), _F32),
        jax.ShapeDtypeStruct((bsz, SSM_HEADS, seq), _F32),
        jax.ShapeDtypeStruct((bsz, ATTN_KV_HEADS * V_ROWS, seq), _BF16),
        jax.ShapeDtypeStruct((bsz, seq, KV_WIDTH), _BF16),
        jax.ShapeDtypeStruct((bsz, seq, IDX_DIM), _BF16),
        jax.ShapeDtypeStruct((bsz, seq // CHUNK, SSM_WIDTH, CHUNK), _F32),
        jax.ShapeDtypeStruct((bsz, seq // CHUNK, SSM_WIDTH, CHUNK), _BF16),
        jax.ShapeDtypeStruct((bsz, seq // CHUNK, BC_WIDTH, CHUNK), _BF16),
        jax.ShapeDtypeStruct((bsz, seq, BC_WIDTH // 2), _BF16),
        jax.ShapeDtypeStruct((bsz, seq, N_BRANCH * D_MODEL), _BF16),
        jax.ShapeDtypeStruct((bsz, seq, ATTN_WIDTH), _F32),
    )
    slabs = lambda rows: pl.BlockSpec((1, tm // CHUNK, rows, CHUNK), lambda b, i: (b, i, 0, 0))
    out_specs = (feat(ATTN_WIDTH), feat(IDX_HEADS * IDX_DIM), feat(W_IDX_ROWS), feat(SSM_HEADS),
                 feat(ATTN_KV_HEADS * V_ROWS), tok(KV_WIDTH), tok(IDX_DIM), slabs(SSM_WIDTH), slabs(SSM_WIDTH),
                 slabs(BC_WIDTH), tok(BC_WIDTH // 2), tok(N_BRANCH * D_MODEL), tok(ATTN_WIDTH))
    return pl.pallas_call(
        _proj_kernel,
        grid=(bsz, seq // tm),
        in_specs=in_specs,
        out_specs=out_specs,
        out_shape=out_shape,
        scratch_shapes=[pltpu.VMEM((SSM_WIDTH, LANES), _F32), pltpu.VMEM((BC_WIDTH, LANES), _F32)],
        compiler_params=pltpu.CompilerParams(
            dimension_semantics=("parallel", "arbitrary"), vmem_limit_bytes=VMEM_LIMIT_BYTES),
        name="proj",
    )(x, nw, cos_t, sin_t, *weights, *conv_params)


def _attn_kernel(q_t_ref, qi_t_ref, wi_t_ref, k_ref, kidx_ref, v_t_ref, za_ref, o_ref,
                 score_scr, score16_scr, sel_scr, bias_scr, qpad_scr, m_scr, alpha_scr, acc_scr, s_scr,
                 *, top_k):
    j = pl.program_id(1)
    n_chunks = j + 1
    tq = Q_TILE
    q_pos = j * tq + lax.broadcasted_iota(jnp.int32, (1, tq), 1)
    key_iota = lax.broadcasted_iota(jnp.int32, (KEY_TILE, tq), 0)

    qi_t = qi_t_ref[0]
    qi_cat = jnp.concatenate([qi_t[hd * IDX_DIM:(hd + 1) * IDX_DIM] for hd in range(IDX_HEADS)], axis=1)
    wi_t = wi_t_ref[0]
    wi_cat = jnp.concatenate([wi_t[hd:hd + 1] for hd in range(IDX_HEADS)], axis=1)

    def score_chunk(c, carry):
        start = pl.multiple_of(c * KEY_TILE, KEY_TILE)
        logits = _dot(kidx_ref[0, pl.ds(start, KEY_TILE), :], qi_cat)
        weighted = jnp.maximum(logits, 0.0) * wi_cat
        score = weighted[:, 0:tq]
        for hd in range(1, IDX_HEADS):
            score = score + weighted[:, hd * tq:(hd + 1) * tq]
        score = jnp.where((start + key_iota) <= q_pos, score, -jnp.inf)
        score_scr[pl.ds(start, KEY_TILE), :] = score
        score16_scr[pl.ds(start, KEY_TILE), :] = score.astype(_BF16)
        return carry

    lax.fori_loop(0, n_chunks, score_chunk, 0)

    def code_to_f32(code):
        return pltpu.bitcast(code ^ ((code >> 31) & jnp.int32(0x7FFFFFFF)), _F32)

    def count_ge(ref, cand, dtype):
        def add_tile(c, acc):
            start = pl.multiple_of(c * KEY_TILE, KEY_TILE)
            inc = jnp.where(ref[pl.ds(start, KEY_TILE), :] >= cand, jnp.ones((), dtype), jnp.zeros((), dtype))
            for part in range(KEY_TILE // COUNT_ROWS):
                acc = acc + inc[part * COUNT_ROWS:(part + 1) * COUNT_ROWS]
            return acc

        acc = lax.fori_loop(0, n_chunks // 2, lambda c, acc: add_tile(2 * c + 1, add_tile(2 * c, acc)),
                            jnp.zeros((COUNT_ROWS, tq), dtype))
        acc = lax.fori_loop(0, n_chunks % 2, lambda c, acc: add_tile(n_chunks - 1, acc), acc)
        return jnp.sum(acc.astype(_F32), axis=0, keepdims=True)

    def bf16_code(block):
        return jnp.where(block >= 0, block, block | jnp.int32(0xFFFF))

    def coarse_bit(i, block):
        cand = block + (jnp.int32(1) << (31 - i))
        hit = count_ge(score16_scr, code_to_f32(bf16_code(cand)).astype(_BF16), _BF16) >= top_k
        return jnp.where(hit, cand, block)

    def fine_bit(i, carry):
        code, n_above = carry
        cand = code + (jnp.int32(1) << (16 - i))
        cnt = count_ge(score_scr, code_to_f32(cand), _F32)
        hit = cnt >= top_k
        return jnp.where(hit, cand, code), jnp.where(hit, n_above, cnt)

    sel_scr[0:1, :] = jnp.full((1, tq), -jnp.inf, _F32)
    sel_scr[1:2, :] = jnp.zeros((1, tq), _F32)

    @pl.when((j + 1) * tq > top_k)
    def _():
        int_min = jnp.int32(-2 ** 31)
        coarse = lax.fori_loop(0, 16, coarse_bit, jnp.full((1, tq), int_min, jnp.int32))
        base = bf16_code(coarse) - jnp.int32(1 << 16)
        fine, n_above = lax.fori_loop(0, 17, fine_bit, (base, jnp.zeros((1, tq), _F32)))
        few = (q_pos + 1) <= top_k
        sel_scr[0:1, :] = jnp.where(few, -jnp.inf, code_to_f32(fine))
        sel_scr[1:2, :] = jnp.where(few, 0.0, top_k - n_above)

    thr = sel_scr[0:1, :]
    n_ties_kept = sel_scr[1:2, :]

    row = lax.broadcasted_iota(jnp.int32, (KEY_TILE, KEY_TILE), 0)
    col = lax.broadcasted_iota(jnp.int32, (KEY_TILE, KEY_TILE), 1)
    strict_lower = jnp.where(col < row, 1.0, 0.0).astype(_BF16)

    def bias_tile(c, ties_before):
        start = pl.multiple_of(c * KEY_TILE, KEY_TILE)
        s = score_scr[pl.ds(start, KEY_TILE), :]
        tie = jnp.where(s == thr, 1.0, 0.0)
        rank = _dot(strict_lower, tie.astype(_BF16)) + ties_before
        keep = (s > thr) | ((s == thr) & (rank < n_ties_kept))
        bias_scr[...] = jnp.where(keep, 0.0, NEG_BIG)
        return ties_before + jnp.sum(tie, axis=0, keepdims=True)

    n_pairs = ATTN_HEADS // 2
    zeros_half = jnp.zeros((HEAD_DIM, 2 * tq), _BF16)
    for pair in range(n_pairs):
        g = (2 * pair) // ATTN_GROUP
        q_cat = jnp.concatenate([q_t_ref[0, (2 * pair + hd) * HEAD_DIM:(2 * pair + hd + 1) * HEAD_DIM, :]
                                 for hd in range(2)], axis=1)
        qpad_scr[pair] = jnp.concatenate([q_cat, zeros_half] if g == 0 else [zeros_half, q_cat], axis=0)
    m_scr[...] = jnp.full(m_scr.shape, NEG_BIG, _F32)
    acc_scr[...] = jnp.zeros(acc_scr.shape, _F32)

    def logits_stage(c, pair):
        start = pl.multiple_of(c * KEY_TILE, KEY_TILE)
        b = bias_scr[...]
        s = _dot(k_ref[0, pl.ds(start, KEY_TILE), :], qpad_scr[pair]) + jnp.concatenate([b, b], axis=1)
        m_old = m_scr[pair]
        m_new = jnp.maximum(m_old, jnp.max(s, axis=0, keepdims=True))
        s_scr[pair] = s
        alpha_scr[pair] = jnp.exp2(m_old - m_new)
        m_scr[pair] = m_new

    def value_stage(c, pair):
        start = pl.multiple_of(c * KEY_TILE, KEY_TILE)
        g = (2 * pair) // ATTN_GROUP
        p = jnp.exp2(s_scr[pair] - m_scr[pair])
        v_c = v_t_ref[0, g * V_ROWS:(g + 1) * V_ROWS, pl.ds(start, KEY_TILE)]
        acc_scr[pair] = alpha_scr[pair] * acc_scr[pair] + _dot(v_c, p.astype(_BF16))

    ties = bias_tile(0, jnp.zeros((1, tq), _F32))
    for pair in range(n_pairs):
        logits_stage(0, pair)

    def attn_tile(c, ties):
        ties = bias_tile(c + 1, ties)
        for pair in range(n_pairs):
            value_stage(c, pair)
            logits_stage(c + 1, pair)
        return ties

    lax.fori_loop(0, n_chunks - 1, attn_tile, ties)
    for pair in range(n_pairs):
        value_stage(n_chunks - 1, pair)

    for pair in range(n_pairs):
        o_t = acc_scr[pair, :HEAD_DIM] / acc_scr[pair, HEAD_DIM:HEAD_DIM + 1]
        both = jnp.concatenate([o_t[:, :tq], o_t[:, tq:]], axis=0)
        z = za_ref[0, :, pair * LANES:(pair + 1) * LANES]
        o_ref[0, :, pair * LANES:(pair + 1) * LANES] = (both.T * _silu(z)).astype(o_ref.dtype)


def _attention(q_t, qi_t, wi_t, k, kidx, v_t, za):
    bsz, seq, _ = k.shape
    tq = Q_TILE
    top_k = min(TOPK_MAX, seq // 4)
    feat = lambda rows: pl.BlockSpec((1, rows, tq), lambda b, j: (b, 0, j))
    full = lambda s1, s2: pl.BlockSpec((1, s1, s2), lambda b, j: (b, 0, 0))
    return pl.pallas_call(
        functools.partial(_attn_kernel, top_k=top_k),
        grid=(bsz, seq // tq),
        in_specs=[feat(ATTN_WIDTH), feat(IDX_HEADS * IDX_DIM), feat(W_IDX_ROWS),
                  full(seq, KV_WIDTH), full(seq, IDX_DIM), full(ATTN_KV_HEADS * V_ROWS, seq),
                  pl.BlockSpec((1, tq, ATTN_WIDTH), lambda b, j: (b, j, 0))],
        out_specs=pl.BlockSpec((1, tq, ATTN_WIDTH), lambda b, j: (b, j, 0)),
        out_shape=jax.ShapeDtypeStruct((bsz, seq, ATTN_WIDTH), _BF16),
        scratch_shapes=[pltpu.VMEM((seq, tq), _F32), pltpu.VMEM((seq, tq), _BF16),
                        pltpu.VMEM((SUBLANES, tq), _F32), pltpu.VMEM((KEY_TILE, tq), _F32),
                        pltpu.VMEM((ATTN_HEADS // 2, 2 * HEAD_DIM, 2 * tq), _BF16),
                        pltpu.VMEM((ATTN_HEADS // 2, 1, 2 * tq), _F32),
                        pltpu.VMEM((ATTN_HEADS // 2, 1, 2 * tq), _F32),
                        pltpu.VMEM((ATTN_HEADS // 2, V_ROWS, 2 * tq), _F32),
                        pltpu.VMEM((ATTN_HEADS // 2, KEY_TILE, 2 * tq), _F32)],
        compiler_params=pltpu.CompilerParams(
            dimension_semantics=("parallel", "arbitrary"), vmem_limit_bytes=VMEM_LIMIT_BYTES),
        name="attn",
    )(q_t, qi_t, wi_t, k, kidx, v_t, za)


def _split3(x):
    hi = x.astype(_BF16)
    r1 = x - hi.astype(_F32)
    mid = r1.astype(_BF16)
    lo = (r1 - mid.astype(_F32)).astype(_BF16)
    return hi, mid, lo


def _ssd_kernel(xs_t_ref, bcs_t_ref, b_tok_ref, dt_ref, zb_t_ref, dtb_ref, alog_ref, dskip_ref, nw_ref, o_t_ref,
                state_scr, y_scr, dt_scr, acum_scr, acum_t_scr, cols_even_scr, cols_odd_scr):
    i = pl.program_id(1)
    r = lax.broadcasted_iota(jnp.int32, (CHUNK, CHUNK), 0)
    c = lax.broadcasted_iota(jnp.int32, (CHUNK, CHUNK), 1)
    causal_t = r <= c

    @pl.when(i == 0)
    def _():
        state_scr[...] = jnp.zeros_like(state_scr)
        upper_b = jnp.where(causal_t, 1.0, 0.0).astype(_BF16)
        dt_in_t = dt_ref[0] + dtb_ref[...]
        dt_t = jnp.maximum(dt_in_t, 0.0) + jnp.log1p(jnp.exp(-jnp.abs(dt_in_t)))
        dt_scr[...] = dt_t
        parts = _split3(dt_t * (-jnp.exp(alog_ref[...])))
        pad = jnp.zeros((LANES - SSM_HEADS, CHUNK), _F32)
        for n in range(dt_t.shape[1] // CHUNK):
            sl = slice(n * CHUNK, (n + 1) * CHUNK)
            a_cum_t = sum(_dot(part[:, sl], upper_b) for part in parts)
            acum_t_scr[:, sl] = a_cum_t
            acum_scr[sl, :] = jnp.concatenate([a_cum_t, pad], axis=0).T

    def spread_columns(chunk, cols_ref):
        a_cum = acum_scr[pl.ds(pl.multiple_of(chunk * CHUNK, CHUNK), CHUNK), :]
        for hh in range(SSM_HEADS):
            cols_ref[:, hh * LANES:(hh + 1) * LANES] = jnp.broadcast_to(a_cum[:, hh:hh + 1], (CHUNK, LANES))

    @pl.when(i == 0)
    def _():
        spread_columns(0, cols_even_scr)

    n_chunks = pl.num_programs(1) * SSD_STEP_CHUNKS

    def chunk_step(ck, cols_ref, next_cols_ref):
        chunk = i * SSD_STEP_CHUNKS + ck
        start = pl.multiple_of(chunk * CHUNK, CHUNK)
        dt_t = dt_scr[:, pl.ds(start, CHUNK)]
        a_cum_t = acum_t_scr[:, pl.ds(start, CHUNK)]
        heads_per_group = SSM_HEADS // SSM_GROUPS

        for grp in range(SSM_GROUPS):
            b_t = bcs_t_ref[0, ck, grp * SSM_STATE:(grp + 1) * SSM_STATE, :]
            c_t = bcs_t_ref[0, ck, BC_WIDTH // 2 + grp * SSM_STATE: BC_WIDTH // 2 + (grp + 1) * SSM_STATE, :]
            b_tok = b_tok_ref[0, ck * CHUNK:(ck + 1) * CHUNK, grp * SSM_STATE:(grp + 1) * SSM_STATE]
            cb_t = _dot(b_tok, c_t)
            cb_t = jnp.where(causal_t, cb_t, 0.0)
            if grp == 0:
                spread_columns(jnp.minimum(chunk + 1, n_chunks - 1), next_cols_ref)

            for hh in range(grp * heads_per_group, (grp + 1) * heads_per_group):
                rows = slice(hh * SSM_HEAD_DIM, (hh + 1) * SSM_HEAD_DIM)
                acum_row = a_cum_t[hh:hh + 1, :]
                last = acum_row[:, CHUNK - 1:CHUNK]
                seg_t = acum_row - cols_ref[:, hh * LANES:(hh + 1) * LANES]
                m_t = (cb_t * jnp.exp(jnp.minimum(seg_t, 0.0))).astype(_BF16)
                x_h = xs_t_ref[0, ck, rows, :].astype(_F32)
                xd = x_h * dt_t[hh:hh + 1, :]
                y = _dot(xd.astype(_BF16), m_t)
                st_prev = state_scr[hh]
                y_off = _dot(st_prev.astype(_BF16), c_t) * jnp.exp(acum_row)
                w = (xd * jnp.exp(last - acum_row)).astype(_BF16)
                st_new = lax.dot_general(w, b_t, _NT, preferred_element_type=_F32)
                state_scr[hh] = jnp.exp(last) * st_prev + st_new
                y_scr[rows, :] = y + y_off + dskip_ref[rows, :] * x_h

        gw = SSM_WIDTH // SSM_GROUPS
        for grp in range(SSM_GROUPS):
            sl = slice(grp * gw, (grp + 1) * gw)
            yz = y_scr[sl, :] * _silu(zb_t_ref[0, ck, sl, :])
            ms = jnp.mean(yz * yz, axis=0, keepdims=True)
            o_t_ref[0, ck, sl, :] = (yz * lax.rsqrt(ms + EPS) * nw_ref[sl, :]).astype(o_t_ref.dtype)

    for ck in range(SSD_STEP_CHUNKS):
        if ck % 2 == 0:
            chunk_step(ck, cols_even_scr, cols_odd_scr)
        else:
            chunk_step(ck, cols_odd_scr, cols_even_scr)


def _ssd(xs_t, bcs_t, b_tok, dt_t, zb_t, dtb, alog, dskip, nw):
    bsz, seq = b_tok.shape[:2]
    nck = SSD_STEP_CHUNKS
    feat = lambda rows: pl.BlockSpec((1, nck, rows, CHUNK), lambda b, i: (b, i, 0, 0))
    consts = (dtb, alog, dskip, nw)
    return pl.pallas_call(
        _ssd_kernel,
        grid=(bsz, seq // (nck * CHUNK)),
        in_specs=[feat(SSM_WIDTH), feat(BC_WIDTH),
                  pl.BlockSpec((1, nck * CHUNK, BC_WIDTH // 2), lambda b, i: (b, i, 0)),
                  pl.BlockSpec((1, SSM_HEADS, seq), lambda b, i: (b, 0, 0)),
                  feat(SSM_WIDTH)] + [_const_spec(a.shape) for a in consts],
        out_specs=feat(SSM_WIDTH),
        out_shape=jax.ShapeDtypeStruct((bsz, seq // CHUNK, SSM_WIDTH, CHUNK), _BF16),
        scratch_shapes=[pltpu.VMEM((SSM_HEADS, SSM_HEAD_DIM, SSM_STATE), _F32),
                        pltpu.VMEM((SSM_WIDTH, CHUNK), _F32),
                        pltpu.VMEM((SSM_HEADS, seq), _F32), pltpu.VMEM((seq, LANES), _F32),
                        pltpu.VMEM((SSM_HEADS, seq), _F32),
                        pltpu.VMEM((CHUNK, SSM_HEADS * LANES), _F32),
                        pltpu.VMEM((CHUNK, SSM_HEADS * LANES), _F32)],
        compiler_params=pltpu.CompilerParams(
            dimension_semantics=("parallel", "arbitrary"), vmem_limit_bytes=VMEM_LIMIT_BYTES),
        name="ssd",
    )(xs_t, bcs_t, b_tok, dt_t, zb_t, *consts)


def _merge_kernel(x_ref, oa_ref, ob_t_ref, g_ref, gb_ref, wa_ref, wb_ref, wo_ref, fw_ref, o_ref, *, final_norm):
    gates = jax.nn.sigmoid(g_ref[0] + gb_ref[...])
    ob_t = jnp.concatenate([ob_t_ref[0, ck] for ck in range(ob_t_ref.shape[1])], axis=1)
    branch_b = lax.dot_general(ob_t, wb_ref[...], _TN, preferred_element_type=_F32)
    merged = gates[:, :D_MODEL] * _dot(oa_ref[0], wa_ref[...]) + gates[:, D_MODEL:] * branch_b
    y = x_ref[0] + _dot(merged.astype(_BF16), wo_ref[...])
    if final_norm:
        y = y * lax.rsqrt(jnp.mean(y * y, axis=-1, keepdims=True) + EPS) * fw_ref[...]
    o_ref[0] = y


def _merge(x, o_a, o_b_t, gates, gate_bias, wa, wb, wo, fw, final_norm):
    bsz, seq, _ = x.shape
    tm = MERGE_ROWS
    tok = lambda width: pl.BlockSpec((1, tm, width), lambda b, i: (b, i, 0))
    consts = (gate_bias, wa, wb, wo, fw)
    return pl.pallas_call(
        functools.partial(_merge_kernel, final_norm=final_norm),
        grid=(bsz, seq // tm),
        in_specs=[tok(D_MODEL), tok(ATTN_WIDTH), pl.BlockSpec((1, tm // CHUNK, SSM_WIDTH, CHUNK), lambda b, i: (b, i, 0, 0)),
                  tok(N_BRANCH * D_MODEL)] + [_const_spec(a.shape) for a in consts],
        out_specs=tok(D_MODEL),
        out_shape=jax.ShapeDtypeStruct((bsz, seq, D_MODEL), _F32),
        compiler_params=pltpu.CompilerParams(
            dimension_semantics=("parallel", "parallel"), vmem_limit_bytes=VMEM_LIMIT_BYTES),
        name="merge",
    )(x, o_a, o_b_t, gates, *consts)


def _cast_kernel(w_ref, o_ref):
    segs = o_ref.shape[1] // LANES
    for k in range(segs):
        o_ref[:, k * LANES:(k + 1) * LANES] = w_ref[0, pl.ds(k, o_ref.shape[0], stride=segs), :].astype(o_ref.dtype)


def _layer_transposed_bf16(w_all, layer):
    depth, cols, rows = w_all.shape
    segs = cols // LANES
    w_rows = jnp.swapaxes(w_all, 1, 2).reshape(depth, rows * segs, LANES)
    tm = CAST_ROWS
    return pl.pallas_call(
        _cast_kernel,
        grid=(pl.cdiv(rows, tm),),
        in_specs=[pl.BlockSpec((1, tm * segs, LANES), lambda i: (layer, i, 0))],
        out_specs=pl.BlockSpec((tm, cols), lambda i: (i, 0)),
        out_shape=jax.ShapeDtypeStruct((rows, cols), _BF16),
        compiler_params=pltpu.CompilerParams(dimension_semantics=("parallel",)),
        name="cast",
    )(w_rows)


def _layer_weights(w_in_all, layer):
    wt16 = _layer_transposed_bf16(w_in_all, layer)
    o = SPLIT_OFFSETS
    seg = lambda i: wt16[o[i]:o[i + 1]]
    w_q, w_k, w_v, w_za, w_qi, w_ki, w_wi, w_zb, w_xb, w_b, w_c, w_dt, w_g = (seg(i) for i in range(13))
    pad = lambda w, n: jnp.pad(w, ((0, n - w.shape[0]), (0, 0)))
    w_misc = pad(jnp.concatenate([w_ki, pad(w_wi, W_IDX_ROWS), w_dt], axis=0), LANES)
    w_t = jnp.concatenate([w_q, w_qi, w_k, w_misc, w_v], axis=0)
    return w_t, w_zb, w_xb, jnp.concatenate([w_b, w_c], axis=0), w_g.T, w_za.T


def _rope_tables(positions):
    inv_freq = ROPE_THETA ** (-jnp.arange(0, ROT_DIM, 2, dtype=_F32) / ROT_DIM)
    ang = jnp.swapaxes(positions.astype(_F32)[..., None] * inv_freq, 1, 2)
    return jnp.cos(ang), jnp.sin(ang)


def _lane_broadcast(v):
    return jnp.broadcast_to(v[..., None], v.shape + (LANES,))


def kernel(x, positions, norm_w, w_in, gate_bias, conv_w, conv_b, dt_bias, a_log, d_skip,
           ssm_norm_w, w_branch_a, w_branch_b, w_out, final_norm_w):
    depth = norm_w.shape[0]
    cos_t, sin_t = _rope_tables(positions)
    for i in range(depth):
        weights = _layer_weights(w_in, i)
        conv_params = (_lane_broadcast(conv_w[i][:, :SSM_WIDTH]), _lane_broadcast(conv_b[i][:SSM_WIDTH]),
                       _lane_broadcast(conv_w[i][:, SSM_WIDTH:]), _lane_broadcast(conv_b[i][SSM_WIDTH:]))
        (q_t, qi_t, wi_t, dt_t, v_t, k, kidx, zb_t, xs_t, bcs_t, b_tok, gates, za) = _projection(
            x, norm_w[i][None, :], cos_t, sin_t, weights, conv_params)
        o_a = _attention(q_t, qi_t, wi_t, k, kidx, v_t, za)
        o_b_t = _ssd(xs_t, bcs_t, b_tok, dt_t, zb_t, dt_bias[i][:, None], a_log[i][:, None],
                     _lane_broadcast(jnp.repeat(d_skip[i], SSM_HEAD_DIM)), _lane_broadcast(ssm_norm_w[i]))
        x = _merge(x, o_a, o_b_t, gates, gate_bias[i][None, :], w_branch_a[i].astype(_BF16),
                   w_branch_b[i].astype(_BF16), w_out[i].astype(_BF16), final_norm_w[None, :],
                   final_norm=(i == depth - 1))
    return x
```

```python
import functools

import numpy as np
import jax
import jax.numpy as jnp
from jax import lax
from jax.experimental import pallas as pl
from jax.experimental.pallas import tpu as pltpu

D_MODEL = 1024
ATTN_HEADS = 8
ATTN_KV_HEADS = 2
HEAD_DIM = 64
ATTN_GROUP = ATTN_HEADS // ATTN_KV_HEADS
ATTN_WIDTH = ATTN_HEADS * HEAD_DIM
KV_WIDTH = ATTN_KV_HEADS * HEAD_DIM
ROT_DIM = HEAD_DIM // 4
ROT_HALF = ROT_DIM // 2
ROPE_THETA = 500000.0
IDX_HEADS = 4
IDX_DIM = 64
TOPK_MAX = 256
SSM_HEADS = 16
SSM_HEAD_DIM = 64
SSM_WIDTH = SSM_HEADS * SSM_HEAD_DIM
SSM_GROUPS = 4
SSM_STATE = 64
CONV_K = 4
CHUNK = 128
BC_WIDTH = 2 * SSM_GROUPS * SSM_STATE
N_BRANCH = 2
EPS = 1e-6
SPLIT_SIZES = (ATTN_WIDTH, KV_WIDTH, KV_WIDTH, ATTN_WIDTH,
               IDX_HEADS * IDX_DIM, IDX_DIM, IDX_HEADS,
               SSM_WIDTH, SSM_WIDTH, SSM_GROUPS * SSM_STATE, SSM_GROUPS * SSM_STATE, SSM_HEADS,
               N_BRANCH * D_MODEL)
SPLIT_OFFSETS = tuple(int(o) for o in np.cumsum((0,) + SPLIT_SIZES))

LANES = 128
SUBLANES = 8
VMEM_LIMIT_BYTES = 56 * 1024 * 1024

PROJ_ROWS = 512
CONV_ROWS = 256
CAST_ROWS = 512
ATTN_SEQS = 2
Q_TILE = 256
KEY_TILE = 256
COUNT_ROWS = 32
LOG2_E = 1.4426950408889634
MERGE_ROWS = 512
SSD_STEP_CHUNKS = 4
W_IDX_ROWS = 8
V_ROWS = HEAD_DIM + 16
NEG_BIG = -1e30

_F32 = jnp.float32
_BF16 = jnp.bfloat16
_NT = (((1,), (1,)), ((), ()))
_TN = (((0,), (0,)), ((), ()))


def _dot(a, b):
    return jnp.dot(a, b, preferred_element_type=_F32)


def _silu(x):
    return x * jax.nn.sigmoid(x)


def _causal_conv_silu_t(x, tail_ref, w_ref, b_ref):
    cols = x.shape[1]
    reps = cols // LANES
    tail = tail_ref[...]
    lane = lax.broadcasted_iota(jnp.int32, (x.shape[0], LANES), 1)
    wide = lambda a: jnp.concatenate([a] * reps, axis=1)
    y = x * wide(w_ref[CONV_K - 1]) + wide(b_ref[...])
    for shift in range(1, CONV_K):
        rolled = pltpu.roll(x, shift, 1)
        head = jnp.where(lane < shift, pltpu.roll(tail, shift, 1), rolled[:, :LANES])
        shifted = jnp.concatenate([head, rolled[:, LANES:]], axis=1)
        y = y + shifted * wide(w_ref[CONV_K - 1 - shift])
    tail_ref[...] = x[:, cols - LANES:]
    return _silu(y)


def _proj_kernel(x_ref, nw_ref, cos_t_ref, sin_t_ref,
                 w_t_ref, w_zb_ref, w_xb_ref, w_bc_ref, w_g_ref, w_za_ref,
                 cwx_ref, cbx_ref, cwbc_ref, cbbc_ref,
                 q_t_ref, qi_t_ref, wi_t_ref, dt_t_ref, v_t_ref, k_ref, kidx_ref,
                 zb_t_ref, xs_t_ref, bcs_t_ref, b_tok_ref, g_ref, za_ref,
                 tailx_scr, tailbc_scr):
    @pl.when(pl.program_id(1) == 0)
    def _():
        tailx_scr[...] = jnp.zeros_like(tailx_scr)
        tailbc_scr[...] = jnp.zeros_like(tailbc_scr)

    x = x_ref[0]
    h = x * lax.rsqrt(jnp.mean(x * x, axis=-1, keepdims=True) + EPS) * nw_ref[...]
    h = h.astype(_BF16)

    t = lax.dot_general(w_t_ref[...], h, _NT, preferred_element_type=_F32)
    cos_t = cos_t_ref[0]
    sin_t = sin_t_ref[0]

    def rope_head(block, hd):
        x1 = block[hd * HEAD_DIM: hd * HEAD_DIM + ROT_HALF]
        x2 = block[hd * HEAD_DIM + ROT_HALF: hd * HEAD_DIM + ROT_DIM]
        return jnp.concatenate([x1 * cos_t - x2 * sin_t, x2 * cos_t + x1 * sin_t], axis=0)

    def rope_store(block, n_heads, scale, out_ref):
        out_ref[0] = (block * scale).astype(out_ref.dtype)
        for hd in range(n_heads):
            out_ref[0, hd * HEAD_DIM: hd * HEAD_DIM + ROT_DIM, :] = (
                rope_head(block, hd) * scale).astype(out_ref.dtype)

    def rope_value(block, n_heads):
        parts = []
        for hd in range(n_heads):
            parts += [rope_head(block, hd), block[hd * HEAD_DIM + ROT_DIM:(hd + 1) * HEAD_DIM]]
        if block.shape[0] > n_heads * HEAD_DIM:
            parts.append(block[n_heads * HEAD_DIM:])
        return jnp.concatenate(parts, axis=0)

    o_qi = ATTN_WIDTH
    o_k = o_qi + IDX_HEADS * IDX_DIM
    o_ki = o_k + KV_WIDTH
    o_v = o_ki + LANES
    rope_store(t[:o_qi], ATTN_HEADS, LOG2_E * HEAD_DIM ** -0.5, q_t_ref)
    rope_store(t[o_qi:o_k], IDX_HEADS, IDX_DIM ** -0.5, qi_t_ref)
    o_wi = o_ki + IDX_DIM
    o_dt = o_wi + W_IDX_ROWS
    wi_t_ref[0] = t[o_wi:o_dt] * (IDX_HEADS ** -0.5)
    dt_t_ref[0] = t[o_dt:o_dt + SSM_HEADS]
    for g in range(ATTN_KV_HEADS):
        v_t_ref[0, g * V_ROWS:g * V_ROWS + HEAD_DIM, :] = t[o_v + g * HEAD_DIM:o_v + (g + 1) * HEAD_DIM].astype(_BF16)
        v_t_ref[0, g * V_ROWS + HEAD_DIM:(g + 1) * V_ROWS, :] = jnp.ones((V_ROWS - HEAD_DIM, x.shape[0]), _BF16)
    k_ref[0] = rope_value(t[o_k:o_ki], ATTN_KV_HEADS).T.astype(_BF16)
    kidx_ref[0] = rope_value(t[o_ki:o_v], 1).T[:, :IDX_DIM].astype(_BF16)

    def feat_major(w_ref, r0, rows):
        return lax.dot_general(w_ref[r0:r0 + rows], h, _NT, preferred_element_type=_F32)

    def store_chunks(out_ref, sl, value):
        for ck in range(value.shape[1] // CHUNK):
            out_ref[0, ck, sl, :] = value[:, ck * CHUNK:(ck + 1) * CHUNK].astype(out_ref.dtype)

    rc = CONV_ROWS
    for n in range(SSM_WIDTH // rc):
        sl = slice(n * rc, (n + 1) * rc)
        store_chunks(xs_t_ref, sl, _causal_conv_silu_t(
            feat_major(w_xb_ref, n * rc, rc), tailx_scr.at[sl], cwx_ref.at[:, sl], cbx_ref.at[sl]))
        store_chunks(zb_t_ref, sl, feat_major(w_zb_ref, n * rc, rc))
    for n in range(BC_WIDTH // rc):
        sl = slice(n * rc, (n + 1) * rc)
        bc = _causal_conv_silu_t(
            feat_major(w_bc_ref, n * rc, rc), tailbc_scr.at[sl], cwbc_ref.at[:, sl], cbbc_ref.at[sl])
        store_chunks(bcs_t_ref, sl, bc)
        if (n + 1) * rc <= BC_WIDTH // 2:
            b_tok_ref[0, :, sl] = bc.T.astype(_BF16)
    g_ref[0] = _dot(h, w_g_ref[...]).astype(_BF16)
    za_ref[0] = _dot(h, w_za_ref[...])


def _const_spec(shape):
    nd = len(shape)
    return pl.BlockSpec(shape, lambda *_: (0,) * nd, pipeline_mode=pl.Buffered(1))


def _projection(x, nw, cos_t, sin_t, weights, conv_params):
    bsz, seq, _ = x.shape
    tm = PROJ_ROWS
    tok = lambda width: pl.BlockSpec((1, tm, width), lambda b, i: (b, i, 0))
    feat = lambda rows: pl.BlockSpec((1, rows, tm), lambda b, i: (b, 0, i))
    in_specs = [tok(D_MODEL), _const_spec((1, D_MODEL)), feat(ROT_HALF), feat(ROT_HALF)
                ] + [_const_spec(w.shape) for w in weights + conv_params]
    out_shape = (
        jax.ShapeDtypeStruct((bsz, ATTN_WIDTH, seq), _BF16),
        jax.ShapeDtypeStruct((bsz, IDX_HEADS * IDX_DIM, seq), _BF16),
        jax.ShapeDtypeStruct((bsz, W_IDX_ROWS, seq), _F32),
        jax.ShapeDtypeStruct((bsz, SSM_HEADS, seq), _F32),
        jax.ShapeDtypeStruct((bsz, ATTN_KV_HEADS * V_ROWS, seq), _BF16),
        jax.ShapeDtypeStruct((bsz, seq, KV_WIDTH), _BF16),
        jax.ShapeDtypeStruct((bsz, seq, IDX_DIM), _BF16),
        jax.ShapeDtypeStruct((bsz, seq // CHUNK, SSM_WIDTH, CHUNK), _F32),
        jax.ShapeDtypeStruct((bsz, seq // CHUNK, SSM_WIDTH, CHUNK), _BF16),
        jax.ShapeDtypeStruct((bsz, seq // CHUNK, BC_WIDTH, CHUNK), _BF16),
        jax.ShapeDtypeStruct((bsz, seq, BC_WIDTH // 2), _BF16),
        jax.ShapeDtypeStruct((bsz, seq, N_BRANCH * D_MODEL), _BF16),
        jax.ShapeDtypeStruct((bsz, seq, ATTN_WIDTH), _F32),
    )
    slabs = lambda rows: pl.BlockSpec((1, tm // CHUNK, rows, CHUNK), lambda b, i: (b, i, 0, 0))
    out_specs = (feat(ATTN_WIDTH), feat(IDX_HEADS * IDX_DIM), feat(W_IDX_ROWS), feat(SSM_HEADS),
                 feat(ATTN_KV_HEADS * V_ROWS), tok(KV_WIDTH), tok(IDX_DIM), slabs(SSM_WIDTH), slabs(SSM_WIDTH),
                 slabs(BC_WIDTH), tok(BC_WIDTH // 2), tok(N_BRANCH * D_MODEL), tok(ATTN_WIDTH))
    return pl.pallas_call(
        _proj_kernel,
        grid=(bsz, seq // tm),
        in_specs=in_specs,
        out_specs=out_specs,
        out_shape=out_shape,
        scratch_shapes=[pltpu.VMEM((SSM_WIDTH, LANES), _F32), pltpu.VMEM((BC_WIDTH, LANES), _F32)],
        compiler_params=pltpu.CompilerParams(
            dimension_semantics=("parallel", "arbitrary"), vmem_limit_bytes=VMEM_LIMIT_BYTES),
        name="proj",
    )(x, nw, cos_t, sin_t, *weights, *conv_params)


def _attn_kernel(q_t_ref, qi_t_ref, wi_t_ref, k_ref, kidx_ref, v_t_ref, za_ref, o_ref,
                 score_scr, score16_scr, sel_scr, bias_scr, qpad_scr, m_scr, alpha_scr, acc_scr, s_scr,
                 *, top_k):
    j = pl.program_id(1)
    n_chunks = j + 1
    tq = Q_TILE
    seqs = range(ATTN_SEQS)
    q_pos = j * tq + lax.broadcasted_iota(jnp.int32, (1, tq), 1)
    key_iota = lax.broadcasted_iota(jnp.int32, (KEY_TILE, tq), 0)

    qi_cat, wi_cat = [], []
    for e in seqs:
        qi_t = qi_t_ref[e]
        qi_cat.append(jnp.concatenate([qi_t[hd * IDX_DIM:(hd + 1) * IDX_DIM] for hd in range(IDX_HEADS)], axis=1))
        wi_t = wi_t_ref[e]
        wi_cat.append(jnp.concatenate([wi_t[hd:hd + 1] for hd in range(IDX_HEADS)], axis=1))

    def score_chunk(c, carry):
        start = pl.multiple_of(c * KEY_TILE, KEY_TILE)
        for e in seqs:
            logits = _dot(kidx_ref[e, pl.ds(start, KEY_TILE), :], qi_cat[e])
            weighted = jnp.maximum(logits, 0.0) * wi_cat[e]
            score = weighted[:, 0:tq]
            for hd in range(1, IDX_HEADS):
                score = score + weighted[:, hd * tq:(hd + 1) * tq]
            score = jnp.where((start + key_iota) <= q_pos, score, -jnp.inf)
            score_scr[e, pl.ds(start, KEY_TILE), :] = score
            score16_scr[e, pl.ds(start, KEY_TILE), :] = score.astype(_BF16)
        return carry

    lax.fori_loop(0, n_chunks, score_chunk, 0)

    def code_to_f32(code):
        return pltpu.bitcast(code ^ ((code >> 31) & jnp.int32(0x7FFFFFFF)), _F32)

    def count_ge(ref, cands, dtype):
        def add_tile(c, accs):
            start = pl.multiple_of(c * KEY_TILE, KEY_TILE)
            out = []
            for e in seqs:
                acc = accs[e]
                inc = jnp.where(ref[e, pl.ds(start, KEY_TILE), :] >= cands[e],
                                jnp.ones((), dtype), jnp.zeros((), dtype))
                for part in range(KEY_TILE // COUNT_ROWS):
                    acc = acc + inc[part * COUNT_ROWS:(part + 1) * COUNT_ROWS]
                out.append(acc)
            return tuple(out)

        accs = lax.fori_loop(0, n_chunks, add_tile,
                             tuple(jnp.zeros((COUNT_ROWS, tq), dtype) for _ in seqs))
        return [jnp.sum(acc.astype(_F32), axis=0, keepdims=True) for acc in accs]

    def bf16_code(block):
        return jnp.where(block >= 0, block, block | jnp.int32(0xFFFF))

    def coarse_bit(i, blocks):
        cands = [block + (jnp.int32(1) << (31 - i)) for block in blocks]
        cnts = count_ge(score16_scr, [code_to_f32(bf16_code(cand)).astype(_BF16) for cand in cands], _BF16)
        return tuple(jnp.where(cnt >= top_k, cand, block) for cnt, cand, block in zip(cnts, cands, blocks))

    def fine_bit(i, carry):
        codes, n_aboves = carry
        cands = [code + (jnp.int32(1) << (16 - i)) for code in codes]
        cnts = count_ge(score_scr, [code_to_f32(cand) for cand in cands], _F32)
        hits = [cnt >= top_k for cnt in cnts]
        return (tuple(jnp.where(hit, cand, code) for hit, cand, code in zip(hits, cands, codes)),
                tuple(jnp.where(hit, n_above, cnt) for hit, n_above, cnt in zip(hits, n_aboves, cnts)))

    for e in seqs:
        sel_scr[e, 0:1, :] = jnp.full((1, tq), -jnp.inf, _F32)
        sel_scr[e, 1:2, :] = jnp.zeros((1, tq), _F32)

    @pl.when((j + 1) * tq > top_k)
    def _():
        int_min = jnp.int32(-2 ** 31)
        coarse = lax.fori_loop(0, 16, coarse_bit, tuple(jnp.full((1, tq), int_min, jnp.int32) for _ in seqs))
        bases = tuple(bf16_code(block) - jnp.int32(1 << 16) for block in coarse)
        fine, n_aboves = lax.fori_loop(0, 17, fine_bit, (bases, tuple(jnp.zeros((1, tq), _F32) for _ in seqs)))
        few = (q_pos + 1) <= top_k
        for e in seqs:
            sel_scr[e, 0:1, :] = jnp.where(few, -jnp.inf, code_to_f32(fine[e]))
            sel_scr[e, 1:2, :] = jnp.where(few, 0.0, top_k - n_aboves[e])

    thr = [sel_scr[e, 0:1, :] for e in seqs]
    n_ties_kept = [sel_scr[e, 1:2, :] for e in seqs]

    row = lax.broadcasted_iota(jnp.int32, (KEY_TILE, KEY_TILE), 0)
    col = lax.broadcasted_iota(jnp.int32, (KEY_TILE, KEY_TILE), 1)
    strict_lower = jnp.where(col < row, 1.0, 0.0).astype(_BF16)

    def bias_tile(c, ties_before):
        start = pl.multiple_of(c * KEY_TILE, KEY_TILE)
        out = []
        for e in seqs:
            s = score_scr[e, pl.ds(start, KEY_TILE), :]
            tie = jnp.where(s == thr[e], 1.0, 0.0)
            rank = _dot(strict_lower, tie.astype(_BF16)) + ties_before[e]
            keep = (s > thr[e]) | ((s == thr[e]) & (rank < n_ties_kept[e]))
            bias_scr[e] = jnp.where(keep, 0.0, NEG_BIG)
            out.append(ties_before[e] + jnp.sum(tie, axis=0, keepdims=True))
        return tuple(out)

    n_pairs = ATTN_HEADS // 2
    zeros_half = jnp.zeros((HEAD_DIM, 2 * tq), _BF16)
    for e in seqs:
        for pair in range(n_pairs):
            g = (2 * pair) // ATTN_GROUP
            q_cat = jnp.concatenate([q_t_ref[e, (2 * pair + hd) * HEAD_DIM:(2 * pair + hd + 1) * HEAD_DIM, :]
                                     for hd in range(2)], axis=1)
            qpad_scr[e, pair] = jnp.concatenate([q_cat, zeros_half] if g == 0 else [zeros_half, q_cat], axis=0)
    m_scr[...] = jnp.full(m_scr.shape, NEG_BIG, _F32)
    acc_scr[...] = jnp.zeros(acc_scr.shape, _F32)

    def logits_stage(c, e, pair):
        start = pl.multiple_of(c * KEY_TILE, KEY_TILE)
        b = bias_scr[e]
        s = _dot(k_ref[e, pl.ds(start, KEY_TILE), :], qpad_scr[e, pair]) + jnp.concatenate([b, b], axis=1)
        m_old = m_scr[e, pair]
        m_new = jnp.maximum(m_old, jnp.max(s, axis=0, keepdims=True))
        s_scr[e, pair] = s
        alpha_scr[e, pair] = jnp.exp2(m_old - m_new)
        m_scr[e, pair] = m_new

    def value_stage(c, e, pair):
        start = pl.multiple_of(c * KEY_TILE, KEY_TILE)
        g = (2 * pair) // ATTN_GROUP
        p = jnp.exp2(s_scr[e, pair] - m_scr[e, pair])
        v_c = v_t_ref[e, g * V_ROWS:(g + 1) * V_ROWS, pl.ds(start, KEY_TILE)]
        acc_scr[e, pair] = alpha_scr[e, pair] * acc_scr[e, pair] + _dot(v_c, p.astype(_BF16))

    ties = bias_tile(0, tuple(jnp.zeros((1, tq), _F32) for _ in seqs))
    for pair in range(n_pairs):
        for e in seqs:
            logits_stage(0, e, pair)

    def attn_tile(c, ties):
        ties = bias_tile(c + 1, ties)
        for pair in range(n_pairs):
            for e in seqs:
                value_stage(c, e, pair)
                logits_stage(c + 1, e, pair)
        return ties

    lax.fori_loop(0, n_chunks - 1, attn_tile, ties)
    for pair in range(n_pairs):
        for e in seqs:
            value_stage(n_chunks - 1, e, pair)

    for pair in range(n_pairs):
        for e in seqs:
            o_t = acc_scr[e, pair, :HEAD_DIM] / acc_scr[e, pair, HEAD_DIM:HEAD_DIM + 1]
            both = jnp.concatenate([o_t[:, :tq], o_t[:, tq:]], axis=0)
            z = za_ref[e, :, pair * LANES:(pair + 1) * LANES]
            o_ref[e, :, pair * LANES:(pair + 1) * LANES] = (both.T * _silu(z)).astype(o_ref.dtype)


def _attention(q_t, qi_t, wi_t, k, kidx, v_t, za):
    bsz, seq, _ = k.shape
    tq = Q_TILE
    ns = ATTN_SEQS
    top_k = min(TOPK_MAX, seq // 4)
    feat = lambda rows: pl.BlockSpec((ns, rows, tq), lambda b, j: (b, 0, j))
    full = lambda s1, s2: pl.BlockSpec((ns, s1, s2), lambda b, j: (b, 0, 0))
    n_pairs = ATTN_HEADS // 2
    return pl.pallas_call(
        functools.partial(_attn_kernel, top_k=top_k),
        grid=(bsz // ns, seq // tq),
        in_specs=[feat(ATTN_WIDTH), feat(IDX_HEADS * IDX_DIM), feat(W_IDX_ROWS),
                  full(seq, KV_WIDTH), full(seq, IDX_DIM), full(ATTN_KV_HEADS * V_ROWS, seq),
                  pl.BlockSpec((ns, tq, ATTN_WIDTH), lambda b, j: (b, j, 0))],
        out_specs=pl.BlockSpec((ns, tq, ATTN_WIDTH), lambda b, j: (b, j, 0)),
        out_shape=jax.ShapeDtypeStruct((bsz, seq, ATTN_WIDTH), _BF16),
        scratch_shapes=[pltpu.VMEM((ns, seq, tq), _F32), pltpu.VMEM((ns, seq, tq), _BF16),
                        pltpu.VMEM((ns, SUBLANES, tq), _F32), pltpu.VMEM((ns, KEY_TILE, tq), _F32),
                        pltpu.VMEM((ns, n_pairs, 2 * HEAD_DIM, 2 * tq), _BF16),
                        pltpu.VMEM((ns, n_pairs, 1, 2 * tq), _F32),
                        pltpu.VMEM((ns, n_pairs, 1, 2 * tq), _F32),
                        pltpu.VMEM((ns, n_pairs, V_ROWS, 2 * tq), _F32),
                        pltpu.VMEM((ns, n_pairs, KEY_TILE, 2 * tq), _F32)],
        compiler_params=pltpu.CompilerParams(
            dimension_semantics=("parallel", "arbitrary"), vmem_limit_bytes=VMEM_LIMIT_BYTES),
        name="attn",
    )(q_t, qi_t, wi_t, k, kidx, v_t, za)


def _split3(x):
    hi = x.astype(_BF16)
    r1 = x - hi.astype(_F32)
    mid = r1.astype(_BF16)
    lo = (r1 - mid.astype(_F32)).astype(_BF16)
    return hi, mid, lo


def _ssd_kernel(xs_t_ref, bcs_t_ref, b_tok_ref, dt_ref, zb_t_ref, dtb_ref, alog_ref, dskip_ref, nw_ref, o_t_ref,
                state_scr, y_scr, dt_scr, acum_scr, acum_t_scr, cols_even_scr, cols_odd_scr):
    i = pl.program_id(1)
    r = lax.broadcasted_iota(jnp.int32, (CHUNK, CHUNK), 0)
    c = lax.broadcasted_iota(jnp.int32, (CHUNK, CHUNK), 1)
    causal_t = r <= c

    @pl.when(i == 0)
    def _():
        state_scr[...] = jnp.zeros_like(state_scr)
        upper_b = jnp.where(causal_t, 1.0, 0.0).astype(_BF16)
        dt_in_t = dt_ref[0] + dtb_ref[...]
        dt_t = jnp.maximum(dt_in_t, 0.0) + jnp.log1p(jnp.exp(-jnp.abs(dt_in_t)))
        dt_scr[...] = dt_t
        parts = _split3(dt_t * (-jnp.exp(alog_ref[...])))
        pad = jnp.zeros((LANES - SSM_HEADS, CHUNK), _F32)
        for n in range(dt_t.shape[1] // CHUNK):
            sl = slice(n * CHUNK, (n + 1) * CHUNK)
            a_cum_t = sum(_dot(part[:, sl], upper_b) for part in parts)
            acum_t_scr[:, sl] = a_cum_t
            acum_scr[sl, :] = jnp.concatenate([a_cum_t, pad], axis=0).T

    def spread_columns(chunk, cols_ref):
        a_cum = acum_scr[pl.ds(pl.multiple_of(chunk * CHUNK, CHUNK), CHUNK), :]
        for hh in range(SSM_HEADS):
            cols_ref[:, hh * LANES:(hh + 1) * LANES] = jnp.broadcast_to(a_cum[:, hh:hh + 1], (CHUNK, LANES))

    @pl.when(i == 0)
    def _():
        spread_columns(0, cols_even_scr)

    n_chunks = pl.num_programs(1) * SSD_STEP_CHUNKS

    def chunk_step(ck, cols_ref, next_cols_ref):
        chunk = i * SSD_STEP_CHUNKS + ck
        start = pl.multiple_of(chunk * CHUNK, CHUNK)
        dt_t = dt_scr[:, pl.ds(start, CHUNK)]
        a_cum_t = acum_t_scr[:, pl.ds(start, CHUNK)]
        heads_per_group = SSM_HEADS // SSM_GROUPS

        for grp in range(SSM_GROUPS):
            b_t = bcs_t_ref[0, ck, grp * SSM_STATE:(grp + 1) * SSM_STATE, :]
            c_t = bcs_t_ref[0, ck, BC_WIDTH // 2 + grp * SSM_STATE: BC_WIDTH // 2 + (grp + 1) * SSM_STATE, :]
            b_tok = b_tok_ref[0, ck * CHUNK:(ck + 1) * CHUNK, grp * SSM_STATE:(grp + 1) * SSM_STATE]
            cb_t = _dot(b_tok, c_t)
            cb_t = jnp.where(causal_t, cb_t, 0.0)
            if grp == 0:
                spread_columns(jnp.minimum(chunk + 1, n_chunks - 1), next_cols_ref)

            for hh in range(grp * heads_per_group, (grp + 1) * heads_per_group):
                rows = slice(hh * SSM_HEAD_DIM, (hh + 1) * SSM_HEAD_DIM)
                acum_row = a_cum_t[hh:hh + 1, :]
                last = acum_row[:, CHUNK - 1:CHUNK]
                seg_t = acum_row - cols_ref[:, hh * LANES:(hh + 1) * LANES]
                m_t = (cb_t * jnp.exp(jnp.minimum(seg_t, 0.0))).astype(_BF16)
                x_h = xs_t_ref[0, ck, rows, :].astype(_F32)
                xd = x_h * dt_t[hh:hh + 1, :]
                y = _dot(xd.astype(_BF16), m_t)
                st_prev = state_scr[hh]
                y_off = _dot(st_prev.astype(_BF16), c_t) * jnp.exp(acum_row)
                w = (xd * jnp.exp(last - acum_row)).astype(_BF16)
                st_new = lax.dot_general(w, b_t, _NT, preferred_element_type=_F32)
                state_scr[hh] = jnp.exp(last) * st_prev + st_new
                y_scr[rows, :] = y + y_off + dskip_ref[rows, :] * x_h

        gw = SSM_WIDTH // SSM_GROUPS
        for grp in range(SSM_GROUPS):
            sl = slice(grp * gw, (grp + 1) * gw)
            yz = y_scr[sl, :] * _silu(zb_t_ref[0, ck, sl, :])
            ms = jnp.mean(yz * yz, axis=0, keepdims=True)
            o_t_ref[0, ck, sl, :] = (yz * lax.rsqrt(ms + EPS) * nw_ref[sl, :]).astype(o_t_ref.dtype)

    for ck in range(SSD_STEP_CHUNKS):
        if ck % 2 == 0:
            chunk_step(ck, cols_even_scr, cols_odd_scr)
        else:
            chunk_step(ck, cols_odd_scr, cols_even_scr)


def _ssd(xs_t, bcs_t, b_tok, dt_t, zb_t, dtb, alog, dskip, nw):
    bsz, seq = b_tok.shape[:2]
    nck = SSD_STEP_CHUNKS
    feat = lambda rows: pl.BlockSpec((1, nck, rows, CHUNK), lambda b, i: (b, i, 0, 0))
    consts = (dtb, alog, dskip, nw)
    return pl.pallas_call(
        _ssd_kernel,
        grid=(bsz, seq // (nck * CHUNK)),
        in_specs=[feat(SSM_WIDTH), feat(BC_WIDTH),
                  pl.BlockSpec((1, nck * CHUNK, BC_WIDTH // 2), lambda b, i: (b, i, 0)),
                  pl.BlockSpec((1, SSM_HEADS, seq), lambda b, i: (b, 0, 0)),
                  feat(SSM_WIDTH)] + [_const_spec(a.shape) for a in consts],
        out_specs=feat(SSM_WIDTH),
        out_shape=jax.ShapeDtypeStruct((bsz, seq // CHUNK, SSM_WIDTH, CHUNK), _BF16),
        scratch_shapes=[pltpu.VMEM((SSM_HEADS, SSM_HEAD_DIM, SSM_STATE), _F32),
                        pltpu.VMEM((SSM_WIDTH, CHUNK), _F32),
                        pltpu.VMEM((SSM_HEADS, seq), _F32), pltpu.VMEM((seq, LANES), _F32),
                        pltpu.VMEM((SSM_HEADS, seq), _F32),
                        pltpu.VMEM((CHUNK, SSM_HEADS * LANES), _F32),
                        pltpu.VMEM((CHUNK, SSM_HEADS * LANES), _F32)],
        compiler_params=pltpu.CompilerParams(
            dimension_semantics=("parallel", "arbitrary"), vmem_limit_bytes=VMEM_LIMIT_BYTES),
        name="ssd",
    )(xs_t, bcs_t, b_tok, dt_t, zb_t, *consts)


def _merge_kernel(x_ref, oa_ref, ob_t_ref, g_ref, gb_ref, wa_ref, wb_ref, wo_ref, fw_ref, o_ref, *, final_norm):
    gates = jax.nn.sigmoid(g_ref[0] + gb_ref[...])
    ob_t = jnp.concatenate([ob_t_ref[0, ck] for ck in range(ob_t_ref.shape[1])], axis=1)
    branch_b = lax.dot_general(ob_t, wb_ref[...], _TN, preferred_element_type=_F32)
    merged = gates[:, :D_MODEL] * _dot(oa_ref[0], wa_ref[...]) + gates[:, D_MODEL:] * branch_b
    y = x_ref[0] + _dot(merged.astype(_BF16), wo_ref[...])
    if final_norm:
        y = y * lax.rsqrt(jnp.mean(y * y, axis=-1, keepdims=True) + EPS) * fw_ref[...]
    o_ref[0] = y


def _merge(x, o_a, o_b_t, gates, gate_bias, wa, wb, wo, fw, final_norm):
    bsz, seq, _ = x.shape
    tm = MERGE_ROWS
    tok = lambda width: pl.BlockSpec((1, tm, width), lambda b, i: (b, i, 0))
    consts = (gate_bias, wa, wb, wo, fw)
    return pl.pallas_call(
        functools.partial(_merge_kernel, final_norm=final_norm),
        grid=(bsz, seq // tm),
        in_specs=[tok(D_MODEL), tok(ATTN_WIDTH), pl.BlockSpec((1, tm // CHUNK, SSM_WIDTH, CHUNK), lambda b, i: (b, i, 0, 0)),
                  tok(N_BRANCH * D_MODEL)] + [_const_spec(a.shape) for a in consts],
        out_specs=tok(D_MODEL),
        out_shape=jax.ShapeDtypeStruct((bsz, seq, D_MODEL), _F32),
        compiler_params=pltpu.CompilerParams(
            dimension_semantics=("parallel", "parallel"), vmem_limit_bytes=VMEM_LIMIT_BYTES),
        name="merge",
    )(x, o_a, o_b_t, gates, *consts)


def _cast_kernel(w_ref, o_ref):
    segs = o_ref.shape[1] // LANES
    for k in range(segs):
        o_ref[:, k * LANES:(k + 1) * LANES] = w_ref[0, pl.ds(k, o_ref.shape[0], stride=segs), :].astype(o_ref.dtype)


def _layer_transposed_bf16(w_all, layer):
    depth, cols, rows = w_all.shape
    segs = cols // LANES
    w_rows = jnp.swapaxes(w_all, 1, 2).reshape(depth, rows * segs, LANES)
    tm = CAST_ROWS
    return pl.pallas_call(
        _cast_kernel,
        grid=(pl.cdiv(rows, tm),),
        in_specs=[pl.BlockSpec((1, tm * segs, LANES), lambda i: (layer, i, 0))],
        out_specs=pl.BlockSpec((tm, cols), lambda i: (i, 0)),
        out_shape=jax.ShapeDtypeStruct((rows, cols), _BF16),
        compiler_params=pltpu.CompilerParams(dimension_semantics=("parallel",)),
        name="cast",
    )(w_rows)


def _layer_weights(w_in_all, layer):
    wt16 = _layer_transposed_bf16(w_in_all, layer)
    o = SPLIT_OFFSETS
    seg = lambda i: wt16[o[i]:o[i + 1]]
    w_q, w_k, w_v, w_za, w_qi, w_ki, w_wi, w_zb, w_xb, w_b, w_c, w_dt, w_g = (seg(i) for i in range(13))
    pad = lambda w, n: jnp.pad(w, ((0, n - w.shape[0]), (0, 0)))
    w_misc = pad(jnp.concatenate([w_ki, pad(w_wi, W_IDX_ROWS), w_dt], axis=0), LANES)
    w_t = jnp.concatenate([w_q, w_qi, w_k, w_misc, w_v], axis=0)
    return w_t, w_zb, w_xb, jnp.concatenate([w_b, w_c], axis=0), w_g.T, w_za.T


def _rope_tables(positions):
    inv_freq = ROPE_THETA ** (-jnp.arange(0, ROT_DIM, 2, dtype=_F32) / ROT_DIM)
    ang = jnp.swapaxes(positions.astype(_F32)[..., None] * inv_freq, 1, 2)
    return jnp.cos(ang), jnp.sin(ang)


def _lane_broadcast(v):
    return jnp.broadcast_to(v[..., None], v.shape + (LANES,))


def kernel(x, positions, norm_w, w_in, gate_bias, conv_w, conv_b, dt_bias, a_log, d_skip,
           ssm_norm_w, w_branch_a, w_branch_b, w_out, final_norm_w):
    depth = norm_w.shape[0]
    cos_t, sin_t = _rope_tables(positions)
    for i in range(depth):
        weights = _layer_weights(w_in, i)
        conv_params = (_lane_broadcast(conv_w[i][:, :SSM_WIDTH]), _lane_broadcast(conv_b[i][:SSM_WIDTH]),
                       _lane_broadcast(conv_w[i][:, SSM_WIDTH:]), _lane_broadcast(conv_b[i][SSM_WIDTH:]))
        (q_t, qi_t, wi_t, dt_t, v_t, k, kidx, zb_t, xs_t, bcs_t, b_tok, gates, za) = _projection(
            x, norm_w[i][None, :], cos_t, sin_t, weights, conv_params)
        o_a = _attention(q_t, qi_t, wi_t, k, kidx, v_t, za)
        o_b_t = _ssd(xs_t, bcs_t, b_tok, dt_t, zb_t, dt_bias[i][:, None], a_log[i][:, None],
                     _lane_broadcast(jnp.repeat(d_skip[i], SSM_HEAD_DIM)), _lane_broadcast(ssm_norm_w[i]))
        x = _merge(x, o_a, o_b_t, gates, gate_bias[i][None, :], w_branch_a[i].astype(_BF16),
                   w_branch_b[i].astype(_BF16), w_out[i].astype(_BF16), final_norm_w[None, :],
                   final_norm=(i == depth - 1))
    return x
```

```python
import functools

import numpy as np
import jax
import jax.numpy as jnp
from jax import lax
from jax.experimental import pallas as pl
from jax.experimental.pallas import tpu as pltpu

D_MODEL = 1024
ATTN_HEADS = 8
ATTN_KV_HEADS = 2
HEAD_DIM = 64
ATTN_GROUP = ATTN_HEADS // ATTN_KV_HEADS
ATTN_WIDTH = ATTN_HEADS * HEAD_DIM
KV_WIDTH = ATTN_KV_HEADS * HEAD_DIM
ROT_DIM = HEAD_DIM // 4
ROT_HALF = ROT_DIM // 2
ROPE_THETA = 500000.0
IDX_HEADS = 4
IDX_DIM = 64
TOPK_MAX = 256
SSM_HEADS = 16
SSM_HEAD_DIM = 64
SSM_WIDTH = SSM_HEADS * SSM_HEAD_DIM
SSM_GROUPS = 4
SSM_STATE = 64
CONV_K = 4
CHUNK = 128
BC_WIDTH = 2 * SSM_GROUPS * SSM_STATE
N_BRANCH = 2
EPS = 1e-6
SPLIT_SIZES = (ATTN_WIDTH, KV_WIDTH, KV_WIDTH, ATTN_WIDTH,
               IDX_HEADS * IDX_DIM, IDX_DIM, IDX_HEADS,
               SSM_WIDTH, SSM_WIDTH, SSM_GROUPS * SSM_STATE, SSM_GROUPS * SSM_STATE, SSM_HEADS,
               N_BRANCH * D_MODEL)
SPLIT_OFFSETS = tuple(int(o) for o in np.cumsum((0,) + SPLIT_SIZES))

LANES = 128
SUBLANES = 8
VMEM_LIMIT_BYTES = 56 * 1024 * 1024

PROJ_ROWS = 512
CONV_ROWS = 256
CAST_ROWS = 512
ATTN_SEQS = 4
Q_TILE = 256
KEY_TILE = 256
COUNT_ROWS = 32
LOG2_E = 1.4426950408889634
MERGE_ROWS = 512
SSD_STEP_CHUNKS = 4
W_IDX_ROWS = 8
V_ROWS = HEAD_DIM + 16
NEG_BIG = -1e30

_F32 = jnp.float32
_BF16 = jnp.bfloat16
_NT = (((1,), (1,)), ((), ()))
_TN = (((0,), (0,)), ((), ()))


def _dot(a, b):
    return jnp.dot(a, b, preferred_element_type=_F32)


def _silu(x):
    return x * jax.nn.sigmoid(x)


def _causal_conv_silu_t(x, tail_ref, w_ref, b_ref):
    cols = x.shape[1]
    reps = cols // LANES
    tail = tail_ref[...]
    lane = lax.broadcasted_iota(jnp.int32, (x.shape[0], LANES), 1)
    wide = lambda a: jnp.concatenate([a] * reps, axis=1)
    y = x * wide(w_ref[CONV_K - 1]) + wide(b_ref[...])
    for shift in range(1, CONV_K):
        rolled = pltpu.roll(x, shift, 1)
        head = jnp.where(lane < shift, pltpu.roll(tail, shift, 1), rolled[:, :LANES])
        shifted = jnp.concatenate([head, rolled[:, LANES:]], axis=1)
        y = y + shifted * wide(w_ref[CONV_K - 1 - shift])
    tail_ref[...] = x[:, cols - LANES:]
    return _silu(y)


def _proj_kernel(x_ref, nw_ref, cos_t_ref, sin_t_ref,
                 w_t_ref, w_zb_ref, w_xb_ref, w_bc_ref, w_g_ref, w_za_ref,
                 cwx_ref, cbx_ref, cwbc_ref, cbbc_ref,
                 q_t_ref, qi_t_ref, wi_t_ref, dt_t_ref, v_t_ref, k_ref, kidx_ref,
                 zb_t_ref, xs_t_ref, bcs_t_ref, b_tok_ref, g_ref, za_ref,
                 tailx_scr, tailbc_scr):
    @pl.when(pl.program_id(1) == 0)
    def _():
        tailx_scr[...] = jnp.zeros_like(tailx_scr)
        tailbc_scr[...] = jnp.zeros_like(tailbc_scr)

    x = x_ref[0]
    h = x * lax.rsqrt(jnp.mean(x * x, axis=-1, keepdims=True) + EPS) * nw_ref[...]
    h = h.astype(_BF16)

    t = lax.dot_general(w_t_ref[...], h, _NT, preferred_element_type=_F32)
    cos_t = cos_t_ref[0]
    sin_t = sin_t_ref[0]

    def rope_head(block, hd):
        x1 = block[hd * HEAD_DIM: hd * HEAD_DIM + ROT_HALF]
        x2 = block[hd * HEAD_DIM + ROT_HALF: hd * HEAD_DIM + ROT_DIM]
        return jnp.concatenate([x1 * cos_t - x2 * sin_t, x2 * cos_t + x1 * sin_t], axis=0)

    def rope_store(block, n_heads, scale, out_ref):
        out_ref[0] = (block * scale).astype(out_ref.dtype)
        for hd in range(n_heads):
            out_ref[0, hd * HEAD_DIM: hd * HEAD_DIM + ROT_DIM, :] = (
                rope_head(block, hd) * scale).astype(out_ref.dtype)

    def rope_value(block, n_heads):
        parts = []
        for hd in range(n_heads):
            parts += [rope_head(block, hd), block[hd * HEAD_DIM + ROT_DIM:(hd + 1) * HEAD_DIM]]
        if block.shape[0] > n_heads * HEAD_DIM:
            parts.append(block[n_heads * HEAD_DIM:])
        return jnp.concatenate(parts, axis=0)

    o_qi = ATTN_WIDTH
    o_k = o_qi + IDX_HEADS * IDX_DIM
    o_ki = o_k + KV_WIDTH
    o_v = o_ki + LANES
    rope_store(t[:o_qi], ATTN_HEADS, LOG2_E * HEAD_DIM ** -0.5, q_t_ref)
    rope_store(t[o_qi:o_k], IDX_HEADS, IDX_DIM ** -0.5, qi_t_ref)
    o_wi = o_ki + IDX_DIM
    o_dt = o_wi + W_IDX_ROWS
    wi_t_ref[0] = t[o_wi:o_dt] * (IDX_HEADS ** -0.5)
    dt_t_ref[0] = t[o_dt:o_dt + SSM_HEADS]
    for g in range(ATTN_KV_HEADS):
        v_t_ref[0, g * V_ROWS:g * V_ROWS + HEAD_DIM, :] = t[o_v + g * HEAD_DIM:o_v + (g + 1) * HEAD_DIM].astype(_BF16)
        v_t_ref[0, g * V_ROWS + HEAD_DIM:(g + 1) * V_ROWS, :] = jnp.ones((V_ROWS - HEAD_DIM, x.shape[0]), _BF16)
    k_ref[0] = rope_value(t[o_k:o_ki], ATTN_KV_HEADS).T.astype(_BF16)
    kidx_ref[0] = rope_value(t[o_ki:o_v], 1).T[:, :IDX_DIM].astype(_BF16)

    def feat_major(w_ref, r0, rows):
        return lax.dot_general(w_ref[r0:r0 + rows], h, _NT, preferred_element_type=_F32)

    def store_chunks(out_ref, sl, value):
        for ck in range(value.shape[1] // CHUNK):
            out_ref[0, ck, sl, :] = value[:, ck * CHUNK:(ck + 1) * CHUNK].astype(out_ref.dtype)

    rc = CONV_ROWS
    for n in range(SSM_WIDTH // rc):
        sl = slice(n * rc, (n + 1) * rc)
        store_chunks(xs_t_ref, sl, _causal_conv_silu_t(
            feat_major(w_xb_ref, n * rc, rc), tailx_scr.at[sl], cwx_ref.at[:, sl], cbx_ref.at[sl]))
        store_chunks(zb_t_ref, sl, feat_major(w_zb_ref, n * rc, rc))
    for n in range(BC_WIDTH // rc):
        sl = slice(n * rc, (n + 1) * rc)
        bc = _causal_conv_silu_t(
            feat_major(w_bc_ref, n * rc, rc), tailbc_scr.at[sl], cwbc_ref.at[:, sl], cbbc_ref.at[sl])
        store_chunks(bcs_t_ref, sl, bc)
        if (n + 1) * rc <= BC_WIDTH // 2:
            b_tok_ref[0, :, sl] = bc.T.astype(_BF16)
    g_ref[0] = _dot(h, w_g_ref[...]).astype(_BF16)
    za_ref[0] = _dot(h, w_za_ref[...])


def _const_spec(shape):
    nd = len(shape)
    return pl.BlockSpec(shape, lambda *_: (0,) * nd, pipeline_mode=pl.Buffered(1))


def _projection(x, nw, cos_t, sin_t, weights, conv_params):
    bsz, seq, _ = x.shape
    tm = PROJ_ROWS
    tok = lambda width: pl.BlockSpec((1, tm, width), lambda b, i: (b, i, 0))
    feat = lambda rows: pl.BlockSpec((1, rows, tm), lambda b, i: (b, 0, i))
    in_specs = [tok(D_MODEL), _const_spec((1, D_MODEL)), feat(ROT_HALF), feat(ROT_HALF)
                ] + [_const_spec(w.shape) for w in weights + conv_params]
    out_shape = (
        jax.ShapeDtypeStruct((bsz, ATTN_WIDTH, seq), _BF16),
        jax.ShapeDtypeStruct((bsz, IDX_HEADS * IDX_DIM, seq), _BF16),
        jax.ShapeDtypeStruct((bsz, W_IDX_ROWS, seq), _F32),
        jax.ShapeDtypeStruct((bsz, SSM_HEADS, seq), _F32),
        jax.ShapeDtypeStruct((bsz, ATTN_KV_HEADS * V_ROWS, seq), _BF16),
        jax.ShapeDtypeStruct((bsz, seq, KV_WIDTH), _BF16),
        jax.ShapeDtypeStruct((bsz, seq, IDX_DIM), _BF16),
        jax.ShapeDtypeStruct((bsz, seq // CHUNK, SSM_WIDTH, CHUNK), _F32),
        jax.ShapeDtypeStruct((bsz, seq // CHUNK, SSM_WIDTH, CHUNK), _BF16),
        jax.ShapeDtypeStruct((bsz, seq // CHUNK, BC_WIDTH, CHUNK), _BF16),
        jax.ShapeDtypeStruct((bsz, seq, BC_WIDTH // 2), _BF16),
        jax.ShapeDtypeStruct((bsz, seq, N_BRANCH * D_MODEL), _BF16),
        jax.ShapeDtypeStruct((bsz, seq, ATTN_WIDTH), _F32),
    )
    slabs = lambda rows: pl.BlockSpec((1, tm // CHUNK, rows, CHUNK), lambda b, i: (b, i, 0, 0))
    out_specs = (feat(ATTN_WIDTH), feat(IDX_HEADS * IDX_DIM), feat(W_IDX_ROWS), feat(SSM_HEADS),
                 feat(ATTN_KV_HEADS * V_ROWS), tok(KV_WIDTH), tok(IDX_DIM), slabs(SSM_WIDTH), slabs(SSM_WIDTH),
                 slabs(BC_WIDTH), tok(BC_WIDTH // 2), tok(N_BRANCH * D_MODEL), tok(ATTN_WIDTH))
    return pl.pallas_call(
        _proj_kernel,
        grid=(bsz, seq // tm),
        in_specs=in_specs,
        out_specs=out_specs,
        out_shape=out_shape,
        scratch_shapes=[pltpu.VMEM((SSM_WIDTH, LANES), _F32), pltpu.VMEM((BC_WIDTH, LANES), _F32)],
        compiler_params=pltpu.CompilerParams(
            dimension_semantics=("parallel", "arbitrary"), vmem_limit_bytes=VMEM_LIMIT_BYTES),
        name="proj",
    )(x, nw, cos_t, sin_t, *weights, *conv_params)


def _attn_kernel(q_t_ref, qi_t_ref, wi_t_ref, k_ref, kidx_ref, v_t_ref, za_ref, o_ref,
                 score_scr, score16_scr, sel_scr, bias_scr, qpad_scr, m_scr, alpha_scr, acc_scr, s_scr,
                 *, top_k):
    j = pl.program_id(1)
    n_chunks = j + 1
    tq = Q_TILE
    seqs = range(ATTN_SEQS)
    q_pos = j * tq + lax.broadcasted_iota(jnp.int32, (1, tq), 1)
    key_iota = lax.broadcasted_iota(jnp.int32, (KEY_TILE, tq), 0)

    qi_cat, wi_cat = [], []
    for e in seqs:
        qi_t = qi_t_ref[e]
        qi_cat.append(jnp.concatenate([qi_t[hd * IDX_DIM:(hd + 1) * IDX_DIM] for hd in range(IDX_HEADS)], axis=1))
        wi_t = wi_t_ref[e]
        wi_cat.append(jnp.concatenate([wi_t[hd:hd + 1] for hd in range(IDX_HEADS)], axis=1))

    def score_chunk(c, carry):
        start = pl.multiple_of(c * KEY_TILE, KEY_TILE)
        for e in seqs:
            logits = _dot(kidx_ref[e, pl.ds(start, KEY_TILE), :], qi_cat[e])
            weighted = jnp.maximum(logits, 0.0) * wi_cat[e]
            score = weighted[:, 0:tq]
            for hd in range(1, IDX_HEADS):
                score = score + weighted[:, hd * tq:(hd + 1) * tq]
            score = jnp.where((start + key_iota) <= q_pos, score, -jnp.inf)
            score_scr[e, pl.ds(start, KEY_TILE), :] = score
            score16_scr[e, pl.ds(start, KEY_TILE), :] = score.astype(_BF16)
        return carry

    lax.fori_loop(0, n_chunks, score_chunk, 0)

    def code_to_f32(code):
        return pltpu.bitcast(code ^ ((code >> 31) & jnp.int32(0x7FFFFFFF)), _F32)

    def count_ge(ref, cands, dtype):
        def add_tile(c, accs):
            start = pl.multiple_of(c * KEY_TILE, KEY_TILE)
            out = []
            for e in seqs:
                acc = accs[e]
                inc = jnp.where(ref[e, pl.ds(start, KEY_TILE), :] >= cands[e],
                                jnp.ones((), dtype), jnp.zeros((), dtype))
                for part in range(KEY_TILE // COUNT_ROWS):
                    acc = acc + inc[part * COUNT_ROWS:(part + 1) * COUNT_ROWS]
                out.append(acc)
            return tuple(out)

        accs = lax.fori_loop(0, n_chunks, add_tile,
                             tuple(jnp.zeros((COUNT_ROWS, tq), dtype) for _ in seqs))
        return [jnp.sum(acc.astype(_F32), axis=0, keepdims=True) for acc in accs]

    def bf16_code(block):
        return jnp.where(block >= 0, block, block | jnp.int32(0xFFFF))

    def coarse_bit(i, blocks):
        cands = [block + (jnp.int32(1) << (31 - i)) for block in blocks]
        cnts = count_ge(score16_scr, [code_to_f32(bf16_code(cand)).astype(_BF16) for cand in cands], _BF16)
        return tuple(jnp.where(cnt >= top_k, cand, block) for cnt, cand, block in zip(cnts, cands, blocks))

    def fine_bit(i, carry):
        codes, n_aboves = carry
        cands = [code + (jnp.int32(1) << (16 - i)) for code in codes]
        cnts = count_ge(score_scr, [code_to_f32(cand) for cand in cands], _F32)
        hits = [cnt >= top_k for cnt in cnts]
        return (tuple(jnp.where(hit, cand, code) for hit, cand, code in zip(hits, cands, codes)),
                tuple(jnp.where(hit, n_above, cnt) for hit, n_above, cnt in zip(hits, n_aboves, cnts)))

    for e in seqs:
        sel_scr[e, 0:1, :] = jnp.full((1, tq), -jnp.inf, _F32)
        sel_scr[e, 1:2, :] = jnp.zeros((1, tq), _F32)

    @pl.when((j + 1) * tq > top_k)
    def _():
        int_min = jnp.int32(-2 ** 31)
        coarse = lax.fori_loop(0, 16, coarse_bit, tuple(jnp.full((1, tq), int_min, jnp.int32) for _ in seqs))
        bases = tuple(bf16_code(block) - jnp.int32(1 << 16) for block in coarse)
        fine, n_aboves = lax.fori_loop(0, 17, fine_bit, (bases, tuple(jnp.zeros((1, tq), _F32) for _ in seqs)))
        few = (q_pos + 1) <= top_k
        for e in seqs:
            sel_scr[e, 0:1, :] = jnp.where(few, -jnp.inf, code_to_f32(fine[e]))
            sel_scr[e, 1:2, :] = jnp.where(few, 0.0, top_k - n_aboves[e])

    thr = [sel_scr[e, 0:1, :] for e in seqs]
    n_ties_kept = [sel_scr[e, 1:2, :] for e in seqs]

    row = lax.broadcasted_iota(jnp.int32, (KEY_TILE, KEY_TILE), 0)
    col = lax.broadcasted_iota(jnp.int32, (KEY_TILE, KEY_TILE), 1)
    strict_lower = jnp.where(col < row, 1.0, 0.0).astype(_BF16)

    def bias_tile(c, ties_before):
        start = pl.multiple_of(c * KEY_TILE, KEY_TILE)
        out = []
        for e in seqs:
            s = score_scr[e, pl.ds(start, KEY_TILE), :]
            tie = jnp.where(s == thr[e], 1.0, 0.0)
            rank = _dot(strict_lower, tie.astype(_BF16)) + ties_before[e]
            keep = (s > thr[e]) | ((s == thr[e]) & (rank < n_ties_kept[e]))
            bias_scr[e] = jnp.where(keep, 0.0, NEG_BIG)
            out.append(ties_before[e] + jnp.sum(tie, axis=0, keepdims=True))
        return tuple(out)

    n_pairs = ATTN_HEADS // 2
    zeros_half = jnp.zeros((HEAD_DIM, 2 * tq), _BF16)
    for e in seqs:
        for pair in range(n_pairs):
            g = (2 * pair) // ATTN_GROUP
            q_cat = jnp.concatenate([q_t_ref[e, (2 * pair + hd) * HEAD_DIM:(2 * pair + hd + 1) * HEAD_DIM, :]
                                     for hd in range(2)], axis=1)
            qpad_scr[e, pair] = jnp.concatenate([q_cat, zeros_half] if g == 0 else [zeros_half, q_cat], axis=0)
    m_scr[...] = jnp.full(m_scr.shape, NEG_BIG, _F32)
    acc_scr[...] = jnp.zeros(acc_scr.shape, _F32)

    def logits_stage(c, e, pair):
        start = pl.multiple_of(c * KEY_TILE, KEY_TILE)
        b = bias_scr[e]
        s = _dot(k_ref[e, pl.ds(start, KEY_TILE), :], qpad_scr[e, pair]) + jnp.concatenate([b, b], axis=1)
        m_old = m_scr[e, pair]
        m_new = jnp.maximum(m_old, jnp.max(s, axis=0, keepdims=True))
        s_scr[e, pair] = s
        alpha_scr[e, pair] = jnp.exp2(m_old - m_new)
        m_scr[e, pair] = m_new

    def value_stage(c, e, pair):
        start = pl.multiple_of(c * KEY_TILE, KEY_TILE)
        g = (2 * pair) // ATTN_GROUP
        p = jnp.exp2(s_scr[e, pair] - m_scr[e, pair])
        v_c = v_t_ref[e, g * V_ROWS:(g + 1) * V_ROWS, pl.ds(start, KEY_TILE)]
        acc_scr[e, pair] = alpha_scr[e, pair] * acc_scr[e, pair] + _dot(v_c, p.astype(_BF16))

    ties = bias_tile(0, tuple(jnp.zeros((1, tq), _F32) for _ in seqs))
    for pair in range(n_pairs):
        for e in seqs:
            logits_stage(0, e, pair)

    def attn_tile(c, ties):
        ties = bias_tile(c + 1, ties)
        for pair in range(n_pairs):
            for e in seqs:
                value_stage(c, e, pair)
                logits_stage(c + 1, e, pair)
        return ties

    lax.fori_loop(0, n_chunks - 1, attn_tile, ties)
    for pair in range(n_pairs):
        for e in seqs:
            value_stage(n_chunks - 1, e, pair)

    for pair in range(n_pairs):
        for e in seqs:
            o_t = acc_scr[e, pair, :HEAD_DIM] / acc_scr[e, pair, HEAD_DIM:HEAD_DIM + 1]
            both = jnp.concatenate([o_t[:, :tq], o_t[:, tq:]], axis=0)
            z = za_ref[e, :, pair * LANES:(pair + 1) * LANES]
            o_ref[e, :, pair * LANES:(pair + 1) * LANES] = (both.T * _silu(z)).astype(o_ref.dtype)


def _attention(q_t, qi_t, wi_t, k, kidx, v_t, za):
    bsz, seq, _ = k.shape
    tq = Q_TILE
    ns = ATTN_SEQS
    top_k = min(TOPK_MAX, seq // 4)
    feat = lambda rows: pl.BlockSpec((ns, rows, tq), lambda b, j: (b, 0, j))
    full = lambda s1, s2: pl.BlockSpec((ns, s1, s2), lambda b, j: (b, 0, 0))
    n_pairs = ATTN_HEADS // 2
    return pl.pallas_call(
        functools.partial(_attn_kernel, top_k=top_k),
        grid=(bsz // ns, seq // tq),
        in_specs=[feat(ATTN_WIDTH), feat(IDX_HEADS * IDX_DIM), feat(W_IDX_ROWS),
                  full(seq, KV_WIDTH), full(seq, IDX_DIM), full(ATTN_KV_HEADS * V_ROWS, seq),
                  pl.BlockSpec((ns, tq, ATTN_WIDTH), lambda b, j: (b, j, 0))],
        out_specs=pl.BlockSpec((ns, tq, ATTN_WIDTH), lambda b, j: (b, j, 0)),
        out_shape=jax.ShapeDtypeStruct((bsz, seq, ATTN_WIDTH), _BF16),
        scratch_shapes=[pltpu.VMEM((ns, seq, tq), _F32), pltpu.VMEM((ns, seq, tq), _BF16),
                        pltpu.VMEM((ns, SUBLANES, tq), _F32), pltpu.VMEM((ns, KEY_TILE, tq), _F32),
                        pltpu.VMEM((ns, n_pairs, 2 * HEAD_DIM, 2 * tq), _BF16),
                        pltpu.VMEM((ns, n_pairs, 1, 2 * tq), _F32),
                        pltpu.VMEM((ns, n_pairs, 1, 2 * tq), _F32),
                        pltpu.VMEM((ns, n_pairs, V_ROWS, 2 * tq), _F32),
                        pltpu.VMEM((ns, n_pairs, KEY_TILE, 2 * tq), _F32)],
        compiler_params=pltpu.CompilerParams(
            dimension_semantics=("parallel", "arbitrary"), vmem_limit_bytes=VMEM_LIMIT_BYTES),
        name="attn",
    )(q_t, qi_t, wi_t, k, kidx, v_t, za)


def _split3(x):
    hi = x.astype(_BF16)
    r1 = x - hi.astype(_F32)
    mid = r1.astype(_BF16)
    lo = (r1 - mid.astype(_F32)).astype(_BF16)
    return hi, mid, lo


def _ssd_kernel(xs_t_ref, bcs_t_ref, b_tok_ref, dt_ref, zb_t_ref, dtb_ref, alog_ref, dskip_ref, nw_ref, o_t_ref,
                state_scr, y_scr, dt_scr, acum_scr, acum_t_scr, cols_even_scr, cols_odd_scr):
    i = pl.program_id(1)
    r = lax.broadcasted_iota(jnp.int32, (CHUNK, CHUNK), 0)
    c = lax.broadcasted_iota(jnp.int32, (CHUNK, CHUNK), 1)
    causal_t = r <= c

    @pl.when(i == 0)
    def _():
        state_scr[...] = jnp.zeros_like(state_scr)
        upper_b = jnp.where(causal_t, 1.0, 0.0).astype(_BF16)
        dt_in_t = dt_ref[0] + dtb_ref[...]
        dt_t = jnp.maximum(dt_in_t, 0.0) + jnp.log1p(jnp.exp(-jnp.abs(dt_in_t)))
        dt_scr[...] = dt_t
        parts = _split3(dt_t * (-jnp.exp(alog_ref[...])))
        pad = jnp.zeros((LANES - SSM_HEADS, CHUNK), _F32)
        for n in range(dt_t.shape[1] // CHUNK):
            sl = slice(n * CHUNK, (n + 1) * CHUNK)
            a_cum_t = sum(_dot(part[:, sl], upper_b) for part in parts)
            acum_t_scr[:, sl] = a_cum_t
            acum_scr[sl, :] = jnp.concatenate([a_cum_t, pad], axis=0).T

    def spread_columns(chunk, cols_ref):
        a_cum = acum_scr[pl.ds(pl.multiple_of(chunk * CHUNK, CHUNK), CHUNK), :]
        for hh in range(SSM_HEADS):
            cols_ref[:, hh * LANES:(hh + 1) * LANES] = jnp.broadcast_to(a_cum[:, hh:hh + 1], (CHUNK, LANES))

    @pl.when(i == 0)
    def _():
        spread_columns(0, cols_even_scr)

    n_chunks = pl.num_programs(1) * SSD_STEP_CHUNKS

    def chunk_step(ck, cols_ref, next_cols_ref):
        chunk = i * SSD_STEP_CHUNKS + ck
        start = pl.multiple_of(chunk * CHUNK, CHUNK)
        dt_t = dt_scr[:, pl.ds(start, CHUNK)]
        a_cum_t = acum_t_scr[:, pl.ds(start, CHUNK)]
        heads_per_group = SSM_HEADS // SSM_GROUPS

        for grp in range(SSM_GROUPS):
            b_t = bcs_t_ref[0, ck, grp * SSM_STATE:(grp + 1) * SSM_STATE, :]
            c_t = bcs_t_ref[0, ck, BC_WIDTH // 2 + grp * SSM_STATE: BC_WIDTH // 2 + (grp + 1) * SSM_STATE, :]
            b_tok = b_tok_ref[0, ck * CHUNK:(ck + 1) * CHUNK, grp * SSM_STATE:(grp + 1) * SSM_STATE]
            cb_t = _dot(b_tok, c_t)
            cb_t = jnp.where(causal_t, cb_t, 0.0)
            if grp == 0:
                spread_columns(jnp.minimum(chunk + 1, n_chunks - 1), next_cols_ref)

            for hh in range(grp * heads_per_group, (grp + 1) * heads_per_group):
                rows = slice(hh * SSM_HEAD_DIM, (hh + 1) * SSM_HEAD_DIM)
                acum_row = a_cum_t[hh:hh + 1, :]
                last = acum_row[:, CHUNK - 1:CHUNK]
                seg_t = acum_row - cols_ref[:, hh * LANES:(hh + 1) * LANES]
                m_t = (cb_t * jnp.exp(jnp.minimum(seg_t, 0.0))).astype(_BF16)
                x_h = xs_t_ref[0, ck, rows, :].astype(_F32)
                xd = x_h * dt_t[hh:hh + 1, :]
                y = _dot(xd.astype(_BF16), m_t)
                st_prev = state_scr[hh]
                y_off = _dot(st_prev.astype(_BF16), c_t) * jnp.exp(acum_row)
                w = (xd * jnp.exp(last - acum_row)).astype(_BF16)
                st_new = lax.dot_general(w, b_t, _NT, preferred_element_type=_F32)
                state_scr[hh] = jnp.exp(last) * st_prev + st_new
                y_scr[rows, :] = y + y_off + dskip_ref[rows, :] * x_h

        gw = SSM_WIDTH // SSM_GROUPS
        for grp in range(SSM_GROUPS):
            sl = slice(grp * gw, (grp + 1) * gw)
            yz = y_scr[sl, :] * _silu(zb_t_ref[0, ck, sl, :])
            ms = jnp.mean(yz * yz, axis=0, keepdims=True)
            o_t_ref[0, ck, sl, :] = (yz * lax.rsqrt(ms + EPS) * nw_ref[sl, :]).astype(o_t_ref.dtype)

    for ck in range(SSD_STEP_CHUNKS):
        if ck % 2 == 0:
            chunk_step(ck, cols_even_scr, cols_odd_scr)
        else:
            chunk_step(ck, cols_odd_scr, cols_even_scr)


def _ssd(xs_t, bcs_t, b_tok, dt_t, zb_t, dtb, alog, dskip, nw):
    bsz, seq = b_tok.shape[:2]
    nck = SSD_STEP_CHUNKS
    feat = lambda rows: pl.BlockSpec((1, nck, rows, CHUNK), lambda b, i: (b, i, 0, 0))
    consts = (dtb, alog, dskip, nw)
    return pl.pallas_call(
        _ssd_kernel,
        grid=(bsz, seq // (nck * CHUNK)),
        in_specs=[feat(SSM_WIDTH), feat(BC_WIDTH),
                  pl.BlockSpec((1, nck * CHUNK, BC_WIDTH // 2), lambda b, i: (b, i, 0)),
                  pl.BlockSpec((1, SSM_HEADS, seq), lambda b, i: (b, 0, 0)),
                  feat(SSM_WIDTH)] + [_const_spec(a.shape) for a in consts],
        out_specs=feat(SSM_WIDTH),
        out_shape=jax.ShapeDtypeStruct((bsz, seq // CHUNK, SSM_WIDTH, CHUNK), _BF16),
        scratch_shapes=[pltpu.VMEM((SSM_HEADS, SSM_HEAD_DIM, SSM_STATE), _F32),
                        pltpu.VMEM((SSM_WIDTH, CHUNK), _F32),
                        pltpu.VMEM((SSM_HEADS, seq), _F32), pltpu.VMEM((seq, LANES), _F32),
                        pltpu.VMEM((SSM_HEADS, seq), _F32),
                        pltpu.VMEM((CHUNK, SSM_HEADS * LANES), _F32),
                        pltpu.VMEM((CHUNK, SSM_HEADS * LANES), _F32)],
        compiler_params=pltpu.CompilerParams(
            dimension_semantics=("parallel", "arbitrary"), vmem_limit_bytes=VMEM_LIMIT_BYTES),
        name="ssd",
    )(xs_t, bcs_t, b_tok, dt_t, zb_t, *consts)


def _merge_kernel(x_ref, oa_ref, ob_t_ref, g_ref, gb_ref, wa_ref, wb_ref, wo_ref, fw_ref, o_ref, *, final_norm):
    gates = jax.nn.sigmoid(g_ref[0] + gb_ref[...])
    ob_t = jnp.concatenate([ob_t_ref[0, ck] for ck in range(ob_t_ref.shape[1])], axis=1)
    branch_b = lax.dot_general(ob_t, wb_ref[...], _TN, preferred_element_type=_F32)
    merged = gates[:, :D_MODEL] * _dot(oa_ref[0], wa_ref[...]) + gates[:, D_MODEL:] * branch_b
    y = x_ref[0] + _dot(merged.astype(_BF16), wo_ref[...])
    if final_norm:
        y = y * lax.rsqrt(jnp.mean(y * y, axis=-1, keepdims=True) + EPS) * fw_ref[...]
    o_ref[0] = y


def _merge(x, o_a, o_b_t, gates, gate_bias, wa, wb, wo, fw, final_norm):
    bsz, seq, _ = x.shape
    tm = MERGE_ROWS
    tok = lambda width: pl.BlockSpec((1, tm, width), lambda b, i: (b, i, 0))
    consts = (gate_bias, wa, wb, wo, fw)
    return pl.pallas_call(
        functools.partial(_merge_kernel, final_norm=final_norm),
        grid=(bsz, seq // tm),
        in_specs=[tok(D_MODEL), tok(ATTN_WIDTH), pl.BlockSpec((1, tm // CHUNK, SSM_WIDTH, CHUNK), lambda b, i: (b, i, 0, 0)),
                  tok(N_BRANCH * D_MODEL)] + [_const_spec(a.shape) for a in consts],
        out_specs=tok(D_MODEL),
        out_shape=jax.ShapeDtypeStruct((bsz, seq, D_MODEL), _F32),
        compiler_params=pltpu.CompilerParams(
            dimension_semantics=("parallel", "parallel"), vmem_limit_bytes=VMEM_LIMIT_BYTES),
        name="merge",
    )(x, o_a, o_b_t, gates, *consts)


def _cast_kernel(w_ref, o_ref):
    segs = o_ref.shape[1] // LANES
    for k in range(segs):
        o_ref[:, k * LANES:(k + 1) * LANES] = w_ref[0, pl.ds(k, o_ref.shape[0], stride=segs), :].astype(o_ref.dtype)


def _layer_transposed_bf16(w_all, layer):
    depth, cols, rows = w_all.shape
    segs = cols // LANES
    w_rows = jnp.swapaxes(w_all, 1, 2).reshape(depth, rows * segs, LANES)
    tm = CAST_ROWS
    return pl.pallas_call(
        _cast_kernel,
        grid=(pl.cdiv(rows, tm),),
        in_specs=[pl.BlockSpec((1, tm * segs, LANES), lambda i: (layer, i, 0))],
        out_specs=pl.BlockSpec((tm, cols), lambda i: (i, 0)),
        out_shape=jax.ShapeDtypeStruct((rows, cols), _BF16),
        compiler_params=pltpu.CompilerParams(dimension_semantics=("parallel",)),
        name="cast",
    )(w_rows)


def _layer_weights(w_in_all, layer):
    wt16 = _layer_transposed_bf16(w_in_all, layer)
    o = SPLIT_OFFSETS
    seg = lambda i: wt16[o[i]:o[i + 1]]
    w_q, w_k, w_v, w_za, w_qi, w_ki, w_wi, w_zb, w_xb, w_b, w_c, w_dt, w_g = (seg(i) for i in range(13))
    pad = lambda w, n: jnp.pad(w, ((0, n - w.shape[0]), (0, 0)))
    w_misc = pad(jnp.concatenate([w_ki, pad(w_wi, W_IDX_ROWS), w_dt], axis=0), LANES)
    w_t = jnp.concatenate([w_q, w_qi, w_k, w_misc, w_v], axis=0)
    return w_t, w_zb, w_xb, jnp.concatenate([w_b, w_c], axis=0), w_g.T, w_za.T


def _rope_tables(positions):
    inv_freq = ROPE_THETA ** (-jnp.arange(0, ROT_DIM, 2, dtype=_F32) / ROT_DIM)
    ang = jnp.swapaxes(positions.astype(_F32)[..., None] * inv_freq, 1, 2)
    return jnp.cos(ang), jnp.sin(ang)


def _lane_broadcast(v):
    return jnp.broadcast_to(v[..., None], v.shape + (LANES,))


def kernel(x, positions, norm_w, w_in, gate_bias, conv_w, conv_b, dt_bias, a_log, d_skip,
           ssm_norm_w, w_branch_a, w_branch_b, w_out, final_norm_w):
    depth = norm_w.shape[0]
    cos_t, sin_t = _rope_tables(positions)
    for i in range(depth):
        weights = _layer_weights(w_in, i)
        conv_params = (_lane_broadcast(conv_w[i][:, :SSM_WIDTH]), _lane_broadcast(conv_b[i][:SSM_WIDTH]),
                       _lane_broadcast(conv_w[i][:, SSM_WIDTH:]), _lane_broadcast(conv_b[i][SSM_WIDTH:]))
        (q_t, qi_t, wi_t, dt_t, v_t, k, kidx, zb_t, xs_t, bcs_t, b_tok, gates, za) = _projection(
            x, norm_w[i][None, :], cos_t, sin_t, weights, conv_params)
        o_a = _attention(q_t, qi_t, wi_t, k, kidx, v_t, za)
        o_b_t = _ssd(xs_t, bcs_t, b_tok, dt_t, zb_t, dt_bias[i][:, None], a_log[i][:, None],
                     _lane_broadcast(jnp.repeat(d_skip[i], SSM_HEAD_DIM)), _lane_broadcast(ssm_norm_w[i]))
        x = _merge(x, o_a, o_b_t, gates, gate_bias[i][None, :], w_branch_a[i].astype(_BF16),
                   w_branch_b[i].astype(_BF16), w_out[i].astype(_BF16), final_norm_w[None, :],
                   final_norm=(i == depth - 1))
    return x
```

```python
import functools

import numpy as np
import jax
import jax.numpy as jnp
from jax import lax
from jax.experimental import pallas as pl
from jax.experimental.pallas import tpu as pltpu

D_MODEL = 1024
ATTN_HEADS = 8
ATTN_KV_HEADS = 2
HEAD_DIM = 64
ATTN_GROUP = ATTN_HEADS // ATTN_KV_HEADS
ATTN_WIDTH = ATTN_HEADS * HEAD_DIM
KV_WIDTH = ATTN_KV_HEADS * HEAD_DIM
ROT_DIM = HEAD_DIM // 4
ROT_HALF = ROT_DIM // 2
ROPE_THETA = 500000.0
IDX_HEADS = 4
IDX_DIM = 64
TOPK_MAX = 256
SSM_HEADS = 16
SSM_HEAD_DIM = 64
SSM_WIDTH = SSM_HEADS * SSM_HEAD_DIM
SSM_GROUPS = 4
SSM_STATE = 64
CONV_K = 4
CHUNK = 128
BC_WIDTH = 2 * SSM_GROUPS * SSM_STATE
N_BRANCH = 2
EPS = 1e-6
SPLIT_SIZES = (ATTN_WIDTH, KV_WIDTH, KV_WIDTH, ATTN_WIDTH,
               IDX_HEADS * IDX_DIM, IDX_DIM, IDX_HEADS,
               SSM_WIDTH, SSM_WIDTH, SSM_GROUPS * SSM_STATE, SSM_GROUPS * SSM_STATE, SSM_HEADS,
               N_BRANCH * D_MODEL)
SPLIT_OFFSETS = tuple(int(o) for o in np.cumsum((0,) + SPLIT_SIZES))

LANES = 128
SUBLANES = 8
VMEM_LIMIT_BYTES = 56 * 1024 * 1024

PROJ_ROWS = 512
CONV_ROWS = 256
CAST_ROWS = 512
ATTN_SEQS = 4
Q_TILE = 256
KEY_TILE = 256
COUNT_ROWS_F32 = 16
COUNT_ROWS_BF16 = 32
LOG2_E = 1.4426950408889634
MERGE_ROWS = 512
SSD_STEP_CHUNKS = 4
W_IDX_ROWS = 8
V_ROWS = HEAD_DIM + 16
NEG_BIG = -1e30

_F32 = jnp.float32
_BF16 = jnp.bfloat16
_NT = (((1,), (1,)), ((), ()))
_TN = (((0,), (0,)), ((), ()))


def _dot(a, b):
    return jnp.dot(a, b, preferred_element_type=_F32)


def _silu(x):
    return x * jax.nn.sigmoid(x)


def _causal_conv_silu_t(x, tail_ref, w_ref, b_ref):
    cols = x.shape[1]
    reps = cols // LANES
    tail = tail_ref[...]
    lane = lax.broadcasted_iota(jnp.int32, (x.shape[0], LANES), 1)
    wide = lambda a: jnp.concatenate([a] * reps, axis=1)
    y = x * wide(w_ref[CONV_K - 1]) + wide(b_ref[...])
    for shift in range(1, CONV_K):
        rolled = pltpu.roll(x, shift, 1)
        head = jnp.where(lane < shift, pltpu.roll(tail, shift, 1), rolled[:, :LANES])
        shifted = jnp.concatenate([head, rolled[:, LANES:]], axis=1)
        y = y + shifted * wide(w_ref[CONV_K - 1 - shift])
    tail_ref[...] = x[:, cols - LANES:]
    return _silu(y)


def _proj_kernel(x_ref, nw_ref, cos_t_ref, sin_t_ref,
                 w_t_ref, w_zb_ref, w_xb_ref, w_bc_ref, w_g_ref, w_za_ref,
                 cwx_ref, cbx_ref, cwbc_ref, cbbc_ref,
                 q_t_ref, qi_t_ref, wi_t_ref, dt_t_ref, v_t_ref, k_ref, kidx_ref,
                 zb_t_ref, xs_t_ref, bcs_t_ref, b_tok_ref, g_ref, za_ref,
                 tailx_scr, tailbc_scr):
    @pl.when(pl.program_id(1) == 0)
    def _():
        tailx_scr[...] = jnp.zeros_like(tailx_scr)
        tailbc_scr[...] = jnp.zeros_like(tailbc_scr)

    x = x_ref[0]
    h = x * lax.rsqrt(jnp.mean(x * x, axis=-1, keepdims=True) + EPS) * nw_ref[...]
    h = h.astype(_BF16)

    t = lax.dot_general(w_t_ref[...], h, _NT, preferred_element_type=_F32)
    cos_t = cos_t_ref[0]
    sin_t = sin_t_ref[0]

    def rope_head(block, hd):
        x1 = block[hd * HEAD_DIM: hd * HEAD_DIM + ROT_HALF]
        x2 = block[hd * HEAD_DIM + ROT_HALF: hd * HEAD_DIM + ROT_DIM]
        return jnp.concatenate([x1 * cos_t - x2 * sin_t, x2 * cos_t + x1 * sin_t], axis=0)

    def rope_store(block, n_heads, scale, out_ref):
        out_ref[0] = (block * scale).astype(out_ref.dtype)
        for hd in range(n_heads):
            out_ref[0, hd * HEAD_DIM: hd * HEAD_DIM + ROT_DIM, :] = (
                rope_head(block, hd) * scale).astype(out_ref.dtype)

    def rope_value(block, n_heads):
        parts = []
        for hd in range(n_heads):
            parts += [rope_head(block, hd), block[hd * HEAD_DIM + ROT_DIM:(hd + 1) * HEAD_DIM]]
        if block.shape[0] > n_heads * HEAD_DIM:
            parts.append(block[n_heads * HEAD_DIM:])
        return jnp.concatenate(parts, axis=0)

    o_qi = ATTN_WIDTH
    o_k = o_qi + IDX_HEADS * IDX_DIM
    o_ki = o_k + KV_WIDTH
    o_v = o_ki + LANES
    rope_store(t[:o_qi], ATTN_HEADS, LOG2_E * HEAD_DIM ** -0.5, q_t_ref)
    rope_store(t[o_qi:o_k], IDX_HEADS, IDX_DIM ** -0.5, qi_t_ref)
    o_wi = o_ki + IDX_DIM
    o_dt = o_wi + W_IDX_ROWS
    wi_t_ref[0] = t[o_wi:o_dt] * (IDX_HEADS ** -0.5)
    dt_t_ref[0] = t[o_dt:o_dt + SSM_HEADS]
    for g in range(ATTN_KV_HEADS):
        v_t_ref[0, g * V_ROWS:g * V_ROWS + HEAD_DIM, :] = t[o_v + g * HEAD_DIM:o_v + (g + 1) * HEAD_DIM].astype(_BF16)
        v_t_ref[0, g * V_ROWS + HEAD_DIM:(g + 1) * V_ROWS, :] = jnp.ones((V_ROWS - HEAD_DIM, x.shape[0]), _BF16)
    k_ref[0] = rope_value(t[o_k:o_ki], ATTN_KV_HEADS).T.astype(_BF16)
    kidx_ref[0] = rope_value(t[o_ki:o_v], 1).T[:, :IDX_DIM].astype(_BF16)

    def feat_major(w_ref, r0, rows):
        return lax.dot_general(w_ref[r0:r0 + rows], h, _NT, preferred_element_type=_F32)

    def store_chunks(out_ref, sl, value):
        for ck in range(value.shape[1] // CHUNK):
            out_ref[0, ck, sl, :] = value[:, ck * CHUNK:(ck + 1) * CHUNK].astype(out_ref.dtype)

    rc = CONV_ROWS
    for n in range(SSM_WIDTH // rc):
        sl = slice(n * rc, (n + 1) * rc)
        store_chunks(xs_t_ref, sl, _causal_conv_silu_t(
            feat_major(w_xb_ref, n * rc, rc), tailx_scr.at[sl], cwx_ref.at[:, sl], cbx_ref.at[sl]))
        store_chunks(zb_t_ref, sl, feat_major(w_zb_ref, n * rc, rc))
    for n in range(BC_WIDTH // rc):
        sl = slice(n * rc, (n + 1) * rc)
        bc = _causal_conv_silu_t(
            feat_major(w_bc_ref, n * rc, rc), tailbc_scr.at[sl], cwbc_ref.at[:, sl], cbbc_ref.at[sl])
        store_chunks(bcs_t_ref, sl, bc)
        if (n + 1) * rc <= BC_WIDTH // 2:
            b_tok_ref[0, :, sl] = bc.T.astype(_BF16)
    g_ref[0] = _dot(h, w_g_ref[...]).astype(_BF16)
    za_ref[0] = _dot(h, w_za_ref[...])


def _const_spec(shape):
    nd = len(shape)
    return pl.BlockSpec(shape, lambda *_: (0,) * nd, pipeline_mode=pl.Buffered(1))


def _projection(x, nw, cos_t, sin_t, weights, conv_params):
    bsz, seq, _ = x.shape
    tm = PROJ_ROWS
    tok = lambda width: pl.BlockSpec((1, tm, width), lambda b, i: (b, i, 0))
    feat = lambda rows: pl.BlockSpec((1, rows, tm), lambda b, i: (b, 0, i))
    in_specs = [tok(D_MODEL), _const_spec((1, D_MODEL)), feat(ROT_HALF), feat(ROT_HALF)
                ] + [_const_spec(w.shape) for w in weights + conv_params]
    out_shape = (
        jax.ShapeDtypeStruct((bsz, ATTN_WIDTH, seq), _BF16),
        jax.ShapeDtypeStruct((bsz, IDX_HEADS * IDX_DIM, seq), _BF16),
        jax.ShapeDtypeStruct((bsz, W_IDX_ROWS, seq), _F32),
        jax.ShapeDtypeStruct((bsz, SSM_HEADS, seq), _F32),
        jax.ShapeDtypeStruct((bsz, ATTN_KV_HEADS * V_ROWS, seq), _BF16),
        jax.ShapeDtypeStruct((bsz, seq, KV_WIDTH), _BF16),
        jax.ShapeDtypeStruct((bsz, seq, IDX_DIM), _BF16),
        jax.ShapeDtypeStruct((bsz, seq // CHUNK, SSM_WIDTH, CHUNK), _F32),
        jax.ShapeDtypeStruct((bsz, seq // CHUNK, SSM_WIDTH, CHUNK), _BF16),
        jax.ShapeDtypeStruct((bsz, seq // CHUNK, BC_WIDTH, CHUNK), _BF16),
        jax.ShapeDtypeStruct((bsz, seq, BC_WIDTH // 2), _BF16),
        jax.ShapeDtypeStruct((bsz, seq, N_BRANCH * D_MODEL), _BF16),
        jax.ShapeDtypeStruct((bsz, seq, ATTN_WIDTH), _F32),
    )
    slabs = lambda rows: pl.BlockSpec((1, tm // CHUNK, rows, CHUNK), lambda b, i: (b, i, 0, 0))
    out_specs = (feat(ATTN_WIDTH), feat(IDX_HEADS * IDX_DIM), feat(W_IDX_ROWS), feat(SSM_HEADS),
                 feat(ATTN_KV_HEADS * V_ROWS), tok(KV_WIDTH), tok(IDX_DIM), slabs(SSM_WIDTH), slabs(SSM_WIDTH),
                 slabs(BC_WIDTH), tok(BC_WIDTH // 2), tok(N_BRANCH * D_MODEL), tok(ATTN_WIDTH))
    return pl.pallas_call(
        _proj_kernel,
        grid=(bsz, seq // tm),
        in_specs=in_specs,
        out_specs=out_specs,
        out_shape=out_shape,
        scratch_shapes=[pltpu.VMEM((SSM_WIDTH, LANES), _F32), pltpu.VMEM((BC_WIDTH, LANES), _F32)],
        compiler_params=pltpu.CompilerParams(
            dimension_semantics=("parallel", "arbitrary"), vmem_limit_bytes=VMEM_LIMIT_BYTES),
        name="proj",
    )(x, nw, cos_t, sin_t, *weights, *conv_params)


def _attn_kernel(q_t_ref, qi_t_ref, wi_t_ref, k_ref, kidx_ref, v_t_ref, za_ref, o_ref,
                 score_scr, score16_scr, sel_scr, bias_scr, qpad_scr, m_scr, alpha_scr, acc_scr, s_scr,
                 *, top_k):
    j = pl.program_id(1)
    n_chunks = j + 1
    tq = Q_TILE
    seqs = range(ATTN_SEQS)
    q_pos = j * tq + lax.broadcasted_iota(jnp.int32, (1, tq), 1)
    key_iota = lax.broadcasted_iota(jnp.int32, (KEY_TILE, tq), 0)

    qi_cat, wi_cat = [], []
    for e in seqs:
        qi_t = qi_t_ref[e]
        qi_cat.append(jnp.concatenate([qi_t[hd * IDX_DIM:(hd + 1) * IDX_DIM] for hd in range(IDX_HEADS)], axis=1))
        wi_t = wi_t_ref[e]
        wi_cat.append(jnp.concatenate([wi_t[hd:hd + 1] for hd in range(IDX_HEADS)], axis=1))

    def score_chunk(c, carry):
        start = pl.multiple_of(c * KEY_TILE, KEY_TILE)
        for e in seqs:
            logits = _dot(kidx_ref[e, pl.ds(start, KEY_TILE), :], qi_cat[e])
            weighted = jnp.maximum(logits, 0.0) * wi_cat[e]
            score = weighted[:, 0:tq]
            for hd in range(1, IDX_HEADS):
                score = score + weighted[:, hd * tq:(hd + 1) * tq]
            score = jnp.where((start + key_iota) <= q_pos, score, -jnp.inf)
            score_scr[e, pl.ds(start, KEY_TILE), :] = score
            score16_scr[e, pl.ds(start, KEY_TILE), :] = score.astype(_BF16)
        return carry

    lax.fori_loop(0, n_chunks, score_chunk, 0)

    def code_to_f32(code):
        return pltpu.bitcast(code ^ ((code >> 31) & jnp.int32(0x7FFFFFFF)), _F32)

    def count_ge(ref, cands, dtype):
        count_rows = COUNT_ROWS_BF16 if dtype == _BF16 else COUNT_ROWS_F32

        def add_tile(c, accs):
            start = pl.multiple_of(c * KEY_TILE, KEY_TILE)
            out = []
            for e in seqs:
                acc = accs[e]
                inc = jnp.where(ref[e, pl.ds(start, KEY_TILE), :] >= cands[e],
                                jnp.ones((), dtype), jnp.zeros((), dtype))
                for part in range(KEY_TILE // count_rows):
                    acc = acc + inc[part * count_rows:(part + 1) * count_rows]
                out.append(acc)
            return tuple(out)

        accs = lax.fori_loop(0, n_chunks, add_tile,
                             tuple(jnp.zeros((count_rows, tq), dtype) for _ in seqs))
        return [jnp.sum(acc.astype(_F32), axis=0, keepdims=True) for acc in accs]

    def bf16_code(block):
        return jnp.where(block >= 0, block, block | jnp.int32(0xFFFF))

    def coarse_bit(i, blocks):
        cands = [block + (jnp.int32(1) << (31 - i)) for block in blocks]
        cnts = count_ge(score16_scr, [code_to_f32(bf16_code(cand)).astype(_BF16) for cand in cands], _BF16)
        return tuple(jnp.where(cnt >= top_k, cand, block) for cnt, cand, block in zip(cnts, cands, blocks))

    def fine_bit(i, carry):
        codes, n_aboves = carry
        cands = [code + (jnp.int32(1) << (16 - i)) for code in codes]
        cnts = count_ge(score_scr, [code_to_f32(cand) for cand in cands], _F32)
        hits = [cnt >= top_k for cnt in cnts]
        return (tuple(jnp.where(hit, cand, code) for hit, cand, code in zip(hits, cands, codes)),
                tuple(jnp.where(hit, n_above, cnt) for hit, n_above, cnt in zip(hits, n_aboves, cnts)))

    for e in seqs:
        sel_scr[e, 0:1, :] = jnp.full((1, tq), -jnp.inf, _F32)
        sel_scr[e, 1:2, :] = jnp.zeros((1, tq), _F32)

    @pl.when((j + 1) * tq > top_k)
    def _():
        int_min = jnp.int32(-2 ** 31)
        coarse = lax.fori_loop(0, 16, coarse_bit, tuple(jnp.full((1, tq), int_min, jnp.int32) for _ in seqs))
        bases = tuple(bf16_code(block) - jnp.int32(1 << 16) for block in coarse)
        fine, n_aboves = lax.fori_loop(0, 17, fine_bit, (bases, tuple(jnp.zeros((1, tq), _F32) for _ in seqs)))
        few = (q_pos + 1) <= top_k
        for e in seqs:
            sel_scr[e, 0:1, :] = jnp.where(few, -jnp.inf, code_to_f32(fine[e]))
            sel_scr[e, 1:2, :] = jnp.where(few, 0.0, top_k - n_aboves[e])

    thr = [sel_scr[e, 0:1, :] for e in seqs]
    n_ties_kept = [sel_scr[e, 1:2, :] for e in seqs]

    row = lax.broadcasted_iota(jnp.int32, (KEY_TILE, KEY_TILE), 0)
    col = lax.broadcasted_iota(jnp.int32, (KEY_TILE, KEY_TILE), 1)
    strict_lower = jnp.where(col < row, 1.0, 0.0).astype(_BF16)

    def bias_tile(c, ties_before):
        start = pl.multiple_of(c * KEY_TILE, KEY_TILE)
        out = []
        for e in seqs:
            s = score_scr[e, pl.ds(start, KEY_TILE), :]
            tie = jnp.where(s == thr[e], 1.0, 0.0)
            rank = _dot(strict_lower, tie.astype(_BF16)) + ties_before[e]
            keep = (s > thr[e]) | ((s == thr[e]) & (rank < n_ties_kept[e]))
            bias_scr[e] = jnp.where(keep, 0.0, NEG_BIG)
            out.append(ties_before[e] + jnp.sum(tie, axis=0, keepdims=True))
        return tuple(out)

    n_pairs = ATTN_HEADS // 2
    zeros_half = jnp.zeros((HEAD_DIM, 2 * tq), _BF16)
    for e in seqs:
        for pair in range(n_pairs):
            g = (2 * pair) // ATTN_GROUP
            q_cat = jnp.concatenate([q_t_ref[e, (2 * pair + hd) * HEAD_DIM:(2 * pair + hd + 1) * HEAD_DIM, :]
                                     for hd in range(2)], axis=1)
            qpad_scr[e, pair] = jnp.concatenate([q_cat, zeros_half] if g == 0 else [zeros_half, q_cat], axis=0)
    m_scr[...] = jnp.full(m_scr.shape, NEG_BIG, _F32)
    acc_scr[...] = jnp.zeros(acc_scr.shape, _F32)

    def logits_stage(c, e, pair):
        start = pl.multiple_of(c * KEY_TILE, KEY_TILE)
        b = bias_scr[e]
        s = _dot(k_ref[e, pl.ds(start, KEY_TILE), :], qpad_scr[e, pair]) + jnp.concatenate([b, b], axis=1)
        m_old = m_scr[e, pair]
        m_new = jnp.maximum(m_old, jnp.max(s, axis=0, keepdims=True))
        s_scr[e, pair] = s
        alpha_scr[e, pair] = jnp.exp2(m_old - m_new)
        m_scr[e, pair] = m_new

    def value_stage(c, e, pair):
        start = pl.multiple_of(c * KEY_TILE, KEY_TILE)
        g = (2 * pair) // ATTN_GROUP
        p = jnp.exp2(s_scr[e, pair] - m_scr[e, pair])
        v_c = v_t_ref[e, g * V_ROWS:(g + 1) * V_ROWS, pl.ds(start, KEY_TILE)]
        acc_scr[e, pair] = alpha_scr[e, pair] * acc_scr[e, pair] + _dot(v_c, p.astype(_BF16))

    ties = bias_tile(0, tuple(jnp.zeros((1, tq), _F32) for _ in seqs))
    for pair in range(n_pairs):
        for e in seqs:
            logits_stage(0, e, pair)

    def attn_tile(c, ties):
        ties = bias_tile(c + 1, ties)
        for pair in range(n_pairs):
            for e in seqs:
                value_stage(c, e, pair)
                logits_stage(c + 1, e, pair)
        return ties

    lax.fori_loop(0, n_chunks - 1, attn_tile, ties)
    for pair in range(n_pairs):
        for e in seqs:
            value_stage(n_chunks - 1, e, pair)

    for pair in range(n_pairs):
        for e in seqs:
            o_t = acc_scr[e, pair, :HEAD_DIM] / acc_scr[e, pair, HEAD_DIM:HEAD_DIM + 1]
            both = jnp.concatenate([o_t[:, :tq], o_t[:, tq:]], axis=0)
            z = za_ref[e, :, pair * LANES:(pair + 1) * LANES]
            o_ref[e, :, pair * LANES:(pair + 1) * LANES] = (both.T * _silu(z)).astype(o_ref.dtype)


def _attention(q_t, qi_t, wi_t, k, kidx, v_t, za):
    bsz, seq, _ = k.shape
    tq = Q_TILE
    ns = ATTN_SEQS
    top_k = min(TOPK_MAX, seq // 4)
    feat = lambda rows: pl.BlockSpec((ns, rows, tq), lambda b, j: (b, 0, j))
    full = lambda s1, s2: pl.BlockSpec((ns, s1, s2), lambda b, j: (b, 0, 0))
    n_pairs = ATTN_HEADS // 2
    return pl.pallas_call(
        functools.partial(_attn_kernel, top_k=top_k),
        grid=(bsz // ns, seq // tq),
        in_specs=[feat(ATTN_WIDTH), feat(IDX_HEADS * IDX_DIM), feat(W_IDX_ROWS),
                  full(seq, KV_WIDTH), full(seq, IDX_DIM), full(ATTN_KV_HEADS * V_ROWS, seq),
                  pl.BlockSpec((ns, tq, ATTN_WIDTH), lambda b, j: (b, j, 0))],
        out_specs=pl.BlockSpec((ns, tq, ATTN_WIDTH), lambda b, j: (b, j, 0)),
        out_shape=jax.ShapeDtypeStruct((bsz, seq, ATTN_WIDTH), _BF16),
        scratch_shapes=[pltpu.VMEM((ns, seq, tq), _F32), pltpu.VMEM((ns, seq, tq), _BF16),
                        pltpu.VMEM((ns, SUBLANES, tq), _F32), pltpu.VMEM((ns, KEY_TILE, tq), _F32),
                        pltpu.VMEM((ns, n_pairs, 2 * HEAD_DIM, 2 * tq), _BF16),
                        pltpu.VMEM((ns, n_pairs, 1, 2 * tq), _F32),
                        pltpu.VMEM((ns, n_pairs, 1, 2 * tq), _F32),
                        pltpu.VMEM((ns, n_pairs, V_ROWS, 2 * tq), _F32),
                        pltpu.VMEM((ns, n_pairs, KEY_TILE, 2 * tq), _F32)],
        compiler_params=pltpu.CompilerParams(
            dimension_semantics=("parallel", "arbitrary"), vmem_limit_bytes=VMEM_LIMIT_BYTES),
        name="attn",
    )(q_t, qi_t, wi_t, k, kidx, v_t, za)


def _split3(x):
    hi = x.astype(_BF16)
    r1 = x - hi.astype(_F32)
    mid = r1.astype(_BF16)
    lo = (r1 - mid.astype(_F32)).astype(_BF16)
    return hi, mid, lo


def _ssd_kernel(xs_t_ref, bcs_t_ref, b_tok_ref, dt_ref, zb_t_ref, dtb_ref, alog_ref, dskip_ref, nw_ref, o_t_ref,
                state_scr, y_scr, dt_scr, acum_scr, acum_t_scr, cols_even_scr, cols_odd_scr):
    i = pl.program_id(1)
    r = lax.broadcasted_iota(jnp.int32, (CHUNK, CHUNK), 0)
    c = lax.broadcasted_iota(jnp.int32, (CHUNK, CHUNK), 1)
    causal_t = r <= c

    @pl.when(i == 0)
    def _():
        state_scr[...] = jnp.zeros_like(state_scr)
        upper_b = jnp.where(causal_t, 1.0, 0.0).astype(_BF16)
        dt_in_t = dt_ref[0] + dtb_ref[...]
        dt_t = jnp.maximum(dt_in_t, 0.0) + jnp.log1p(jnp.exp(-jnp.abs(dt_in_t)))
        dt_scr[...] = dt_t
        parts = _split3(dt_t * (-jnp.exp(alog_ref[...])))
        pad = jnp.zeros((LANES - SSM_HEADS, CHUNK), _F32)
        for n in range(dt_t.shape[1] // CHUNK):
            sl = slice(n * CHUNK, (n + 1) * CHUNK)
            a_cum_t = sum(_dot(part[:, sl], upper_b) for part in parts)
            acum_t_scr[:, sl] = a_cum_t
            acum_scr[sl, :] = jnp.concatenate([a_cum_t, pad], axis=0).T

    def spread_columns(chunk, cols_ref):
        a_cum = acum_scr[pl.ds(pl.multiple_of(chunk * CHUNK, CHUNK), CHUNK), :]
        for hh in range(SSM_HEADS):
            cols_ref[:, hh * LANES:(hh + 1) * LANES] = jnp.broadcast_to(a_cum[:, hh:hh + 1], (CHUNK, LANES))

    @pl.when(i == 0)
    def _():
        spread_columns(0, cols_even_scr)

    n_chunks = pl.num_programs(1) * SSD_STEP_CHUNKS

    def chunk_step(ck, cols_ref, next_cols_ref):
        chunk = i * SSD_STEP_CHUNKS + ck
        start = pl.multiple_of(chunk * CHUNK, CHUNK)
        dt_t = dt_scr[:, pl.ds(start, CHUNK)]
        a_cum_t = acum_t_scr[:, pl.ds(start, CHUNK)]
        heads_per_group = SSM_HEADS // SSM_GROUPS

        for grp in range(SSM_GROUPS):
            b_t = bcs_t_ref[0, ck, grp * SSM_STATE:(grp + 1) * SSM_STATE, :]
            c_t = bcs_t_ref[0, ck, BC_WIDTH // 2 + grp * SSM_STATE: BC_WIDTH // 2 + (grp + 1) * SSM_STATE, :]
            b_tok = b_tok_ref[0, ck * CHUNK:(ck + 1) * CHUNK, grp * SSM_STATE:(grp + 1) * SSM_STATE]
            cb_t = _dot(b_tok, c_t)
            cb_t = jnp.where(causal_t, cb_t, 0.0)
            if grp == 0:
                spread_columns(jnp.minimum(chunk + 1, n_chunks - 1), next_cols_ref)

            for hh in range(grp * heads_per_group, (grp + 1) * heads_per_group):
                rows = slice(hh * SSM_HEAD_DIM, (hh + 1) * SSM_HEAD_DIM)
                acum_row = a_cum_t[hh:hh + 1, :]
                last = acum_row[:, CHUNK - 1:CHUNK]
                seg_t = acum_row - cols_ref[:, hh * LANES:(hh + 1) * LANES]
                m_t = (cb_t * jnp.exp(jnp.minimum(seg_t, 0.0))).astype(_BF16)
                x_h = xs_t_ref[0, ck, rows, :].astype(_F32)
                xd = x_h * dt_t[hh:hh + 1, :]
                y = _dot(xd.astype(_BF16), m_t)
                st_prev = state_scr[hh]
                y_off = _dot(st_prev.astype(_BF16), c_t) * jnp.exp(acum_row)
                w = (xd * jnp.exp(last - acum_row)).astype(_BF16)
                st_new = lax.dot_general(w, b_t, _NT, preferred_element_type=_F32)
                state_scr[hh] = jnp.exp(last) * st_prev + st_new
                y_scr[rows, :] = y + y_off + dskip_ref[rows, :] * x_h

        gw = SSM_WIDTH // SSM_GROUPS
        for grp in range(SSM_GROUPS):
            sl = slice(grp * gw, (grp + 1) * gw)
            yz = y_scr[sl, :] * _silu(zb_t_ref[0, ck, sl, :])
            ms = jnp.mean(yz * yz, axis=0, keepdims=True)
            o_t_ref[0, ck, sl, :] = (yz * lax.rsqrt(ms + EPS) * nw_ref[sl, :]).astype(o_t_ref.dtype)

    for ck in range(SSD_STEP_CHUNKS):
        if ck % 2 == 0:
            chunk_step(ck, cols_even_scr, cols_odd_scr)
        else:
            chunk_step(ck, cols_odd_scr, cols_even_scr)


def _ssd(xs_t, bcs_t, b_tok, dt_t, zb_t, dtb, alog, dskip, nw):
    bsz, seq = b_tok.shape[:2]
    nck = SSD_STEP_CHUNKS
    feat = lambda rows: pl.BlockSpec((1, nck, rows, CHUNK), lambda b, i: (b, i, 0, 0))
    consts = (dtb, alog, dskip, nw)
    return pl.pallas_call(
        _ssd_kernel,
        grid=(bsz, seq // (nck * CHUNK)),
        in_specs=[feat(SSM_WIDTH), feat(BC_WIDTH),
                  pl.BlockSpec((1, nck * CHUNK, BC_WIDTH // 2), lambda b, i: (b, i, 0)),
                  pl.BlockSpec((1, SSM_HEADS, seq), lambda b, i: (b, 0, 0)),
                  feat(SSM_WIDTH)] + [_const_spec(a.shape) for a in consts],
        out_specs=feat(SSM_WIDTH),
        out_shape=jax.ShapeDtypeStruct((bsz, seq // CHUNK, SSM_WIDTH, CHUNK), _BF16),
        scratch_shapes=[pltpu.VMEM((SSM_HEADS, SSM_HEAD_DIM, SSM_STATE), _F32),
                        pltpu.VMEM((SSM_WIDTH, CHUNK), _F32),
                        pltpu.VMEM((SSM_HEADS, seq), _F32), pltpu.VMEM((seq, LANES), _F32),
                        pltpu.VMEM((SSM_HEADS, seq), _F32),
                        pltpu.VMEM((CHUNK, SSM_HEADS * LANES), _F32),
                        pltpu.VMEM((CHUNK, SSM_HEADS * LANES), _F32)],
        compiler_params=pltpu.CompilerParams(
            dimension_semantics=("parallel", "arbitrary"), vmem_limit_bytes=VMEM_LIMIT_BYTES),
        name="ssd",
    )(xs_t, bcs_t, b_tok, dt_t, zb_t, *consts)


def _merge_kernel(x_ref, oa_ref, ob_t_ref, g_ref, gb_ref, wa_ref, wb_ref, wo_ref, fw_ref, o_ref, *, final_norm):
    gates = jax.nn.sigmoid(g_ref[0] + gb_ref[...])
    ob_t = jnp.concatenate([ob_t_ref[0, ck] for ck in range(ob_t_ref.shape[1])], axis=1)
    branch_b = lax.dot_general(ob_t, wb_ref[...], _TN, preferred_element_type=_F32)
    merged = gates[:, :D_MODEL] * _dot(oa_ref[0], wa_ref[...]) + gates[:, D_MODEL:] * branch_b
    y = x_ref[0] + _dot(merged.astype(_BF16), wo_ref[...])
    if final_norm:
        y = y * lax.rsqrt(jnp.mean(y * y, axis=-1, keepdims=True) + EPS) * fw_ref[...]
    o_ref[0] = y


def _merge(x, o_a, o_b_t, gates, gate_bias, wa, wb, wo, fw, final_norm):
    bsz, seq, _ = x.shape
    tm = MERGE_ROWS
    tok = lambda width: pl.BlockSpec((1, tm, width), lambda b, i: (b, i, 0))
    consts = (gate_bias, wa, wb, wo, fw)
    return pl.pallas_call(
        functools.partial(_merge_kernel, final_norm=final_norm),
        grid=(bsz, seq // tm),
        in_specs=[tok(D_MODEL), tok(ATTN_WIDTH), pl.BlockSpec((1, tm // CHUNK, SSM_WIDTH, CHUNK), lambda b, i: (b, i, 0, 0)),
                  tok(N_BRANCH * D_MODEL)] + [_const_spec(a.shape) for a in consts],
        out_specs=tok(D_MODEL),
        out_shape=jax.ShapeDtypeStruct((bsz, seq, D_MODEL), _F32),
        compiler_params=pltpu.CompilerParams(
            dimension_semantics=("parallel", "parallel"), vmem_limit_bytes=VMEM_LIMIT_BYTES),
        name="merge",
    )(x, o_a, o_b_t, gates, *consts)


def _cast_kernel(w_ref, o_ref):
    segs = o_ref.shape[1] // LANES
    for k in range(segs):
        o_ref[:, k * LANES:(k + 1) * LANES] = w_ref[0, pl.ds(k, o_ref.shape[0], stride=segs), :].astype(o_ref.dtype)


def _layer_transposed_bf16(w_all, layer):
    depth, cols, rows = w_all.shape
    segs = cols // LANES
    w_rows = jnp.swapaxes(w_all, 1, 2).reshape(depth, rows * segs, LANES)
    tm = CAST_ROWS
    return pl.pallas_call(
        _cast_kernel,
        grid=(pl.cdiv(rows, tm),),
        in_specs=[pl.BlockSpec((1, tm * segs, LANES), lambda i: (layer, i, 0))],
        out_specs=pl.BlockSpec((tm, cols), lambda i: (i, 0)),
        out_shape=jax.ShapeDtypeStruct((rows, cols), _BF16),
        compiler_params=pltpu.CompilerParams(dimension_semantics=("parallel",)),
        name="cast",
    )(w_rows)


def _layer_weights(w_in_all, layer):
    wt16 = _layer_transposed_bf16(w_in_all, layer)
    o = SPLIT_OFFSETS
    seg = lambda i: wt16[o[i]:o[i + 1]]
    w_q, w_k, w_v, w_za, w_qi, w_ki, w_wi, w_zb, w_xb, w_b, w_c, w_dt, w_g = (seg(i) for i in range(13))
    pad = lambda w, n: jnp.pad(w, ((0, n - w.shape[0]), (0, 0)))
    w_misc = pad(jnp.concatenate([w_ki, pad(w_wi, W_IDX_ROWS), w_dt], axis=0), LANES)
    w_t = jnp.concatenate([w_q, w_qi, w_k, w_misc, w_v], axis=0)
    return w_t, w_zb, w_xb, jnp.concatenate([w_b, w_c], axis=0), w_g.T, w_za.T


def _rope_tables(positions):
    inv_freq = ROPE_THETA ** (-jnp.arange(0, ROT_DIM, 2, dtype=_F32) / ROT_DIM)
    ang = jnp.swapaxes(positions.astype(_F32)[..., None] * inv_freq, 1, 2)
    return jnp.cos(ang), jnp.sin(ang)


def _lane_broadcast(v):
    return jnp.broadcast_to(v[..., None], v.shape + (LANES,))


def kernel(x, positions, norm_w, w_in, gate_bias, conv_w, conv_b, dt_bias, a_log, d_skip,
           ssm_norm_w, w_branch_a, w_branch_b, w_out, final_norm_w):
    depth = norm_w.shape[0]
    cos_t, sin_t = _rope_tables(positions)
    for i in range(depth):
        weights = _layer_weights(w_in, i)
        conv_params = (_lane_broadcast(conv_w[i][:, :SSM_WIDTH]), _lane_broadcast(conv_b[i][:SSM_WIDTH]),
                       _lane_broadcast(conv_w[i][:, SSM_WIDTH:]), _lane_broadcast(conv_b[i][SSM_WIDTH:]))
        (q_t, qi_t, wi_t, dt_t, v_t, k, kidx, zb_t, xs_t, bcs_t, b_tok, gates, za) = _projection(
            x, norm_w[i][None, :], cos_t, sin_t, weights, conv_params)
        o_a = _attention(q_t, qi_t, wi_t, k, kidx, v_t, za)
        o_b_t = _ssd(xs_t, bcs_t, b_tok, dt_t, zb_t, dt_bias[i][:, None], a_log[i][:, None],
                     _lane_broadcast(jnp.repeat(d_skip[i], SSM_HEAD_DIM)), _lane_broadcast(ssm_norm_w[i]))
        x = _merge(x, o_a, o_b_t, gates, gate_bias[i][None, :], w_branch_a[i].astype(_BF16),
                   w_branch_b[i].astype(_BF16), w_out[i].astype(_BF16), final_norm_w[None, :],
                   final_norm=(i == depth - 1))
    return x
```
